```python
import math
import jax, jax.numpy as jnp
from jax import lax
import numpy as np

D_MODEL = 1024
BATCH = 8
SEQ = 2048
DEPTH = 1
DEC_BATCH = 128
DEC_SEQ = 4
PAST_LEN = 16384
PAGE_SIZE = 128

MIX_SSM = D_MODEL // 2
MIX_CONV = D_MODEL - MIX_SSM
SSM_GROUP_CH = 16
SSM_GROUPS = MIX_SSM // SSM_GROUP_CH
SSM_STATE = 64
CONV_HEADS = 8
CONV_K = 3
N_MEM = 256
XATTN_HEADS = 4
XATTN_HEAD_DIM = D_MODEL // XATTN_HEADS
D_FF = 256 * ((8 * D_MODEL // 3 + 255) // 256)
FFN_CONV_K = 3
EPS = 1e-6

kernel_name = "hymba_s5_shortconv_memxattn_step"

F32 = jnp.float32


def rms_norm(x, g):
    xf = x.astype(F32)
    y = xf * lax.rsqrt(jnp.mean(xf * xf, axis=-1, keepdims=True) + EPS)
    return (y * g.astype(F32)).astype(x.dtype)


def causal_dwconv(x, w, prev):
    k = w.shape[0]
    t = x.shape[1]
    xp = jnp.concatenate([prev.astype(x.dtype), x], axis=1)
    y = xp[:, 0:t] * w[0]
    for j in range(1, k):
        y = y + xp[:, j:j + t] * w[j]
    return y, xp[:, t:]


def _complex_affine_combine(e1, e2):
    ar1, ai1, br1, bi1 = e1
    ar2, ai2, br2, bi2 = e2
    return (ar2 * ar1 - ai2 * ai1,
            ar2 * ai1 + ai2 * ar1,
            ar2 * br1 - ai2 * bi1 + br2,
            ar2 * bi1 + ai2 * br1 + bi2)


def s5_ssm(u, A_re, A_im, log_dt, B_re, B_im, C_re, C_im, D, s0):
    b, t, _ = u.shape
    G, P, N = SSM_GROUPS, SSM_GROUP_CH, SSM_STATE
    uf = u.astype(F32).reshape(b, t, G, P)
    dt = jnp.exp(log_dt.astype(F32))[:, None]
    lr = A_re.astype(F32)
    li = A_im.astype(F32)
    mag = jnp.exp(dt * lr)
    ph = dt * li
    ar = mag * jnp.cos(ph)
    ai = mag * jnp.sin(ph)
    den = lr * lr + li * li
    cr = ((ar - 1.0) * lr + ai * li) / den
    ci = (ai * lr - (ar - 1.0) * li) / den
    Br = B_re.astype(F32)
    Bi = B_im.astype(F32)
    bbr = cr[..., None] * Br - ci[..., None] * Bi
    bbi = cr[..., None] * Bi + ci[..., None] * Br
    bu_r = jnp.einsum('gnp,btgp->tbgn', bbr, uf)
    bu_i = jnp.einsum('gnp,btgp->tbgn', bbi, uf)
    a_r = jnp.broadcast_to(ar[None, None], (t, 1, G, N))
    a_i = jnp.broadcast_to(ai[None, None], (t, 1, G, N))
    acum_r, acum_i, s_r, s_i = lax.associative_scan(
        _complex_affine_combine, (a_r, a_i, bu_r, bu_i), axis=0)
    if s0 is not None:
        s0r = s0[0].astype(F32)[None]
        s0i = s0[1].astype(F32)[None]
        s_r, s_i = (s_r + acum_r * s0r - acum_i * s0i,
                    s_i + acum_r * s0i + acum_i * s0r)
    y = (jnp.einsum('gpn,tbgn->btgp', C_re.astype(F32), s_r)
         - jnp.einsum('gpn,tbgn->btgp', C_im.astype(F32), s_i)
         + D.astype(F32) * uf)
    return (y.reshape(b, t, MIX_SSM).astype(u.dtype),
            s_r[-1].astype(u.dtype), s_i[-1].astype(u.dtype))


def memory_kv(mem, g, w_k, w_v):
    b = mem.shape[0]
    m = rms_norm(mem, g)
    k = (m @ w_k).reshape(b, N_MEM, XATTN_HEADS, XATTN_HEAD_DIM)
    v = (m @ w_v).reshape(b, N_MEM, XATTN_HEADS, XATTN_HEAD_DIM)
    return k, v


def cross_attention(h, k, v, w_q, w_xo):
    b, t, _ = h.shape
    q = (h @ w_q).reshape(b, t, XATTN_HEADS, XATTN_HEAD_DIM)
    s = jnp.einsum('bthd,bmhd->bhtm', q, k.astype(q.dtype)).astype(F32) * (XATTN_HEAD_DIM ** -0.5)
    pr = jax.nn.softmax(s, axis=-1).astype(h.dtype)
    o = jnp.einsum('bhtm,bmhd->bthd', pr, v.astype(h.dtype)).reshape(b, t, XATTN_HEADS * XATTN_HEAD_DIM)
    return o @ w_xo


def decoder_layer(x, mem_k, mem_v, prev, p):
    b = x.shape[0]
    if prev is None:
        s0 = None
        conv_prev = jnp.zeros((b, CONV_K - 1, MIX_CONV), x.dtype)
        ffn_prev = jnp.zeros((b, FFN_CONV_K - 1, D_FF), x.dtype)
    else:
        s0 = (prev[0], prev[1])
        conv_prev, ffn_prev = prev[2], prev[3]

    h = rms_norm(x, p['norm_mix'])
    z = h @ p['w_in']
    u, xin, bg, cg = jnp.split(z, [MIX_SSM, MIX_SSM + MIX_CONV, MIX_SSM + 2 * MIX_CONV], axis=-1)
    y_ssm, s_re, s_im = s5_ssm(u, p['ssm_A_re'], p['ssm_A_im'], p['ssm_log_dt'], p['ssm_B_re'],
                               p['ssm_B_im'], p['ssm_C_re'], p['ssm_C_im'], p['ssm_D'], s0)
    y_ssm = jax.nn.gelu(y_ssm)
    y_ssm = y_ssm * jax.nn.sigmoid(y_ssm @ p['w_glu'] + p['b_glu'])
    conv_y, conv_buf = causal_dwconv(cg * xin, p['conv_w'], conv_prev)
    y_conv = bg * conv_y
    mixed = jnp.concatenate([rms_norm(y_ssm, p['norm_ssm_out']),
                             rms_norm(y_conv, p['norm_conv_out'])], axis=-1)
    x = x + mixed @ p['w_out']

    x = x + cross_attention(rms_norm(x, p['norm_xattn']), mem_k, mem_v, p['w_q'], p['w_xo'])

    h = rms_norm(x, p['norm_ffn'])
    a, ffn_buf = causal_dwconv(h @ p['w_up'], p['ffn_conv_w'], ffn_prev)
    x = x + (jax.nn.gelu(a) * (h @ p['w_gate'])) @ p['w_down']
    return x, (s_re, s_im, conv_buf, ffn_buf)


def setup_inputs(seed: int = 0) -> dict:
    key = jax.random.key(seed)
    ks = iter(jax.random.split(key, 48))
    L, D = DEPTH, D_MODEL
    G, N, P = SSM_GROUPS, SSM_STATE, SSM_GROUP_CH

    def nrm(shape, scale):
        return scale * jax.random.normal(next(ks), shape, F32)

    def gain(shape):
        return 1.0 + nrm(shape, 0.02)

    n_idx = jnp.arange(N, dtype=F32)
    return {
        "x_prompt": nrm((BATCH, SEQ, D), 1.0),
        "x_sample": nrm((DEC_BATCH, DEC_SEQ, D), 1.0),
        "mem_prompt": nrm((BATCH, N_MEM, D), 1.0),
        "cache_mem_k": nrm((L, DEC_BATCH, N_MEM, XATTN_HEADS, XATTN_HEAD_DIM), 1.0),
        "cache_mem_v": nrm((L, DEC_BATCH, N_MEM, XATTN_HEADS, XATTN_HEAD_DIM), 1.0),
        "state_ssm_re": nrm((L, DEC_BATCH, G, N), 0.1),
        "state_ssm_im": nrm((L, DEC_BATCH, G, N), 0.1),
        "state_conv": nrm((L, DEC_BATCH, CONV_K - 1, MIX_CONV), 1.0),
        "state_ffn_conv": nrm((L, DEC_BATCH, FFN_CONV_K - 1, D_FF), 1.0),
        "norm_mix": gain((L, D)),
        "w_in": nrm((L, D, MIX_SSM + 3 * MIX_CONV), D ** -0.5),
        "ssm_A_re": -0.5 + nrm((L, G, N), 0.01),
        "ssm_A_im": math.pi * n_idx + nrm((L, G, N), 0.01),
        "ssm_log_dt": jax.random.uniform(next(ks), (L, G), F32, math.log(1e-3), math.log(1e-1)),
        "ssm_B_re": nrm((L, G, N, P), (2 * P) ** -0.5),
        "ssm_B_im": nrm((L, G, N, P), (2 * P) ** -0.5),
        "ssm_C_re": nrm((L, G, P, N), (2 * N) ** -0.5),
        "ssm_C_im": nrm((L, G, P, N), (2 * N) ** -0.5),
        "ssm_D": nrm((L, G, P), 1.0),
        "w_glu": nrm((L, MIX_SSM, MIX_SSM), MIX_SSM ** -0.5),
        "b_glu": nrm((L, MIX_SSM), 0.02),
        "conv_w": nrm((L, CONV_K, MIX_CONV), CONV_K ** -0.5),
        "norm_ssm_out": gain((L, MIX_SSM)),
        "norm_conv_out": gain((L, MIX_CONV)),
        "w_out": nrm((L, D, D), D ** -0.5),
        "norm_xattn": gain((L, D)),
        "norm_mem": gain((L, D)),
        "w_q": nrm((L, D, XATTN_HEADS * XATTN_HEAD_DIM), D ** -0.5),
        "w_k": nrm((L, D, XATTN_HEADS * XATTN_HEAD_DIM), D ** -0.5),
        "w_v": nrm((L, D, XATTN_HEADS * XATTN_HEAD_DIM), D ** -0.5),
        "w_xo": nrm((L, XATTN_HEADS * XATTN_HEAD_DIM, D), D ** -0.5),
        "norm_ffn": gain((L, D)),
        "w_up": nrm((L, D, D_FF), D ** -0.5),
        "w_gate": nrm((L, D, D_FF), D ** -0.5),
        "ffn_conv_w": nrm((L, FFN_CONV_K, D_FF), FFN_CONV_K ** -0.5),
        "w_down": nrm((L, D_FF, D), D_FF ** -0.5),
        "norm_final": gain((D,)),
    }


def reference(x_prompt, x_sample, mem_prompt, cache_mem_k, cache_mem_v, state_ssm_re, state_ssm_im,
              state_conv, state_ffn_conv, norm_mix, w_in, ssm_A_re, ssm_A_im, ssm_log_dt, ssm_B_re,
              ssm_B_im, ssm_C_re, ssm_C_im, ssm_D, w_glu, b_glu, conv_w, norm_ssm_out, norm_conv_out,
              w_out, norm_xattn, norm_mem, w_q, w_k, w_v, w_xo, norm_ffn, w_up, w_gate, ffn_conv_w,
              w_down, norm_final):
    yp, ys = x_prompt, x_sample
    mk_p, mv_p, sre_p, sim_p, cb_p, fb_p = [], [], [], [], [], []
    sre_s, sim_s, cb_s, fb_s = [], [], [], []
    for l in range(DEPTH):
        p = dict(norm_mix=norm_mix[l], w_in=w_in[l], ssm_A_re=ssm_A_re[l], ssm_A_im=ssm_A_im[l],
                 ssm_log_dt=ssm_log_dt[l], ssm_B_re=ssm_B_re[l], ssm_B_im=ssm_B_im[l],
                 ssm_C_re=ssm_C_re[l], ssm_C_im=ssm_C_im[l], ssm_D=ssm_D[l], w_glu=w_glu[l],
                 b_glu=b_glu[l], conv_w=conv_w[l], norm_ssm_out=norm_ssm_out[l],
                 norm_conv_out=norm_conv_out[l], w_out=w_out[l], norm_xattn=norm_xattn[l],
                 w_q=w_q[l], w_xo=w_xo[l], norm_ffn=norm_ffn[l], w_up=w_up[l], w_gate=w_gate[l],
                 ffn_conv_w=ffn_conv_w[l], w_down=w_down[l])
        mk, mv = memory_kv(mem_prompt, norm_mem[l], w_k[l], w_v[l])
        yp, (a0, a1, a2, a3) = decoder_layer(yp, mk, mv, None, p)
        mk_p.append(mk); mv_p.append(mv)
        sre_p.append(a0); sim_p.append(a1); cb_p.append(a2); fb_p.append(a3)
        ys, (c0, c1, c2, c3) = decoder_layer(
            ys, cache_mem_k[l], cache_mem_v[l],
            (state_ssm_re[l], state_ssm_im[l], state_conv[l], state_ffn_conv[l]), p)
        sre_s.append(c0); sim_s.append(c1); cb_s.append(c2); fb_s.append(c3)
    yp = rms_norm(yp, norm_final)
    ys = rms_norm(ys, norm_final)
    return (yp, ys,
            jnp.stack(mk_p), jnp.stack(mv_p), jnp.stack(sre_p), jnp.stack(sim_p),
            jnp.stack(cb_p), jnp.stack(fb_p),
            jnp.stack(sre_s), jnp.stack(sim_s), jnp.stack(cb_s), jnp.stack(fb_s))
```

```python
import functools

import jax
import jax.numpy as jnp
from jax import lax
from jax.experimental import pallas as pl
from jax.experimental.pallas import tpu as pltpu

F32 = jnp.float32
BF16 = jnp.bfloat16

EPS = 1e-6
SSM_GROUP_CH = 16
SSM_STATE = 64
CONV_K = 3
N_MEM = 256
XATTN_HEADS = 4
SSM_CHUNK = 16
V7X_VMEM_LIMIT = 56 * 1024 * 1024


def _rms(x, g):
    ms = jnp.mean(x * x, axis=-1, keepdims=True)
    return x * lax.rsqrt(ms + EPS) * g


def _dot(a, b):
    return jnp.dot(a, b, preferred_element_type=F32)


def _const_spec(shape):
    nd = len(shape)
    return pl.BlockSpec(shape, lambda *_: (0,) * nd, pipeline_mode=pl.Buffered(1))


def _params(*sem):
    return pltpu.CompilerParams(dimension_semantics=sem, vmem_limit_bytes=V7X_VMEM_LIMIT)


def _ssm_prep_kernel(ld_ref, lr_r_ref, li_r_ref, lr_c_ref, li_c_ref, brt_ref, bit_ref, brt_t_ref, bit_t_ref,
                     crt_t_ref, cit_t_ref, d_ref,
                     m_ref, sre_ref, sim_ref, ore_ref, oim_ref, apow_ref, *, gb, chunk, short):
    P, N = SSM_GROUP_CH, SSM_STATE
    LP = chunk * P
    lane_j = (lax.broadcasted_iota(jnp.int32, (1, LP), 1) // P).astype(F32)
    row_j = (lax.broadcasted_iota(jnp.int32, (LP, 1), 0) // P).astype(F32)
    lane_i = lax.broadcasted_iota(jnp.int32, (P, LP), 1)
    row_i = lax.broadcasted_iota(jnp.int32, (P, LP), 0)
    for gi in range(gb):
        dt = jnp.exp(ld_ref[gi])
        lr_r, li_r = lr_r_ref[gi], li_r_ref[gi]
        lr_c, li_c = lr_c_ref[gi], li_c_ref[gi]

        def apow(j, lr, li):
            mag = jnp.exp(j * (dt * lr))
            ph = j * (dt * li)
            return mag * jnp.cos(ph), mag * jnp.sin(ph)

        ar, ai = apow(1.0, lr_r, li_r)
        den = lr_r * lr_r + li_r * li_r
        cr = ((ar - 1.0) * lr_r + ai * li_r) / den
        ci = (ai * lr_r - (ar - 1.0) * li_r) / den

        brt, bit = brt_ref[gi], bit_ref[gi]
        bbt_re = cr * brt - ci * bit
        bbt_im = cr * bit + ci * brt
        brt_t, bit_t = brt_t_ref[gi], bit_t_ref[gi]
        bbt_re_t = cr * brt_t - ci * bit_t
        bbt_im_t = cr * bit_t + ci * brt_t

        er, ei = apow((chunk - 1.0) - row_j, lr_r, li_r)
        sre_ref[gi] = (er * bbt_re_t - ei * bbt_im_t).astype(sre_ref.dtype)
        sim_ref[gi] = (er * bbt_im_t + ei * bbt_re_t).astype(sim_ref.dtype)

        crt_t, cit_t = crt_t_ref[gi], cit_t_ref[gi]
        er, ei = apow(lane_j, lr_c, li_c)
        r_re = crt_t * er - cit_t * ei
        r_im = crt_t * ei + cit_t * er
        er, ei = apow(lane_j + 1.0, lr_c, li_c)
        ore_ref[gi] = (crt_t * er - cit_t * ei).astype(ore_ref.dtype)
        oim_ref[gi] = (-(crt_t * ei + cit_t * er)).astype(oim_ref.dtype)

        krow = (jnp.dot(bbt_re, r_re, preferred_element_type=F32, precision=lax.Precision.HIGHEST)
                - jnp.dot(bbt_im, r_im, preferred_element_type=F32, precision=lax.Precision.HIGHEST))
        krow = krow + jnp.where(lane_i == row_i, d_ref[gi], 0.0)
        for ti in range(chunk):
            blk = krow if ti == 0 else pltpu.roll(krow, ti * P, axis=1)
            blk = jnp.where(lane_i >= ti * P, blk, 0.0)
            m_ref[gi, ti * P:(ti + 1) * P, :] = blk.astype(m_ref.dtype)

        a16r, a16i = apow(float(chunk), lr_r, li_r)
        a4r, a4i = apow(float(short), lr_r, li_r)
        apow_ref[gi] = jnp.concatenate([a16r, a16i, a4r, a4i], axis=0)


def _ssm_prep(A_re, A_im, log_dt, B_re, B_im, C_re, C_im, D, *, chunk, short, gb=8):
    G, N = A_re.shape
    P = SSM_GROUP_CH
    LP = chunk * P
    brt = B_re.transpose(0, 2, 1)
    bit = B_im.transpose(0, 2, 1)
    crt_t = jnp.tile(C_re.transpose(0, 2, 1), (1, 1, chunk))
    cit_t = jnp.tile(C_im.transpose(0, 2, 1), (1, 1, chunk))
    ins = [log_dt.reshape(G, 1, 1), A_re.reshape(G, 1, N), A_im.reshape(G, 1, N),
           A_re.reshape(G, N, 1), A_im.reshape(G, N, 1), brt, bit,
           jnp.tile(brt, (1, chunk, 1)), jnp.tile(bit, (1, chunk, 1)), crt_t, cit_t, D.reshape(G, P, 1)]
    gspec = lambda *s: pl.BlockSpec((gb,) + s, lambda i: (i, 0, 0))
    in_specs = [gspec(*a.shape[1:]) for a in ins]
    out_shape = [jax.ShapeDtypeStruct((G, LP, LP), BF16),
                 jax.ShapeDtypeStruct((G, LP, N), BF16), jax.ShapeDtypeStruct((G, LP, N), BF16),
                 jax.ShapeDtypeStruct((G, N, LP), BF16), jax.ShapeDtypeStruct((G, N, LP), BF16),
                 jax.ShapeDtypeStruct((G, 4, N), F32)]
    out_specs = [gspec(*o.shape[1:]) for o in out_shape]
    return pl.pallas_call(
        functools.partial(_ssm_prep_kernel, gb=gb, chunk=chunk, short=short),
        grid=(G // gb,), in_specs=in_specs, out_specs=out_specs, out_shape=out_shape,
        compiler_params=_params("arbitrary"), name="ssm_prep")(*ins)


def _causal_conv3(v, w_ref, s_ref, *, tm, ts, hdr):
    s_ref[hdr:hdr + tm, :] = v
    xm1 = s_ref[hdr - ts:hdr - ts + tm, :]
    xm2 = s_ref[hdr - 2 * ts:hdr - 2 * ts + tm, :]
    return w_ref[0:1, :] * xm2 + w_ref[1:2, :] * xm1 + w_ref[2:3, :] * v


def _mix_in_kernel(*refs, tm, ts, hdr, has_prev, ms, mc):
    if has_prev:
        x_ref, g_ref, w_ref, cw_ref, gc_ref, prev_ref, u_ref, yc_ref, buf_ref, s_ref = refs
    else:
        x_ref, g_ref, w_ref, cw_ref, gc_ref, u_ref, yc_ref, buf_ref, s_ref = refs
    if has_prev:
        s_ref[hdr - 2 * ts:hdr, :] = prev_ref[...]
    else:
        @pl.when(pl.program_id(1) == 0)
        def _():
            s_ref[0:hdr, :] = jnp.zeros((hdr, mc), F32)

    hb = _rms(x_ref[...], g_ref[...]).astype(BF16)
    u_ref[...] = _dot(hb, w_ref[:, 0:ms]).astype(u_ref.dtype)
    xin = _dot(hb, w_ref[:, ms:ms + mc])
    cg = _dot(hb, w_ref[:, ms + 2 * mc:ms + 3 * mc])
    conv = _causal_conv3(cg * xin, cw_ref, s_ref, tm=tm, ts=ts, hdr=hdr)
    bg = _dot(hb, w_ref[:, ms + mc:ms + 2 * mc])
    yc_ref[...] = _rms(bg * conv, gc_ref[...]).astype(yc_ref.dtype)
    tail = s_ref[hdr + tm - 2 * ts:hdr + tm, :]
    buf_ref[...] = tail.reshape(buf_ref.shape)
    if not has_prev:
        s_ref[0:hdr, :] = s_ref[tm:tm + hdr, :]


def _mix_in(x2d, g, w_bf, conv_w, g_conv, prev, *, nb, nt, tm, ts, hdr, ms, mc):
    rows, d = x2d.shape
    has_prev = prev is not None
    row_spec = lambda c: pl.BlockSpec((tm, c), lambda b, t: (b * nt + t, 0))
    in_specs = [row_spec(d), _const_spec((1, d)), _const_spec(w_bf.shape), _const_spec(conv_w.shape),
                _const_spec((1, mc))]
    ins = [x2d, g.reshape(1, d), w_bf, conv_w, g_conv.reshape(1, mc)]
    if has_prev:
        in_specs.append(_const_spec(prev.shape))
        ins.append(prev)
        buf_shape = jax.ShapeDtypeStruct((2 * ts, mc), F32)
        buf_spec = pl.BlockSpec((2 * ts, mc), lambda b, t: (0, 0))
    else:
        buf_shape = jax.ShapeDtypeStruct((nb, 2 * ts, mc), F32)
        buf_spec = pl.BlockSpec((None, 2 * ts, mc), lambda b, t: (b, 0, 0))
    return pl.pallas_call(
        functools.partial(_mix_in_kernel, tm=tm, ts=ts, hdr=hdr, has_prev=has_prev, ms=ms, mc=mc),
        grid=(nb, nt), in_specs=in_specs,
        out_specs=[row_spec(ms), row_spec(mc), buf_spec],
        out_shape=[jax.ShapeDtypeStruct((rows, ms), BF16), jax.ShapeDtypeStruct((rows, mc), BF16), buf_shape],
        scratch_shapes=[pltpu.VMEM((hdr + tm, mc), F32)],
        compiler_params=_params("parallel", "arbitrary"), name="mix_in")(*ins)


def _ssm_kernel(*refs, gb, nb, nchunk, has_s0):
    if has_s0:
        (u_ref, m_ref, sre_ref, sim_ref, ore_ref, oim_ref, ar_ref, ai_ref, s0r_ref, s0i_ref,
         y_ref, fr_ref, fi_ref, lre, lim, ire, iim) = refs
    else:
        (u_ref, m_ref, sre_ref, sim_ref, ore_ref, oim_ref, ar_ref, ai_ref,
         y_ref, fr_ref, fi_ref, lre, lim, ire, iim) = refs
    n = SSM_STATE
    for gi in range(gb):
        u = u_ref[gi]
        lre[gi] = _dot(u, sre_ref[gi])
        lim[gi] = _dot(u, sim_ref[gi])
    ar = [jnp.broadcast_to(ar_ref[gi], (nb, n)) for gi in range(gb)]
    ai = [jnp.broadcast_to(ai_ref[gi], (nb, n)) for gi in range(gb)]
    if has_s0:
        st0 = tuple((s0r_ref[gi], s0i_ref[gi]) for gi in range(gb))
    else:
        st0 = tuple((jnp.zeros((nb, n), F32), jnp.zeros((nb, n), F32)) for _ in range(gb))

    def step(c, st):
        r = c * nb if isinstance(c, int) else pl.multiple_of(c * nb, nb)
        new = []
        for gi in range(gb):
            sr, si = st[gi]
            ire[gi, pl.ds(r, nb), :] = sr
            iim[gi, pl.ds(r, nb), :] = si
            nr = ar[gi] * sr - ai[gi] * si + lre[gi, pl.ds(r, nb), :]
            ni = ar[gi] * si + ai[gi] * sr + lim[gi, pl.ds(r, nb), :]
            new.append((nr, ni))
        return tuple(new)

    if nchunk == 1:
        st = step(0, st0)
    else:
        st = lax.fori_loop(0, nchunk, step, st0, unroll=4)
    for gi in range(gb):
        fr_ref[gi] = st[gi][0]
        fi_ref[gi] = st[gi][1]
        u = u_ref[gi]
        y = (_dot(u, m_ref[gi]) + _dot(ire[gi].astype(BF16), ore_ref[gi])
             + _dot(iim[gi].astype(BF16), oim_ref[gi]))
        y_ref[gi] = y.astype(y_ref.dtype)


def _ssm(u_g, mats, a_re, a_im, s0, *, nb, nchunk, gb=4):
    G, R, LP = u_g.shape
    n = SSM_STATE
    m, sre, sim, ore, oim = mats
    has_s0 = s0 is not None
    gspec = lambda *s: pl.BlockSpec((gb,) + s, lambda i: (i, 0, 0))
    ins = [u_g, m, sre, sim, ore, oim, a_re, a_im]
    if has_s0:
        ins += [s0[0], s0[1]]
    in_specs = [gspec(*a.shape[1:]) for a in ins]
    out_shape = [jax.ShapeDtypeStruct((G, R, LP), F32), jax.ShapeDtypeStruct((G, nb, n), F32),
                 jax.ShapeDtypeStruct((G, nb, n), F32)]
    return pl.pallas_call(
        functools.partial(_ssm_kernel, gb=gb, nb=nb, nchunk=nchunk, has_s0=has_s0),
        grid=(G // gb,), in_specs=in_specs, out_specs=[gspec(*o.shape[1:]) for o in out_shape],
        out_shape=out_shape, scratch_shapes=[pltpu.VMEM((gb, R, n), F32)] * 4,
        compiler_params=_params("arbitrary"), name="ssm_chunk")(*ins)


def _mix_out_kernel(x_ref, ys_ref, yc_ref, wg_ref, bg_ref, gs_ref, wo_ref, gx_ref, wq_ref, x1_ref, q_ref,
                    *, ms, q_scale):
    y = jax.nn.gelu(ys_ref[...].astype(F32))
    y = y * jax.nn.sigmoid(_dot(y.astype(BF16), wg_ref[...]) + bg_ref[...])
    ysn = _rms(y, gs_ref[...]).astype(BF16)
    x1 = x_ref[...] + _dot(ysn, wo_ref[0:ms, :]) + _dot(yc_ref[...], wo_ref[ms:, :])
    x1_ref[...] = x1
    h = _rms(x1, gx_ref[...]).astype(BF16)
    q_ref[...] = (_dot(h, wq_ref[...]) * q_scale).astype(q_ref.dtype)


def _mix_out(x2d, ys, yc, w_glu, b_glu, g_ssm, w_out, g_x, w_q, *, tm, q_dtype):
    rows, d = x2d.shape
    ms = ys.shape[1]
    hd = d // XATTN_HEADS
    row_spec = lambda c: pl.BlockSpec((tm, c), lambda i: (i, 0))
    in_specs = [row_spec(d), row_spec(ms), row_spec(yc.shape[1]), _const_spec(w_glu.shape), _const_spec((1, ms)),
                _const_spec((1, ms)), _const_spec(w_out.shape), _const_spec((1, d)), _const_spec(w_q.shape)]
    return pl.pallas_call(
        functools.partial(_mix_out_kernel, ms=ms, q_scale=hd ** -0.5),
        grid=(rows // tm,), in_specs=in_specs, out_specs=[row_spec(d), row_spec(d)],
        out_shape=[jax.ShapeDtypeStruct((rows, d), F32), jax.ShapeDtypeStruct((rows, d), q_dtype)],
        compiler_params=_params("parallel"), name="mix_out")(
            x2d, ys, yc, w_glu, b_glu.reshape(1, ms), g_ssm.reshape(1, ms), w_out, g_x.reshape(1, d), w_q)


def _kv_kernel(m_ref, g_ref, wk_ref, wv_ref, k_ref, v_ref):
    h = _rms(m_ref[...], g_ref[...]).astype(BF16)
    k_ref[...] = _dot(h, wk_ref[...])
    v_ref[...] = _dot(h, wv_ref[...])


def _kv_proj(mem2d, g, w_k, w_v, *, tm):
    rows, d = mem2d.shape
    row_spec = pl.BlockSpec((tm, d), lambda i: (i, 0))
    return pl.pallas_call(
        _kv_kernel, grid=(rows // tm,),
        in_specs=[row_spec, _const_spec((1, d)), _const_spec(w_k.shape), _const_spec(w_v.shape)],
        out_specs=[row_spec, row_spec],
        out_shape=[jax.ShapeDtypeStruct((rows, d), F32)] * 2,
        compiler_params=_params("parallel"), name="kv_proj")(mem2d, g.reshape(1, d), w_k, w_v)


def _attn_kernel(q_ref, k_ref, v_ref, o_ref, *, sb, tq, hd):
    for j in range(sb):
        q = q_ref[j * tq:(j + 1) * tq, :].astype(BF16)
        for h in range(XATTN_HEADS):
            cols = slice(h * hd, (h + 1) * hd)
            kh = k_ref[j * N_MEM:(j + 1) * N_MEM, cols].astype(BF16)
            vh = v_ref[j * N_MEM:(j + 1) * N_MEM, cols].astype(BF16)
            s = lax.dot_general(q[:, cols], kh, (((1,), (1,)), ((), ())), preferred_element_type=F32)
            p = jnp.exp(s - jnp.max(s, axis=-1, keepdims=True))
            p = p / jnp.sum(p, axis=-1, keepdims=True)
            o_ref[j * tq:(j + 1) * tq, cols] = _dot(p.astype(BF16), vh).astype(o_ref.dtype)


def _attn(q2d, k2d, v2d, *, nseq, sb, tq, nq, o_dtype):
    rows, d = q2d.shape
    hd = d // XATTN_HEADS
    q_spec = pl.BlockSpec((sb * tq, d), lambda b, t: (b * nq + t, 0))
    kv_spec = pl.BlockSpec((sb * N_MEM, d), lambda b, t: (b, 0))
    return pl.pallas_call(
        functools.partial(_attn_kernel, sb=sb, tq=tq, hd=hd),
        grid=(nseq // sb, nq), in_specs=[q_spec, kv_spec, kv_spec], out_specs=q_spec,
        out_shape=jax.ShapeDtypeStruct((rows, d), o_dtype),
        compiler_params=_params("parallel", "arbitrary"), name="xattn")(q2d, k2d, v2d)


def _ffn_kernel(*refs, tm, ts, hdr, has_prev, dff):
    if has_prev:
        (x_ref, o_ref, wxo_ref, gf_ref, wup_ref, wgate_ref, cw_ref, wdn_ref, gl_ref, prev_ref,
         y_ref, buf_ref, s_ref) = refs
    else:
        (x_ref, o_ref, wxo_ref, gf_ref, wup_ref, wgate_ref, cw_ref, wdn_ref, gl_ref,
         y_ref, buf_ref, s_ref) = refs
    if has_prev:
        s_ref[hdr - 2 * ts:hdr, :] = prev_ref[...]
    else:
        @pl.when(pl.program_id(1) == 0)
        def _():
            s_ref[0:hdr, :] = jnp.zeros((hdr, dff), F32)

    x2 = x_ref[...] + _dot(o_ref[...].astype(BF16), wxo_ref[...])
    hb = _rms(x2, gf_ref[...]).astype(BF16)
    a = _causal_conv3(_dot(hb, wup_ref[...]), cw_ref, s_ref, tm=tm, ts=ts, hdr=hdr)
    act = (jax.nn.gelu(a) * _dot(hb, wgate_ref[...])).astype(BF16)
    x3 = x2 + _dot(act, wdn_ref[...])
    y_ref[...] = _rms(x3, gl_ref[...])
    tail = s_ref[hdr + tm - 2 * ts:hdr + tm, :]
    buf_ref[...] = tail.reshape(buf_ref.shape)
    if not has_prev:
        s_ref[0:hdr, :] = s_ref[tm:tm + hdr, :]


def _ffn(x2d, o2d, w_xo, g_ffn, w_up, w_gate, conv_w, w_down, g_last, prev, *, nb, nt, tm, ts, hdr):
    rows, d = x2d.shape
    dff = w_up.shape[1]
    has_prev = prev is not None
    row_spec = pl.BlockSpec((tm, d), lambda b, t: (b * nt + t, 0))
    in_specs = [row_spec, row_spec, _const_spec(w_xo.shape), _const_spec((1, d)), _const_spec(w_up.shape),
                _const_spec(w_gate.shape), _const_spec(conv_w.shape), _const_spec(w_down.shape),
                _const_spec((1, d))]
    ins = [x2d, o2d, w_xo, g_ffn.reshape(1, d), w_up, w_gate, conv_w, w_down, g_last.reshape(1, d)]
    if has_prev:
        in_specs.append(_const_spec(prev.shape))
        ins.append(prev)
        buf_shape = jax.ShapeDtypeStruct((2 * ts, dff), F32)
        buf_spec = pl.BlockSpec((2 * ts, dff), lambda b, t: (0, 0))
    else:
        buf_shape = jax.ShapeDtypeStruct((nb, 2 * ts, dff), F32)
        buf_spec = pl.BlockSpec((None, 2 * ts, dff), lambda b, t: (b, 0, 0))
    return pl.pallas_call(
        functools.partial(_ffn_kernel, tm=tm, ts=ts, hdr=hdr, has_prev=has_prev, dff=dff),
        grid=(nb, nt), in_specs=in_specs, out_specs=[row_spec, buf_spec],
        out_shape=[jax.ShapeDtypeStruct((rows, d), F32), buf_shape],
        scratch_shapes=[pltpu.VMEM((hdr + tm, dff), F32)],
        compiler_params=_params("parallel", "arbitrary"), name="conv_ffn")(*ins)


def _layer(x2d, k2d, v2d, prev, p, mats, g_last, *, nseq, tlen, time_major):
    rows, d = x2d.shape
    G, P, N = mats["G"], SSM_GROUP_CH, SSM_STATE
    ms = G * P
    mc = p["conv_w"].shape[1]
    dff = p["w_up"].shape[1]
    if time_major:
        chunk, nchunk = tlen, 1
        geom = dict(nb=1, nt=1, tm=rows, ts=nseq, hdr=2 * nseq)
        conv_prev = prev[2].transpose(1, 0, 2).reshape(2 * nseq, mc)
        ffn_prev = prev[3].transpose(1, 0, 2).reshape(2 * nseq, dff)
        s0 = (prev[0].transpose(1, 0, 2), prev[1].transpose(1, 0, 2))
        ssm_mats, a_re, a_im = mats["short"], mats["a_short"][0], mats["a_short"][1]
    else:
        chunk, nchunk = SSM_CHUNK, tlen // SSM_CHUNK
        tm = 512
        geom = dict(nb=nseq, nt=tlen // tm, tm=tm, ts=1, hdr=8)
        conv_prev = ffn_prev = s0 = None
        ssm_mats, a_re, a_im = mats["long"], mats["a_long"][0], mats["a_long"][1]

    u, ycn, conv_buf = _mix_in(x2d, p["norm_mix"], p["w_in"], p["conv_w"], p["norm_conv_out"], conv_prev,
                               ms=ms, mc=mc, **geom)

    if time_major:
        u_g = u.reshape(chunk, nseq, G, P).transpose(2, 1, 0, 3).reshape(G, nseq, chunk * P)
    else:
        u_g = u.reshape(nseq, nchunk, chunk, G, P).transpose(3, 1, 0, 2, 4).reshape(G, nchunk * nseq, chunk * P)
    y_g, f_re, f_im = _ssm(u_g, ssm_mats, a_re, a_im, s0, nb=nseq, nchunk=nchunk)
    if time_major:
        y_ssm = y_g.reshape(G, nseq, chunk, P).transpose(2, 1, 0, 3).reshape(rows, ms)
    else:
        y_ssm = y_g.reshape(G, nchunk, nseq, chunk, P).transpose(2, 1, 3, 0, 4).reshape(rows, ms)

    x1, q = _mix_out(x2d, y_ssm, ycn, p["w_glu"], p["b_glu"], p["norm_ssm_out"], p["w_out"], p["norm_xattn"],
                     p["w_q"], tm=512, q_dtype=F32 if time_major else BF16)
    if time_major:
        q_sm = q.reshape(tlen, nseq, d).transpose(1, 0, 2).reshape(rows, d)
        o_sm = _attn(q_sm, k2d, v2d, nseq=nseq, sb=4, tq=tlen, nq=1, o_dtype=F32)
        o = o_sm.reshape(nseq, tlen, d).transpose(1, 0, 2).reshape(rows, d)
    else:
        o = _attn(q, k2d, v2d, nseq=nseq, sb=1, tq=512, nq=tlen // 512, o_dtype=BF16)

    if not time_major:
        geom = dict(nb=nseq, nt=tlen // 256, tm=256, ts=1, hdr=8)
    y, ffn_buf = _ffn(x1, o, p["w_xo"], p["norm_ffn"], p["w_up"], p["w_gate"], p["ffn_conv_w"], p["w_down"],
                      g_last, ffn_prev, **geom)

    f_re, f_im = f_re.transpose(1, 0, 2), f_im.transpose(1, 0, 2)
    if time_major:
        conv_buf = conv_buf.reshape(2, nseq, mc).transpose(1, 0, 2)
        ffn_buf = ffn_buf.reshape(2, nseq, dff).transpose(1, 0, 2)
    return y, (f_re, f_im, conv_buf, ffn_buf)


def kernel(x_prompt, x_sample, mem_prompt, cache_mem_k, cache_mem_v, state_ssm_re, state_ssm_im, state_conv, state_ffn_conv, norm_mix, w_in, ssm_A_re, ssm_A_im, ssm_log_dt, ssm_B_re, ssm_B_im, ssm_C_re, ssm_C_im, ssm_D, w_glu, b_glu, conv_w, norm_ssm_out, norm_conv_out, w_out, norm_xattn, norm_mem, w_q, w_k, w_v, w_xo, norm_ffn, w_up, w_gate, ffn_conv_w, w_down, norm_final):
    depth = w_in.shape[0]
    assert depth == 1, "the final norm is fused into the last (only) layer's ConvFFN kernel"
    nbp, tp, d = x_prompt.shape
    nbs, tsm, _ = x_sample.shape
    G = ssm_A_re.shape[1]
    P = SSM_GROUP_CH
    assert tp % 512 == 0 and tp % SSM_CHUNK == 0 and tsm <= SSM_CHUNK

    xp = x_prompt.reshape(nbp * tp, d)
    xs = x_sample.transpose(1, 0, 2).reshape(tsm * nbs, d)
    outs_p, outs_s, mk_p, mv_p = [], [], [], []
    for l in range(depth):
        p = dict(norm_mix=norm_mix[l], w_in=w_in[l].astype(BF16), w_glu=w_glu[l].astype(BF16), b_glu=b_glu[l],
                 conv_w=conv_w[l], norm_ssm_out=norm_ssm_out[l], norm_conv_out=norm_conv_out[l],
                 w_out=w_out[l].astype(BF16), norm_xattn=norm_xattn[l], w_q=w_q[l].astype(BF16),
                 w_xo=w_xo[l].astype(BF16), norm_ffn=norm_ffn[l], w_up=w_up[l].astype(BF16),
                 w_gate=w_gate[l].astype(BF16), ffn_conv_w=ffn_conv_w[l], w_down=w_down[l].astype(BF16))
        m, sre, sim, ore, oim, apow = _ssm_prep(ssm_A_re[l], ssm_A_im[l], ssm_log_dt[l], ssm_B_re[l], ssm_B_im[l],
                                                ssm_C_re[l], ssm_C_im[l], ssm_D[l], chunk=SSM_CHUNK, short=tsm)
        lp_s = tsm * P
        lp_l = SSM_CHUNK * P
        mats = dict(
            G=G,
            long=(m, sre, sim, ore, oim),
            short=(m[:, :lp_s, :lp_s], sre[:, lp_l - lp_s:, :], sim[:, lp_l - lp_s:, :],
                   ore[:, :, :lp_s], oim[:, :, :lp_s]),
            a_long=(apow[:, 0:1, :], apow[:, 1:2, :]), a_short=(apow[:, 2:3, :], apow[:, 3:4, :]))

        k2d, v2d = _kv_proj(mem_prompt.reshape(nbp * N_MEM, d), norm_mem[l], w_k[l].astype(BF16),
                            w_v[l].astype(BF16), tm=512)
        mk_p.append(k2d.reshape(nbp, N_MEM, XATTN_HEADS, d // XATTN_HEADS))
        mv_p.append(v2d.reshape(nbp, N_MEM, XATTN_HEADS, d // XATTN_HEADS))

        xp, st_p = _layer(xp, k2d, v2d, None, p, mats, norm_final, nseq=nbp, tlen=tp, time_major=False)
        outs_p.append(st_p)
        xs, st_s = _layer(xs, cache_mem_k[l].reshape(nbs * N_MEM, d), cache_mem_v[l].reshape(nbs * N_MEM, d),
                          (state_ssm_re[l], state_ssm_im[l], state_conv[l], state_ffn_conv[l]), p, mats,
                          norm_final, nseq=nbs, tlen=tsm, time_major=True)
        outs_s.append(st_s)

    yp = xp.reshape(nbp, tp, d)
    ys = xs.reshape(tsm, nbs, d).transpose(1, 0, 2)
    stack = lambda outs, i: jnp.stack([o[i] for o in outs])
    return (yp, ys, jnp.stack(mk_p), jnp.stack(mv_p),
            stack(outs_p, 0), stack(outs_p, 1), stack(outs_p, 2), stack(outs_p, 3),
            stack(outs_s, 0), stack(outs_s, 1), stack(outs_s, 2), stack(outs_s, 3))
```

```python
import functools

import jax
import jax.numpy as jnp
from jax import lax
from jax.experimental import pallas as pl
from jax.experimental.pallas import tpu as pltpu

F32 = jnp.float32
BF16 = jnp.bfloat16

EPS = 1e-6
SSM_GROUP_CH = 16
SSM_STATE = 64
CONV_K = 3
N_MEM = 256
XATTN_HEADS = 4
SSM_CHUNK = 16
LANES = 128
V7X_VMEM_LIMIT = 56 * 1024 * 1024


def _rms(x, g):
    ms = jnp.mean(x * x, axis=-1, keepdims=True)
    return x * lax.rsqrt(ms + EPS) * g


def _dot(a, b):
    return jnp.dot(a, b, preferred_element_type=F32)


def _const_spec(shape):
    nd = len(shape)
    return pl.BlockSpec(shape, lambda *_: (0,) * nd, pipeline_mode=pl.Buffered(1))


def _params(*sem):
    return pltpu.CompilerParams(dimension_semantics=sem, vmem_limit_bytes=V7X_VMEM_LIMIT)


def _ssm_prep_kernel(ld_ref, lr_r_ref, li_r_ref, lr_c_ref, li_c_ref, brt_ref, bit_ref, brt_t_ref, bit_t_ref,
                     crt_t_ref, cit_t_ref, d_ref,
                     m_ref, sre_ref, sim_ref, ore_ref, oim_ref, apow_ref, *, gb, chunk, short):
    P, N = SSM_GROUP_CH, SSM_STATE
    LP = chunk * P
    lane_j = (lax.broadcasted_iota(jnp.int32, (1, LP), 1) // P).astype(F32)
    row_j = (lax.broadcasted_iota(jnp.int32, (LP, 1), 0) // P).astype(F32)
    lane_i = lax.broadcasted_iota(jnp.int32, (P, LP), 1)
    row_i = lax.broadcasted_iota(jnp.int32, (P, LP), 0)
    for gi in range(gb):
        dt = jnp.exp(ld_ref[gi])
        lr_r, li_r = lr_r_ref[gi], li_r_ref[gi]
        lr_c, li_c = lr_c_ref[gi], li_c_ref[gi]

        def apow(j, lr, li):
            mag = jnp.exp(j * (dt * lr))
            ph = j * (dt * li)
            return mag * jnp.cos(ph), mag * jnp.sin(ph)

        ar, ai = apow(1.0, lr_r, li_r)
        den = lr_r * lr_r + li_r * li_r
        cr = ((ar - 1.0) * lr_r + ai * li_r) / den
        ci = (ai * lr_r - (ar - 1.0) * li_r) / den

        brt, bit = brt_ref[gi], bit_ref[gi]
        bbt_re = cr * brt - ci * bit
        bbt_im = cr * bit + ci * brt
        brt_t, bit_t = brt_t_ref[gi], bit_t_ref[gi]
        bbt_re_t = cr * brt_t - ci * bit_t
        bbt_im_t = cr * bit_t + ci * brt_t

        er, ei = apow((chunk - 1.0) - row_j, lr_r, li_r)
        sre_ref[gi] = (er * bbt_re_t - ei * bbt_im_t).astype(sre_ref.dtype)
        sim_ref[gi] = (er * bbt_im_t + ei * bbt_re_t).astype(sim_ref.dtype)

        crt_t, cit_t = crt_t_ref[gi], cit_t_ref[gi]
        er, ei = apow(lane_j, lr_c, li_c)
        r_re = crt_t * er - cit_t * ei
        r_im = crt_t * ei + cit_t * er
        er, ei = apow(lane_j + 1.0, lr_c, li_c)
        ore_ref[gi] = (crt_t * er - cit_t * ei).astype(ore_ref.dtype)
        oim_ref[gi] = (-(crt_t * ei + cit_t * er)).astype(oim_ref.dtype)

        krow = (jnp.dot(bbt_re, r_re, preferred_element_type=F32, precision=lax.Precision.HIGHEST)
                - jnp.dot(bbt_im, r_im, preferred_element_type=F32, precision=lax.Precision.HIGHEST))
        krow = krow + jnp.where(lane_i == row_i, d_ref[gi], 0.0)
        for ti in range(chunk):
            blk = krow if ti == 0 else pltpu.roll(krow, ti * P, axis=1)
            blk = jnp.where(lane_i >= ti * P, blk, 0.0)
            m_ref[gi, ti * P:(ti + 1) * P, :] = blk.astype(m_ref.dtype)

        a16r, a16i = apow(float(chunk), lr_r, li_r)
        a4r, a4i = apow(float(short), lr_r, li_r)
        apow_ref[gi] = jnp.concatenate([a16r, a16i, a4r, a4i], axis=0)


def _ssm_prep(A_re, A_im, log_dt, B_re, B_im, C_re, C_im, D, *, chunk, short, gb=8):
    G, N = A_re.shape
    P = SSM_GROUP_CH
    LP = chunk * P
    brt = B_re.transpose(0, 2, 1)
    bit = B_im.transpose(0, 2, 1)
    crt_t = jnp.tile(C_re.transpose(0, 2, 1), (1, 1, chunk))
    cit_t = jnp.tile(C_im.transpose(0, 2, 1), (1, 1, chunk))
    ins = [log_dt.reshape(G, 1, 1), A_re.reshape(G, 1, N), A_im.reshape(G, 1, N),
           A_re.reshape(G, N, 1), A_im.reshape(G, N, 1), brt, bit,
           jnp.tile(brt, (1, chunk, 1)), jnp.tile(bit, (1, chunk, 1)), crt_t, cit_t, D.reshape(G, P, 1)]
    gspec = lambda *s: pl.BlockSpec((gb,) + s, lambda i: (i, 0, 0))
    in_specs = [gspec(*a.shape[1:]) for a in ins]
    out_shape = [jax.ShapeDtypeStruct((G, LP, LP), BF16),
                 jax.ShapeDtypeStruct((G, LP, N), BF16), jax.ShapeDtypeStruct((G, LP, N), BF16),
                 jax.ShapeDtypeStruct((G, N, LP), BF16), jax.ShapeDtypeStruct((G, N, LP), BF16),
                 jax.ShapeDtypeStruct((G, 4, N), F32)]
    out_specs = [gspec(*o.shape[1:]) for o in out_shape]
    return pl.pallas_call(
        functools.partial(_ssm_prep_kernel, gb=gb, chunk=chunk, short=short),
        grid=(G // gb,), in_specs=in_specs, out_specs=out_specs, out_shape=out_shape,
        compiler_params=_params("arbitrary"), name="ssm_prep")(*ins)


def _causal_conv3(v, w_ref, s_ref, *, tm, ts, hdr):
    s_ref[hdr:hdr + tm, :] = v
    xm1 = s_ref[hdr - ts:hdr - ts + tm, :]
    xm2 = s_ref[hdr - 2 * ts:hdr - 2 * ts + tm, :]
    return w_ref[0:1, :] * xm2 + w_ref[1:2, :] * xm1 + w_ref[2:3, :] * v


def _segment_conv3(v, w_ref, s_ref, *, nseg, seg, ts, hdr):
    out = [_causal_conv3(v[b * seg:(b + 1) * seg], w_ref, s_ref.at[b], tm=seg, ts=ts, hdr=hdr)
           for b in range(nseg)]
    return out[0] if nseg == 1 else jnp.concatenate(out, axis=0)


def _conv_state_begin(s_ref, prev_ref, *, ts, hdr):
    if prev_ref is not None:
        s_ref[:, hdr - 2 * ts:hdr, :] = prev_ref[...]
    else:
        @pl.when(pl.program_id(0) == 0)
        def _():
            s_ref[:, 0:hdr, :] = jnp.zeros((s_ref.shape[0], hdr, s_ref.shape[2]), F32)


def _conv_state_end(s_ref, buf_ref, *, seg, ts, hdr, carry):
    buf_ref[...] = s_ref[:, hdr + seg - 2 * ts:hdr + seg, :]
    if carry:
        s_ref[:, 0:hdr, :] = s_ref[:, seg:seg + hdr, :]


def _piece_transpose8(v):
    width = LANES // 8
    piece = lax.broadcasted_iota(jnp.int32, v[0].shape, 1) // width
    for bit in range(3):
        s = 1 << bit
        hi = ((piece >> bit) & 1) == 1
        nv = list(v)
        for i in range(8):
            if i & s:
                continue
            a, b = v[i], v[i + s]
            nv[i] = jnp.where(hi, pltpu.roll(b, s * width, axis=1), a)
            nv[i + s] = jnp.where(hi, b, pltpu.roll(a, LANES - s * width, axis=1))
        v = nv
    return v


def _tokens_to_groups(zu_ref, u_ref, *, nseg, seg):
    nc = seg // SSM_CHUNK
    for h in range(SSM_CHUNK // 8):
        for q in range(zu_ref.shape[0]):
            xs = [jnp.concatenate([zu_ref[q, pl.ds(SSM_CHUNK * c + 8 * h + i, nseg, stride=seg), :]
                                   for c in range(nc)], axis=0) for i in range(8)]
            w = _piece_transpose8(xs)
            for k in range(8):
                u_ref[8 * q + k, :, h * LANES:(h + 1) * LANES] = w[k].astype(u_ref.dtype)


def _groups_to_tokens(y_ref, ys_ref, *, nseg, seg):
    nc = seg // SSM_CHUNK
    for h in range(SSM_CHUNK // 8):
        for q in range(ys_ref.shape[0]):
            v = _piece_transpose8([y_ref[8 * q + k, :, h * LANES:(h + 1) * LANES].astype(F32) for k in range(8)])
            for i in range(8):
                for c in range(nc):
                    ys_ref[q, pl.ds(SSM_CHUNK * c + 8 * h + i, nseg, stride=seg), :] = v[i][c * nseg:(c + 1) * nseg]


def _mix_in_kernel(*refs, nseg, seg, ts, hdr, has_prev, grouped, ms, mc):
    x_ref, g_ref, w_ref, cw_ref, gc_ref = refs[:5]
    refs = refs[5:]
    prev_ref = None
    if has_prev:
        prev_ref, refs = refs[0], refs[1:]
    u_ref, yc_ref, buf_ref, s_ref = refs[:4]
    tm = nseg * seg
    _conv_state_begin(s_ref, prev_ref, ts=ts, hdr=hdr)

    hb = _rms(x_ref[...].reshape(tm, x_ref.shape[-1]), g_ref[...]).astype(BF16)
    zu = _dot(hb, w_ref[:, 0:ms])
    if grouped:
        zu_ref = refs[4]
        for q in range(ms // LANES):
            zu_ref[q] = zu[:, q * LANES:(q + 1) * LANES]
        _tokens_to_groups(zu_ref, u_ref, nseg=nseg, seg=seg)
    else:
        u_ref[...] = zu.astype(u_ref.dtype).reshape(u_ref.shape)
    xin = _dot(hb, w_ref[:, ms:ms + mc])
    cg = _dot(hb, w_ref[:, ms + 2 * mc:ms + 3 * mc])
    conv = _segment_conv3(cg * xin, cw_ref, s_ref, nseg=nseg, seg=seg, ts=ts, hdr=hdr)
    bg = _dot(hb, w_ref[:, ms + mc:ms + 2 * mc])
    yc_ref[...] = _rms(bg * conv, gc_ref[...]).astype(yc_ref.dtype).reshape(yc_ref.shape)
    _conv_state_end(s_ref, buf_ref, seg=seg, ts=ts, hdr=hdr, carry=not has_prev)


def _mix_in(x3d, g, w_bf, conv_w, g_conv, prev, *, seg, ts, hdr, grouped, ms, mc):
    nseg, tlen, d = x3d.shape
    has_prev = prev is not None
    G = ms // SSM_GROUP_CH
    row_spec = lambda c: pl.BlockSpec((nseg, seg, c), lambda t: (0, t, 0))
    in_specs = [row_spec(d), _const_spec((1, d)), _const_spec(w_bf.shape), _const_spec(conv_w.shape),
                _const_spec((1, mc))]
    ins = [x3d, g.reshape(1, d), w_bf, conv_w, g_conv.reshape(1, mc)]
    if has_prev:
        in_specs.append(_const_spec(prev.shape))
        ins.append(prev)
    scratch = [pltpu.VMEM((nseg, hdr + seg, mc), F32)]
    if grouped:
        lp = SSM_CHUNK * SSM_GROUP_CH
        rg = seg // SSM_CHUNK * nseg
        u_shape = jax.ShapeDtypeStruct((G, tlen // SSM_CHUNK * nseg, lp), BF16)
        u_spec = pl.BlockSpec((G, rg, lp), lambda t: (0, t, 0))
        scratch.append(pltpu.VMEM((ms // LANES, nseg * seg, LANES), F32))
    else:
        u_shape = jax.ShapeDtypeStruct((nseg, tlen, ms), BF16)
        u_spec = row_spec(ms)
    return pl.pallas_call(
        functools.partial(_mix_in_kernel, nseg=nseg, seg=seg, ts=ts, hdr=hdr, has_prev=has_prev, grouped=grouped,
                          ms=ms, mc=mc),
        grid=(tlen // seg,), in_specs=in_specs,
        out_specs=[u_spec, row_spec(mc), pl.BlockSpec((nseg, 2 * ts, mc), lambda t: (0, 0, 0))],
        out_shape=[u_shape, jax.ShapeDtypeStruct((nseg, tlen, mc), BF16),
                   jax.ShapeDtypeStruct((nseg, 2 * ts, mc), F32)],
        scratch_shapes=scratch,
        compiler_params=_params("arbitrary"), name="mix_in")(*ins)


def _ssm_kernel(*refs, gb, nb, nchunk, has_s0):
    if has_s0:
        (u_ref, m_ref, sre_ref, sim_ref, ore_ref, oim_ref, ar_ref, ai_ref, s0r_ref, s0i_ref,
         y_ref, fr_ref, fi_ref, lre, lim, ire, iim) = refs
    else:
        (u_ref, m_ref, sre_ref, sim_ref, ore_ref, oim_ref, ar_ref, ai_ref,
         y_ref, fr_ref, fi_ref, lre, lim, ire, iim) = refs
    n = SSM_STATE
    for gi in range(gb):
        u = u_ref[gi]
        lre[gi] = _dot(u, sre_ref[gi])
        lim[gi] = _dot(u, sim_ref[gi])
    ar = [jnp.broadcast_to(ar_ref[gi], (nb, n)) for gi in range(gb)]
    ai = [jnp.broadcast_to(ai_ref[gi], (nb, n)) for gi in range(gb)]
    if has_s0:
        st0 = tuple((s0r_ref[gi], s0i_ref[gi]) for gi in range(gb))
    else:
        st0 = tuple((jnp.zeros((nb, n), F32), jnp.zeros((nb, n), F32)) for _ in range(gb))

    def step(c, st):
        r = c * nb if isinstance(c, int) else pl.multiple_of(c * nb, nb)
        new = []
        for gi in range(gb):
            sr, si = st[gi]
            ire[gi, pl.ds(r, nb), :] = sr
            iim[gi, pl.ds(r, nb), :] = si
            nr = ar[gi] * sr - ai[gi] * si + lre[gi, pl.ds(r, nb), :]
            ni = ar[gi] * si + ai[gi] * sr + lim[gi, pl.ds(r, nb), :]
            new.append((nr, ni))
        return tuple(new)

    if nchunk == 1:
        st = step(0, st0)
    else:
        st = lax.fori_loop(0, nchunk, step, st0, unroll=4)
    for gi in range(gb):
        fr_ref[gi] = st[gi][0]
        fi_ref[gi] = st[gi][1]
        u = u_ref[gi]
        y = (_dot(u, m_ref[gi]) + _dot(ire[gi].astype(BF16), ore_ref[gi])
             + _dot(iim[gi].astype(BF16), oim_ref[gi]))
        y_ref[gi] = y.astype(y_ref.dtype)


def _ssm(u_g, mats, a_re, a_im, s0, *, nb, nchunk, gb=4):
    G, R, LP = u_g.shape
    n = SSM_STATE
    m, sre, sim, ore, oim = mats
    has_s0 = s0 is not None
    gspec = lambda *s: pl.BlockSpec((gb,) + s, lambda i: (i, 0, 0))
    ins = [u_g, m, sre, sim, ore, oim, a_re, a_im]
    if has_s0:
        ins += [s0[0], s0[1]]
    in_specs = [gspec(*a.shape[1:]) for a in ins]
    out_shape = [jax.ShapeDtypeStruct((G, R, LP), F32), jax.ShapeDtypeStruct((G, nb, n), F32),
                 jax.ShapeDtypeStruct((G, nb, n), F32)]
    return pl.pallas_call(
        functools.partial(_ssm_kernel, gb=gb, nb=nb, nchunk=nchunk, has_s0=has_s0),
        grid=(G // gb,), in_specs=in_specs, out_specs=[gspec(*o.shape[1:]) for o in out_shape],
        out_shape=out_shape, scratch_shapes=[pltpu.VMEM((gb, R, n), F32)] * 4,
        compiler_params=_params("arbitrary"), name="ssm_chunk")(*ins)


def _mix_out_kernel(x_ref, ys_ref, yc_ref, wg_ref, bg_ref, gs_ref, wo_ref, gx_ref, wq_ref, x1_ref, q_ref, *scratch,
                    nseg, seg, grouped, ms, q_scale):
    tm = nseg * seg
    if grouped:
        yt_ref, = scratch
        _groups_to_tokens(ys_ref, yt_ref, nseg=nseg, seg=seg)
        ys = jnp.concatenate([yt_ref[q] for q in range(ms // LANES)], axis=1)
    else:
        ys = ys_ref[...].reshape(tm, ms)
    y = jax.nn.gelu(ys)
    y = y * jax.nn.sigmoid(_dot(y.astype(BF16), wg_ref[...]) + bg_ref[...])
    ysn = _rms(y, gs_ref[...]).astype(BF16)
    yc = yc_ref[...].reshape(tm, yc_ref.shape[-1])
    x1 = x_ref[...].reshape(tm, x_ref.shape[-1]) + _dot(ysn, wo_ref[0:ms, :]) + _dot(yc, wo_ref[ms:, :])
    x1_ref[...] = x1.reshape(x1_ref.shape)
    h = _rms(x1, gx_ref[...]).astype(BF16)
    q_ref[...] = (_dot(h, wq_ref[...]) * q_scale).astype(q_ref.dtype).reshape(q_ref.shape)


def _mix_out(x3d, ys, yc, w_glu, b_glu, g_ssm, w_out, g_x, w_q, *, seg, grouped, q_dtype):
    nseg, tlen, d = x3d.shape
    ms = w_glu.shape[0]
    hd = d // XATTN_HEADS
    row_spec = lambda c: pl.BlockSpec((nseg, seg, c), lambda t: (0, t, 0))
    scratch = []
    if grouped:
        ys_spec = pl.BlockSpec((ys.shape[0], seg // SSM_CHUNK * nseg, ys.shape[2]), lambda t: (0, t, 0))
        scratch.append(pltpu.VMEM((ms // LANES, nseg * seg, LANES), F32))
    else:
        ys_spec = row_spec(ms)
    in_specs = [row_spec(d), ys_spec, row_spec(yc.shape[-1]), _const_spec(w_glu.shape), _const_spec((1, ms)),
                _const_spec((1, ms)), _const_spec(w_out.shape), _const_spec((1, d)), _const_spec(w_q.shape)]
    return pl.pallas_call(
        functools.partial(_mix_out_kernel, nseg=nseg, seg=seg, grouped=grouped, ms=ms, q_scale=hd ** -0.5),
        grid=(tlen // seg,), in_specs=in_specs, out_specs=[row_spec(d), row_spec(d)],
        out_shape=[jax.ShapeDtypeStruct((nseg, tlen, d), F32), jax.ShapeDtypeStruct((nseg, tlen, d), q_dtype)],
        scratch_shapes=scratch,
        compiler_params=_params("parallel"), name="mix_out")(
            x3d, ys, yc, w_glu, b_glu.reshape(1, ms), g_ssm.reshape(1, ms), w_out, g_x.reshape(1, d), w_q)


def _kv_kernel(m_ref, g_ref, wk_ref, wv_ref, k_ref, v_ref):
    h = _rms(m_ref[...], g_ref[...]).astype(BF16)
    k_ref[...] = _dot(h, wk_ref[...])
    v_ref[...] = _dot(h, wv_ref[...])


def _kv_proj(mem2d, g, w_k, w_v, *, tm):
    rows, d = mem2d.shape
    row_spec = pl.BlockSpec((tm, d), lambda i: (i, 0))
    return pl.pallas_call(
        _kv_kernel, grid=(rows // tm,),
        in_specs=[row_spec, _const_spec((1, d)), _const_spec(w_k.shape), _const_spec(w_v.shape)],
        out_specs=[row_spec, row_spec],
        out_shape=[jax.ShapeDtypeStruct((rows, d), F32)] * 2,
        compiler_params=_params("parallel"), name="kv_proj")(mem2d, g.reshape(1, d), w_k, w_v)


def _attn_kernel(q_ref, k_ref, v_ref, o_ref, *, sb, tq, hd):
    for j in range(sb):
        q = q_ref[j * tq:(j + 1) * tq, :].astype(BF16)
        for h in range(XATTN_HEADS):
            cols = slice(h * hd, (h + 1) * hd)
            kh = k_ref[j * N_MEM:(j + 1) * N_MEM, cols].astype(BF16)
            vh = v_ref[j * N_MEM:(j + 1) * N_MEM, cols].astype(BF16)
            s = lax.dot_general(q[:, cols], kh, (((1,), (1,)), ((), ())), preferred_element_type=F32)
            p = jnp.exp(s - jnp.max(s, axis=-1, keepdims=True))
            p = p / jnp.sum(p, axis=-1, keepdims=True)
            o_ref[j * tq:(j + 1) * tq, cols] = _dot(p.astype(BF16), vh).astype(o_ref.dtype)


def _attn(q2d, k2d, v2d, *, nseq, sb, tq, nq, o_dtype):
    rows, d = q2d.shape
    hd = d // XATTN_HEADS
    q_spec = pl.BlockSpec((sb * tq, d), lambda b, t: (b * nq + t, 0))
    kv_spec = pl.BlockSpec((sb * N_MEM, d), lambda b, t: (b, 0))
    return pl.pallas_call(
        functools.partial(_attn_kernel, sb=sb, tq=tq, hd=hd),
        grid=(nseq // sb, nq), in_specs=[q_spec, kv_spec, kv_spec], out_specs=q_spec,
        out_shape=jax.ShapeDtypeStruct((rows, d), o_dtype),
        compiler_params=_params("parallel", "arbitrary"), name="xattn")(q2d, k2d, v2d)


def _ffn_kernel(*refs, tm, ts, hdr, has_prev, dff):
    if has_prev:
        (x_ref, o_ref, wxo_ref, gf_ref, wup_ref, wgate_ref, cw_ref, wdn_ref, gl_ref, prev_ref,
         y_ref, buf_ref, s_ref) = refs
    else:
        (x_ref, o_ref, wxo_ref, gf_ref, wup_ref, wgate_ref, cw_ref, wdn_ref, gl_ref,
         y_ref, buf_ref, s_ref) = refs
    if has_prev:
        s_ref[hdr - 2 * ts:hdr, :] = prev_ref[...]
    else:
        @pl.when(pl.program_id(1) == 0)
        def _():
            s_ref[0:hdr, :] = jnp.zeros((hdr, dff), F32)

    x2 = x_ref[...] + _dot(o_ref[...].astype(BF16), wxo_ref[...])
    hb = _rms(x2, gf_ref[...]).astype(BF16)
    a = _causal_conv3(_dot(hb, wup_ref[...]), cw_ref, s_ref, tm=tm, ts=ts, hdr=hdr)
    act = (jax.nn.gelu(a) * _dot(hb, wgate_ref[...])).astype(BF16)
    x3 = x2 + _dot(act, wdn_ref[...])
    y_ref[...] = _rms(x3, gl_ref[...])
    tail = s_ref[hdr + tm - 2 * ts:hdr + tm, :]
    buf_ref[...] = tail.reshape(buf_ref.shape)
    if not has_prev:
        s_ref[0:hdr, :] = s_ref[tm:tm + hdr, :]


def _ffn(x2d, o2d, w_xo, g_ffn, w_up, w_gate, conv_w, w_down, g_last, prev, *, nb, nt, tm, ts, hdr):
    rows, d = x2d.shape
    dff = w_up.shape[1]
    has_prev = prev is not None
    row_spec = pl.BlockSpec((tm, d), lambda b, t: (b * nt + t, 0))
    in_specs = [row_spec, row_spec, _const_spec(w_xo.shape), _const_spec((1, d)), _const_spec(w_up.shape),
                _const_spec(w_gate.shape), _const_spec(conv_w.shape), _const_spec(w_down.shape),
                _const_spec((1, d))]
    ins = [x2d, o2d, w_xo, g_ffn.reshape(1, d), w_up, w_gate, conv_w, w_down, g_last.reshape(1, d)]
    if has_prev:
        in_specs.append(_const_spec(prev.shape))
        ins.append(prev)
        buf_shape = jax.ShapeDtypeStruct((2 * ts, dff), F32)
        buf_spec = pl.BlockSpec((2 * ts, dff), lambda b, t: (0, 0))
    else:
        buf_shape = jax.ShapeDtypeStruct((nb, 2 * ts, dff), F32)
        buf_spec = pl.BlockSpec((None, 2 * ts, dff), lambda b, t: (b, 0, 0))
    return pl.pallas_call(
        functools.partial(_ffn_kernel, tm=tm, ts=ts, hdr=hdr, has_prev=has_prev, dff=dff),
        grid=(nb, nt), in_specs=in_specs, out_specs=[row_spec, buf_spec],
        out_shape=[jax.ShapeDtypeStruct((rows, d), F32), buf_shape],
        scratch_shapes=[pltpu.VMEM((hdr + tm, dff), F32)],
        compiler_params=_params("parallel", "arbitrary"), name="conv_ffn")(*ins)


def _layer(x3d, k2d, v2d, prev, p, mats, g_last, *, nseq, tlen, time_major):
    d = x3d.shape[-1]
    rows = nseq * tlen
    G, P = mats["G"], SSM_GROUP_CH
    ms = G * P
    mc = p["conv_w"].shape[1]
    dff = p["w_up"].shape[1]
    if time_major:
        chunk, nchunk = tlen, 1
        mix_geom = dict(seg=rows, ts=nseq, hdr=2 * nseq, grouped=False)
        ffn_geom = dict(nb=1, nt=1, tm=rows, ts=nseq, hdr=2 * nseq)
        conv_prev = prev[2].transpose(1, 0, 2).reshape(1, 2 * nseq, mc)
        ffn_prev = prev[3].transpose(1, 0, 2).reshape(2 * nseq, dff)
        s0 = (prev[0].transpose(1, 0, 2), prev[1].transpose(1, 0, 2))
        ssm_mats, a_re, a_im = mats["short"], mats["a_short"][0], mats["a_short"][1]
    else:
        chunk, nchunk = SSM_CHUNK, tlen // SSM_CHUNK
        mix_geom = dict(seg=128, ts=1, hdr=8, grouped=True)
        ffn_geom = dict(nb=nseq, nt=tlen // 256, tm=256, ts=1, hdr=8)
        conv_prev = ffn_prev = s0 = None
        ssm_mats, a_re, a_im = mats["long"], mats["a_long"][0], mats["a_long"][1]

    u, ycn, conv_buf = _mix_in(x3d, p["norm_mix"], p["w_in"], p["conv_w"], p["norm_conv_out"], conv_prev,
                               ms=ms, mc=mc, **mix_geom)
    if time_major:
        u = u.reshape(chunk, nseq, G, P).transpose(2, 1, 0, 3).reshape(G, nseq, chunk * P)
    y_g, f_re, f_im = _ssm(u, ssm_mats, a_re, a_im, s0, nb=nseq, nchunk=nchunk)
    if time_major:
        y_g = y_g.reshape(G, nseq, chunk, P).transpose(2, 1, 0, 3).reshape(1, rows, ms)

    x1, q = _mix_out(x3d, y_g, ycn, p["w_glu"], p["b_glu"], p["norm_ssm_out"], p["w_out"], p["norm_xattn"],
                     p["w_q"], seg=mix_geom["seg"], grouped=mix_geom["grouped"],
                     q_dtype=F32 if time_major else BF16)
    x1 = x1.reshape(rows, d)
    if time_major:
        q_sm = q.reshape(tlen, nseq, d).transpose(1, 0, 2).reshape(rows, d)
        o_sm = _attn(q_sm, k2d, v2d, nseq=nseq, sb=4, tq=tlen, nq=1, o_dtype=F32)
        o = o_sm.reshape(nseq, tlen, d).transpose(1, 0, 2).reshape(rows, d)
    else:
        o = _attn(q.reshape(rows, d), k2d, v2d, nseq=nseq, sb=1, tq=512, nq=tlen // 512, o_dtype=BF16)

    y, ffn_buf = _ffn(x1, o, p["w_xo"], p["norm_ffn"], p["w_up"], p["w_gate"], p["ffn_conv_w"], p["w_down"],
                      g_last, ffn_prev, **ffn_geom)

    f_re, f_im = f_re.transpose(1, 0, 2), f_im.transpose(1, 0, 2)
    if time_major:
        conv_buf = conv_buf.reshape(2, nseq, mc).transpose(1, 0, 2)
        ffn_buf = ffn_buf.reshape(2, nseq, dff).transpose(1, 0, 2)
    return y, (f_re, f_im, conv_buf, ffn_buf)


def kernel(x_prompt, x_sample, mem_prompt, cache_mem_k, cache_mem_v, state_ssm_re, state_ssm_im, state_conv, state_ffn_conv, norm_mix, w_in, ssm_A_re, ssm_A_im, ssm_log_dt, ssm_B_re, ssm_B_im, ssm_C_re, ssm_C_im, ssm_D, w_glu, b_glu, conv_w, norm_ssm_out, norm_conv_out, w_out, norm_xattn, norm_mem, w_q, w_k, w_v, w_xo, norm_ffn, w_up, w_gate, ffn_conv_w, w_down, norm_final):
    depth = w_in.shape[0]
    assert depth == 1, "the final norm is fused into the last (only) layer's ConvFFN kernel"
    nbp, tp, d = x_prompt.shape
    nbs, tsm, _ = x_sample.shape
    G = ssm_A_re.shape[1]
    P = SSM_GROUP_CH
    assert tp % 512 == 0 and tp % SSM_CHUNK == 0 and tsm <= SSM_CHUNK and nbp == 8

    xp = x_prompt
    xs = x_sample.transpose(1, 0, 2).reshape(1, tsm * nbs, d)
    outs_p, outs_s, mk_p, mv_p = [], [], [], []
    for l in range(depth):
        p = dict(norm_mix=norm_mix[l], w_in=w_in[l].astype(BF16), w_glu=w_glu[l].astype(BF16), b_glu=b_glu[l],
                 conv_w=conv_w[l], norm_ssm_out=norm_ssm_out[l], norm_conv_out=norm_conv_out[l],
                 w_out=w_out[l].astype(BF16), norm_xattn=norm_xattn[l], w_q=w_q[l].astype(BF16),
                 w_xo=w_xo[l].astype(BF16), norm_ffn=norm_ffn[l], w_up=w_up[l].astype(BF16),
                 w_gate=w_gate[l].astype(BF16), ffn_conv_w=ffn_conv_w[l], w_down=w_down[l].astype(BF16))
        m, sre, sim, ore, oim, apow = _ssm_prep(ssm_A_re[l], ssm_A_im[l], ssm_log_dt[l], ssm_B_re[l], ssm_B_im[l],
                                                ssm_C_re[l], ssm_C_im[l], ssm_D[l], chunk=SSM_CHUNK, short=tsm)
        lp_s = tsm * P
        lp_l = SSM_CHUNK * P
        mats = dict(
            G=G,
            long=(m, sre, sim, ore, oim),
            short=(m[:, :lp_s, :lp_s], sre[:, lp_l - lp_s:, :], sim[:, lp_l - lp_s:, :],
                   ore[:, :, :lp_s], oim[:, :, :lp_s]),
            a_long=(apow[:, 0:1, :], apow[:, 1:2, :]), a_short=(apow[:, 2:3, :], apow[:, 3:4, :]))

        k2d, v2d = _kv_proj(mem_prompt.reshape(nbp * N_MEM, d), norm_mem[l], w_k[l].astype(BF16),
                            w_v[l].astype(BF16), tm=512)
        mk_p.append(k2d.reshape(nbp, N_MEM, XATTN_HEADS, d // XATTN_HEADS))
        mv_p.append(v2d.reshape(nbp, N_MEM, XATTN_HEADS, d // XATTN_HEADS))

        xp, st_p = _layer(xp, k2d, v2d, None, p, mats, norm_final, nseq=nbp, tlen=tp, time_major=False)
        outs_p.append(st_p)
        xs, st_s = _layer(xs, cache_mem_k[l].reshape(nbs * N_MEM, d), cache_mem_v[l].reshape(nbs * N_MEM, d),
                          (state_ssm_re[l], state_ssm_im[l], state_conv[l], state_ffn_conv[l]), p, mats,
                          norm_final, nseq=nbs, tlen=tsm, time_major=True)
        outs_s.append(st_s)

    yp = xp.reshape(nbp, tp, d)
    ys = xs.reshape(tsm, nbs, d).transpose(1, 0, 2)
    stack = lambda outs, i: jnp.stack([o[i] for o in outs])
    return (yp, ys, jnp.stack(mk_p), jnp.stack(mv_p),
            stack(outs_p, 0), stack(outs_p, 1), stack(outs_p, 2), stack(outs_p, 3),
            stack(outs_s, 0), stack(outs_s, 1), stack(outs_s, 2), stack(outs_s, 3))
```

```python
import functools

import jax
import jax.numpy as jnp
from jax import lax
from jax.experimental import pallas as pl
from jax.experimental.pallas import tpu as pltpu

F32 = jnp.float32
BF16 = jnp.bfloat16

EPS = 1e-6
SSM_GROUP_CH = 16
SSM_STATE = 64
CONV_K = 3
N_MEM = 256
XATTN_HEADS = 4
SSM_CHUNK = 16
LANES = 128
V7X_VMEM_LIMIT = 56 * 1024 * 1024


def _rms(x, g):
    ms = jnp.mean(x * x, axis=-1, keepdims=True)
    return x * lax.rsqrt(ms + EPS) * g


def _dot(a, b):
    return jnp.dot(a, b, preferred_element_type=F32)


def _const_spec(shape):
    nd = len(shape)
    return pl.BlockSpec(shape, lambda *_: (0,) * nd, pipeline_mode=pl.Buffered(1))


def _params(*sem):
    return pltpu.CompilerParams(dimension_semantics=sem, vmem_limit_bytes=V7X_VMEM_LIMIT)


def _ssm_prep_kernel(ld_ref, lr_r_ref, li_r_ref, lr_c_ref, li_c_ref, brt_ref, bit_ref, brt_t_ref, bit_t_ref,
                     crt_t_ref, cit_t_ref, d_ref,
                     m_ref, sre_ref, sim_ref, ore_ref, oim_ref, apow_ref, *, gb, chunk, short):
    P, N = SSM_GROUP_CH, SSM_STATE
    LP = chunk * P
    lane_j = lax.broadcasted_iota(jnp.int32, (1, LP), 1) // P
    row_j = lax.broadcasted_iota(jnp.int32, (LP, 1), 0) // P
    lane_i = lax.broadcasted_iota(jnp.int32, (P, LP), 1)
    row_i = lax.broadcasted_iota(jnp.int32, (P, LP), 0)
    nbits = chunk.bit_length()

    def squarings(lr, li, dt):
        mag = jnp.exp(dt * lr)
        pr, pi = mag * jnp.cos(dt * li), mag * jnp.sin(dt * li)
        out = [(pr, pi)]
        for _ in range(nbits - 1):
            pr, pi = pr * pr - pi * pi, 2.0 * pr * pi
            out.append((pr, pi))
        return out

    def cpow(pows, j):
        er = ei = None
        for b, (pr, pi) in enumerate(pows):
            if isinstance(j, int):
                if not (j >> b) & 1:
                    continue
                er, ei = (pr, pi) if er is None else (er * pr - ei * pi, er * pi + ei * pr)
            else:
                bit = ((j >> b) & 1) == 1
                if er is None:
                    er, ei = jnp.where(bit, pr, 1.0), jnp.where(bit, pi, 0.0)
                else:
                    er, ei = jnp.where(bit, er * pr - ei * pi, er), jnp.where(bit, er * pi + ei * pr, ei)
        return er, ei

    for gi in range(gb):
        dt = jnp.exp(ld_ref[gi])
        lr_r, li_r = lr_r_ref[gi], li_r_ref[gi]
        pows_r = squarings(lr_r, li_r, dt)
        pows_c = squarings(lr_c_ref[gi], li_c_ref[gi], dt)

        ar, ai = pows_r[0]
        den = lr_r * lr_r + li_r * li_r
        cr = ((ar - 1.0) * lr_r + ai * li_r) / den
        ci = (ai * lr_r - (ar - 1.0) * li_r) / den

        brt, bit = brt_ref[gi], bit_ref[gi]
        bbt_re = cr * brt - ci * bit
        bbt_im = cr * bit + ci * brt
        brt_t, bit_t = brt_t_ref[gi], bit_t_ref[gi]
        bbt_re_t = cr * brt_t - ci * bit_t
        bbt_im_t = cr * bit_t + ci * brt_t

        er, ei = cpow(pows_r, (chunk - 1) - row_j)
        sre_ref[gi] = (er * bbt_re_t - ei * bbt_im_t).astype(sre_ref.dtype)
        sim_ref[gi] = (er * bbt_im_t + ei * bbt_re_t).astype(sim_ref.dtype)

        crt_t, cit_t = crt_t_ref[gi], cit_t_ref[gi]
        er, ei = cpow(pows_c, lane_j)
        r_re = crt_t * er - cit_t * ei
        r_im = crt_t * ei + cit_t * er
        er, ei = cpow(pows_c, lane_j + 1)
        ore_ref[gi] = (crt_t * er - cit_t * ei).astype(ore_ref.dtype)
        oim_ref[gi] = (-(crt_t * ei + cit_t * er)).astype(oim_ref.dtype)

        krow = (jnp.dot(bbt_re, r_re, preferred_element_type=F32, precision=lax.Precision.HIGHEST)
                - jnp.dot(bbt_im, r_im, preferred_element_type=F32, precision=lax.Precision.HIGHEST))
        krow = krow + jnp.where(lane_i == row_i, d_ref[gi], 0.0)
        for ti in range(chunk):
            blk = krow if ti == 0 else pltpu.roll(krow, ti * P, axis=1)
            blk = jnp.where(lane_i >= ti * P, blk, 0.0)
            m_ref[gi, ti * P:(ti + 1) * P, :] = blk.astype(m_ref.dtype)

        a16r, a16i = cpow(pows_r, chunk)
        a4r, a4i = cpow(pows_r, short)
        apow_ref[gi] = jnp.concatenate([a16r, a16i, a4r, a4i], axis=0)


def _ssm_prep(A_re, A_im, log_dt, B_re, B_im, C_re, C_im, D, *, chunk, short, gb=8):
    G, N = A_re.shape
    P = SSM_GROUP_CH
    LP = chunk * P
    brt = B_re.transpose(0, 2, 1)
    bit = B_im.transpose(0, 2, 1)
    crt_t = jnp.tile(C_re.transpose(0, 2, 1), (1, 1, chunk))
    cit_t = jnp.tile(C_im.transpose(0, 2, 1), (1, 1, chunk))
    ins = [log_dt.reshape(G, 1, 1), A_re.reshape(G, 1, N), A_im.reshape(G, 1, N),
           A_re.reshape(G, N, 1), A_im.reshape(G, N, 1), brt, bit,
           jnp.tile(brt, (1, chunk, 1)), jnp.tile(bit, (1, chunk, 1)), crt_t, cit_t, D.reshape(G, P, 1)]
    gspec = lambda *s: pl.BlockSpec((gb,) + s, lambda i: (i, 0, 0))
    in_specs = [gspec(*a.shape[1:]) for a in ins]
    out_shape = [jax.ShapeDtypeStruct((G, LP, LP), BF16),
                 jax.ShapeDtypeStruct((G, LP, N), BF16), jax.ShapeDtypeStruct((G, LP, N), BF16),
                 jax.ShapeDtypeStruct((G, N, LP), BF16), jax.ShapeDtypeStruct((G, N, LP), BF16),
                 jax.ShapeDtypeStruct((G, 4, N), F32)]
    out_specs = [gspec(*o.shape[1:]) for o in out_shape]
    return pl.pallas_call(
        functools.partial(_ssm_prep_kernel, gb=gb, chunk=chunk, short=short),
        grid=(G // gb,), in_specs=in_specs, out_specs=out_specs, out_shape=out_shape,
        compiler_params=_params("arbitrary"), name="ssm_prep")(*ins)


def _causal_conv3(v, w_ref, s_ref, *, tm, ts, hdr):
    s_ref[hdr:hdr + tm, :] = v
    xm1 = s_ref[hdr - ts:hdr - ts + tm, :]
    xm2 = s_ref[hdr - 2 * ts:hdr - 2 * ts + tm, :]
    return w_ref[0:1, :] * xm2 + w_ref[1:2, :] * xm1 + w_ref[2:3, :] * v


def _segment_conv3(v, w_ref, s_ref, *, nseg, seg, ts, hdr):
    out = [_causal_conv3(v[b * seg:(b + 1) * seg], w_ref, s_ref.at[b], tm=seg, ts=ts, hdr=hdr)
           for b in range(nseg)]
    return out[0] if nseg == 1 else jnp.concatenate(out, axis=0)


def _conv_state_begin(s_ref, prev_ref, *, ts, hdr):
    if prev_ref is not None:
        s_ref[:, hdr - 2 * ts:hdr, :] = prev_ref[...]
    else:
        @pl.when(pl.program_id(0) == 0)
        def _():
            s_ref[:, 0:hdr, :] = jnp.zeros((s_ref.shape[0], hdr, s_ref.shape[2]), F32)


def _conv_state_end(s_ref, buf_ref, *, seg, ts, hdr, carry):
    buf_ref[...] = s_ref[:, hdr + seg - 2 * ts:hdr + seg, :]
    if carry:
        s_ref[:, 0:hdr, :] = s_ref[:, seg:seg + hdr, :]


def _piece_transpose8(v):
    width = LANES // 8
    piece = lax.broadcasted_iota(jnp.int32, v[0].shape, 1) // width
    for bit in range(3):
        s = 1 << bit
        hi = ((piece >> bit) & 1) == 1
        nv = list(v)
        for i in range(8):
            if i & s:
                continue
            a, b = v[i], v[i + s]
            nv[i] = jnp.where(hi, pltpu.roll(b, s * width, axis=1), a)
            nv[i + s] = jnp.where(hi, b, pltpu.roll(a, LANES - s * width, axis=1))
        v = nv
    return v


def _tokens_to_groups(zu_ref, u_ref, *, nseg, seg):
    nc = seg // SSM_CHUNK
    for h in range(SSM_CHUNK // 8):
        for q in range(zu_ref.shape[0]):
            xs = [jnp.concatenate([zu_ref[q, pl.ds(SSM_CHUNK * c + 8 * h + i, nseg, stride=seg), :]
                                   for c in range(nc)], axis=0) for i in range(8)]
            w = _piece_transpose8(xs)
            for k in range(8):
                u_ref[8 * q + k, :, h * LANES:(h + 1) * LANES] = w[k].astype(u_ref.dtype)


def _groups_to_tokens(y_ref, ys_ref, *, nseg, seg):
    nc = seg // SSM_CHUNK
    for h in range(SSM_CHUNK // 8):
        for q in range(ys_ref.shape[0]):
            v = _piece_transpose8([y_ref[8 * q + k, :, h * LANES:(h + 1) * LANES].astype(F32) for k in range(8)])
            for i in range(8):
                for c in range(nc):
                    ys_ref[q, pl.ds(SSM_CHUNK * c + 8 * h + i, nseg, stride=seg), :] = v[i][c * nseg:(c + 1) * nseg]


def _mix_in_kernel(*refs, nseg, seg, ts, hdr, has_prev, grouped, ms, mc):
    x_ref, g_ref, w_ref, cw_ref, gc_ref = refs[:5]
    refs = refs[5:]
    prev_ref = None
    if has_prev:
        prev_ref, refs = refs[0], refs[1:]
    u_ref, yc_ref, buf_ref, s_ref = refs[:4]
    tm = nseg * seg
    _conv_state_begin(s_ref, prev_ref, ts=ts, hdr=hdr)

    hb = _rms(x_ref[...].reshape(tm, x_ref.shape[-1]), g_ref[...]).astype(BF16)
    zu = _dot(hb, w_ref[:, 0:ms])
    if grouped:
        zu_ref = refs[4]
        for q in range(ms // LANES):
            zu_ref[q] = zu[:, q * LANES:(q + 1) * LANES]
        _tokens_to_groups(zu_ref, u_ref, nseg=nseg, seg=seg)
    else:
        u_ref[...] = zu.astype(u_ref.dtype).reshape(u_ref.shape)
    xin = _dot(hb, w_ref[:, ms:ms + mc])
    cg = _dot(hb, w_ref[:, ms + 2 * mc:ms + 3 * mc])
    conv = _segment_conv3(cg * xin, cw_ref, s_ref, nseg=nseg, seg=seg, ts=ts, hdr=hdr)
    bg = _dot(hb, w_ref[:, ms + mc:ms + 2 * mc])
    yc_ref[...] = _rms(bg * conv, gc_ref[...]).astype(yc_ref.dtype).reshape(yc_ref.shape)
    _conv_state_end(s_ref, buf_ref, seg=seg, ts=ts, hdr=hdr, carry=not has_prev)


def _mix_in(x3d, g, w_bf, conv_w, g_conv, prev, *, seg, ts, hdr, grouped, ms, mc):
    nseg, tlen, d = x3d.shape
    has_prev = prev is not None
    G = ms // SSM_GROUP_CH
    row_spec = lambda c: pl.BlockSpec((nseg, seg, c), lambda t: (0, t, 0))
    in_specs = [row_spec(d), _const_spec((1, d)), _const_spec(w_bf.shape), _const_spec(conv_w.shape),
                _const_spec((1, mc))]
    ins = [x3d, g.reshape(1, d), w_bf, conv_w, g_conv.reshape(1, mc)]
    if has_prev:
        in_specs.append(_const_spec(prev.shape))
        ins.append(prev)
    scratch = [pltpu.VMEM((nseg, hdr + seg, mc), F32)]
    if grouped:
        lp = SSM_CHUNK * SSM_GROUP_CH
        rg = seg // SSM_CHUNK * nseg
        u_shape = jax.ShapeDtypeStruct((G, tlen // SSM_CHUNK * nseg, lp), BF16)
        u_spec = pl.BlockSpec((G, rg, lp), lambda t: (0, t, 0))
        scratch.append(pltpu.VMEM((ms // LANES, nseg * seg, LANES), F32))
    else:
        u_shape = jax.ShapeDtypeStruct((nseg, tlen, ms), BF16)
        u_spec = row_spec(ms)
    return pl.pallas_call(
        functools.partial(_mix_in_kernel, nseg=nseg, seg=seg, ts=ts, hdr=hdr, has_prev=has_prev, grouped=grouped,
                          ms=ms, mc=mc),
        grid=(tlen // seg,), in_specs=in_specs,
        out_specs=[u_spec, row_spec(mc), pl.BlockSpec((nseg, 2 * ts, mc), lambda t: (0, 0, 0))],
        out_shape=[u_shape, jax.ShapeDtypeStruct((nseg, tlen, mc), BF16),
                   jax.ShapeDtypeStruct((nseg, 2 * ts, mc), F32)],
        scratch_shapes=scratch,
        compiler_params=_params("arbitrary"), name="mix_in")(*ins)


def _ssm_kernel(*refs, gb, nb, nchunk, has_s0):
    if has_s0:
        (u_ref, m_ref, sre_ref, sim_ref, ore_ref, oim_ref, ar_ref, ai_ref, s0r_ref, s0i_ref,
         y_ref, fr_ref, fi_ref, lre, lim, ire, iim) = refs
    else:
        (u_ref, m_ref, sre_ref, sim_ref, ore_ref, oim_ref, ar_ref, ai_ref,
         y_ref, fr_ref, fi_ref, lre, lim, ire, iim) = refs
    n = SSM_STATE
    for gi in range(gb):
        u = u_ref[gi]
        lre[gi] = _dot(u, sre_ref[gi])
        lim[gi] = _dot(u, sim_ref[gi])
    ar = [jnp.broadcast_to(ar_ref[gi], (nb, n)) for gi in range(gb)]
    ai = [jnp.broadcast_to(ai_ref[gi], (nb, n)) for gi in range(gb)]
    if has_s0:
        st0 = tuple((s0r_ref[gi], s0i_ref[gi]) for gi in range(gb))
    else:
        st0 = tuple((jnp.zeros((nb, n), F32), jnp.zeros((nb, n), F32)) for _ in range(gb))

    def step(c, st):
        r = c * nb if isinstance(c, int) else pl.multiple_of(c * nb, nb)
        new = []
        for gi in range(gb):
            sr, si = st[gi]
            ire[gi, pl.ds(r, nb), :] = sr
            iim[gi, pl.ds(r, nb), :] = si
            nr = ar[gi] * sr - ai[gi] * si + lre[gi, pl.ds(r, nb), :]
            ni = ar[gi] * si + ai[gi] * sr + lim[gi, pl.ds(r, nb), :]
            new.append((nr, ni))
        return tuple(new)

    if nchunk == 1:
        st = step(0, st0)
    else:
        st = lax.fori_loop(0, nchunk, step, st0, unroll=4)
    for gi in range(gb):
        fr_ref[gi] = st[gi][0]
        fi_ref[gi] = st[gi][1]
        u = u_ref[gi]
        y = (_dot(u, m_ref[gi]) + _dot(ire[gi].astype(BF16), ore_ref[gi])
             + _dot(iim[gi].astype(BF16), oim_ref[gi]))
        y_ref[gi] = y.astype(y_ref.dtype)


def _ssm(u_g, mats, a_re, a_im, s0, *, nb, nchunk, gb=4):
    G, R, LP = u_g.shape
    n = SSM_STATE
    m, sre, sim, ore, oim = mats
    has_s0 = s0 is not None
    gspec = lambda *s: pl.BlockSpec((gb,) + s, lambda i: (i, 0, 0))
    ins = [u_g, m, sre, sim, ore, oim, a_re, a_im]
    if has_s0:
        ins += [s0[0], s0[1]]
    in_specs = [gspec(*a.shape[1:]) for a in ins]
    out_shape = [jax.ShapeDtypeStruct((G, R, LP), F32), jax.ShapeDtypeStruct((G, nb, n), F32),
                 jax.ShapeDtypeStruct((G, nb, n), F32)]
    return pl.pallas_call(
        functools.partial(_ssm_kernel, gb=gb, nb=nb, nchunk=nchunk, has_s0=has_s0),
        grid=(G // gb,), in_specs=in_specs, out_specs=[gspec(*o.shape[1:]) for o in out_shape],
        out_shape=out_shape, scratch_shapes=[pltpu.VMEM((gb, R, n), F32)] * 4,
        compiler_params=_params("arbitrary"), name="ssm_chunk")(*ins)


def _mix_out_kernel(x_ref, ys_ref, yc_ref, wg_ref, bg_ref, gs_ref, wo_ref, gx_ref, wq_ref, x1_ref, q_ref, *scratch,
                    nseg, seg, grouped, ms, q_scale):
    tm = nseg * seg
    if grouped:
        yt_ref, = scratch
        _groups_to_tokens(ys_ref, yt_ref, nseg=nseg, seg=seg)
        ys = jnp.concatenate([yt_ref[q] for q in range(ms // LANES)], axis=1)
    else:
        ys = ys_ref[...].reshape(tm, ms)
    y = jax.nn.gelu(ys)
    y = y * jax.nn.sigmoid(_dot(y.astype(BF16), wg_ref[...]) + bg_ref[...])
    ysn = _rms(y, gs_ref[...]).astype(BF16)
    yc = yc_ref[...].reshape(tm, yc_ref.shape[-1])
    x1 = x_ref[...].reshape(tm, x_ref.shape[-1]) + _dot(ysn, wo_ref[0:ms, :]) + _dot(yc, wo_ref[ms:, :])
    x1_ref[...] = x1.reshape(x1_ref.shape)
    h = _rms(x1, gx_ref[...]).astype(BF16)
    q_ref[...] = (_dot(h, wq_ref[...]) * q_scale).astype(q_ref.dtype).reshape(q_ref.shape)


def _mix_out(x3d, ys, yc, w_glu, b_glu, g_ssm, w_out, g_x, w_q, *, seg, grouped, q_dtype):
    nseg, tlen, d = x3d.shape
    ms = w_glu.shape[0]
    hd = d // XATTN_HEADS
    row_spec = lambda c: pl.BlockSpec((nseg, seg, c), lambda t: (0, t, 0))
    scratch = []
    if grouped:
        ys_spec = pl.BlockSpec((ys.shape[0], seg // SSM_CHUNK * nseg, ys.shape[2]), lambda t: (0, t, 0))
        scratch.append(pltpu.VMEM((ms // LANES, nseg * seg, LANES), F32))
    else:
        ys_spec = row_spec(ms)
    in_specs = [row_spec(d), ys_spec, row_spec(yc.shape[-1]), _const_spec(w_glu.shape), _const_spec((1, ms)),
                _const_spec((1, ms)), _const_spec(w_out.shape), _const_spec((1, d)), _const_spec(w_q.shape)]
    return pl.pallas_call(
        functools.partial(_mix_out_kernel, nseg=nseg, seg=seg, grouped=grouped, ms=ms, q_scale=hd ** -0.5),
        grid=(tlen // seg,), in_specs=in_specs, out_specs=[row_spec(d), row_spec(d)],
        out_shape=[jax.ShapeDtypeStruct((nseg, tlen, d), F32), jax.ShapeDtypeStruct((nseg, tlen, d), q_dtype)],
        scratch_shapes=scratch,
        compiler_params=_params("parallel"), name="mix_out")(
            x3d, ys, yc, w_glu, b_glu.reshape(1, ms), g_ssm.reshape(1, ms), w_out, g_x.reshape(1, d), w_q)


def _kv_kernel(m_ref, g_ref, wk_ref, wv_ref, k_ref, v_ref):
    h = _rms(m_ref[...], g_ref[...]).astype(BF16)
    k_ref[...] = _dot(h, wk_ref[...])
    v_ref[...] = _dot(h, wv_ref[...])


def _kv_proj(mem2d, g, w_k, w_v, *, tm):
    rows, d = mem2d.shape
    row_spec = pl.BlockSpec((tm, d), lambda i: (i, 0))
    return pl.pallas_call(
        _kv_kernel, grid=(rows // tm,),
        in_specs=[row_spec, _const_spec((1, d)), _const_spec(w_k.shape), _const_spec(w_v.shape)],
        out_specs=[row_spec, row_spec],
        out_shape=[jax.ShapeDtypeStruct((rows, d), F32)] * 2,
        compiler_params=_params("parallel"), name="kv_proj")(mem2d, g.reshape(1, d), w_k, w_v)


def _attn_kernel(q_ref, k_ref, v_ref, o_ref, *, sb, tq, hd):
    pairs = [(j, h) for j in range(sb) for h in range(XATTN_HEADS)]
    scores = []
    for j, h in pairs:
        qh = q_ref[j * tq:(j + 1) * tq, h * hd:(h + 1) * hd].astype(BF16)
        kh = k_ref[j * N_MEM:(j + 1) * N_MEM, h * hd:(h + 1) * hd].astype(BF16)
        scores.append(lax.dot_general(qh, kh, (((1,), (1,)), ((), ())), preferred_element_type=F32))
    s = jnp.concatenate(scores, axis=0)
    p = jnp.exp(s - jnp.max(s, axis=-1, keepdims=True))
    p = p / jnp.sum(p, axis=-1, keepdims=True)
    for i, (j, h) in enumerate(pairs):
        vh = v_ref[j * N_MEM:(j + 1) * N_MEM, h * hd:(h + 1) * hd].astype(BF16)
        ph = p[i * tq:(i + 1) * tq].astype(BF16)
        o_ref[j * tq:(j + 1) * tq, h * hd:(h + 1) * hd] = _dot(ph, vh).astype(o_ref.dtype)


def _attn(q2d, k2d, v2d, *, nseq, sb, tq, nq, o_dtype):
    rows, d = q2d.shape
    hd = d // XATTN_HEADS
    q_spec = pl.BlockSpec((sb * tq, d), lambda b, t: (b * nq + t, 0))
    kv_spec = pl.BlockSpec((sb * N_MEM, d), lambda b, t: (b, 0))
    return pl.pallas_call(
        functools.partial(_attn_kernel, sb=sb, tq=tq, hd=hd),
        grid=(nseq // sb, nq), in_specs=[q_spec, kv_spec, kv_spec], out_specs=q_spec,
        out_shape=jax.ShapeDtypeStruct((rows, d), o_dtype),
        compiler_params=_params("parallel", "arbitrary"), name="xattn")(q2d, k2d, v2d)


def _ffn_kernel(*refs, tm, ts, hdr, has_prev, dff):
    if has_prev:
        (x_ref, o_ref, wxo_ref, gf_ref, wup_ref, wgate_ref, cw_ref, wdn_ref, gl_ref, prev_ref,
         y_ref, buf_ref, s_ref) = refs
    else:
        (x_ref, o_ref, wxo_ref, gf_ref, wup_ref, wgate_ref, cw_ref, wdn_ref, gl_ref,
         y_ref, buf_ref, s_ref) = refs
    if has_prev:
        s_ref[hdr - 2 * ts:hdr, :] = prev_ref[...]
    else:
        @pl.when(pl.program_id(1) == 0)
        def _():
            s_ref[0:hdr, :] = jnp.zeros((hdr, dff), F32)

    x2 = x_ref[...] + _dot(o_ref[...].astype(BF16), wxo_ref[...])
    hb = _rms(x2, gf_ref[...]).astype(BF16)
    a = _causal_conv3(_dot(hb, wup_ref[...]), cw_ref, s_ref, tm=tm, ts=ts, hdr=hdr)
    act = (jax.nn.gelu(a) * _dot(hb, wgate_ref[...])).astype(BF16)
    x3 = x2 + _dot(act, wdn_ref[...])
    y_ref[...] = _rms(x3, gl_ref[...])
    tail = s_ref[hdr + tm - 2 * ts:hdr + tm, :]
    buf_ref[...] = tail.reshape(buf_ref.shape)
    if not has_prev:
        s_ref[0:hdr, :] = s_ref[tm:tm + hdr, :]


def _ffn(x2d, o2d, w_xo, g_ffn, w_up, w_gate, conv_w, w_down, g_last, prev, *, nb, nt, tm, ts, hdr):
    rows, d = x2d.shape
    dff = w_up.shape[1]
    has_prev = prev is not None
    row_spec = pl.BlockSpec((tm, d), lambda b, t: (b * nt + t, 0))
    in_specs = [row_spec, row_spec, _const_spec(w_xo.shape), _const_spec((1, d)), _const_spec(w_up.shape),
                _const_spec(w_gate.shape), _const_spec(conv_w.shape), _const_spec(w_down.shape),
                _const_spec((1, d))]
    ins = [x2d, o2d, w_xo, g_ffn.reshape(1, d), w_up, w_gate, conv_w, w_down, g_last.reshape(1, d)]
    if has_prev:
        in_specs.append(_const_spec(prev.shape))
        ins.append(prev)
        buf_shape = jax.ShapeDtypeStruct((2 * ts, dff), F32)
        buf_spec = pl.BlockSpec((2 * ts, dff), lambda b, t: (0, 0))
    else:
        buf_shape = jax.ShapeDtypeStruct((nb, 2 * ts, dff), F32)
        buf_spec = pl.BlockSpec((None, 2 * ts, dff), lambda b, t: (b, 0, 0))
    return pl.pallas_call(
        functools.partial(_ffn_kernel, tm=tm, ts=ts, hdr=hdr, has_prev=has_prev, dff=dff),
        grid=(nb, nt), in_specs=in_specs, out_specs=[row_spec, buf_spec],
        out_shape=[jax.ShapeDtypeStruct((rows, d), F32), buf_shape],
        scratch_shapes=[pltpu.VMEM((hdr + tm, dff), F32)],
        compiler_params=_params("parallel", "arbitrary"), name="conv_ffn")(*ins)


def _layer(x3d, k2d, v2d, prev, p, mats, g_last, *, nseq, tlen, time_major):
    d = x3d.shape[-1]
    rows = nseq * tlen
    G, P = mats["G"], SSM_GROUP_CH
    ms = G * P
    mc = p["conv_w"].shape[1]
    dff = p["w_up"].shape[1]
    if time_major:
        chunk, nchunk = tlen, 1
        mix_geom = dict(seg=rows, ts=nseq, hdr=2 * nseq, grouped=False)
        ffn_geom = dict(nb=1, nt=1, tm=rows, ts=nseq, hdr=2 * nseq)
        conv_prev = prev[2].transpose(1, 0, 2).reshape(1, 2 * nseq, mc)
        ffn_prev = prev[3].transpose(1, 0, 2).reshape(2 * nseq, dff)
        s0 = (prev[0].transpose(1, 0, 2), prev[1].transpose(1, 0, 2))
        ssm_mats, a_re, a_im = mats["short"], mats["a_short"][0], mats["a_short"][1]
    else:
        chunk, nchunk = SSM_CHUNK, tlen // SSM_CHUNK
        mix_geom = dict(seg=128, ts=1, hdr=8, grouped=True)
        ffn_geom = dict(nb=nseq, nt=tlen // 256, tm=256, ts=1, hdr=8)
        conv_prev = ffn_prev = s0 = None
        ssm_mats, a_re, a_im = mats["long"], mats["a_long"][0], mats["a_long"][1]

    u, ycn, conv_buf = _mix_in(x3d, p["norm_mix"], p["w_in"], p["conv_w"], p["norm_conv_out"], conv_prev,
                               ms=ms, mc=mc, **mix_geom)
    if time_major:
        u = u.reshape(chunk, nseq, G, P).transpose(2, 1, 0, 3).reshape(G, nseq, chunk * P)
    y_g, f_re, f_im = _ssm(u, ssm_mats, a_re, a_im, s0, nb=nseq, nchunk=nchunk, gb=16 if time_major else 4)
    if time_major:
        y_g = y_g.reshape(G, nseq, chunk, P).transpose(2, 1, 0, 3).reshape(1, rows, ms)

    x1, q = _mix_out(x3d, y_g, ycn, p["w_glu"], p["b_glu"], p["norm_ssm_out"], p["w_out"], p["norm_xattn"],
                     p["w_q"], seg=mix_geom["seg"], grouped=mix_geom["grouped"],
                     q_dtype=F32 if time_major else BF16)
    x1 = x1.reshape(rows, d)
    if time_major:
        q_sm = q.reshape(tlen, nseq, d).transpose(1, 0, 2).reshape(rows, d)
        o_sm = _attn(q_sm, k2d, v2d, nseq=nseq, sb=4, tq=tlen, nq=1, o_dtype=F32)
        o = o_sm.reshape(nseq, tlen, d).transpose(1, 0, 2).reshape(rows, d)
    else:
        o = _attn(q.reshape(rows, d), k2d, v2d, nseq=nseq, sb=1, tq=512, nq=tlen // 512, o_dtype=BF16)

    y, ffn_buf = _ffn(x1, o, p["w_xo"], p["norm_ffn"], p["w_up"], p["w_gate"], p["ffn_conv_w"], p["w_down"],
                      g_last, ffn_prev, **ffn_geom)

    f_re, f_im = f_re.transpose(1, 0, 2), f_im.transpose(1, 0, 2)
    if time_major:
        conv_buf = conv_buf.reshape(2, nseq, mc).transpose(1, 0, 2)
        ffn_buf = ffn_buf.reshape(2, nseq, dff).transpose(1, 0, 2)
    return y, (f_re, f_im, conv_buf, ffn_buf)


def kernel(x_prompt, x_sample, mem_prompt, cache_mem_k, cache_mem_v, state_ssm_re, state_ssm_im, state_conv, state_ffn_conv, norm_mix, w_in, ssm_A_re, ssm_A_im, ssm_log_dt, ssm_B_re, ssm_B_im, ssm_C_re, ssm_C_im, ssm_D, w_glu, b_glu, conv_w, norm_ssm_out, norm_conv_out, w_out, norm_xattn, norm_mem, w_q, w_k, w_v, w_xo, norm_ffn, w_up, w_gate, ffn_conv_w, w_down, norm_final):
    depth = w_in.shape[0]
    assert depth == 1, "the final norm is fused into the last (only) layer's ConvFFN kernel"
    nbp, tp, d = x_prompt.shape
    nbs, tsm, _ = x_sample.shape
    G = ssm_A_re.shape[1]
    P = SSM_GROUP_CH
    assert tp % 512 == 0 and tp % SSM_CHUNK == 0 and tsm <= SSM_CHUNK and nbp == 8

    xp = x_prompt
    xs = x_sample.transpose(1, 0, 2).reshape(1, tsm * nbs, d)
    outs_p, outs_s, mk_p, mv_p = [], [], [], []
    for l in range(depth):
        p = dict(norm_mix=norm_mix[l], w_in=w_in[l].astype(BF16), w_glu=w_glu[l].astype(BF16), b_glu=b_glu[l],
                 conv_w=conv_w[l], norm_ssm_out=norm_ssm_out[l], norm_conv_out=norm_conv_out[l],
                 w_out=w_out[l].astype(BF16), norm_xattn=norm_xattn[l], w_q=w_q[l].astype(BF16),
                 w_xo=w_xo[l].astype(BF16), norm_ffn=norm_ffn[l], w_up=w_up[l].astype(BF16),
                 w_gate=w_gate[l].astype(BF16), ffn_conv_w=ffn_conv_w[l], w_down=w_down[l].astype(BF16))
        m, sre, sim, ore, oim, apow = _ssm_prep(ssm_A_re[l], ssm_A_im[l], ssm_log_dt[l], ssm_B_re[l], ssm_B_im[l],
                                                ssm_C_re[l], ssm_C_im[l], ssm_D[l], chunk=SSM_CHUNK, short=tsm)
        lp_s = tsm * P
        lp_l = SSM_CHUNK * P
        mats = dict(
            G=G,
            long=(m, sre, sim, ore, oim),
            short=(m[:, :lp_s, :lp_s], sre[:, lp_l - lp_s:, :], sim[:, lp_l - lp_s:, :],
                   ore[:, :, :lp_s], oim[:, :, :lp_s]),
            a_long=(apow[:, 0:1, :], apow[:, 1:2, :]), a_short=(apow[:, 2:3, :], apow[:, 3:4, :]))

        k2d, v2d = _kv_proj(mem_prompt.reshape(nbp * N_MEM, d), norm_mem[l], w_k[l].astype(BF16),
                            w_v[l].astype(BF16), tm=512)
        mk_p.append(k2d.reshape(nbp, N_MEM, XATTN_HEADS, d // XATTN_HEADS))
        mv_p.append(v2d.reshape(nbp, N_MEM, XATTN_HEADS, d // XATTN_HEADS))

        xp, st_p = _layer(xp, k2d, v2d, None, p, mats, norm_final, nseq=nbp, tlen=tp, time_major=False)
        outs_p.append(st_p)
        xs, st_s = _layer(xs, cache_mem_k[l].reshape(nbs * N_MEM, d), cache_mem_v[l].reshape(nbs * N_MEM, d),
                          (state_ssm_re[l], state_ssm_im[l], state_conv[l], state_ffn_conv[l]), p, mats,
                          norm_final, nseq=nbs, tlen=tsm, time_major=True)
        outs_s.append(st_s)

    yp = xp.reshape(nbp, tp, d)
    ys = xs.reshape(tsm, nbs, d).transpose(1, 0, 2)
    stack = lambda outs, i: jnp.stack([o[i] for o in outs])
    return (yp, ys, jnp.stack(mk_p), jnp.stack(mv_p),
            stack(outs_p, 0), stack(outs_p, 1), stack(outs_p, 2), stack(outs_p, 3),
            stack(outs_s, 0), stack(outs_s, 1), stack(outs_s, 2), stack(outs_s, 3))
```

```python
import functools

import jax
import jax.numpy as jnp
from jax import lax
from jax.experimental import pallas as pl
from jax.experimental.pallas import tpu as pltpu

F32 = jnp.float32
BF16 = jnp.bfloat16

EPS = 1e-6
SSM_GROUP_CH = 16
SSM_STATE = 64
CONV_K = 3
N_MEM = 256
XATTN_HEADS = 4
SSM_CHUNK = 16
LANES = 128
MIX_STEPS = 128
ATTN_ROWS = 512
FFN_ROWS = 256
ATTN_STEP_SEQS = 4
V7X_VMEM_LIMIT = 56 * 1024 * 1024


def _rms(x, g):
    ms = jnp.mean(x * x, axis=-1, keepdims=True)
    return x * lax.rsqrt(ms + EPS) * g


def _dot(a, b):
    return jnp.dot(a, b, preferred_element_type=F32)


def _const_spec(shape):
    nd = len(shape)
    return pl.BlockSpec(shape, lambda *_: (0,) * nd, pipeline_mode=pl.Buffered(1))


def _params(*sem):
    return pltpu.CompilerParams(dimension_semantics=sem, vmem_limit_bytes=V7X_VMEM_LIMIT)


def _ssm_prep_kernel(ld_ref, lr_r_ref, li_r_ref, lr_c_ref, li_c_ref, brt_ref, bit_ref, crt_t_ref, cit_t_ref, d_ref,
                     m_ref, sre_ref, sim_ref, ore_ref, oim_ref, apow_ref, bband_ref, cband_ref, arow_ref,
                     *, gb, chunk):
    P, N = SSM_GROUP_CH, SSM_STATE
    LP = chunk * P
    bband_ref[...] = jnp.zeros(bband_ref.shape, bband_ref.dtype)
    cband_ref[...] = jnp.zeros(cband_ref.shape, cband_ref.dtype)
    lane_j = lax.broadcasted_iota(jnp.int32, (1, LP), 1) // P
    row_j = lax.broadcasted_iota(jnp.int32, (LP, 1), 0) // P
    lane_i = lax.broadcasted_iota(jnp.int32, (P, LP), 1)
    row_i = lax.broadcasted_iota(jnp.int32, (P, LP), 0)
    nbits = chunk.bit_length()

    def squarings(lr, li, dt):
        mag = jnp.exp(dt * lr)
        pr, pi = mag * jnp.cos(dt * li), mag * jnp.sin(dt * li)
        out = [(pr, pi)]
        for _ in range(nbits - 1):
            pr, pi = pr * pr - pi * pi, 2.0 * pr * pi
            out.append((pr, pi))
        return out

    def cpow(pows, j):
        er = ei = None
        for b, (pr, pi) in enumerate(pows):
            if isinstance(j, int):
                if not (j >> b) & 1:
                    continue
                er, ei = (pr, pi) if er is None else (er * pr - ei * pi, er * pi + ei * pr)
            else:
                bit = ((j >> b) & 1) == 1
                if er is None:
                    er, ei = jnp.where(bit, pr, 1.0), jnp.where(bit, pi, 0.0)
                else:
                    er, ei = jnp.where(bit, er * pr - ei * pi, er), jnp.where(bit, er * pi + ei * pr, ei)
        return er, ei

    for gi in range(gb):
        dt = jnp.exp(ld_ref[gi])
        lr_r, li_r = lr_r_ref[gi], li_r_ref[gi]
        pows_r = squarings(lr_r, li_r, dt)
        pows_c = squarings(lr_c_ref[gi], li_c_ref[gi], dt)

        ar, ai = pows_r[0]
        den = lr_r * lr_r + li_r * li_r
        cr = ((ar - 1.0) * lr_r + ai * li_r) / den
        ci = (ai * lr_r - (ar - 1.0) * li_r) / den

        brt, bit = brt_ref[gi], bit_ref[gi]
        bbt_re = cr * brt - ci * bit
        bbt_im = cr * bit + ci * brt
        bbt_re_t = jnp.concatenate([bbt_re] * chunk, axis=0)
        bbt_im_t = jnp.concatenate([bbt_im] * chunk, axis=0)

        er, ei = cpow(pows_r, (chunk - 1) - row_j)
        sre_ref[gi] = (er * bbt_re_t - ei * bbt_im_t).astype(sre_ref.dtype)
        sim_ref[gi] = (er * bbt_im_t + ei * bbt_re_t).astype(sim_ref.dtype)

        crt_t, cit_t = crt_t_ref[gi], cit_t_ref[gi]
        er, ei = cpow(pows_c, lane_j)
        r_re = crt_t * er - cit_t * ei
        r_im = crt_t * ei + cit_t * er
        er, ei = cpow(pows_c, lane_j + 1)
        ore_ref[gi] = (crt_t * er - cit_t * ei).astype(ore_ref.dtype)
        oim_ref[gi] = (-(crt_t * ei + cit_t * er)).astype(oim_ref.dtype)

        krow = (jnp.dot(bbt_re, r_re, preferred_element_type=F32, precision=lax.Precision.HIGHEST)
                - jnp.dot(bbt_im, r_im, preferred_element_type=F32, precision=lax.Precision.HIGHEST))
        krow = krow + jnp.where(lane_i == row_i, d_ref[gi], 0.0)
        for ti in range(chunk):
            blk = krow if ti == 0 else pltpu.roll(krow, ti * P, axis=1)
            blk = jnp.where(lane_i >= ti * P, blk, 0.0)
            m_ref[gi, ti * P:(ti + 1) * P, :] = blk.astype(m_ref.dtype)

        apow_ref[gi] = jnp.concatenate(cpow(pows_r, chunk), axis=0)

        rows, cols = slice(gi * P, (gi + 1) * P), slice(gi * N, (gi + 1) * N)
        cols_im = slice((gb + gi) * N, (gb + gi + 1) * N)
        bband_ref[rows, cols] = bbt_re
        bband_ref[rows, cols_im] = bbt_im
        cband_ref[cols, rows] = crt_t[:, rows]
        cband_ref[cols_im, rows] = -cit_t[:, rows]
        arow_ref[0:1, cols] = ar
        arow_ref[1:2, cols] = ai


def _ssm_prep(A_re, A_im, log_dt, B_re, B_im, C_re, C_im, D, *, chunk, gb=8):
    G, N = A_re.shape
    P = SSM_GROUP_CH
    LP = chunk * P
    brt = B_re.transpose(0, 2, 1)
    bit = B_im.transpose(0, 2, 1)
    crt_t = jnp.tile(C_re.transpose(0, 2, 1), (1, 1, chunk))
    cit_t = jnp.tile(C_im.transpose(0, 2, 1), (1, 1, chunk))
    ins = [log_dt.reshape(G, 1, 1), A_re.reshape(G, 1, N), A_im.reshape(G, 1, N),
           A_re.reshape(G, N, 1), A_im.reshape(G, N, 1), brt, bit, crt_t, cit_t, D.reshape(G, P, 1)]
    gspec = lambda *s: pl.BlockSpec((gb,) + s, lambda i: (i, 0, 0))
    in_specs = [gspec(*a.shape[1:]) for a in ins]
    out_shape = [jax.ShapeDtypeStruct((G, LP, LP), BF16),
                 jax.ShapeDtypeStruct((G, LP, N), BF16), jax.ShapeDtypeStruct((G, LP, N), BF16),
                 jax.ShapeDtypeStruct((G, N, LP), BF16), jax.ShapeDtypeStruct((G, N, LP), BF16),
                 jax.ShapeDtypeStruct((G, 2, N), F32)]
    out_specs = [gspec(*o.shape[1:]) for o in out_shape]
    band_shape = [jax.ShapeDtypeStruct((G // gb, gb * P, 2 * gb * N), F32),
                  jax.ShapeDtypeStruct((G // gb, 2 * gb * N, gb * P), F32),
                  jax.ShapeDtypeStruct((G // gb, 2, gb * N), F32)]
    out_shape += band_shape
    out_specs += [pl.BlockSpec((None,) + o.shape[1:], lambda i: (i, 0, 0)) for o in band_shape]
    return pl.pallas_call(
        functools.partial(_ssm_prep_kernel, gb=gb, chunk=chunk),
        grid=(G // gb,), in_specs=in_specs, out_specs=out_specs, out_shape=out_shape,
        compiler_params=_params("arbitrary"), name="ssm_prep")(*ins)


def _causal_conv3(v, w_ref, s_ref, *, tm, ts, hdr):
    s_ref[hdr:hdr + tm, :] = v
    xm1 = s_ref[hdr - ts:hdr - ts + tm, :]
    xm2 = s_ref[hdr - 2 * ts:hdr - 2 * ts + tm, :]
    return w_ref[0:1, :] * xm2 + w_ref[1:2, :] * xm1 + w_ref[2:3, :] * v


def _segment_conv3(v, w_ref, s_ref, *, nseg, seg, ts, hdr):
    out = [_causal_conv3(v[b * seg:(b + 1) * seg], w_ref, s_ref.at[b], tm=seg, ts=ts, hdr=hdr)
           for b in range(nseg)]
    return out[0] if nseg == 1 else jnp.concatenate(out, axis=0)


def _conv_state_begin(s_ref, prev_ref, *, ts, hdr):
    if prev_ref is not None:
        s_ref[:, hdr - 2 * ts:hdr, :] = prev_ref[...]
    else:
        @pl.when(pl.program_id(0) == 0)
        def _():
            s_ref[:, 0:hdr, :] = jnp.zeros((s_ref.shape[0], hdr, s_ref.shape[2]), F32)


def _conv_state_end(s_ref, buf_ref, *, seg, ts, hdr, carry):
    buf_ref[...] = s_ref[:, hdr + seg - 2 * ts:hdr + seg, :]
    if carry:
        s_ref[:, 0:hdr, :] = s_ref[:, seg:seg + hdr, :]


def _piece_transpose8(v):
    width = LANES // 8
    piece = lax.broadcasted_iota(jnp.int32, v[0].shape, 1) // width
    for bit in range(3):
        s = 1 << bit
        hi = ((piece >> bit) & 1) == 1
        nv = list(v)
        for i in range(8):
            if i & s:
                continue
            a, b = v[i], v[i + s]
            nv[i] = jnp.where(hi, pltpu.roll(b, s * width, axis=1), a)
            nv[i + s] = jnp.where(hi, b, pltpu.roll(a, LANES - s * width, axis=1))
        v = nv
    return v


def _tokens_to_groups(zu_ref, u_ref, *, nseg, seg):
    nc = seg // SSM_CHUNK
    for h in range(SSM_CHUNK // 8):
        for q in range(zu_ref.shape[0]):
            xs = [jnp.concatenate([zu_ref[q, pl.ds(SSM_CHUNK * c + 8 * h + i, nseg, stride=seg), :]
                                   for c in range(nc)], axis=0) for i in range(8)]
            w = _piece_transpose8(xs)
            for k in range(8):
                u_ref[8 * q + k, :, h * LANES:(h + 1) * LANES] = w[k].astype(u_ref.dtype)


def _groups_to_tokens(y_ref, ys_ref, *, nseg, seg):
    nc = seg // SSM_CHUNK
    for h in range(SSM_CHUNK // 8):
        for q in range(ys_ref.shape[0]):
            v = _piece_transpose8([y_ref[8 * q + k, :, h * LANES:(h + 1) * LANES].astype(F32) for k in range(8)])
            for i in range(8):
                for c in range(nc):
                    ys_ref[q, pl.ds(SSM_CHUNK * c + 8 * h + i, nseg, stride=seg), :] = v[i][c * nseg:(c + 1) * nseg]


def _mix_in_core(x, g_ref, w_ref, cw_ref, gc_ref, s_ref, *, nseg, seg, ts, hdr, ms, mc):
    hb = _rms(x, g_ref[...]).astype(BF16)
    zu = _dot(hb, w_ref[:, 0:ms])
    xin = _dot(hb, w_ref[:, ms:ms + mc])
    cg = _dot(hb, w_ref[:, ms + 2 * mc:ms + 3 * mc])
    conv = _segment_conv3(cg * xin, cw_ref, s_ref, nseg=nseg, seg=seg, ts=ts, hdr=hdr)
    bg = _dot(hb, w_ref[:, ms + mc:ms + 2 * mc])
    return zu, _rms(bg * conv, gc_ref[...]).astype(BF16)


def _mix_in_kernel(x_ref, g_ref, w_ref, cw_ref, gc_ref, u_ref, yc_ref, buf_ref, s_ref, zu_ref,
                   *, nseg, seg, hdr, ms, mc):
    tm = nseg * seg
    _conv_state_begin(s_ref, None, ts=1, hdr=hdr)
    zu, ycn = _mix_in_core(x_ref[...].reshape(tm, x_ref.shape[-1]), g_ref, w_ref, cw_ref, gc_ref, s_ref,
                           nseg=nseg, seg=seg, ts=1, hdr=hdr, ms=ms, mc=mc)
    for q in range(ms // LANES):
        zu_ref[q] = zu[:, q * LANES:(q + 1) * LANES]
    _tokens_to_groups(zu_ref, u_ref, nseg=nseg, seg=seg)
    yc_ref[...] = ycn.reshape(yc_ref.shape)
    _conv_state_end(s_ref, buf_ref, seg=seg, ts=1, hdr=hdr, carry=True)


def _mix_in(x3d, g, w_bf, conv_w, g_conv, *, seg, hdr, ms, mc):
    nseg, tlen, d = x3d.shape
    G = ms // SSM_GROUP_CH
    lp = SSM_CHUNK * SSM_GROUP_CH
    row_spec = lambda c: pl.BlockSpec((nseg, seg, c), lambda t: (0, t, 0))
    in_specs = [row_spec(d), _const_spec((1, d)), _const_spec(w_bf.shape), _const_spec(conv_w.shape),
                _const_spec((1, mc))]
    return pl.pallas_call(
        functools.partial(_mix_in_kernel, nseg=nseg, seg=seg, hdr=hdr, ms=ms, mc=mc),
        grid=(tlen // seg,), in_specs=in_specs,
        out_specs=[pl.BlockSpec((G, seg // SSM_CHUNK * nseg, lp), lambda t: (0, t, 0)), row_spec(mc),
                   pl.BlockSpec((nseg, 2, mc), lambda t: (0, 0, 0))],
        out_shape=[jax.ShapeDtypeStruct((G, tlen // SSM_CHUNK * nseg, lp), BF16),
                   jax.ShapeDtypeStruct((nseg, tlen, mc), BF16), jax.ShapeDtypeStruct((nseg, 2, mc), F32)],
        scratch_shapes=[pltpu.VMEM((nseg, hdr + seg, mc), F32), pltpu.VMEM((ms // LANES, nseg * seg, LANES), F32)],
        compiler_params=_params("arbitrary"), name="mix_in")(x3d, g.reshape(1, d), w_bf, conv_w,
                                                              g_conv.reshape(1, mc))


def _ssm_kernel(u_ref, m_ref, sre_ref, sim_ref, ore_ref, oim_ref, ap_ref, y_ref, fr_ref, fi_ref,
                lre, lim, ire, iim, *, gb, nb, nchunk):
    n = SSM_STATE
    for gi in range(gb):
        u = u_ref[gi]
        lre[gi] = _dot(u, sre_ref[gi])
        lim[gi] = _dot(u, sim_ref[gi])
    ar = [jnp.broadcast_to(ap_ref[gi, 0:1, :], (nb, n)) for gi in range(gb)]
    ai = [jnp.broadcast_to(ap_ref[gi, 1:2, :], (nb, n)) for gi in range(gb)]
    st0 = tuple((jnp.zeros((nb, n), F32), jnp.zeros((nb, n), F32)) for _ in range(gb))

    def step(c, st):
        r = pl.multiple_of(c * nb, nb)
        new = []
        for gi in range(gb):
            sr, si = st[gi]
            ire[gi, pl.ds(r, nb), :] = sr
            iim[gi, pl.ds(r, nb), :] = si
            nr = ar[gi] * sr - ai[gi] * si + lre[gi, pl.ds(r, nb), :]
            ni = ar[gi] * si + ai[gi] * sr + lim[gi, pl.ds(r, nb), :]
            new.append((nr, ni))
        return tuple(new)

    st = lax.fori_loop(0, nchunk, step, st0, unroll=4)
    for gi in range(gb):
        fr_ref[gi] = st[gi][0]
        fi_ref[gi] = st[gi][1]
        u = u_ref[gi]
        y = (_dot(u, m_ref[gi]) + _dot(ire[gi].astype(BF16), ore_ref[gi])
             + _dot(iim[gi].astype(BF16), oim_ref[gi]))
        y_ref[gi] = y.astype(y_ref.dtype)


def _ssm(u_g, mats, apow, *, nb, nchunk, gb=4):
    G, R, LP = u_g.shape
    n = SSM_STATE
    ins = [u_g, *mats, apow]
    gspec = lambda *s: pl.BlockSpec((gb,) + s, lambda i: (i, 0, 0))
    out_shape = [jax.ShapeDtypeStruct((G, R, LP), F32), jax.ShapeDtypeStruct((G, nb, n), F32),
                 jax.ShapeDtypeStruct((G, nb, n), F32)]
    return pl.pallas_call(
        functools.partial(_ssm_kernel, gb=gb, nb=nb, nchunk=nchunk),
        grid=(G // gb,), in_specs=[gspec(*a.shape[1:]) for a in ins],
        out_specs=[gspec(*o.shape[1:]) for o in out_shape],
        out_shape=out_shape, scratch_shapes=[pltpu.VMEM((gb, R, n), F32)] * 4,
        compiler_params=_params("arbitrary"), name="ssm_chunk")(*ins)


def _mix_out_core(x, ys, yc, wg_ref, bg_ref, gs_ref, wo_ref, gx_ref, wq_ref, *, ms, q_scale):
    y = jax.nn.gelu(ys)
    y = y * jax.nn.sigmoid(_dot(y.astype(BF16), wg_ref[...]) + bg_ref[...])
    ysn = _rms(y, gs_ref[...]).astype(BF16)
    x1 = x + _dot(ysn, wo_ref[0:ms, :]) + _dot(yc, wo_ref[ms:, :])
    h = _rms(x1, gx_ref[...]).astype(BF16)
    return x1, _dot(h, wq_ref[...]) * q_scale


def _mix_out_kernel(x_ref, ys_ref, yc_ref, wg_ref, bg_ref, gs_ref, wo_ref, gx_ref, wq_ref, x1_ref, q_ref, yt_ref,
                    *, nseg, seg, ms, q_scale):
    tm = nseg * seg
    _groups_to_tokens(ys_ref, yt_ref, nseg=nseg, seg=seg)
    ys = jnp.concatenate([yt_ref[q] for q in range(ms // LANES)], axis=1)
    x1, q = _mix_out_core(x_ref[...].reshape(tm, x_ref.shape[-1]), ys, yc_ref[...].reshape(tm, yc_ref.shape[-1]),
                          wg_ref, bg_ref, gs_ref, wo_ref, gx_ref, wq_ref, ms=ms, q_scale=q_scale)
    x1_ref[...] = x1.reshape(x1_ref.shape)
    q_ref[...] = q.astype(q_ref.dtype).reshape(q_ref.shape)


def _mix_out(x3d, ys, yc, w_glu, b_glu, g_ssm, w_out, g_x, w_q, *, seg):
    nseg, tlen, d = x3d.shape
    ms = w_glu.shape[0]
    hd = d // XATTN_HEADS
    row_spec = lambda c: pl.BlockSpec((nseg, seg, c), lambda t: (0, t, 0))
    ys_spec = pl.BlockSpec((ys.shape[0], seg // SSM_CHUNK * nseg, ys.shape[2]), lambda t: (0, t, 0))
    in_specs = [row_spec(d), ys_spec, row_spec(yc.shape[-1]), _const_spec(w_glu.shape), _const_spec((1, ms)),
                _const_spec((1, ms)), _const_spec(w_out.shape), _const_spec((1, d)), _const_spec(w_q.shape)]
    return pl.pallas_call(
        functools.partial(_mix_out_kernel, nseg=nseg, seg=seg, ms=ms, q_scale=hd ** -0.5),
        grid=(tlen // seg,), in_specs=in_specs, out_specs=[row_spec(d), row_spec(d)],
        out_shape=[jax.ShapeDtypeStruct((nseg, tlen, d), F32), jax.ShapeDtypeStruct((nseg, tlen, d), BF16)],
        scratch_shapes=[pltpu.VMEM((ms // LANES, nseg * seg, LANES), F32)],
        compiler_params=_params("parallel"), name="mix_out")(
            x3d, ys, yc, w_glu, b_glu.reshape(1, ms), g_ssm.reshape(1, ms), w_out, g_x.reshape(1, d), w_q)


def _step_mix_kernel(x_ref, g_ref, w_ref, cw_ref, gc_ref, prev_ref, s0r_ref, s0i_ref, bband_ref, cband_ref, arow_ref,
                     d_ref, wg_ref, bg_ref, gs_ref, wo_ref, gx_ref, wq_ref,
                     x1_ref, q_ref, buf_ref, fr_ref, fi_ref, s_ref, ys_ref, *, nseq, tlen, ms, mc, q_scale):
    rows = nseq * tlen
    _conv_state_begin(s_ref, prev_ref, ts=nseq, hdr=2 * nseq)
    x = x_ref[...]
    zu, ycn = _mix_in_core(x, g_ref, w_ref, cw_ref, gc_ref, s_ref, nseg=1, seg=rows, ts=nseq, hdr=2 * nseq,
                           ms=ms, mc=mc)
    _conv_state_end(s_ref, buf_ref, seg=rows, ts=nseq, hdr=2 * nseq, carry=False)

    wu, ws = bband_ref.shape[1], arow_ref.shape[2]
    for i in range(bband_ref.shape[0]):
        ucols, scols = slice(i * wu, (i + 1) * wu), slice(i * ws, (i + 1) * ws)
        sr, si = s0r_ref[:, scols], s0i_ref[:, scols]
        ar, ai = arow_ref[i, 0:1, :], arow_ref[i, 1:2, :]
        bband, cband = bband_ref[i].astype(BF16), cband_ref[i].astype(BF16)
        for t in range(tlen):
            ut = zu[t * nseq:(t + 1) * nseq, ucols]
            bu = _dot(ut.astype(BF16), bband)
            sr, si = ar * sr - ai * si + bu[:, 0:ws], ar * si + ai * sr + bu[:, ws:]
            cs = _dot(jnp.concatenate([sr, si], axis=1).astype(BF16), cband)
            ys_ref[t * nseq:(t + 1) * nseq, ucols] = cs + d_ref[:, ucols] * ut
        fr_ref[:, scols] = sr
        fi_ref[:, scols] = si

    x1, q = _mix_out_core(x, ys_ref[...], ycn, wg_ref, bg_ref, gs_ref, wo_ref, gx_ref, wq_ref, ms=ms, q_scale=q_scale)
    x1_ref[...] = x1
    q_ref[...] = q.astype(q_ref.dtype)


def _step_mix(x2d, p, conv_prev, s0_re, s0_im, bands, d_row, *, nseq, tlen):
    rows, d = x2d.shape
    ms = p["w_glu"].shape[0]
    mc = p["conv_w"].shape[1]
    hd = d // XATTN_HEADS
    bband, cband, arow = bands
    ins = [x2d, p["norm_mix"].reshape(1, d), p["w_in"], p["conv_w"], p["norm_conv_out"].reshape(1, mc), conv_prev,
           s0_re, s0_im, bband, cband, arow, d_row, p["w_glu"], p["b_glu"].reshape(1, ms),
           p["norm_ssm_out"].reshape(1, ms), p["w_out"], p["norm_xattn"].reshape(1, d), p["w_q"]]
    full = lambda shape: pl.BlockSpec(shape, lambda i: (0,) * len(shape))
    out_shape = [jax.ShapeDtypeStruct((rows, d), F32), jax.ShapeDtypeStruct((rows, d), F32),
                 jax.ShapeDtypeStruct(conv_prev.shape, F32), jax.ShapeDtypeStruct(s0_re.shape, F32),
                 jax.ShapeDtypeStruct(s0_im.shape, F32)]
    return pl.pallas_call(
        functools.partial(_step_mix_kernel, nseq=nseq, tlen=tlen, ms=ms, mc=mc, q_scale=hd ** -0.5),
        grid=(1,), in_specs=[_const_spec(a.shape) for a in ins], out_specs=[full(o.shape) for o in out_shape],
        out_shape=out_shape,
        scratch_shapes=[pltpu.VMEM((1, 2 * nseq + rows, mc), F32), pltpu.VMEM((rows, ms), F32)],
        compiler_params=_params("arbitrary"), name="step_mix")(*ins)


def _kv_kernel(m_ref, g_ref, wk_ref, wv_ref, k_ref, v_ref):
    h = _rms(m_ref[...], g_ref[...]).astype(BF16)
    k_ref[...] = _dot(h, wk_ref[...])
    v_ref[...] = _dot(h, wv_ref[...])


def _kv_proj(mem2d, g, w_k, w_v, *, tm):
    rows, d = mem2d.shape
    row_spec = pl.BlockSpec((tm, d), lambda i: (i, 0))
    return pl.pallas_call(
        _kv_kernel, grid=(rows // tm,),
        in_specs=[row_spec, _const_spec((1, d)), _const_spec(w_k.shape), _const_spec(w_v.shape)],
        out_specs=[row_spec, row_spec],
        out_shape=[jax.ShapeDtypeStruct((rows, d), F32)] * 2,
        compiler_params=_params("parallel"), name="kv_proj")(mem2d, g.reshape(1, d), w_k, w_v)


def _attn_kernel(q_ref, k_ref, v_ref, o_ref, *, sb, tq, hd):
    pairs = [(j, h) for j in range(sb) for h in range(XATTN_HEADS)]
    scores = []
    for j, h in pairs:
        qh = q_ref[j * tq:(j + 1) * tq, h * hd:(h + 1) * hd].astype(BF16)
        kh = k_ref[j * N_MEM:(j + 1) * N_MEM, h * hd:(h + 1) * hd].astype(BF16)
        scores.append(lax.dot_general(qh, kh, (((1,), (1,)), ((), ())), preferred_element_type=F32))
    s = jnp.concatenate(scores, axis=0)
    p = jnp.exp(s - jnp.max(s, axis=-1, keepdims=True))
    p = p / jnp.sum(p, axis=-1, keepdims=True)
    for i, (j, h) in enumerate(pairs):
        vh = v_ref[j * N_MEM:(j + 1) * N_MEM, h * hd:(h + 1) * hd].astype(BF16)
        ph = p[i * tq:(i + 1) * tq].astype(BF16)
        o_ref[j * tq:(j + 1) * tq, h * hd:(h + 1) * hd] = _dot(ph, vh).astype(o_ref.dtype)


def _attn(q2d, k2d, v2d, *, nseq, sb, tq, nq, o_dtype):
    rows, d = q2d.shape
    hd = d // XATTN_HEADS
    q_spec = pl.BlockSpec((sb * tq, d), lambda b, t: (b * nq + t, 0))
    kv_spec = pl.BlockSpec((sb * N_MEM, d), lambda b, t: (b, 0))
    return pl.pallas_call(
        functools.partial(_attn_kernel, sb=sb, tq=tq, hd=hd),
        grid=(nseq // sb, nq), in_specs=[q_spec, kv_spec, kv_spec], out_specs=q_spec,
        out_shape=jax.ShapeDtypeStruct((rows, d), o_dtype),
        compiler_params=_params("parallel", "arbitrary"), name="xattn")(q2d, k2d, v2d)


def _ffn_kernel(*refs, tm, ts, hdr, has_prev, dff):
    if has_prev:
        (x_ref, o_ref, wxo_ref, gf_ref, wup_ref, wgate_ref, cw_ref, wdn_ref, gl_ref, prev_ref,
         y_ref, buf_ref, s_ref) = refs
    else:
        (x_ref, o_ref, wxo_ref, gf_ref, wup_ref, wgate_ref, cw_ref, wdn_ref, gl_ref,
         y_ref, buf_ref, s_ref) = refs
    if has_prev:
        s_ref[hdr - 2 * ts:hdr, :] = prev_ref[...]
    else:
        @pl.when(pl.program_id(1) == 0)
        def _():
            s_ref[0:hdr, :] = jnp.zeros((hdr, dff), F32)

    x2 = x_ref[...] + _dot(o_ref[...].astype(BF16), wxo_ref[...])
    hb = _rms(x2, gf_ref[...]).astype(BF16)
    a = _causal_conv3(_dot(hb, wup_ref[...]), cw_ref, s_ref, tm=tm, ts=ts, hdr=hdr)
    act = (jax.nn.gelu(a) * _dot(hb, wgate_ref[...])).astype(BF16)
    x3 = x2 + _dot(act, wdn_ref[...])
    y_ref[...] = _rms(x3, gl_ref[...])
    tail = s_ref[hdr + tm - 2 * ts:hdr + tm, :]
    buf_ref[...] = tail.reshape(buf_ref.shape)
    if not has_prev:
        s_ref[0:hdr, :] = s_ref[tm:tm + hdr, :]


def _ffn(x2d, o2d, w_xo, g_ffn, w_up, w_gate, conv_w, w_down, g_last, prev, *, nb, nt, tm, ts, hdr):
    rows, d = x2d.shape
    dff = w_up.shape[1]
    has_prev = prev is not None
    row_spec = pl.BlockSpec((tm, d), lambda b, t: (b * nt + t, 0))
    in_specs = [row_spec, row_spec, _const_spec(w_xo.shape), _const_spec((1, d)), _const_spec(w_up.shape),
                _const_spec(w_gate.shape), _const_spec(conv_w.shape), _const_spec(w_down.shape),
                _const_spec((1, d))]
    ins = [x2d, o2d, w_xo, g_ffn.reshape(1, d), w_up, w_gate, conv_w, w_down, g_last.reshape(1, d)]
    if has_prev:
        in_specs.append(_const_spec(prev.shape))
        ins.append(prev)
        buf_shape = jax.ShapeDtypeStruct((2 * ts, dff), F32)
        buf_spec = pl.BlockSpec((2 * ts, dff), lambda b, t: (0, 0))
    else:
        buf_shape = jax.ShapeDtypeStruct((nb, 2 * ts, dff), F32)
        buf_spec = pl.BlockSpec((None, 2 * ts, dff), lambda b, t: (b, 0, 0))
    return pl.pallas_call(
        functools.partial(_ffn_kernel, tm=tm, ts=ts, hdr=hdr, has_prev=has_prev, dff=dff),
        grid=(nb, nt), in_specs=in_specs, out_specs=[row_spec, buf_spec],
        out_shape=[jax.ShapeDtypeStruct((rows, d), F32), buf_shape],
        scratch_shapes=[pltpu.VMEM((hdr + tm, dff), F32)],
        compiler_params=_params("parallel", "arbitrary"), name="conv_ffn")(*ins)


def _prompt_layer(x3d, k2d, v2d, p, mats, apow, g_last):
    nseq, tlen, d = x3d.shape
    rows = nseq * tlen
    ms = p["w_glu"].shape[0]
    mc = p["conv_w"].shape[1]
    u, ycn, conv_buf = _mix_in(x3d, p["norm_mix"], p["w_in"], p["conv_w"], p["norm_conv_out"],
                               seg=MIX_STEPS, hdr=8, ms=ms, mc=mc)
    y_g, f_re, f_im = _ssm(u, mats, apow, nb=nseq, nchunk=tlen // SSM_CHUNK)
    x1, q = _mix_out(x3d, y_g, ycn, p["w_glu"], p["b_glu"], p["norm_ssm_out"], p["w_out"], p["norm_xattn"],
                     p["w_q"], seg=MIX_STEPS)
    o = _attn(q.reshape(rows, d), k2d, v2d, nseq=nseq, sb=1, tq=ATTN_ROWS, nq=tlen // ATTN_ROWS, o_dtype=BF16)
    y, ffn_buf = _ffn(x1.reshape(rows, d), o, p["w_xo"], p["norm_ffn"], p["w_up"], p["w_gate"], p["ffn_conv_w"],
                      p["w_down"], g_last, None, nb=nseq, nt=tlen // FFN_ROWS, tm=FFN_ROWS, ts=1, hdr=8)
    return y, (f_re.transpose(1, 0, 2), f_im.transpose(1, 0, 2), conv_buf, ffn_buf)


def _step_layer(x_bm, k2d, v2d, prev, p, bands, d_row, g_last):
    nseq, tlen, d = x_bm.shape
    rows = nseq * tlen
    G, N = prev[0].shape[1:]
    to_rows = lambda a: a.transpose(1, 0, 2).reshape(a.shape[1] * nseq, a.shape[2])
    from_rows = lambda a, k: a.reshape(k, nseq, a.shape[-1]).transpose(1, 0, 2)
    x1, q, conv_buf, f_re, f_im = _step_mix(to_rows(x_bm), p, to_rows(prev[2])[None], prev[0].reshape(nseq, G * N),
                                            prev[1].reshape(nseq, G * N), bands, d_row, nseq=nseq, tlen=tlen)
    o = _attn(from_rows(q, tlen).reshape(rows, d), k2d, v2d, nseq=nseq, sb=ATTN_STEP_SEQS, tq=tlen, nq=1, o_dtype=F32)
    y, ffn_buf = _ffn(x1, to_rows(o.reshape(nseq, tlen, d)), p["w_xo"], p["norm_ffn"], p["w_up"], p["w_gate"],
                      p["ffn_conv_w"], p["w_down"], g_last, to_rows(prev[3]), nb=1, nt=1, tm=rows, ts=nseq,
                      hdr=2 * nseq)
    return from_rows(y, tlen), (f_re.reshape(nseq, G, N), f_im.reshape(nseq, G, N), from_rows(conv_buf[0], 2),
                                from_rows(ffn_buf, 2))


def kernel(x_prompt, x_sample, mem_prompt, cache_mem_k, cache_mem_v, state_ssm_re, state_ssm_im, state_conv, state_ffn_conv, norm_mix, w_in, ssm_A_re, ssm_A_im, ssm_log_dt, ssm_B_re, ssm_B_im, ssm_C_re, ssm_C_im, ssm_D, w_glu, b_glu, conv_w, norm_ssm_out, norm_conv_out, w_out, norm_xattn, norm_mem, w_q, w_k, w_v, w_xo, norm_ffn, w_up, w_gate, ffn_conv_w, w_down, norm_final):
    depth = w_in.shape[0]
    assert depth == 1, "the final norm is fused into the last (only) layer's ConvFFN kernel"
    nbp, tp, d = x_prompt.shape
    nbs = x_sample.shape[0]
    assert tp % ATTN_ROWS == 0 and tp % FFN_ROWS == 0 and tp % MIX_STEPS == 0 and nbp == 8
    hd = d // XATTN_HEADS

    xp, xs = x_prompt, x_sample
    outs_p, outs_s, mk_p, mv_p = [], [], [], []
    for l in range(depth):
        p = dict(norm_mix=norm_mix[l], w_in=w_in[l].astype(BF16), w_glu=w_glu[l].astype(BF16), b_glu=b_glu[l],
                 conv_w=conv_w[l], norm_ssm_out=norm_ssm_out[l], norm_conv_out=norm_conv_out[l],
                 w_out=w_out[l].astype(BF16), norm_xattn=norm_xattn[l], w_q=w_q[l].astype(BF16),
                 w_xo=w_xo[l].astype(BF16), norm_ffn=norm_ffn[l], w_up=w_up[l].astype(BF16),
                 w_gate=w_gate[l].astype(BF16), ffn_conv_w=ffn_conv_w[l], w_down=w_down[l].astype(BF16))
        *mats, apow, bband, cband, arow = _ssm_prep(ssm_A_re[l], ssm_A_im[l], ssm_log_dt[l], ssm_B_re[l],
                                                    ssm_B_im[l], ssm_C_re[l], ssm_C_im[l], ssm_D[l], chunk=SSM_CHUNK)

        k2d, v2d = _kv_proj(mem_prompt.reshape(nbp * N_MEM, d), norm_mem[l], w_k[l].astype(BF16),
                            w_v[l].astype(BF16), tm=ATTN_ROWS)
        mk_p.append(k2d.reshape(nbp, N_MEM, XATTN_HEADS, hd))
        mv_p.append(v2d.reshape(nbp, N_MEM, XATTN_HEADS, hd))

        xp, st_p = _prompt_layer(xp, k2d, v2d, p, mats, apow, norm_final)
        outs_p.append(st_p)
        xs, st_s = _step_layer(xs, cache_mem_k[l].reshape(nbs * N_MEM, d), cache_mem_v[l].reshape(nbs * N_MEM, d),
                               (state_ssm_re[l], state_ssm_im[l], state_conv[l], state_ffn_conv[l]), p,
                               (bband, cband, arow), ssm_D[l].reshape(1, -1), norm_final)
        outs_s.append(st_s)

    stack = lambda outs, i: jnp.stack([o[i] for o in outs])
    return (xp.reshape(nbp, tp, d), xs, jnp.stack(mk_p), jnp.stack(mv_p),
            stack(outs_p, 0), stack(outs_p, 1), stack(outs_p, 2), stack(outs_p, 3),
            stack(outs_s, 0), stack(outs_s, 1), stack(outs_s, 2), stack(outs_s, 3))
```

```python
import functools

import jax
import jax.numpy as jnp
from jax import lax
from jax.experimental import pallas as pl
from jax.experimental.pallas import tpu as pltpu

F32 = jnp.float32
BF16 = jnp.bfloat16

EPS = 1e-6
SSM_GROUP_CH = 16
SSM_STATE = 64
CONV_K = 3
N_MEM = 256
XATTN_HEADS = 4
SSM_CHUNK = 16
LANES = 128
MIX_STEPS = 128
ATTN_ROWS = 512
FFN_ROWS = 256
ATTN_STEP_SEQS = 4
V7X_VMEM_LIMIT = 56 * 1024 * 1024


def _rms(x, g):
    ms = jnp.mean(x * x, axis=-1, keepdims=True)
    return x * lax.rsqrt(ms + EPS) * g


def _dot(a, b):
    return jnp.dot(a, b, preferred_element_type=F32)


def _const_spec(shape):
    nd = len(shape)
    return pl.BlockSpec(shape, lambda *_: (0,) * nd, pipeline_mode=pl.Buffered(1))


def _params(*sem):
    return pltpu.CompilerParams(dimension_semantics=sem, vmem_limit_bytes=V7X_VMEM_LIMIT)


def _ssm_prep_kernel(ld_ref, lr_r_ref, li_r_ref, lr_c_ref, li_c_ref, brt_ref, bit_ref, crt_t_ref, cit_t_ref, d_ref,
                     m_ref, s_ref, o_ref, apow_ref, bband_ref, cband_ref, arow_ref, *, gb, chunk):
    P, N = SSM_GROUP_CH, SSM_STATE
    LP = chunk * P
    bband_ref[...] = jnp.zeros(bband_ref.shape, bband_ref.dtype)
    cband_ref[...] = jnp.zeros(cband_ref.shape, cband_ref.dtype)
    lane_j = lax.broadcasted_iota(jnp.int32, (1, LP), 1) // P
    row_j = lax.broadcasted_iota(jnp.int32, (LP, 1), 0) // P
    lane_i = lax.broadcasted_iota(jnp.int32, (P, LP), 1)
    row_i = lax.broadcasted_iota(jnp.int32, (P, LP), 0)
    nbits = chunk.bit_length()

    def squarings(lr, li, dt):
        mag = jnp.exp(dt * lr)
        pr, pi = mag * jnp.cos(dt * li), mag * jnp.sin(dt * li)
        out = [(pr, pi)]
        for _ in range(nbits - 1):
            pr, pi = pr * pr - pi * pi, 2.0 * pr * pi
            out.append((pr, pi))
        return out

    def cpow(pows, j):
        er = ei = None
        for b, (pr, pi) in enumerate(pows):
            if isinstance(j, int):
                if not (j >> b) & 1:
                    continue
                er, ei = (pr, pi) if er is None else (er * pr - ei * pi, er * pi + ei * pr)
            else:
                bit = ((j >> b) & 1) == 1
                if er is None:
                    er, ei = jnp.where(bit, pr, 1.0), jnp.where(bit, pi, 0.0)
                else:
                    er, ei = jnp.where(bit, er * pr - ei * pi, er), jnp.where(bit, er * pi + ei * pr, ei)
        return er, ei

    for gi in range(gb):
        dt = jnp.exp(ld_ref[gi])
        lr_r, li_r = lr_r_ref[gi], li_r_ref[gi]
        pows_r = squarings(lr_r, li_r, dt)
        pows_c = squarings(lr_c_ref[gi], li_c_ref[gi], dt)

        ar, ai = pows_r[0]
        den = lr_r * lr_r + li_r * li_r
        cr = ((ar - 1.0) * lr_r + ai * li_r) / den
        ci = (ai * lr_r - (ar - 1.0) * li_r) / den

        brt, bit = brt_ref[gi], bit_ref[gi]
        bbt_re = cr * brt - ci * bit
        bbt_im = cr * bit + ci * brt
        bbt_re_t = jnp.concatenate([bbt_re] * chunk, axis=0)
        bbt_im_t = jnp.concatenate([bbt_im] * chunk, axis=0)

        er, ei = cpow(pows_r, (chunk - 1) - row_j)
        s_ref[gi, :, 0:N] = er * bbt_re_t - ei * bbt_im_t
        s_ref[gi, :, N:2 * N] = er * bbt_im_t + ei * bbt_re_t

        crt_t, cit_t = crt_t_ref[gi], cit_t_ref[gi]
        er, ei = cpow(pows_c, lane_j)
        r_re = crt_t * er - cit_t * ei
        r_im = crt_t * ei + cit_t * er
        er, ei = cpow(pows_c, lane_j + 1)
        o_ref[gi, 0:N, :] = (crt_t * er - cit_t * ei).astype(o_ref.dtype)
        o_ref[gi, N:2 * N, :] = (-(crt_t * ei + cit_t * er)).astype(o_ref.dtype)

        krow = (jnp.dot(bbt_re, r_re, preferred_element_type=F32, precision=lax.Precision.HIGHEST)
                - jnp.dot(bbt_im, r_im, preferred_element_type=F32, precision=lax.Precision.HIGHEST))
        krow = krow + jnp.where(lane_i == row_i, d_ref[gi], 0.0)
        for ti in range(chunk):
            blk = krow if ti == 0 else pltpu.roll(krow, ti * P, axis=1)
            blk = jnp.where(lane_i >= ti * P, blk, 0.0)
            m_ref[gi, ti * P:(ti + 1) * P, :] = blk.astype(m_ref.dtype)

        apow_ref[gi] = jnp.concatenate(cpow(pows_r, chunk), axis=0)

        rows, cols = slice(gi * P, (gi + 1) * P), slice(gi * N, (gi + 1) * N)
        cols_im = slice((gb + gi) * N, (gb + gi + 1) * N)
        bband_ref[rows, cols] = bbt_re
        bband_ref[rows, cols_im] = bbt_im
        cband_ref[cols, rows] = crt_t[:, rows]
        cband_ref[cols_im, rows] = -cit_t[:, rows]
        arow_ref[0:1, cols] = ar
        arow_ref[1:2, cols] = ai


def _ssm_prep(A_re, A_im, log_dt, B_re, B_im, C_re, C_im, D, *, chunk, gb=8):
    G, N = A_re.shape
    P = SSM_GROUP_CH
    LP = chunk * P
    brt = B_re.transpose(0, 2, 1)
    bit = B_im.transpose(0, 2, 1)
    crt_t = jnp.tile(C_re.transpose(0, 2, 1), (1, 1, chunk))
    cit_t = jnp.tile(C_im.transpose(0, 2, 1), (1, 1, chunk))
    ins = [log_dt.reshape(G, 1, 1), A_re.reshape(G, 1, N), A_im.reshape(G, 1, N),
           A_re.reshape(G, N, 1), A_im.reshape(G, N, 1), brt, bit, crt_t, cit_t, D.reshape(G, P, 1)]
    gspec = lambda *s: pl.BlockSpec((gb,) + s, lambda i: (i, 0, 0))
    in_specs = [gspec(*a.shape[1:]) for a in ins]
    out_shape = [jax.ShapeDtypeStruct((G, LP, LP), BF16), jax.ShapeDtypeStruct((G, LP, 2 * N), F32),
                 jax.ShapeDtypeStruct((G, 2 * N, LP), BF16), jax.ShapeDtypeStruct((G, 2, N), F32)]
    out_specs = [gspec(*o.shape[1:]) for o in out_shape]
    band_shape = [jax.ShapeDtypeStruct((G // gb, gb * P, 2 * gb * N), F32),
                  jax.ShapeDtypeStruct((G // gb, 2 * gb * N, gb * P), F32),
                  jax.ShapeDtypeStruct((G // gb, 2, gb * N), F32)]
    out_shape += band_shape
    out_specs += [pl.BlockSpec((None,) + o.shape[1:], lambda i: (i, 0, 0)) for o in band_shape]
    return pl.pallas_call(
        functools.partial(_ssm_prep_kernel, gb=gb, chunk=chunk),
        grid=(G // gb,), in_specs=in_specs, out_specs=out_specs, out_shape=out_shape,
        compiler_params=_params("arbitrary"), name="ssm_prep")(*ins)


def _causal_conv3(v, w_ref, s_ref, *, tm, ts, hdr):
    s_ref[hdr:hdr + tm, :] = v
    xm1 = s_ref[hdr - ts:hdr - ts + tm, :]
    xm2 = s_ref[hdr - 2 * ts:hdr - 2 * ts + tm, :]
    return w_ref[0:1, :] * xm2 + w_ref[1:2, :] * xm1 + w_ref[2:3, :] * v


def _segment_conv3(v, w_ref, s_ref, *, nseg, seg, ts, hdr):
    out = [_causal_conv3(v[b * seg:(b + 1) * seg], w_ref, s_ref.at[b], tm=seg, ts=ts, hdr=hdr)
           for b in range(nseg)]
    return out[0] if nseg == 1 else jnp.concatenate(out, axis=0)


def _conv_state_begin(s_ref, prev_ref, *, ts, hdr):
    if prev_ref is not None:
        s_ref[:, hdr - 2 * ts:hdr, :] = prev_ref[...]
    else:
        @pl.when(pl.program_id(0) == 0)
        def _():
            s_ref[:, 0:hdr, :] = jnp.zeros((s_ref.shape[0], hdr, s_ref.shape[2]), F32)


def _conv_state_end(s_ref, buf_ref, *, seg, ts, hdr, carry):
    buf_ref[...] = s_ref[:, hdr + seg - 2 * ts:hdr + seg, :]
    if carry:
        s_ref[:, 0:hdr, :] = s_ref[:, seg:seg + hdr, :]


def _piece_transpose8(v):
    width = LANES // 8
    piece = lax.broadcasted_iota(jnp.int32, v[0].shape, 1) // width
    for bit in range(3):
        s = 1 << bit
        hi = ((piece >> bit) & 1) == 1
        nv = list(v)
        for i in range(8):
            if i & s:
                continue
            a, b = v[i], v[i + s]
            nv[i] = jnp.where(hi, pltpu.roll(b, s * width, axis=1), a)
            nv[i + s] = jnp.where(hi, b, pltpu.roll(a, LANES - s * width, axis=1))
        v = nv
    return v


def _tokens_to_groups(zu_ref, u_ref, *, nseg, seg):
    nc = seg // SSM_CHUNK
    for h in range(SSM_CHUNK // 8):
        for q in range(zu_ref.shape[0]):
            xs = [jnp.concatenate([zu_ref[q, pl.ds(SSM_CHUNK * c + 8 * h + i, nseg, stride=seg), :]
                                   for c in range(nc)], axis=0) for i in range(8)]
            w = _piece_transpose8(xs)
            for k in range(8):
                u_ref[8 * q + k, :, h * LANES:(h + 1) * LANES] = w[k].astype(u_ref.dtype)


def _groups_to_tokens(y_ref, ys_ref, *, nseg, seg):
    nc = seg // SSM_CHUNK
    for h in range(SSM_CHUNK // 8):
        for q in range(ys_ref.shape[0]):
            v = _piece_transpose8([y_ref[8 * q + k, :, h * LANES:(h + 1) * LANES].astype(F32) for k in range(8)])
            for i in range(8):
                for c in range(nc):
                    ys_ref[q, pl.ds(SSM_CHUNK * c + 8 * h + i, nseg, stride=seg), :] = v[i][c * nseg:(c + 1) * nseg]


def _mix_in_core(x, g_ref, w_ref, cw_ref, gc_ref, s_ref, *, nseg, seg, ts, hdr, ms, mc):
    hb = _rms(x, g_ref[...]).astype(BF16)
    zu = _dot(hb, w_ref[:, 0:ms])
    xin = _dot(hb, w_ref[:, ms:ms + mc])
    cg = _dot(hb, w_ref[:, ms + 2 * mc:ms + 3 * mc])
    conv = _segment_conv3(cg * xin, cw_ref, s_ref, nseg=nseg, seg=seg, ts=ts, hdr=hdr)
    bg = _dot(hb, w_ref[:, ms + mc:ms + 2 * mc])
    return zu, _rms(bg * conv, gc_ref[...]).astype(BF16)


def _mix_in_kernel(x_ref, g_ref, w_ref, cw_ref, gc_ref, u_ref, yc_ref, buf_ref, s_ref, zu_ref,
                   *, nseg, seg, hdr, ms, mc):
    tm = nseg * seg
    _conv_state_begin(s_ref, None, ts=1, hdr=hdr)
    zu, ycn = _mix_in_core(x_ref[...].reshape(tm, x_ref.shape[-1]), g_ref, w_ref, cw_ref, gc_ref, s_ref,
                           nseg=nseg, seg=seg, ts=1, hdr=hdr, ms=ms, mc=mc)
    for q in range(ms // LANES):
        zu_ref[q] = zu[:, q * LANES:(q + 1) * LANES]
    _tokens_to_groups(zu_ref, u_ref, nseg=nseg, seg=seg)
    yc_ref[...] = ycn.reshape(yc_ref.shape)
    _conv_state_end(s_ref, buf_ref, seg=seg, ts=1, hdr=hdr, carry=True)


def _mix_in(x3d, g, w_bf, conv_w, g_conv, *, seg, hdr, ms, mc):
    nseg, tlen, d = x3d.shape
    G = ms // SSM_GROUP_CH
    lp = SSM_CHUNK * SSM_GROUP_CH
    row_spec = lambda c: pl.BlockSpec((nseg, seg, c), lambda t: (0, t, 0))
    in_specs = [row_spec(d), _const_spec((1, d)), _const_spec(w_bf.shape), _const_spec(conv_w.shape),
                _const_spec((1, mc))]
    return pl.pallas_call(
        functools.partial(_mix_in_kernel, nseg=nseg, seg=seg, hdr=hdr, ms=ms, mc=mc),
        grid=(tlen // seg,), in_specs=in_specs,
        out_specs=[pl.BlockSpec((G, seg // SSM_CHUNK * nseg, lp), lambda t: (0, t, 0)), row_spec(mc),
                   pl.BlockSpec((nseg, 2, mc), lambda t: (0, 0, 0))],
        out_shape=[jax.ShapeDtypeStruct((G, tlen // SSM_CHUNK * nseg, lp), BF16),
                   jax.ShapeDtypeStruct((nseg, tlen, mc), BF16), jax.ShapeDtypeStruct((nseg, 2, mc), F32)],
        scratch_shapes=[pltpu.VMEM((nseg, hdr + seg, mc), F32), pltpu.VMEM((ms // LANES, nseg * seg, LANES), F32)],
        compiler_params=_params("arbitrary"), name="mix_in")(x3d, g.reshape(1, d), w_bf, conv_w,
                                                              g_conv.reshape(1, mc))


def _ssm_kernel(u_ref, m_ref, s_ref, o_ref, ap_ref, y_ref, fr_ref, fi_ref, loc, ini, *, gb, nb, nchunk):
    n = SSM_STATE
    for gi in range(gb):
        loc[gi] = _dot(u_ref[gi], s_ref[gi].astype(BF16))
    ar = [jnp.broadcast_to(ap_ref[gi, 0:1, :], (nb, n)) for gi in range(gb)]
    ai = [jnp.broadcast_to(ap_ref[gi, 1:2, :], (nb, n)) for gi in range(gb)]
    st0 = tuple((jnp.zeros((nb, n), F32), jnp.zeros((nb, n), F32)) for _ in range(gb))

    def step(c, st):
        r = pl.multiple_of(c * nb, nb)
        new = []
        for gi in range(gb):
            sr, si = st[gi]
            ini[gi, pl.ds(r, nb), 0:n] = sr
            ini[gi, pl.ds(r, nb), n:2 * n] = si
            nr = ar[gi] * sr - ai[gi] * si + loc[gi, pl.ds(r, nb), 0:n]
            ni = ar[gi] * si + ai[gi] * sr + loc[gi, pl.ds(r, nb), n:2 * n]
            new.append((nr, ni))
        return tuple(new)

    st = lax.fori_loop(0, nchunk, step, st0, unroll=4)
    for gi in range(gb):
        fr_ref[gi] = st[gi][0]
        fi_ref[gi] = st[gi][1]
        y = _dot(u_ref[gi], m_ref[gi]) + _dot(ini[gi].astype(BF16), o_ref[gi])
        y_ref[gi] = y.astype(y_ref.dtype)


def _ssm(u_g, mats, apow, *, nb, nchunk, gb=4):
    G, R, LP = u_g.shape
    n = SSM_STATE
    ins = [u_g, *mats, apow]
    gspec = lambda *s: pl.BlockSpec((gb,) + s, lambda i: (i, 0, 0))
    out_shape = [jax.ShapeDtypeStruct((G, R, LP), F32), jax.ShapeDtypeStruct((G, nb, n), F32),
                 jax.ShapeDtypeStruct((G, nb, n), F32)]
    return pl.pallas_call(
        functools.partial(_ssm_kernel, gb=gb, nb=nb, nchunk=nchunk),
        grid=(G // gb,), in_specs=[gspec(*a.shape[1:]) for a in ins],
        out_specs=[gspec(*o.shape[1:]) for o in out_shape],
        out_shape=out_shape, scratch_shapes=[pltpu.VMEM((gb, R, 2 * n), F32)] * 2,
        compiler_params=_params("arbitrary"), name="ssm_chunk")(*ins)


def _mix_out_core(x, ys, yc, wg_ref, bg_ref, gs_ref, wo_ref, gx_ref, wq_ref, *, ms, q_scale):
    y = jax.nn.gelu(ys)
    y = y * jax.nn.sigmoid(_dot(y.astype(BF16), wg_ref[...]) + bg_ref[...])
    ysn = _rms(y, gs_ref[...]).astype(BF16)
    x1 = x + _dot(jnp.concatenate([ysn, yc], axis=1), wo_ref[...])
    h = _rms(x1, gx_ref[...]).astype(BF16)
    return x1, _dot(h, wq_ref[...]) * q_scale


def _mix_out_kernel(x_ref, ys_ref, yc_ref, wg_ref, bg_ref, gs_ref, wo_ref, gx_ref, wq_ref, x1_ref, q_ref, yt_ref,
                    *, nseg, seg, ms, q_scale):
    tm = nseg * seg
    _groups_to_tokens(ys_ref, yt_ref, nseg=nseg, seg=seg)
    ys = jnp.concatenate([yt_ref[q] for q in range(ms // LANES)], axis=1)
    x1, q = _mix_out_core(x_ref[...].reshape(tm, x_ref.shape[-1]), ys, yc_ref[...].reshape(tm, yc_ref.shape[-1]),
                          wg_ref, bg_ref, gs_ref, wo_ref, gx_ref, wq_ref, ms=ms, q_scale=q_scale)
    x1_ref[...] = x1.reshape(x1_ref.shape)
    q_ref[...] = q.astype(q_ref.dtype).reshape(q_ref.shape)


def _mix_out(x3d, ys, yc, w_glu, b_glu, g_ssm, w_out, g_x, w_q, *, seg):
    nseg, tlen, d = x3d.shape
    ms = w_glu.shape[0]
    hd = d // XATTN_HEADS
    row_spec = lambda c: pl.BlockSpec((nseg, seg, c), lambda t: (0, t, 0))
    ys_spec = pl.BlockSpec((ys.shape[0], seg // SSM_CHUNK * nseg, ys.shape[2]), lambda t: (0, t, 0))
    in_specs = [row_spec(d), ys_spec, row_spec(yc.shape[-1]), _const_spec(w_glu.shape), _const_spec((1, ms)),
                _const_spec((1, ms)), _const_spec(w_out.shape), _const_spec((1, d)), _const_spec(w_q.shape)]
    return pl.pallas_call(
        functools.partial(_mix_out_kernel, nseg=nseg, seg=seg, ms=ms, q_scale=hd ** -0.5),
        grid=(tlen // seg,), in_specs=in_specs, out_specs=[row_spec(d), row_spec(d)],
        out_shape=[jax.ShapeDtypeStruct((nseg, tlen, d), F32), jax.ShapeDtypeStruct((nseg, tlen, d), BF16)],
        scratch_shapes=[pltpu.VMEM((ms // LANES, nseg * seg, LANES), F32)],
        compiler_params=_params("parallel"), name="mix_out")(
            x3d, ys, yc, w_glu, b_glu.reshape(1, ms), g_ssm.reshape(1, ms), w_out, g_x.reshape(1, d), w_q)


def _step_mix_kernel(x_ref, g_ref, w_ref, cw_ref, gc_ref, prev_ref, s0r_ref, s0i_ref, bband_ref, cband_ref, arow_ref,
                     d_ref, wg_ref, bg_ref, gs_ref, wo_ref, gx_ref, wq_ref,
                     x1_ref, q_ref, buf_ref, fr_ref, fi_ref, s_ref, ys_ref, *, nseq, tlen, ms, mc, q_scale):
    rows = nseq * tlen
    _conv_state_begin(s_ref, prev_ref, ts=nseq, hdr=2 * nseq)
    x = x_ref[...]
    zu, ycn = _mix_in_core(x, g_ref, w_ref, cw_ref, gc_ref, s_ref, nseg=1, seg=rows, ts=nseq, hdr=2 * nseq,
                           ms=ms, mc=mc)
    _conv_state_end(s_ref, buf_ref, seg=rows, ts=nseq, hdr=2 * nseq, carry=False)

    wu, ws = bband_ref.shape[1], arow_ref.shape[2]
    for i in range(bband_ref.shape[0]):
        ucols, scols = slice(i * wu, (i + 1) * wu), slice(i * ws, (i + 1) * ws)
        sr, si = s0r_ref[:, scols], s0i_ref[:, scols]
        ar, ai = arow_ref[i, 0:1, :], arow_ref[i, 1:2, :]
        bband, cband = bband_ref[i].astype(BF16), cband_ref[i].astype(BF16)
        for t in range(tlen):
            ut = zu[t * nseq:(t + 1) * nseq, ucols]
            bu = _dot(ut.astype(BF16), bband)
            sr, si = ar * sr - ai * si + bu[:, 0:ws], ar * si + ai * sr + bu[:, ws:]
            cs = _dot(jnp.concatenate([sr, si], axis=1).astype(BF16), cband)
            ys_ref[t * nseq:(t + 1) * nseq, ucols] = cs + d_ref[:, ucols] * ut
        fr_ref[:, scols] = sr
        fi_ref[:, scols] = si

    x1, q = _mix_out_core(x, ys_ref[...], ycn, wg_ref, bg_ref, gs_ref, wo_ref, gx_ref, wq_ref, ms=ms, q_scale=q_scale)
    x1_ref[...] = x1
    q_ref[...] = q.astype(q_ref.dtype)


def _step_mix(x2d, p, conv_prev, s0_re, s0_im, bands, d_row, *, nseq, tlen):
    rows, d = x2d.shape
    ms = p["w_glu"].shape[0]
    mc = p["conv_w"].shape[1]
    hd = d // XATTN_HEADS
    bband, cband, arow = bands
    ins = [x2d, p["norm_mix"].reshape(1, d), p["w_in"], p["conv_w"], p["norm_conv_out"].reshape(1, mc), conv_prev,
           s0_re, s0_im, bband, cband, arow, d_row, p["w_glu"], p["b_glu"].reshape(1, ms),
           p["norm_ssm_out"].reshape(1, ms), p["w_out"], p["norm_xattn"].reshape(1, d), p["w_q"]]
    full = lambda shape: pl.BlockSpec(shape, lambda i: (0,) * len(shape))
    out_shape = [jax.ShapeDtypeStruct((rows, d), F32), jax.ShapeDtypeStruct((rows, d), F32),
                 jax.ShapeDtypeStruct(conv_prev.shape, F32), jax.ShapeDtypeStruct(s0_re.shape, F32),
                 jax.ShapeDtypeStruct(s0_im.shape, F32)]
    return pl.pallas_call(
        functools.partial(_step_mix_kernel, nseq=nseq, tlen=tlen, ms=ms, mc=mc, q_scale=hd ** -0.5),
        grid=(1,), in_specs=[_const_spec(a.shape) for a in ins], out_specs=[full(o.shape) for o in out_shape],
        out_shape=out_shape,
        scratch_shapes=[pltpu.VMEM((1, 2 * nseq + rows, mc), F32), pltpu.VMEM((rows, ms), F32)],
        compiler_params=_params("arbitrary"), name="step_mix")(*ins)


def _kv_kernel(m_ref, g_ref, wk_ref, wv_ref, k_ref, v_ref):
    h = _rms(m_ref[...], g_ref[...]).astype(BF16)
    k_ref[...] = _dot(h, wk_ref[...])
    v_ref[...] = _dot(h, wv_ref[...])


def _kv_proj(mem2d, g, w_k, w_v, *, tm):
    rows, d = mem2d.shape
    row_spec = pl.BlockSpec((tm, d), lambda i: (i, 0))
    return pl.pallas_call(
        _kv_kernel, grid=(rows // tm,),
        in_specs=[row_spec, _const_spec((1, d)), _const_spec(w_k.shape), _const_spec(w_v.shape)],
        out_specs=[row_spec, row_spec],
        out_shape=[jax.ShapeDtypeStruct((rows, d), F32)] * 2,
        compiler_params=_params("parallel"), name="kv_proj")(mem2d, g.reshape(1, d), w_k, w_v)


def _softmax_attention(qs, ks, vs):
    tq = qs[0].shape[0]
    s = jnp.concatenate([lax.dot_general(q, k, (((1,), (1,)), ((), ())), preferred_element_type=F32)
                         for q, k in zip(qs, ks)], axis=0)
    p = jnp.exp(s - jnp.max(s, axis=-1, keepdims=True))
    p = p / jnp.sum(p, axis=-1, keepdims=True)
    return [_dot(p[i * tq:(i + 1) * tq].astype(BF16), v) for i, v in enumerate(vs)]


def _head_slices(ref, rows, hd):
    return [ref[rows, h * hd:(h + 1) * hd].astype(BF16) for h in range(XATTN_HEADS)]


def _attn_kernel(q_ref, k_ref, v_ref, o_ref, *, sb, tq, hd):
    qs, ks, vs = [], [], []
    for j in range(sb):
        qs += _head_slices(q_ref, slice(j * tq, (j + 1) * tq), hd)
        ks += _head_slices(k_ref, slice(j * N_MEM, (j + 1) * N_MEM), hd)
        vs += _head_slices(v_ref, slice(j * N_MEM, (j + 1) * N_MEM), hd)
    outs = _softmax_attention(qs, ks, vs)
    for j in range(sb):
        heads = outs[j * XATTN_HEADS:(j + 1) * XATTN_HEADS]
        o_ref[j * tq:(j + 1) * tq, :] = jnp.concatenate(heads, axis=1).astype(o_ref.dtype)


def _attn(q2d, k2d, v2d, *, nseq, sb, tq, nq, o_dtype):
    rows, d = q2d.shape
    hd = d // XATTN_HEADS
    q_spec = pl.BlockSpec((sb * tq, d), lambda b, t: (b * nq + t, 0))
    kv_spec = pl.BlockSpec((sb * N_MEM, d), lambda b, t: (b, 0))
    return pl.pallas_call(
        functools.partial(_attn_kernel, sb=sb, tq=tq, hd=hd),
        grid=(nseq // sb, nq), in_specs=[q_spec, kv_spec, kv_spec], out_specs=q_spec,
        out_shape=jax.ShapeDtypeStruct((rows, d), o_dtype),
        compiler_params=_params("parallel", "arbitrary"), name="xattn")(q2d, k2d, v2d)


def _ffn_kernel(*refs, tm, ts, hdr, has_prev, has_kv, dff):
    x_ref, o_ref = refs[:2]
    refs = refs[2:]
    if has_kv:
        k_ref, v_ref = refs[:2]
        refs = refs[2:]
    wxo_ref, gf_ref, wup_ref, wgate_ref, cw_ref, wdn_ref, gl_ref = refs[:7]
    refs = refs[7:]
    if has_prev:
        prev_ref, refs = refs[0], refs[1:]
    y_ref, buf_ref, s_ref = refs
    if has_prev:
        s_ref[hdr - 2 * ts:hdr, :] = prev_ref[...]
    else:
        @pl.when(pl.program_id(1) == 0)
        def _():
            s_ref[0:hdr, :] = jnp.zeros((hdr, dff), F32)

    if has_kv:
        hd = o_ref.shape[1] // XATTN_HEADS
        rows = slice(None)
        heads = _softmax_attention(_head_slices(o_ref, rows, hd), _head_slices(k_ref, rows, hd),
                                   _head_slices(v_ref, rows, hd))
        o = jnp.concatenate(heads, axis=1).astype(BF16)
    else:
        o = o_ref[...].astype(BF16)
    x2 = x_ref[...] + _dot(o, wxo_ref[...])
    hb = _rms(x2, gf_ref[...]).astype(BF16)
    a = _causal_conv3(_dot(hb, wup_ref[...]), cw_ref, s_ref, tm=tm, ts=ts, hdr=hdr)
    act = (jax.nn.gelu(a) * _dot(hb, wgate_ref[...])).astype(BF16)
    x3 = x2 + _dot(act, wdn_ref[...])
    y_ref[...] = _rms(x3, gl_ref[...])
    tail = s_ref[hdr + tm - 2 * ts:hdr + tm, :]
    buf_ref[...] = tail.reshape(buf_ref.shape)
    if not has_prev:
        s_ref[0:hdr, :] = s_ref[tm:tm + hdr, :]


def _ffn(x2d, o2d, kv, w_xo, g_ffn, w_up, w_gate, conv_w, w_down, g_last, prev, *, nb, nt, tm, ts, hdr):
    rows, d = x2d.shape
    dff = w_up.shape[1]
    has_prev = prev is not None
    row_spec = pl.BlockSpec((tm, d), lambda b, t: (b * nt + t, 0))
    in_specs = [row_spec, row_spec]
    ins = [x2d, o2d]
    if kv is not None:
        in_specs += [pl.BlockSpec((N_MEM, d), lambda b, t: (b, 0))] * 2
        ins += list(kv)
    in_specs += [_const_spec(w_xo.shape), _const_spec((1, d)), _const_spec(w_up.shape), _const_spec(w_gate.shape),
                 _const_spec(conv_w.shape), _const_spec(w_down.shape), _const_spec((1, d))]
    ins += [w_xo, g_ffn.reshape(1, d), w_up, w_gate, conv_w, w_down, g_last.reshape(1, d)]
    if has_prev:
        in_specs.append(_const_spec(prev.shape))
        ins.append(prev)
        buf_shape = jax.ShapeDtypeStruct((2 * ts, dff), F32)
        buf_spec = pl.BlockSpec((2 * ts, dff), lambda b, t: (0, 0))
    else:
        buf_shape = jax.ShapeDtypeStruct((nb, 2 * ts, dff), F32)
        buf_spec = pl.BlockSpec((None, 2 * ts, dff), lambda b, t: (b, 0, 0))
    return pl.pallas_call(
        functools.partial(_ffn_kernel, tm=tm, ts=ts, hdr=hdr, has_prev=has_prev, has_kv=kv is not None, dff=dff),
        grid=(nb, nt), in_specs=in_specs, out_specs=[row_spec, buf_spec],
        out_shape=[jax.ShapeDtypeStruct((rows, d), F32), buf_shape],
        scratch_shapes=[pltpu.VMEM((hdr + tm, dff), F32)],
        compiler_params=_params("parallel", "arbitrary"), name="conv_ffn")(*ins)


def _prompt_layer(x3d, k2d, v2d, p, mats, apow, g_last):
    nseq, tlen, d = x3d.shape
    rows = nseq * tlen
    ms = p["w_glu"].shape[0]
    mc = p["conv_w"].shape[1]
    u, ycn, conv_buf = _mix_in(x3d, p["norm_mix"], p["w_in"], p["conv_w"], p["norm_conv_out"],
                               seg=MIX_STEPS, hdr=8, ms=ms, mc=mc)
    y_g, f_re, f_im = _ssm(u, mats, apow, nb=nseq, nchunk=tlen // SSM_CHUNK)
    x1, q = _mix_out(x3d, y_g, ycn, p["w_glu"], p["b_glu"], p["norm_ssm_out"], p["w_out"], p["norm_xattn"],
                     p["w_q"], seg=MIX_STEPS)
    y, ffn_buf = _ffn(x1.reshape(rows, d), q.reshape(rows, d), (k2d, v2d), p["w_xo"], p["norm_ffn"], p["w_up"],
                      p["w_gate"], p["ffn_conv_w"],
                      p["w_down"], g_last, None, nb=nseq, nt=tlen // FFN_ROWS, tm=FFN_ROWS, ts=1, hdr=8)
    return y, (f_re.transpose(1, 0, 2), f_im.transpose(1, 0, 2), conv_buf, ffn_buf)


def _step_layer(x_bm, k2d, v2d, prev, p, bands, d_row, g_last):
    nseq, tlen, d = x_bm.shape
    rows = nseq * tlen
    G, N = prev[0].shape[1:]
    to_rows = lambda a: a.transpose(1, 0, 2).reshape(a.shape[1] * nseq, a.shape[2])
    from_rows = lambda a, k: a.reshape(k, nseq, a.shape[-1]).transpose(1, 0, 2)
    x1, q, conv_buf, f_re, f_im = _step_mix(to_rows(x_bm), p, to_rows(prev[2])[None], prev[0].reshape(nseq, G * N),
                                            prev[1].reshape(nseq, G * N), bands, d_row, nseq=nseq, tlen=tlen)
    o = _attn(from_rows(q, tlen).reshape(rows, d), k2d, v2d, nseq=nseq, sb=ATTN_STEP_SEQS, tq=tlen, nq=1, o_dtype=F32)
    y, ffn_buf = _ffn(x1, to_rows(o.reshape(nseq, tlen, d)), None, p["w_xo"], p["norm_ffn"], p["w_up"], p["w_gate"],
                      p["ffn_conv_w"], p["w_down"], g_last, to_rows(prev[3]), nb=1, nt=1, tm=rows, ts=nseq,
                      hdr=2 * nseq)
    return from_rows(y, tlen), (f_re.reshape(nseq, G, N), f_im.reshape(nseq, G, N), from_rows(conv_buf[0], 2),
                                from_rows(ffn_buf, 2))


def kernel(x_prompt, x_sample, mem_prompt, cache_mem_k, cache_mem_v, state_ssm_re, state_ssm_im, state_conv, state_ffn_conv, norm_mix, w_in, ssm_A_re, ssm_A_im, ssm_log_dt, ssm_B_re, ssm_B_im, ssm_C_re, ssm_C_im, ssm_D, w_glu, b_glu, conv_w, norm_ssm_out, norm_conv_out, w_out, norm_xattn, norm_mem, w_q, w_k, w_v, w_xo, norm_ffn, w_up, w_gate, ffn_conv_w, w_down, norm_final):
    depth = w_in.shape[0]
    assert depth == 1, "the final norm is fused into the last (only) layer's ConvFFN kernel"
    nbp, tp, d = x_prompt.shape
    nbs = x_sample.shape[0]
    assert tp % ATTN_ROWS == 0 and tp % FFN_ROWS == 0 and tp % MIX_STEPS == 0 and nbp == 8
    hd = d // XATTN_HEADS

    xp, xs = x_prompt, x_sample
    outs_p, outs_s, mk_p, mv_p = [], [], [], []
    for l in range(depth):
        p = dict(norm_mix=norm_mix[l], w_in=w_in[l].astype(BF16), w_glu=w_glu[l].astype(BF16), b_glu=b_glu[l],
                 conv_w=conv_w[l], norm_ssm_out=norm_ssm_out[l], norm_conv_out=norm_conv_out[l],
                 w_out=w_out[l].astype(BF16), norm_xattn=norm_xattn[l], w_q=w_q[l].astype(BF16),
                 w_xo=w_xo[l].astype(BF16), norm_ffn=norm_ffn[l], w_up=w_up[l].astype(BF16),
                 w_gate=w_gate[l].astype(BF16), ffn_conv_w=ffn_conv_w[l], w_down=w_down[l].astype(BF16))
        *mats, apow, bband, cband, arow = _ssm_prep(ssm_A_re[l], ssm_A_im[l], ssm_log_dt[l], ssm_B_re[l],
                                                    ssm_B_im[l], ssm_C_re[l], ssm_C_im[l], ssm_D[l], chunk=SSM_CHUNK)

        k2d, v2d = _kv_proj(mem_prompt.reshape(nbp * N_MEM, d), norm_mem[l], w_k[l].astype(BF16),
                            w_v[l].astype(BF16), tm=ATTN_ROWS)
        mk_p.append(k2d.reshape(nbp, N_MEM, XATTN_HEADS, hd))
        mv_p.append(v2d.reshape(nbp, N_MEM, XATTN_HEADS, hd))

        xp, st_p = _prompt_layer(xp, k2d, v2d, p, mats, apow, norm_final)
        outs_p.append(st_p)
        xs, st_s = _step_layer(xs, cache_mem_k[l].reshape(nbs * N_MEM, d), cache_mem_v[l].reshape(nbs * N_MEM, d),
                               (state_ssm_re[l], state_ssm_im[l], state_conv[l], state_ffn_conv[l]), p,
                               (bband, cband, arow), ssm_D[l].reshape(1, -1), norm_final)
        outs_s.append(st_s)

    stack = lambda outs, i: jnp.stack([o[i] for o in outs])
    return (xp.reshape(nbp, tp, d), xs, jnp.stack(mk_p), jnp.stack(mv_p),
            stack(outs_p, 0), stack(outs_p, 1), stack(outs_p, 2), stack(outs_p, 3),
            stack(outs_s, 0), stack(outs_s, 1), stack(outs_s, 2), stack(outs_s, 3))
```

```python
import functools

import jax
import jax.numpy as jnp
from jax import lax
from jax.experimental import pallas as pl
from jax.experimental.pallas import tpu as pltpu

F32 = jnp.float32
BF16 = jnp.bfloat16

EPS = 1e-6
SSM_GROUP_CH = 16
SSM_STATE = 64
CONV_K = 3
N_MEM = 256
XATTN_HEADS = 4
SSM_CHUNK = 16
LANES = 128
MIX_STEPS = 128
MIX_SUBBLOCKS = 2
ATTN_ROWS = 512
FFN_ROWS = 256
ATTN_STEP_SEQS = 4
V7X_VMEM_LIMIT = 56 * 1024 * 1024


def _rms(x, g):
    ms = jnp.mean(x * x, axis=-1, keepdims=True)
    return x * lax.rsqrt(ms + EPS) * g


def _dot(a, b):
    return jnp.dot(a, b, preferred_element_type=F32)


def _const_spec(shape):
    nd = len(shape)
    return pl.BlockSpec(shape, lambda *_: (0,) * nd, pipeline_mode=pl.Buffered(1))


def _params(*sem):
    return pltpu.CompilerParams(dimension_semantics=sem, vmem_limit_bytes=V7X_VMEM_LIMIT)


def _ssm_prep_kernel(ld_ref, lr_r_ref, li_r_ref, lr_c_ref, li_c_ref, brt_ref, bit_ref, crt_t_ref, cit_t_ref, d_ref,
                     m_ref, s_ref, o_ref, apow_ref, bband_ref, cband_ref, arow_ref, *, gb, chunk):
    P, N = SSM_GROUP_CH, SSM_STATE
    LP = chunk * P
    bband_ref[...] = jnp.zeros(bband_ref.shape, bband_ref.dtype)
    cband_ref[...] = jnp.zeros(cband_ref.shape, cband_ref.dtype)
    lane_j = lax.broadcasted_iota(jnp.int32, (1, LP), 1) // P
    row_j = lax.broadcasted_iota(jnp.int32, (LP, 1), 0) // P
    lane_i = lax.broadcasted_iota(jnp.int32, (P, LP), 1)
    row_i = lax.broadcasted_iota(jnp.int32, (P, LP), 0)
    nbits = chunk.bit_length()

    def squarings(lr, li, dt):
        mag = jnp.exp(dt * lr)
        pr, pi = mag * jnp.cos(dt * li), mag * jnp.sin(dt * li)
        out = [(pr, pi)]
        for _ in range(nbits - 1):
            pr, pi = pr * pr - pi * pi, 2.0 * pr * pi
            out.append((pr, pi))
        return out

    def cpow(pows, j):
        er = ei = None
        for b, (pr, pi) in enumerate(pows):
            if isinstance(j, int):
                if not (j >> b) & 1:
                    continue
                er, ei = (pr, pi) if er is None else (er * pr - ei * pi, er * pi + ei * pr)
            else:
                bit = ((j >> b) & 1) == 1
                if er is None:
                    er, ei = jnp.where(bit, pr, 1.0), jnp.where(bit, pi, 0.0)
                else:
                    er, ei = jnp.where(bit, er * pr - ei * pi, er), jnp.where(bit, er * pi + ei * pr, ei)
        return er, ei

    for gi in range(gb):
        dt = jnp.exp(ld_ref[gi])
        lr_r, li_r = lr_r_ref[gi], li_r_ref[gi]
        pows_r = squarings(lr_r, li_r, dt)
        pows_c = squarings(lr_c_ref[gi], li_c_ref[gi], dt)

        ar, ai = pows_r[0]
        den = lr_r * lr_r + li_r * li_r
        cr = ((ar - 1.0) * lr_r + ai * li_r) / den
        ci = (ai * lr_r - (ar - 1.0) * li_r) / den

        brt, bit = brt_ref[gi], bit_ref[gi]
        bbt_re = cr * brt - ci * bit
        bbt_im = cr * bit + ci * brt
        bbt_re_t = jnp.concatenate([bbt_re] * chunk, axis=0)
        bbt_im_t = jnp.concatenate([bbt_im] * chunk, axis=0)

        er, ei = cpow(pows_r, (chunk - 1) - row_j)
        s_ref[gi, :, 0:N] = er * bbt_re_t - ei * bbt_im_t
        s_ref[gi, :, N:2 * N] = er * bbt_im_t + ei * bbt_re_t

        crt_t, cit_t = crt_t_ref[gi], cit_t_ref[gi]
        er, ei = cpow(pows_c, lane_j)
        r_re = crt_t * er - cit_t * ei
        r_im = crt_t * ei + cit_t * er
        er, ei = cpow(pows_c, lane_j + 1)
        o_ref[gi, 0:N, :] = (crt_t * er - cit_t * ei).astype(o_ref.dtype)
        o_ref[gi, N:2 * N, :] = (-(crt_t * ei + cit_t * er)).astype(o_ref.dtype)

        krow = (jnp.dot(bbt_re, r_re, preferred_element_type=F32, precision=lax.Precision.HIGHEST)
                - jnp.dot(bbt_im, r_im, preferred_element_type=F32, precision=lax.Precision.HIGHEST))
        krow = krow + jnp.where(lane_i == row_i, d_ref[gi], 0.0)
        for ti in range(chunk):
            blk = krow if ti == 0 else pltpu.roll(krow, ti * P, axis=1)
            blk = jnp.where(lane_i >= ti * P, blk, 0.0)
            m_ref[gi, ti * P:(ti + 1) * P, :] = blk.astype(m_ref.dtype)

        apow_ref[gi] = jnp.concatenate(cpow(pows_r, chunk), axis=0)

        rows, cols = slice(gi * P, (gi + 1) * P), slice(gi * N, (gi + 1) * N)
        cols_im = slice((gb + gi) * N, (gb + gi + 1) * N)
        bband_ref[rows, cols] = bbt_re
        bband_ref[rows, cols_im] = bbt_im
        cband_ref[cols, rows] = crt_t[:, rows]
        cband_ref[cols_im, rows] = -cit_t[:, rows]
        arow_ref[0:1, cols] = ar
        arow_ref[1:2, cols] = ai


def _ssm_prep(A_re, A_im, log_dt, B_re, B_im, C_re, C_im, D, *, chunk, gb=8):
    G, N = A_re.shape
    P = SSM_GROUP_CH
    LP = chunk * P
    brt = B_re.transpose(0, 2, 1)
    bit = B_im.transpose(0, 2, 1)
    crt_t = jnp.tile(C_re.transpose(0, 2, 1), (1, 1, chunk))
    cit_t = jnp.tile(C_im.transpose(0, 2, 1), (1, 1, chunk))
    ins = [log_dt.reshape(G, 1, 1), A_re.reshape(G, 1, N), A_im.reshape(G, 1, N),
           A_re.reshape(G, N, 1), A_im.reshape(G, N, 1), brt, bit, crt_t, cit_t, D.reshape(G, P, 1)]
    gspec = lambda *s: pl.BlockSpec((gb,) + s, lambda i: (i, 0, 0))
    in_specs = [gspec(*a.shape[1:]) for a in ins]
    out_shape = [jax.ShapeDtypeStruct((G, LP, LP), BF16), jax.ShapeDtypeStruct((G, LP, 2 * N), F32),
                 jax.ShapeDtypeStruct((G, 2 * N, LP), BF16), jax.ShapeDtypeStruct((G, 2, N), F32)]
    out_specs = [gspec(*o.shape[1:]) for o in out_shape]
    band_shape = [jax.ShapeDtypeStruct((G // gb, gb * P, 2 * gb * N), F32),
                  jax.ShapeDtypeStruct((G // gb, 2 * gb * N, gb * P), F32),
                  jax.ShapeDtypeStruct((G // gb, 2, gb * N), F32)]
    out_shape += band_shape
    out_specs += [pl.BlockSpec((None,) + o.shape[1:], lambda i: (i, 0, 0)) for o in band_shape]
    return pl.pallas_call(
        functools.partial(_ssm_prep_kernel, gb=gb, chunk=chunk),
        grid=(G // gb,), in_specs=in_specs, out_specs=out_specs, out_shape=out_shape,
        compiler_params=_params("arbitrary"), name="ssm_prep")(*ins)


def _causal_conv3(v, w_ref, s_ref, *, tm, ts, hdr):
    s_ref[hdr:hdr + tm, :] = v
    xm1 = s_ref[hdr - ts:hdr - ts + tm, :]
    xm2 = s_ref[hdr - 2 * ts:hdr - 2 * ts + tm, :]
    return w_ref[0:1, :] * xm2 + w_ref[1:2, :] * xm1 + w_ref[2:3, :] * v


def _segment_conv3(v, w_ref, s_ref, *, nseg, seg, ts, hdr):
    out = [_causal_conv3(v[b * seg:(b + 1) * seg], w_ref, s_ref.at[b], tm=seg, ts=ts, hdr=hdr)
           for b in range(nseg)]
    return out[0] if nseg == 1 else jnp.concatenate(out, axis=0)


def _conv_state_begin(s_ref, prev_ref, *, ts, hdr):
    if prev_ref is not None:
        s_ref[:, hdr - 2 * ts:hdr, :] = prev_ref[...]
    else:
        @pl.when(pl.program_id(0) == 0)
        def _():
            s_ref[:, 0:hdr, :] = jnp.zeros((s_ref.shape[0], hdr, s_ref.shape[2]), F32)


def _conv_state_end(s_ref, buf_ref, *, seg, ts, hdr, carry):
    buf_ref[...] = s_ref[:, hdr + seg - 2 * ts:hdr + seg, :]
    if carry:
        s_ref[:, 0:hdr, :] = s_ref[:, seg:seg + hdr, :]


def _piece_transpose8(v):
    width = LANES // 8
    piece = lax.broadcasted_iota(jnp.int32, v[0].shape, 1) // width
    for bit in range(3):
        s = 1 << bit
        hi = ((piece >> bit) & 1) == 1
        nv = list(v)
        for i in range(8):
            if i & s:
                continue
            a, b = v[i], v[i + s]
            nv[i] = jnp.where(hi, pltpu.roll(b, s * width, axis=1), a)
            nv[i + s] = jnp.where(hi, b, pltpu.roll(a, LANES - s * width, axis=1))
        v = nv
    return v


def _tokens_to_groups(zu, u_ref, *, nseg, seg):
    nc = seg // SSM_CHUNK
    ms = zu.shape[1]
    z_tb = pltpu.einshape("btd->tbd", zu.reshape(nseg, seg, ms))
    for h in range(SSM_CHUNK // 8):
        for q in range(ms // LANES):
            xs = [jnp.concatenate([z_tb[SSM_CHUNK * c + 8 * h + i, :, q * LANES:(q + 1) * LANES] for c in range(nc)],
                                  axis=0) for i in range(8)]
            w = _piece_transpose8(xs)
            for k in range(8):
                u_ref[8 * q + k, :, h * LANES:(h + 1) * LANES] = w[k].astype(u_ref.dtype)


def _groups_to_tokens(y_ref, *, nseg, seg):
    nc = seg // SSM_CHUNK
    nq = y_ref.shape[0] // 8
    tiles = [[None] * nq for _ in range(seg)]
    for h in range(SSM_CHUNK // 8):
        for q in range(nq):
            v = _piece_transpose8([y_ref[8 * q + k, :, h * LANES:(h + 1) * LANES].astype(F32) for k in range(8)])
            for i in range(8):
                for c in range(nc):
                    tiles[SSM_CHUNK * c + 8 * h + i][q] = v[i][c * nseg:(c + 1) * nseg]
    y_tb = jnp.stack([jnp.concatenate(row, axis=1) for row in tiles], axis=0)
    return pltpu.einshape("tbd->btd", y_tb).reshape(nseg * seg, nq * LANES)


def _mix_in_core(x, g_ref, w_ref, cw_ref, gc_ref, s_ref, *, nseg, seg, ts, hdr, ms, mc):
    hb = _rms(x, g_ref[...]).astype(BF16)
    zu = _dot(hb, w_ref[:, 0:ms])
    xin = _dot(hb, w_ref[:, ms:ms + mc])
    cg = _dot(hb, w_ref[:, ms + 2 * mc:ms + 3 * mc])
    conv = _segment_conv3(cg * xin, cw_ref, s_ref, nseg=nseg, seg=seg, ts=ts, hdr=hdr)
    bg = _dot(hb, w_ref[:, ms + mc:ms + 2 * mc])
    return zu, _rms(bg * conv, gc_ref[...]).astype(BF16)


def _mix_in_kernel(x_ref, g_ref, w_ref, cw_ref, gc_ref, u_ref, yc_ref, buf_ref, s_ref,
                   *, nseg, seg, nsub, hdr, ms, mc):
    sub = seg // nsub
    rg = sub // SSM_CHUNK * nseg
    _conv_state_begin(s_ref, None, ts=1, hdr=hdr)
    for h in range(nsub):
        steps = slice(h * sub, (h + 1) * sub)
        zu, ycn = _mix_in_core(x_ref[:, steps, :].reshape(nseg * sub, x_ref.shape[-1]), g_ref, w_ref, cw_ref, gc_ref,
                               s_ref, nseg=nseg, seg=sub, ts=1, hdr=hdr, ms=ms, mc=mc)
        _tokens_to_groups(zu, u_ref.at[:, pl.ds(h * rg, rg), :], nseg=nseg, seg=sub)
        yc_ref[:, steps, :] = ycn.reshape(nseg, sub, mc)
        _conv_state_end(s_ref, buf_ref, seg=sub, ts=1, hdr=hdr, carry=True)


def _mix_in(x3d, g, w_bf, conv_w, g_conv, *, seg, hdr, ms, mc):
    nseg, tlen, d = x3d.shape
    G = ms // SSM_GROUP_CH
    lp = SSM_CHUNK * SSM_GROUP_CH
    row_spec = lambda c: pl.BlockSpec((nseg, seg, c), lambda t: (0, t, 0))
    in_specs = [row_spec(d), _const_spec((1, d)), _const_spec(w_bf.shape), _const_spec(conv_w.shape),
                _const_spec((1, mc))]
    return pl.pallas_call(
        functools.partial(_mix_in_kernel, nseg=nseg, seg=seg, nsub=MIX_SUBBLOCKS, hdr=hdr, ms=ms, mc=mc),
        grid=(tlen // seg,), in_specs=in_specs,
        out_specs=[pl.BlockSpec((G, seg // SSM_CHUNK * nseg, lp), lambda t: (0, t, 0)), row_spec(mc),
                   pl.BlockSpec((nseg, 2, mc), lambda t: (0, 0, 0))],
        out_shape=[jax.ShapeDtypeStruct((G, tlen // SSM_CHUNK * nseg, lp), BF16),
                   jax.ShapeDtypeStruct((nseg, tlen, mc), BF16), jax.ShapeDtypeStruct((nseg, 2, mc), F32)],
        scratch_shapes=[pltpu.VMEM((nseg, hdr + seg // MIX_SUBBLOCKS, mc), F32)],
        compiler_params=_params("arbitrary"), name="mix_in")(x3d, g.reshape(1, d), w_bf, conv_w,
                                                              g_conv.reshape(1, mc))


def _ssm_kernel(u_ref, m_ref, s_ref, o_ref, ap_ref, y_ref, fr_ref, fi_ref, lre, lim, ire, iim, *, gb, nb, nchunk):
    n = SSM_STATE
    for gi in range(gb):
        loc = _dot(u_ref[gi], s_ref[gi].astype(BF16))
        lre[gi] = loc[:, 0:n]
        lim[gi] = loc[:, n:2 * n]
    ar = [jnp.broadcast_to(ap_ref[gi, 0:1, :], (nb, n)) for gi in range(gb)]
    ai = [jnp.broadcast_to(ap_ref[gi, 1:2, :], (nb, n)) for gi in range(gb)]
    st0 = tuple((jnp.zeros((nb, n), F32), jnp.zeros((nb, n), F32)) for _ in range(gb))

    def step(c, st):
        r = pl.multiple_of(c * nb, nb)
        new = []
        for gi in range(gb):
            sr, si = st[gi]
            ire[gi, pl.ds(r, nb), :] = sr
            iim[gi, pl.ds(r, nb), :] = si
            nr = ar[gi] * sr - ai[gi] * si + lre[gi, pl.ds(r, nb), :]
            ni = ar[gi] * si + ai[gi] * sr + lim[gi, pl.ds(r, nb), :]
            new.append((nr, ni))
        return tuple(new)

    st = lax.fori_loop(0, nchunk, step, st0, unroll=4)
    for gi in range(gb):
        fr_ref[gi] = st[gi][0]
        fi_ref[gi] = st[gi][1]
        ini = jnp.concatenate([ire[gi], iim[gi]], axis=1).astype(BF16)
        y = _dot(u_ref[gi], m_ref[gi]) + _dot(ini, o_ref[gi])
        y_ref[gi] = y.astype(y_ref.dtype)


def _ssm(u_g, mats, apow, *, nb, nchunk, gb=4):
    G, R, LP = u_g.shape
    n = SSM_STATE
    ins = [u_g, *mats, apow]
    gspec = lambda *s: pl.BlockSpec((gb,) + s, lambda i: (i, 0, 0))
    out_shape = [jax.ShapeDtypeStruct((G, R, LP), F32), jax.ShapeDtypeStruct((G, nb, n), F32),
                 jax.ShapeDtypeStruct((G, nb, n), F32)]
    return pl.pallas_call(
        functools.partial(_ssm_kernel, gb=gb, nb=nb, nchunk=nchunk),
        grid=(G // gb,), in_specs=[gspec(*a.shape[1:]) for a in ins],
        out_specs=[gspec(*o.shape[1:]) for o in out_shape],
        out_shape=out_shape, scratch_shapes=[pltpu.VMEM((gb, R, n), F32)] * 4,
        compiler_params=_params("arbitrary"), name="ssm_chunk")(*ins)


def _mix_out_core(x, ys, yc, wg_ref, bg_ref, gs_ref, wo_ref, gx_ref, wq_ref, *, ms, q_scale):
    y = jax.nn.gelu(ys)
    y = y * jax.nn.sigmoid(_dot(y.astype(BF16), wg_ref[...]) + bg_ref[...])
    ysn = _rms(y, gs_ref[...]).astype(BF16)
    x1 = x + _dot(jnp.concatenate([ysn, yc], axis=1), wo_ref[...])
    h = _rms(x1, gx_ref[...]).astype(BF16)
    return x1, _dot(h, wq_ref[...]) * q_scale


def _mix_out_kernel(x_ref, ys_ref, yc_ref, wg_ref, bg_ref, gs_ref, wo_ref, gx_ref, wq_ref, x1_ref, q_ref,
                    *, nseg, seg, nsub, ms, q_scale):
    sub = seg // nsub
    rg = sub // SSM_CHUNK * nseg
    d = x_ref.shape[-1]
    for h in range(nsub):
        steps = slice(h * sub, (h + 1) * sub)
        ys = _groups_to_tokens(ys_ref.at[:, pl.ds(h * rg, rg), :], nseg=nseg, seg=sub)
        x1, q = _mix_out_core(x_ref[:, steps, :].reshape(nseg * sub, d),
                              ys, yc_ref[:, steps, :].reshape(nseg * sub, yc_ref.shape[-1]),
                              wg_ref, bg_ref, gs_ref, wo_ref, gx_ref, wq_ref, ms=ms, q_scale=q_scale)
        x1_ref[:, steps, :] = x1.reshape(nseg, sub, d)
        q_ref[:, steps, :] = q.astype(q_ref.dtype).reshape(nseg, sub, d)


def _mix_out(x3d, ys, yc, w_glu, b_glu, g_ssm, w_out, g_x, w_q, *, seg):
    nseg, tlen, d = x3d.shape
    ms = w_glu.shape[0]
    hd = d // XATTN_HEADS
    row_spec = lambda c: pl.BlockSpec((nseg, seg, c), lambda t: (0, t, 0))
    ys_spec = pl.BlockSpec((ys.shape[0], seg // SSM_CHUNK * nseg, ys.shape[2]), lambda t: (0, t, 0))
    in_specs = [row_spec(d), ys_spec, row_spec(yc.shape[-1]), _const_spec(w_glu.shape), _const_spec((1, ms)),
                _const_spec((1, ms)), _const_spec(w_out.shape), _const_spec((1, d)), _const_spec(w_q.shape)]
    return pl.pallas_call(
        functools.partial(_mix_out_kernel, nseg=nseg, seg=seg, nsub=MIX_SUBBLOCKS, ms=ms, q_scale=hd ** -0.5),
        grid=(tlen // seg,), in_specs=in_specs, out_specs=[row_spec(d), row_spec(d)],
        out_shape=[jax.ShapeDtypeStruct((nseg, tlen, d), F32), jax.ShapeDtypeStruct((nseg, tlen, d), BF16)],
        compiler_params=_params("parallel"), name="mix_out")(
            x3d, ys, yc, w_glu, b_glu.reshape(1, ms), g_ssm.reshape(1, ms), w_out, g_x.reshape(1, d), w_q)


def _step_mix_kernel(x_ref, g_ref, w_ref, cw_ref, gc_ref, prev_ref, s0r_ref, s0i_ref, bband_ref, cband_ref, arow_ref,
                     d_ref, wg_ref, bg_ref, gs_ref, wo_ref, gx_ref, wq_ref,
                     x1_ref, q_ref, buf_ref, fr_ref, fi_ref, s_ref, ys_ref, *, nseq, tlen, ms, mc, q_scale):
    rows = nseq * tlen
    _conv_state_begin(s_ref, prev_ref, ts=nseq, hdr=2 * nseq)
    x = x_ref[...]
    zu, ycn = _mix_in_core(x, g_ref, w_ref, cw_ref, gc_ref, s_ref, nseg=1, seg=rows, ts=nseq, hdr=2 * nseq,
                           ms=ms, mc=mc)
    _conv_state_end(s_ref, buf_ref, seg=rows, ts=nseq, hdr=2 * nseq, carry=False)

    wu, ws = bband_ref.shape[1], arow_ref.shape[2]
    for i in range(bband_ref.shape[0]):
        ucols, scols = slice(i * wu, (i + 1) * wu), slice(i * ws, (i + 1) * ws)
        sr, si = s0r_ref[:, scols], s0i_ref[:, scols]
        ar, ai = arow_ref[i, 0:1, :], arow_ref[i, 1:2, :]
        bband, cband = bband_ref[i].astype(BF16), cband_ref[i].astype(BF16)
        for t in range(tlen):
            ut = zu[t * nseq:(t + 1) * nseq, ucols]
            bu = _dot(ut.astype(BF16), bband)
            sr, si = ar * sr - ai * si + bu[:, 0:ws], ar * si + ai * sr + bu[:, ws:]
            cs = _dot(jnp.concatenate([sr, si], axis=1).astype(BF16), cband)
            ys_ref[t * nseq:(t + 1) * nseq, ucols] = cs + d_ref[:, ucols] * ut
        fr_ref[:, scols] = sr
        fi_ref[:, scols] = si

    x1, q = _mix_out_core(x, ys_ref[...], ycn, wg_ref, bg_ref, gs_ref, wo_ref, gx_ref, wq_ref, ms=ms, q_scale=q_scale)
    x1_ref[...] = x1
    q_ref[...] = q.astype(q_ref.dtype)


def _step_mix(x2d, p, conv_prev, s0_re, s0_im, bands, d_row, *, nseq, tlen):
    rows, d = x2d.shape
    ms = p["w_glu"].shape[0]
    mc = p["conv_w"].shape[1]
    hd = d // XATTN_HEADS
    bband, cband, arow = bands
    ins = [x2d, p["norm_mix"].reshape(1, d), p["w_in"], p["conv_w"], p["norm_conv_out"].reshape(1, mc), conv_prev,
           s0_re, s0_im, bband, cband, arow, d_row, p["w_glu"], p["b_glu"].reshape(1, ms),
           p["norm_ssm_out"].reshape(1, ms), p["w_out"], p["norm_xattn"].reshape(1, d), p["w_q"]]
    full = lambda shape: pl.BlockSpec(shape, lambda i: (0,) * len(shape))
    out_shape = [jax.ShapeDtypeStruct((rows, d), F32), jax.ShapeDtypeStruct((rows, d), F32),
                 jax.ShapeDtypeStruct(conv_prev.shape, F32), jax.ShapeDtypeStruct(s0_re.shape, F32),
                 jax.ShapeDtypeStruct(s0_im.shape, F32)]
    return pl.pallas_call(
        functools.partial(_step_mix_kernel, nseq=nseq, tlen=tlen, ms=ms, mc=mc, q_scale=hd ** -0.5),
        grid=(1,), in_specs=[_const_spec(a.shape) for a in ins], out_specs=[full(o.shape) for o in out_shape],
        out_shape=out_shape,
        scratch_shapes=[pltpu.VMEM((1, 2 * nseq + rows, mc), F32), pltpu.VMEM((rows, ms), F32)],
        compiler_params=_params("arbitrary"), name="step_mix")(*ins)


def _kv_kernel(m_ref, g_ref, wk_ref, wv_ref, k_ref, v_ref):
    h = _rms(m_ref[...], g_ref[...]).astype(BF16)
    k_ref[...] = _dot(h, wk_ref[...])
    v_ref[...] = _dot(h, wv_ref[...])


def _kv_proj(mem2d, g, w_k, w_v, *, tm):
    rows, d = mem2d.shape
    row_spec = pl.BlockSpec((tm, d), lambda i: (i, 0))
    return pl.pallas_call(
        _kv_kernel, grid=(rows // tm,),
        in_specs=[row_spec, _const_spec((1, d)), _const_spec(w_k.shape), _const_spec(w_v.shape)],
        out_specs=[row_spec, row_spec],
        out_shape=[jax.ShapeDtypeStruct((rows, d), F32)] * 2,
        compiler_params=_params("parallel"), name="kv_proj")(mem2d, g.reshape(1, d), w_k, w_v)


def _softmax_attention(qs, ks, vs):
    tq = qs[0].shape[0]
    s = jnp.concatenate([lax.dot_general(q, k, (((1,), (1,)), ((), ())), preferred_element_type=F32)
                         for q, k in zip(qs, ks)], axis=0)
    p = jnp.exp(s - jnp.max(s, axis=-1, keepdims=True))
    p = p / jnp.sum(p, axis=-1, keepdims=True)
    return [_dot(p[i * tq:(i + 1) * tq].astype(BF16), v) for i, v in enumerate(vs)]


def _head_slices(ref, rows, hd):
    return [ref[rows, h * hd:(h + 1) * hd].astype(BF16) for h in range(XATTN_HEADS)]


def _attn_kernel(q_ref, k_ref, v_ref, o_ref, *, sb, tq, hd):
    qs, ks, vs = [], [], []
    for j in range(sb):
        qs += _head_slices(q_ref, slice(j * tq, (j + 1) * tq), hd)
        ks += _head_slices(k_ref, slice(j * N_MEM, (j + 1) * N_MEM), hd)
        vs += _head_slices(v_ref, slice(j * N_MEM, (j + 1) * N_MEM), hd)
    outs = _softmax_attention(qs, ks, vs)
    for j in range(sb):
        heads = outs[j * XATTN_HEADS:(j + 1) * XATTN_HEADS]
        o_ref[j * tq:(j + 1) * tq, :] = jnp.concatenate(heads, axis=1).astype(o_ref.dtype)


def _attn(q2d, k2d, v2d, *, nseq, sb, tq, nq, o_dtype):
    rows, d = q2d.shape
    hd = d // XATTN_HEADS
    q_spec = pl.BlockSpec((sb * tq, d), lambda b, t: (b * nq + t, 0))
    kv_spec = pl.BlockSpec((sb * N_MEM, d), lambda b, t: (b, 0))
    return pl.pallas_call(
        functools.partial(_attn_kernel, sb=sb, tq=tq, hd=hd),
        grid=(nseq // sb, nq), in_specs=[q_spec, kv_spec, kv_spec], out_specs=q_spec,
        out_shape=jax.ShapeDtypeStruct((rows, d), o_dtype),
        compiler_params=_params("parallel", "arbitrary"), name="xattn")(q2d, k2d, v2d)


def _ffn_kernel(*refs, tm, ts, hdr, has_prev, has_kv, dff):
    x_ref, o_ref = refs[:2]
    refs = refs[2:]
    if has_kv:
        k_ref, v_ref = refs[:2]
        refs = refs[2:]
    wxo_ref, gf_ref, wup_ref, wgate_ref, cw_ref, wdn_ref, gl_ref = refs[:7]
    refs = refs[7:]
    if has_prev:
        prev_ref, refs = refs[0], refs[1:]
    y_ref, buf_ref, s_ref = refs
    if has_prev:
        s_ref[hdr - 2 * ts:hdr, :] = prev_ref[...]
    else:
        @pl.when(pl.program_id(1) == 0)
        def _():
            s_ref[0:hdr, :] = jnp.zeros((hdr, dff), F32)

    if has_kv:
        hd = o_ref.shape[1] // XATTN_HEADS
        rows = slice(None)
        heads = _softmax_attention(_head_slices(o_ref, rows, hd), _head_slices(k_ref, rows, hd),
                                   _head_slices(v_ref, rows, hd))
        o = jnp.concatenate(heads, axis=1).astype(BF16)
    else:
        o = o_ref[...].astype(BF16)
    x2 = x_ref[...] + _dot(o, wxo_ref[...])
    hb = _rms(x2, gf_ref[...]).astype(BF16)
    a = _causal_conv3(_dot(hb, wup_ref[...]), cw_ref, s_ref, tm=tm, ts=ts, hdr=hdr)
    act = (jax.nn.gelu(a) * _dot(hb, wgate_ref[...])).astype(BF16)
    x3 = x2 + _dot(act, wdn_ref[...])
    y_ref[...] = _rms(x3, gl_ref[...])
    tail = s_ref[hdr + tm - 2 * ts:hdr + tm, :]
    buf_ref[...] = tail.reshape(buf_ref.shape)
    if not has_prev:
        s_ref[0:hdr, :] = s_ref[tm:tm + hdr, :]


def _ffn(x2d, o2d, kv, w_xo, g_ffn, w_up, w_gate, conv_w, w_down, g_last, prev, *, nb, nt, tm, ts, hdr):
    rows, d = x2d.shape
    dff = w_up.shape[1]
    has_prev = prev is not None
    row_spec = pl.BlockSpec((tm, d), lambda b, t: (b * nt + t, 0))
    in_specs = [row_spec, row_spec]
    ins = [x2d, o2d]
    if kv is not None:
        in_specs += [pl.BlockSpec((N_MEM, d), lambda b, t: (b, 0))] * 2
        ins += list(kv)
    in_specs += [_const_spec(w_xo.shape), _const_spec((1, d)), _const_spec(w_up.shape), _const_spec(w_gate.shape),
                 _const_spec(conv_w.shape), _const_spec(w_down.shape), _const_spec((1, d))]
    ins += [w_xo, g_ffn.reshape(1, d), w_up, w_gate, conv_w, w_down, g_last.reshape(1, d)]
    if has_prev:
        in_specs.append(_const_spec(prev.shape))
        ins.append(prev)
        buf_shape = jax.ShapeDtypeStruct((2 * ts, dff), F32)
        buf_spec = pl.BlockSpec((2 * ts, dff), lambda b, t: (0, 0))
    else:
        buf_shape = jax.ShapeDtypeStruct((nb, 2 * ts, dff), F32)
        buf_spec = pl.BlockSpec((None, 2 * ts, dff), lambda b, t: (b, 0, 0))
    return pl.pallas_call(
        functools.partial(_ffn_kernel, tm=tm, ts=ts, hdr=hdr, has_prev=has_prev, has_kv=kv is not None, dff=dff),
        grid=(nb, nt), in_specs=in_specs, out_specs=[row_spec, buf_spec],
        out_shape=[jax.ShapeDtypeStruct((rows, d), F32), buf_shape],
        scratch_shapes=[pltpu.VMEM((hdr + tm, dff), F32)],
        compiler_params=_params("parallel", "arbitrary"), name="conv_ffn")(*ins)


def _prompt_layer(x3d, k2d, v2d, p, mats, apow, g_last):
    nseq, tlen, d = x3d.shape
    rows = nseq * tlen
    ms = p["w_glu"].shape[0]
    mc = p["conv_w"].shape[1]
    u, ycn, conv_buf = _mix_in(x3d, p["norm_mix"], p["w_in"], p["conv_w"], p["norm_conv_out"],
                               seg=MIX_STEPS, hdr=8, ms=ms, mc=mc)
    y_g, f_re, f_im = _ssm(u, mats, apow, nb=nseq, nchunk=tlen // SSM_CHUNK)
    x1, q = _mix_out(x3d, y_g, ycn, p["w_glu"], p["b_glu"], p["norm_ssm_out"], p["w_out"], p["norm_xattn"],
                     p["w_q"], seg=MIX_STEPS)
    y, ffn_buf = _ffn(x1.reshape(rows, d), q.reshape(rows, d), (k2d, v2d), p["w_xo"], p["norm_ffn"], p["w_up"],
                      p["w_gate"], p["ffn_conv_w"],
                      p["w_down"], g_last, None, nb=nseq, nt=tlen // FFN_ROWS, tm=FFN_ROWS, ts=1, hdr=8)
    return y, (f_re.transpose(1, 0, 2), f_im.transpose(1, 0, 2), conv_buf, ffn_buf)


def _step_layer(x_bm, k2d, v2d, prev, p, bands, d_row, g_last):
    nseq, tlen, d = x_bm.shape
    rows = nseq * tlen
    G, N = prev[0].shape[1:]
    to_rows = lambda a: a.transpose(1, 0, 2).reshape(a.shape[1] * nseq, a.shape[2])
    from_rows = lambda a, k: a.reshape(k, nseq, a.shape[-1]).transpose(1, 0, 2)
    x1, q, conv_buf, f_re, f_im = _step_mix(to_rows(x_bm), p, to_rows(prev[2])[None], prev[0].reshape(nseq, G * N),
                                            prev[1].reshape(nseq, G * N), bands, d_row, nseq=nseq, tlen=tlen)
    o = _attn(from_rows(q, tlen).reshape(rows, d), k2d, v2d, nseq=nseq, sb=ATTN_STEP_SEQS, tq=tlen, nq=1, o_dtype=F32)
    y, ffn_buf = _ffn(x1, to_rows(o.reshape(nseq, tlen, d)), None, p["w_xo"], p["norm_ffn"], p["w_up"], p["w_gate"],
                      p["ffn_conv_w"], p["w_down"], g_last, to_rows(prev[3]), nb=1, nt=1, tm=rows, ts=nseq,
                      hdr=2 * nseq)
    return from_rows(y, tlen), (f_re.reshape(nseq, G, N), f_im.reshape(nseq, G, N), from_rows(conv_buf[0], 2),
                                from_rows(ffn_buf, 2))


def kernel(x_prompt, x_sample, mem_prompt, cache_mem_k, cache_mem_v, state_ssm_re, state_ssm_im, state_conv, state_ffn_conv, norm_mix, w_in, ssm_A_re, ssm_A_im, ssm_log_dt, ssm_B_re, ssm_B_im, ssm_C_re, ssm_C_im, ssm_D, w_glu, b_glu, conv_w, norm_ssm_out, norm_conv_out, w_out, norm_xattn, norm_mem, w_q, w_k, w_v, w_xo, norm_ffn, w_up, w_gate, ffn_conv_w, w_down, norm_final):
    depth = w_in.shape[0]
    assert depth == 1, "the final norm is fused into the last (only) layer's ConvFFN kernel"
    nbp, tp, d = x_prompt.shape
    nbs = x_sample.shape[0]
    assert tp % ATTN_ROWS == 0 and tp % FFN_ROWS == 0 and tp % MIX_STEPS == 0 and nbp == 8
    hd = d // XATTN_HEADS

    xp, xs = x_prompt, x_sample
    outs_p, outs_s, mk_p, mv_p = [], [], [], []
    for l in range(depth):
        p = dict(norm_mix=norm_mix[l], w_in=w_in[l].astype(BF16), w_glu=w_glu[l].astype(BF16), b_glu=b_glu[l],
                 conv_w=conv_w[l], norm_ssm_out=norm_ssm_out[l], norm_conv_out=norm_conv_out[l],
                 w_out=w_out[l].astype(BF16), norm_xattn=norm_xattn[l], w_q=w_q[l].astype(BF16),
                 w_xo=w_xo[l].astype(BF16), norm_ffn=norm_ffn[l], w_up=w_up[l].astype(BF16),
                 w_gate=w_gate[l].astype(BF16), ffn_conv_w=ffn_conv_w[l], w_down=w_down[l].astype(BF16))
        *mats, apow, bband, cband, arow = _ssm_prep(ssm_A_re[l], ssm_A_im[l], ssm_log_dt[l], ssm_B_re[l],
                                                    ssm_B_im[l], ssm_C_re[l], ssm_C_im[l], ssm_D[l], chunk=SSM_CHUNK)

        k2d, v2d = _kv_proj(mem_prompt.reshape(nbp * N_MEM, d), norm_mem[l], w_k[l].astype(BF16),
                            w_v[l].astype(BF16), tm=ATTN_ROWS)
        mk_p.append(k2d.reshape(nbp, N_MEM, XATTN_HEADS, hd))
        mv_p.append(v2d.reshape(nbp, N_MEM, XATTN_HEADS, hd))

        xp, st_p = _prompt_layer(xp, k2d, v2d, p, mats, apow, norm_final)
        outs_p.append(st_p)
        xs, st_s = _step_layer(xs, cache_mem_k[l].reshape(nbs * N_MEM, d), cache_mem_v[l].reshape(nbs * N_MEM, d),
                               (state_ssm_re[l], state_ssm_im[l], state_conv[l], state_ffn_conv[l]), p,
                               (bband, cband, arow), ssm_D[l].reshape(1, -1), norm_final)
        outs_s.append(st_s)

    stack = lambda outs, i: jnp.stack([o[i] for o in outs])
    return (xp.reshape(nbp, tp, d), xs, jnp.stack(mk_p), jnp.stack(mv_p),
            stack(outs_p, 0), stack(outs_p, 1), stack(outs_p, 2), stack(outs_p, 3),
            stack(outs_s, 0), stack(outs_s, 1), stack(outs_s, 2), stack(outs_s, 3))
```

```python
import functools

import jax
import jax.numpy as jnp
from jax import lax
from jax.experimental import pallas as pl
from jax.experimental.pallas import tpu as pltpu

F32 = jnp.float32
BF16 = jnp.bfloat16

EPS = 1e-6
SSM_GROUP_CH = 16
SSM_STATE = 64
CONV_K = 3
N_MEM = 256
XATTN_HEADS = 4
SSM_CHUNK = 16
LANES = 128
MIX_STEPS = 128
MIX_SUBBLOCKS = 2
ATTN_ROWS = 512
FFN_ROWS = 512
ATTN_STEP_SEQS = 4
V7X_VMEM_LIMIT = 56 * 1024 * 1024


def _rms(x, g):
    ms = jnp.mean(x * x, axis=-1, keepdims=True)
    return x * lax.rsqrt(ms + EPS) * g


def _dot(a, b):
    return jnp.dot(a, b, preferred_element_type=F32)


def _const_spec(shape):
    nd = len(shape)
    return pl.BlockSpec(shape, lambda *_: (0,) * nd, pipeline_mode=pl.Buffered(1))


def _params(*sem):
    return pltpu.CompilerParams(dimension_semantics=sem, vmem_limit_bytes=V7X_VMEM_LIMIT)


def _ssm_prep_kernel(ld_ref, lr_r_ref, li_r_ref, lr_c_ref, li_c_ref, brt_ref, bit_ref, crt_t_ref, cit_t_ref, d_ref,
                     m_ref, s_ref, o_ref, apow_ref, bband_ref, cband_ref, arow_ref, *, gb, chunk):
    P, N = SSM_GROUP_CH, SSM_STATE
    LP = chunk * P
    bband_ref[...] = jnp.zeros(bband_ref.shape, bband_ref.dtype)
    cband_ref[...] = jnp.zeros(cband_ref.shape, cband_ref.dtype)
    lane_j = lax.broadcasted_iota(jnp.int32, (1, LP), 1) // P
    row_j = lax.broadcasted_iota(jnp.int32, (LP, 1), 0) // P
    lane_i = lax.broadcasted_iota(jnp.int32, (P, LP), 1)
    row_i = lax.broadcasted_iota(jnp.int32, (P, LP), 0)
    nbits = chunk.bit_length()

    def squarings(lr, li, dt):
        mag = jnp.exp(dt * lr)
        pr, pi = mag * jnp.cos(dt * li), mag * jnp.sin(dt * li)
        out = [(pr, pi)]
        for _ in range(nbits - 1):
            pr, pi = pr * pr - pi * pi, 2.0 * pr * pi
            out.append((pr, pi))
        return out

    def cpow(pows, j):
        er = ei = None
        for b, (pr, pi) in enumerate(pows):
            if isinstance(j, int):
                if not (j >> b) & 1:
                    continue
                er, ei = (pr, pi) if er is None else (er * pr - ei * pi, er * pi + ei * pr)
            else:
                bit = ((j >> b) & 1) == 1
                if er is None:
                    er, ei = jnp.where(bit, pr, 1.0), jnp.where(bit, pi, 0.0)
                else:
                    er, ei = jnp.where(bit, er * pr - ei * pi, er), jnp.where(bit, er * pi + ei * pr, ei)
        return er, ei

    for gi in range(gb):
        dt = jnp.exp(ld_ref[gi])
        lr_r, li_r = lr_r_ref[gi], li_r_ref[gi]
        pows_r = squarings(lr_r, li_r, dt)
        pows_c = squarings(lr_c_ref[gi], li_c_ref[gi], dt)

        ar, ai = pows_r[0]
        den = lr_r * lr_r + li_r * li_r
        cr = ((ar - 1.0) * lr_r + ai * li_r) / den
        ci = (ai * lr_r - (ar - 1.0) * li_r) / den

        brt, bit = brt_ref[gi], bit_ref[gi]
        bbt_re = cr * brt - ci * bit
        bbt_im = cr * bit + ci * brt
        bbt_re_t = jnp.concatenate([bbt_re] * chunk, axis=0)
        bbt_im_t = jnp.concatenate([bbt_im] * chunk, axis=0)

        er, ei = cpow(pows_r, (chunk - 1) - row_j)
        s_ref[gi, :, 0:N] = er * bbt_re_t - ei * bbt_im_t
        s_ref[gi, :, N:2 * N] = er * bbt_im_t + ei * bbt_re_t

        crt_t, cit_t = crt_t_ref[gi], cit_t_ref[gi]
        er, ei = cpow(pows_c, lane_j)
        r_re = crt_t * er - cit_t * ei
        r_im = crt_t * ei + cit_t * er
        er, ei = cpow(pows_c, lane_j + 1)
        o_ref[gi, 0:N, :] = (crt_t * er - cit_t * ei).astype(o_ref.dtype)
        o_ref[gi, N:2 * N, :] = (-(crt_t * ei + cit_t * er)).astype(o_ref.dtype)

        krow = (jnp.dot(bbt_re, r_re, preferred_element_type=F32, precision=lax.Precision.HIGHEST)
                - jnp.dot(bbt_im, r_im, preferred_element_type=F32, precision=lax.Precision.HIGHEST))
        krow = krow + jnp.where(lane_i == row_i, d_ref[gi], 0.0)
        for ti in range(chunk):
            blk = krow if ti == 0 else pltpu.roll(krow, ti * P, axis=1)
            blk = jnp.where(lane_i >= ti * P, blk, 0.0)
            m_ref[gi, ti * P:(ti + 1) * P, :] = blk.astype(m_ref.dtype)

        apow_ref[gi] = jnp.concatenate(cpow(pows_r, chunk), axis=0)

        rows, cols = slice(gi * P, (gi + 1) * P), slice(gi * N, (gi + 1) * N)
        cols_im = slice((gb + gi) * N, (gb + gi + 1) * N)
        bband_ref[rows, cols] = bbt_re
        bband_ref[rows, cols_im] = bbt_im
        cband_ref[cols, rows] = crt_t[:, rows]
        cband_ref[cols_im, rows] = -cit_t[:, rows]
        arow_ref[0:1, cols] = ar
        arow_ref[1:2, cols] = ai


def _ssm_prep(A_re, A_im, log_dt, B_re, B_im, C_re, C_im, D, *, chunk, gb=8):
    G, N = A_re.shape
    P = SSM_GROUP_CH
    LP = chunk * P
    brt = B_re.transpose(0, 2, 1)
    bit = B_im.transpose(0, 2, 1)
    crt_t = jnp.tile(C_re.transpose(0, 2, 1), (1, 1, chunk))
    cit_t = jnp.tile(C_im.transpose(0, 2, 1), (1, 1, chunk))
    ins = [log_dt.reshape(G, 1, 1), A_re.reshape(G, 1, N), A_im.reshape(G, 1, N),
           A_re.reshape(G, N, 1), A_im.reshape(G, N, 1), brt, bit, crt_t, cit_t, D.reshape(G, P, 1)]
    gspec = lambda *s: pl.BlockSpec((gb,) + s, lambda i: (i, 0, 0))
    in_specs = [gspec(*a.shape[1:]) for a in ins]
    out_shape = [jax.ShapeDtypeStruct((G, LP, LP), BF16), jax.ShapeDtypeStruct((G, LP, 2 * N), F32),
                 jax.ShapeDtypeStruct((G, 2 * N, LP), BF16), jax.ShapeDtypeStruct((G, 2, N), F32)]
    out_specs = [gspec(*o.shape[1:]) for o in out_shape]
    band_shape = [jax.ShapeDtypeStruct((G // gb, gb * P, 2 * gb * N), F32),
                  jax.ShapeDtypeStruct((G // gb, 2 * gb * N, gb * P), F32),
                  jax.ShapeDtypeStruct((G // gb, 2, gb * N), F32)]
    out_shape += band_shape
    out_specs += [pl.BlockSpec((None,) + o.shape[1:], lambda i: (i, 0, 0)) for o in band_shape]
    return pl.pallas_call(
        functools.partial(_ssm_prep_kernel, gb=gb, chunk=chunk),
        grid=(G // gb,), in_specs=in_specs, out_specs=out_specs, out_shape=out_shape,
        compiler_params=_params("arbitrary"), name="ssm_prep")(*ins)


def _conv3_steps_apart(v, w_ref, s_ref, prev, *, ts):
    tm = v.shape[0]
    s_ref[0:2 * ts, :] = prev
    s_ref[2 * ts:2 * ts + tm, :] = v
    out = w_ref[0:1, :] * s_ref[0:tm, :] + w_ref[1:2, :] * s_ref[ts:ts + tm, :] + w_ref[2:3, :] * v
    return out, s_ref[tm:tm + 2 * ts, :]


def _conv3_rows(v, w_ref, c_ref):
    tm = v.shape[0]
    head = c_ref[...]
    row = lax.broadcasted_iota(jnp.int32, head.shape, 0)

    def back(k):
        r = pltpu.roll(v, k, axis=0)
        first = jnp.where(row < k, pltpu.roll(head, k, axis=0), r[0:8])
        return jnp.concatenate([first, r[8:]], axis=0)

    out = w_ref[0:1, :] * back(2) + w_ref[1:2, :] * back(1) + w_ref[2:3, :] * v
    c_ref[...] = v[tm - 8:tm]
    return out


def _piece_transpose8(v):
    width = LANES // 8
    piece = lax.broadcasted_iota(jnp.int32, v[0].shape, 1) // width
    for bit in range(3):
        s = 1 << bit
        hi = ((piece >> bit) & 1) == 1
        nv = list(v)
        for i in range(8):
            if i & s:
                continue
            a, b = v[i], v[i + s]
            nv[i] = jnp.where(hi, pltpu.roll(b, s * width, axis=1), a)
            nv[i + s] = jnp.where(hi, b, pltpu.roll(a, LANES - s * width, axis=1))
        v = nv
    return v


def _tokens_to_groups(zu, u_ref, *, nseg, seg):
    nc = seg // SSM_CHUNK
    ms = zu.shape[1]
    z_tb = jnp.transpose(zu.reshape(nseg, seg, ms), (1, 0, 2))
    for h in range(SSM_CHUNK // 8):
        for q in range(ms // LANES):
            xs = [jnp.concatenate([z_tb[SSM_CHUNK * c + 8 * h + i, :, q * LANES:(q + 1) * LANES] for c in range(nc)],
                                  axis=0) for i in range(8)]
            w = _piece_transpose8(xs)
            for k in range(8):
                u_ref[8 * q + k, :, h * LANES:(h + 1) * LANES] = w[k].astype(u_ref.dtype)


def _groups_to_tokens(y_ref, *, nseg, seg):
    nc = seg // SSM_CHUNK
    nq = y_ref.shape[0] // 8
    tiles = [[None] * nq for _ in range(seg)]
    for h in range(SSM_CHUNK // 8):
        for q in range(nq):
            v = _piece_transpose8([y_ref[8 * q + k, :, h * LANES:(h + 1) * LANES].astype(F32) for k in range(8)])
            for i in range(8):
                for c in range(nc):
                    tiles[SSM_CHUNK * c + 8 * h + i][q] = v[i][c * nseg:(c + 1) * nseg]
    y_tb = jnp.stack([jnp.concatenate(row, axis=1) for row in tiles], axis=0)
    return jnp.transpose(y_tb, (1, 0, 2)).reshape(nseg * seg, nq * LANES)


def _mix_in_core(x, g_ref, w_ref, gc_ref, conv, *, ms, mc):
    hb = _rms(x, g_ref[...]).astype(BF16)
    zu = _dot(hb, w_ref[:, 0:ms])
    xin = _dot(hb, w_ref[:, ms:ms + mc])
    cg = _dot(hb, w_ref[:, ms + 2 * mc:ms + 3 * mc])
    cv = conv(cg * xin)
    bg = _dot(hb, w_ref[:, ms + mc:ms + 2 * mc])
    return zu, _rms(bg * cv, gc_ref[...]).astype(BF16)


def _mix_in_kernel(x_ref, g_ref, w_ref, cw_ref, gc_ref, u_ref, yc_ref, buf_ref, s_ref,
                   *, nseg, seg, nsub, ms, mc):
    sub = seg // nsub
    rg = sub // SSM_CHUNK * nseg

    @pl.when(pl.program_id(0) == 0)
    def _():
        s_ref[...] = jnp.zeros(s_ref.shape, F32)

    def conv(v):
        return jnp.concatenate([_conv3_rows(v[b * sub:(b + 1) * sub], cw_ref, s_ref.at[b]) for b in range(nseg)],
                               axis=0)

    for h in range(nsub):
        steps = slice(h * sub, (h + 1) * sub)
        zu, ycn = _mix_in_core(x_ref[:, steps, :].reshape(nseg * sub, x_ref.shape[-1]), g_ref, w_ref, gc_ref, conv,
                               ms=ms, mc=mc)
        _tokens_to_groups(zu, u_ref.at[:, pl.ds(h * rg, rg), :], nseg=nseg, seg=sub)
        yc_ref[:, steps, :] = ycn.reshape(nseg, sub, mc)
    buf_ref[...] = s_ref[:, 6:8, :]


def _mix_in(x3d, g, w_bf, conv_w, g_conv, *, seg, ms, mc):
    nseg, tlen, d = x3d.shape
    G = ms // SSM_GROUP_CH
    lp = SSM_CHUNK * SSM_GROUP_CH
    row_spec = lambda c: pl.BlockSpec((nseg, seg, c), lambda t: (0, t, 0))
    in_specs = [row_spec(d), _const_spec((1, d)), _const_spec(w_bf.shape), _const_spec(conv_w.shape),
                _const_spec((1, mc))]
    return pl.pallas_call(
        functools.partial(_mix_in_kernel, nseg=nseg, seg=seg, nsub=MIX_SUBBLOCKS, ms=ms, mc=mc),
        grid=(tlen // seg,), in_specs=in_specs,
        out_specs=[pl.BlockSpec((G, seg // SSM_CHUNK * nseg, lp), lambda t: (0, t, 0)), row_spec(mc),
                   pl.BlockSpec((nseg, 2, mc), lambda t: (0, 0, 0))],
        out_shape=[jax.ShapeDtypeStruct((G, tlen // SSM_CHUNK * nseg, lp), BF16),
                   jax.ShapeDtypeStruct((nseg, tlen, mc), BF16), jax.ShapeDtypeStruct((nseg, 2, mc), F32)],
        scratch_shapes=[pltpu.VMEM((nseg, 8, mc), F32)],
        compiler_params=_params("arbitrary"), name="mix_in")(x3d, g.reshape(1, d), w_bf, conv_w,
                                                              g_conv.reshape(1, mc))


def _ssm_kernel(u_ref, m_ref, s_ref, o_ref, ap_ref, y_ref, fr_ref, fi_ref, lre, lim, ire, iim, *, gb, nb, nchunk):
    n = SSM_STATE
    for gi in range(gb):
        loc = _dot(u_ref[gi], s_ref[gi].astype(BF16))
        lre[gi] = loc[:, 0:n]
        lim[gi] = loc[:, n:2 * n]
    ar = [jnp.broadcast_to(ap_ref[gi, 0:1, :], (nb, n)) for gi in range(gb)]
    ai = [jnp.broadcast_to(ap_ref[gi, 1:2, :], (nb, n)) for gi in range(gb)]
    st0 = tuple((jnp.zeros((nb, n), F32), jnp.zeros((nb, n), F32)) for _ in range(gb))

    def step(c, st):
        r = pl.multiple_of(c * nb, nb)
        new = []
        for gi in range(gb):
            sr, si = st[gi]
            ire[gi, pl.ds(r, nb), :] = sr
            iim[gi, pl.ds(r, nb), :] = si
            nr = ar[gi] * sr - ai[gi] * si + lre[gi, pl.ds(r, nb), :]
            ni = ar[gi] * si + ai[gi] * sr + lim[gi, pl.ds(r, nb), :]
            new.append((nr, ni))
        return tuple(new)

    st = lax.fori_loop(0, nchunk, step, st0, unroll=4)
    for gi in range(gb):
        fr_ref[gi] = st[gi][0]
        fi_ref[gi] = st[gi][1]
        ini = jnp.concatenate([ire[gi], iim[gi]], axis=1).astype(BF16)
        y = _dot(u_ref[gi], m_ref[gi]) + _dot(ini, o_ref[gi])
        y_ref[gi] = y.astype(y_ref.dtype)


def _ssm(u_g, mats, apow, *, nb, nchunk, gb=4):
    G, R, LP = u_g.shape
    n = SSM_STATE
    ins = [u_g, *mats, apow]
    gspec = lambda *s: pl.BlockSpec((gb,) + s, lambda i: (i, 0, 0))
    out_shape = [jax.ShapeDtypeStruct((G, R, LP), F32), jax.ShapeDtypeStruct((G, nb, n), F32),
                 jax.ShapeDtypeStruct((G, nb, n), F32)]
    return pl.pallas_call(
        functools.partial(_ssm_kernel, gb=gb, nb=nb, nchunk=nchunk),
        grid=(G // gb,), in_specs=[gspec(*a.shape[1:]) for a in ins],
        out_specs=[gspec(*o.shape[1:]) for o in out_shape],
        out_shape=out_shape, scratch_shapes=[pltpu.VMEM((gb, R, n), F32)] * 4,
        compiler_params=_params("arbitrary"), name="ssm_chunk")(*ins)


def _mix_out_core(x, ys, yc, wg_ref, bg_ref, gs_ref, wo_ref, gx_ref, wq_ref, *, ms, q_scale):
    y = jax.nn.gelu(ys)
    y = y * jax.nn.sigmoid(_dot(y.astype(BF16), wg_ref[...]) + bg_ref[...])
    ysn = _rms(y, gs_ref[...]).astype(BF16)
    x1 = x + _dot(jnp.concatenate([ysn, yc], axis=1), wo_ref[...])
    h = _rms(x1, gx_ref[...]).astype(BF16)
    return x1, _dot(h, wq_ref[...]) * q_scale


def _mix_out_kernel(x_ref, ys_ref, yc_ref, wg_ref, bg_ref, gs_ref, wo_ref, gx_ref, wq_ref, x1_ref, q_ref,
                    *, nseg, seg, nsub, ms, q_scale):
    sub = seg // nsub
    rg = sub // SSM_CHUNK * nseg
    d = x_ref.shape[-1]
    for h in range(nsub):
        steps = slice(h * sub, (h + 1) * sub)
        ys = _groups_to_tokens(ys_ref.at[:, pl.ds(h * rg, rg), :], nseg=nseg, seg=sub)
        x1, q = _mix_out_core(x_ref[:, steps, :].reshape(nseg * sub, d),
                              ys, yc_ref[:, steps, :].reshape(nseg * sub, yc_ref.shape[-1]),
                              wg_ref, bg_ref, gs_ref, wo_ref, gx_ref, wq_ref, ms=ms, q_scale=q_scale)
        x1_ref[:, steps, :] = x1.reshape(nseg, sub, d)
        q_ref[:, steps, :] = q.astype(q_ref.dtype).reshape(nseg, sub, d)


def _mix_out(x3d, ys, yc, w_glu, b_glu, g_ssm, w_out, g_x, w_q, *, seg):
    nseg, tlen, d = x3d.shape
    ms = w_glu.shape[0]
    hd = d // XATTN_HEADS
    row_spec = lambda c: pl.BlockSpec((nseg, seg, c), lambda t: (0, t, 0))
    ys_spec = pl.BlockSpec((ys.shape[0], seg // SSM_CHUNK * nseg, ys.shape[2]), lambda t: (0, t, 0))
    in_specs = [row_spec(d), ys_spec, row_spec(yc.shape[-1]), _const_spec(w_glu.shape), _const_spec((1, ms)),
                _const_spec((1, ms)), _const_spec(w_out.shape), _const_spec((1, d)), _const_spec(w_q.shape)]
    return pl.pallas_call(
        functools.partial(_mix_out_kernel, nseg=nseg, seg=seg, nsub=MIX_SUBBLOCKS, ms=ms, q_scale=hd ** -0.5),
        grid=(tlen // seg,), in_specs=in_specs, out_specs=[row_spec(d), row_spec(d)],
        out_shape=[jax.ShapeDtypeStruct((nseg, tlen, d), F32), jax.ShapeDtypeStruct((nseg, tlen, d), BF16)],
        compiler_params=_params("parallel"), name="mix_out")(
            x3d, ys, yc, w_glu, b_glu.reshape(1, ms), g_ssm.reshape(1, ms), w_out, g_x.reshape(1, d), w_q)


def _step_mix_kernel(x_ref, g_ref, w_ref, cw_ref, gc_ref, prev_ref, s0r_ref, s0i_ref, bband_ref, cband_ref, arow_ref,
                     d_ref, wg_ref, bg_ref, gs_ref, wo_ref, gx_ref, wq_ref,
                     x1_ref, q_ref, buf_ref, fr_ref, fi_ref, s_ref, ys_ref, *, nseq, tlen, ms, mc, q_scale):
    x = x_ref[...]

    def conv(v):
        out, buf_ref[...] = _conv3_steps_apart(v, cw_ref, s_ref, prev_ref[...], ts=nseq)
        return out

    zu, ycn = _mix_in_core(x, g_ref, w_ref, gc_ref, conv, ms=ms, mc=mc)

    wu, ws = bband_ref.shape[1], arow_ref.shape[2]
    for i in range(bband_ref.shape[0]):
        ucols, scols = slice(i * wu, (i + 1) * wu), slice(i * ws, (i + 1) * ws)
        sr, si = s0r_ref[:, scols], s0i_ref[:, scols]
        ar, ai = arow_ref[i, 0:1, :], arow_ref[i, 1:2, :]
        bband, cband = bband_ref[i].astype(BF16), cband_ref[i].astype(BF16)
        for t in range(tlen):
            ut = zu[t * nseq:(t + 1) * nseq, ucols]
            bu = _dot(ut.astype(BF16), bband)
            sr, si = ar * sr - ai * si + bu[:, 0:ws], ar * si + ai * sr + bu[:, ws:]
            cs = _dot(jnp.concatenate([sr, si], axis=1).astype(BF16), cband)
            ys_ref[t * nseq:(t + 1) * nseq, ucols] = cs + d_ref[:, ucols] * ut
        fr_ref[:, scols] = sr
        fi_ref[:, scols] = si

    x1, q = _mix_out_core(x, ys_ref[...], ycn, wg_ref, bg_ref, gs_ref, wo_ref, gx_ref, wq_ref, ms=ms, q_scale=q_scale)
    x1_ref[...] = x1
    q_ref[...] = q.astype(q_ref.dtype)


def _step_mix(x2d, p, conv_prev, s0_re, s0_im, bands, d_row, *, nseq, tlen):
    rows, d = x2d.shape
    ms = p["w_glu"].shape[0]
    mc = p["conv_w"].shape[1]
    hd = d // XATTN_HEADS
    bband, cband, arow = bands
    ins = [x2d, p["norm_mix"].reshape(1, d), p["w_in"], p["conv_w"], p["norm_conv_out"].reshape(1, mc), conv_prev,
           s0_re, s0_im, bband, cband, arow, d_row, p["w_glu"], p["b_glu"].reshape(1, ms),
           p["norm_ssm_out"].reshape(1, ms), p["w_out"], p["norm_xattn"].reshape(1, d), p["w_q"]]
    full = lambda shape: pl.BlockSpec(shape, lambda i: (0,) * len(shape))
    out_shape = [jax.ShapeDtypeStruct((rows, d), F32), jax.ShapeDtypeStruct((rows, d), F32),
                 jax.ShapeDtypeStruct(conv_prev.shape, F32), jax.ShapeDtypeStruct(s0_re.shape, F32),
                 jax.ShapeDtypeStruct(s0_im.shape, F32)]
    return pl.pallas_call(
        functools.partial(_step_mix_kernel, nseq=nseq, tlen=tlen, ms=ms, mc=mc, q_scale=hd ** -0.5),
        grid=(1,), in_specs=[_const_spec(a.shape) for a in ins], out_specs=[full(o.shape) for o in out_shape],
        out_shape=out_shape,
        scratch_shapes=[pltpu.VMEM((2 * nseq + rows, mc), F32), pltpu.VMEM((rows, ms), F32)],
        compiler_params=_params("arbitrary"), name="step_mix")(*ins)


def _kv_kernel(m_ref, g_ref, wk_ref, wv_ref, k_ref, v_ref):
    h = _rms(m_ref[...], g_ref[...]).astype(BF16)
    k_ref[...] = _dot(h, wk_ref[...])
    v_ref[...] = _dot(h, wv_ref[...])


def _kv_proj(mem2d, g, w_k, w_v, *, tm):
    rows, d = mem2d.shape
    row_spec = pl.BlockSpec((tm, d), lambda i: (i, 0))
    return pl.pallas_call(
        _kv_kernel, grid=(rows // tm,),
        in_specs=[row_spec, _const_spec((1, d)), _const_spec(w_k.shape), _const_spec(w_v.shape)],
        out_specs=[row_spec, row_spec],
        out_shape=[jax.ShapeDtypeStruct((rows, d), F32)] * 2,
        compiler_params=_params("parallel"), name="kv_proj")(mem2d, g.reshape(1, d), w_k, w_v)


def _softmax_attention(qs, ks, vs):
    tq = qs[0].shape[0]
    s = jnp.concatenate([lax.dot_general(q, k, (((1,), (1,)), ((), ())), preferred_element_type=F32)
                         for q, k in zip(qs, ks)], axis=0)
    p = jnp.exp(s - jnp.max(s, axis=-1, keepdims=True))
    p = p / jnp.sum(p, axis=-1, keepdims=True)
    return [_dot(p[i * tq:(i + 1) * tq].astype(BF16), v) for i, v in enumerate(vs)]


def _head_slices(ref, rows, hd):
    return [ref[rows, h * hd:(h + 1) * hd].astype(BF16) for h in range(XATTN_HEADS)]


def _attn_kernel(q_ref, k_ref, v_ref, o_ref, *, sb, tq, hd):
    qs, ks, vs = [], [], []
    for j in range(sb):
        qs += _head_slices(q_ref, slice(j * tq, (j + 1) * tq), hd)
        ks += _head_slices(k_ref, slice(j * N_MEM, (j + 1) * N_MEM), hd)
        vs += _head_slices(v_ref, slice(j * N_MEM, (j + 1) * N_MEM), hd)
    outs = _softmax_attention(qs, ks, vs)
    for j in range(sb):
        heads = outs[j * XATTN_HEADS:(j + 1) * XATTN_HEADS]
        o_ref[j * tq:(j + 1) * tq, :] = jnp.concatenate(heads, axis=1).astype(o_ref.dtype)


def _attn(q2d, k2d, v2d, *, nseq, sb, tq, nq, o_dtype):
    rows, d = q2d.shape
    hd = d // XATTN_HEADS
    q_spec = pl.BlockSpec((sb * tq, d), lambda b, t: (b * nq + t, 0))
    kv_spec = pl.BlockSpec((sb * N_MEM, d), lambda b, t: (b, 0))
    return pl.pallas_call(
        functools.partial(_attn_kernel, sb=sb, tq=tq, hd=hd),
        grid=(nseq // sb, nq), in_specs=[q_spec, kv_spec, kv_spec], out_specs=q_spec,
        out_shape=jax.ShapeDtypeStruct((rows, d), o_dtype),
        compiler_params=_params("parallel", "arbitrary"), name="xattn")(q2d, k2d, v2d)


def _ffn_kernel(*refs, ts, has_prev, has_kv):
    x_ref, o_ref = refs[:2]
    refs = refs[2:]
    if has_kv:
        k_ref, v_ref = refs[:2]
        refs = refs[2:]
    wxo_ref, gf_ref, wup_ref, wgate_ref, cw_ref, wdn_ref, gl_ref = refs[:7]
    refs = refs[7:]
    if has_prev:
        prev_ref, refs = refs[0], refs[1:]
    y_ref, buf_ref, s_ref = refs
    if not has_prev:
        @pl.when(pl.program_id(1) == 0)
        def _():
            s_ref[...] = jnp.zeros(s_ref.shape, F32)

    if has_kv:
        hd = o_ref.shape[1] // XATTN_HEADS
        rows = slice(None)
        heads = _softmax_attention(_head_slices(o_ref, rows, hd), _head_slices(k_ref, rows, hd),
                                   _head_slices(v_ref, rows, hd))
        o = jnp.concatenate(heads, axis=1).astype(BF16)
    else:
        o = o_ref[...].astype(BF16)
    x2 = x_ref[...] + _dot(o, wxo_ref[...])
    hb = _rms(x2, gf_ref[...]).astype(BF16)
    up = _dot(hb, wup_ref[...])
    if has_prev:
        a, buf = _conv3_steps_apart(up, cw_ref, s_ref, prev_ref[...], ts=ts)
    else:
        a = _conv3_rows(up, cw_ref, s_ref)
        buf = s_ref[6:8, :]
    buf_ref[...] = buf.reshape(buf_ref.shape)
    act = (jax.nn.gelu(a) * _dot(hb, wgate_ref[...])).astype(BF16)
    x3 = x2 + _dot(act, wdn_ref[...])
    y_ref[...] = _rms(x3, gl_ref[...])


def _ffn(x2d, o2d, kv, w_xo, g_ffn, w_up, w_gate, conv_w, w_down, g_last, prev, *, nb, nt, tm, ts):
    rows, d = x2d.shape
    dff = w_up.shape[1]
    has_prev = prev is not None
    row_spec = pl.BlockSpec((tm, d), lambda b, t: (b * nt + t, 0))
    in_specs = [row_spec, row_spec]
    ins = [x2d, o2d]
    if kv is not None:
        in_specs += [pl.BlockSpec((N_MEM, d), lambda b, t: (b, 0))] * 2
        ins += list(kv)
    in_specs += [_const_spec(w_xo.shape), _const_spec((1, d)), _const_spec(w_up.shape), _const_spec(w_gate.shape),
                 _const_spec(conv_w.shape), _const_spec(w_down.shape), _const_spec((1, d))]
    ins += [w_xo, g_ffn.reshape(1, d), w_up, w_gate, conv_w, w_down, g_last.reshape(1, d)]
    if has_prev:
        in_specs.append(_const_spec(prev.shape))
        ins.append(prev)
        buf_shape = jax.ShapeDtypeStruct((2 * ts, dff), F32)
        buf_spec = pl.BlockSpec((2 * ts, dff), lambda b, t: (0, 0))
    else:
        buf_shape = jax.ShapeDtypeStruct((nb, 2 * ts, dff), F32)
        buf_spec = pl.BlockSpec((None, 2 * ts, dff), lambda b, t: (b, 0, 0))
    return pl.pallas_call(
        functools.partial(_ffn_kernel, ts=ts, has_prev=has_prev, has_kv=kv is not None),
        grid=(nb, nt), in_specs=in_specs, out_specs=[row_spec, buf_spec],
        out_shape=[jax.ShapeDtypeStruct((rows, d), F32), buf_shape],
        scratch_shapes=[pltpu.VMEM((2 * ts + tm if has_prev else 8, dff), F32)],
        compiler_params=_params("parallel", "arbitrary"), name="conv_ffn")(*ins)


def _prompt_layer(x3d, k2d, v2d, p, mats, apow, g_last):
    nseq, tlen, d = x3d.shape
    rows = nseq * tlen
    ms = p["w_glu"].shape[0]
    mc = p["conv_w"].shape[1]
    u, ycn, conv_buf = _mix_in(x3d, p["norm_mix"], p["w_in"], p["conv_w"], p["norm_conv_out"],
                               seg=MIX_STEPS, ms=ms, mc=mc)
    y_g, f_re, f_im = _ssm(u, mats, apow, nb=nseq, nchunk=tlen // SSM_CHUNK)
    x1, q = _mix_out(x3d, y_g, ycn, p["w_glu"], p["b_glu"], p["norm_ssm_out"], p["w_out"], p["norm_xattn"],
                     p["w_q"], seg=MIX_STEPS)
    y, ffn_buf = _ffn(x1.reshape(rows, d), q.reshape(rows, d), (k2d, v2d), p["w_xo"], p["norm_ffn"], p["w_up"],
                      p["w_gate"], p["ffn_conv_w"],
                      p["w_down"], g_last, None, nb=nseq, nt=tlen // FFN_ROWS, tm=FFN_ROWS, ts=1)
    return y, (f_re.transpose(1, 0, 2), f_im.transpose(1, 0, 2), conv_buf, ffn_buf)


def _step_layer(x_bm, k2d, v2d, prev, p, bands, d_row, g_last):
    nseq, tlen, d = x_bm.shape
    rows = nseq * tlen
    G, N = prev[0].shape[1:]
    to_rows = lambda a: a.transpose(1, 0, 2).reshape(a.shape[1] * nseq, a.shape[2])
    from_rows = lambda a, k: a.reshape(k, nseq, a.shape[-1]).transpose(1, 0, 2)
    x1, q, conv_buf, f_re, f_im = _step_mix(to_rows(x_bm), p, to_rows(prev[2]), prev[0].reshape(nseq, G * N),
                                            prev[1].reshape(nseq, G * N), bands, d_row, nseq=nseq, tlen=tlen)
    o = _attn(from_rows(q, tlen).reshape(rows, d), k2d, v2d, nseq=nseq, sb=ATTN_STEP_SEQS, tq=tlen, nq=1, o_dtype=F32)
    y, ffn_buf = _ffn(x1, to_rows(o.reshape(nseq, tlen, d)), None, p["w_xo"], p["norm_ffn"], p["w_up"], p["w_gate"],
                      p["ffn_conv_w"], p["w_down"], g_last, to_rows(prev[3]), nb=1, nt=1, tm=rows, ts=nseq)
    return from_rows(y, tlen), (f_re.reshape(nseq, G, N), f_im.reshape(nseq, G, N), from_rows(conv_buf, 2),
                                from_rows(ffn_buf, 2))


def kernel(x_prompt, x_sample, mem_prompt, cache_mem_k, cache_mem_v, state_ssm_re, state_ssm_im, state_conv, state_ffn_conv, norm_mix, w_in, ssm_A_re, ssm_A_im, ssm_log_dt, ssm_B_re, ssm_B_im, ssm_C_re, ssm_C_im, ssm_D, w_glu, b_glu, conv_w, norm_ssm_out, norm_conv_out, w_out, norm_xattn, norm_mem, w_q, w_k, w_v, w_xo, norm_ffn, w_up, w_gate, ffn_conv_w, w_down, norm_final):
    depth = w_in.shape[0]
    assert depth == 1, "the final norm is fused into the last (only) layer's ConvFFN kernel"
    nbp, tp, d = x_prompt.shape
    nbs = x_sample.shape[0]
    assert tp % ATTN_ROWS == 0 and tp % FFN_ROWS == 0 and tp % MIX_STEPS == 0 and nbp == 8
    hd = d // XATTN_HEADS

    xp, xs = x_prompt, x_sample
    outs_p, outs_s, mk_p, mv_p = [], [], [], []
    for l in range(depth):
        p = dict(norm_mix=norm_mix[l], w_in=w_in[l].astype(BF16), w_glu=w_glu[l].astype(BF16), b_glu=b_glu[l],
                 conv_w=conv_w[l], norm_ssm_out=norm_ssm_out[l], norm_conv_out=norm_conv_out[l],
                 w_out=w_out[l].astype(BF16), norm_xattn=norm_xattn[l], w_q=w_q[l].astype(BF16),
                 w_xo=w_xo[l].astype(BF16), norm_ffn=norm_ffn[l], w_up=w_up[l].astype(BF16),
                 w_gate=w_gate[l].astype(BF16), ffn_conv_w=ffn_conv_w[l], w_down=w_down[l].astype(BF16))
        *mats, apow, bband, cband, arow = _ssm_prep(ssm_A_re[l], ssm_A_im[l], ssm_log_dt[l], ssm_B_re[l],
                                                    ssm_B_im[l], ssm_C_re[l], ssm_C_im[l], ssm_D[l], chunk=SSM_CHUNK)

        k2d, v2d = _kv_proj(mem_prompt.reshape(nbp * N_MEM, d), norm_mem[l], w_k[l].astype(BF16),
                            w_v[l].astype(BF16), tm=ATTN_ROWS)
        mk_p.append(k2d.reshape(nbp, N_MEM, XATTN_HEADS, hd))
        mv_p.append(v2d.reshape(nbp, N_MEM, XATTN_HEADS, hd))

        xp, st_p = _prompt_layer(xp, k2d, v2d, p, mats, apow, norm_final)
        outs_p.append(st_p)
        xs, st_s = _step_layer(xs, cache_mem_k[l].reshape(nbs * N_MEM, d), cache_mem_v[l].reshape(nbs * N_MEM, d),
                               (state_ssm_re[l], state_ssm_im[l], state_conv[l], state_ffn_conv[l]), p,
                               (bband, cband, arow), ssm_D[l].reshape(1, -1), norm_final)
        outs_s.append(st_s)

    stack = lambda outs, i: jnp.stack([o[i] for o in outs])
    return (xp.reshape(nbp, tp, d), xs, jnp.stack(mk_p), jnp.stack(mv_p),
            stack(outs_p, 0), stack(outs_p, 1), stack(outs_p, 2), stack(outs_p, 3),
            stack(outs_s, 0), stack(outs_s, 1), stack(outs_s, 2), stack(outs_s, 3))
```

```python
import functools

import jax
import jax.numpy as jnp
from jax import lax
from jax.experimental import pallas as pl
from jax.experimental.pallas import tpu as pltpu

F32 = jnp.float32
BF16 = jnp.bfloat16

EPS = 1e-6
SSM_GROUP_CH = 16
SSM_STATE = 64
CONV_K = 3
N_MEM = 256
XATTN_HEADS = 4
SSM_CHUNK = 16
LANES = 128
MIX_STEPS = 128
MIX_SUBBLOCKS = 2
ATTN_ROWS = 512
FFN_ROWS = 512
V7X_VMEM_LIMIT = 56 * 1024 * 1024
V7X_VMEM_LIMIT_MAX = 62 * 1024 * 1024


def _rms(x, g):
    ms = jnp.mean(x * x, axis=-1, keepdims=True)
    return x * lax.rsqrt(ms + EPS) * g


def _dot(a, b):
    return jnp.dot(a, b, preferred_element_type=F32)


def _const_spec(shape):
    nd = len(shape)
    return pl.BlockSpec(shape, lambda *_: (0,) * nd, pipeline_mode=pl.Buffered(1))


def _params(*sem, vmem=V7X_VMEM_LIMIT):
    return pltpu.CompilerParams(dimension_semantics=sem, vmem_limit_bytes=vmem)


def _ssm_prep_kernel(ld_ref, lr_r_ref, li_r_ref, lr_c_ref, li_c_ref, brt_ref, bit_ref, crt_t_ref, cit_t_ref, d_ref,
                     m_ref, s_ref, o_ref, apow_ref, bband_ref, cband_ref, arow_ref, *, gb, chunk):
    P, N = SSM_GROUP_CH, SSM_STATE
    LP = chunk * P
    bband_ref[...] = jnp.zeros(bband_ref.shape, bband_ref.dtype)
    cband_ref[...] = jnp.zeros(cband_ref.shape, cband_ref.dtype)
    lane_j = lax.broadcasted_iota(jnp.int32, (1, LP), 1) // P
    row_j = lax.broadcasted_iota(jnp.int32, (LP, 1), 0) // P
    lane_i = lax.broadcasted_iota(jnp.int32, (P, LP), 1)
    row_i = lax.broadcasted_iota(jnp.int32, (P, LP), 0)
    nbits = chunk.bit_length()

    def squarings(lr, li, dt):
        mag = jnp.exp(dt * lr)
        pr, pi = mag * jnp.cos(dt * li), mag * jnp.sin(dt * li)
        out = [(pr, pi)]
        for _ in range(nbits - 1):
            pr, pi = pr * pr - pi * pi, 2.0 * pr * pi
            out.append((pr, pi))
        return out

    def cpow(pows, j):
        er = ei = None
        for b, (pr, pi) in enumerate(pows):
            if isinstance(j, int):
                if not (j >> b) & 1:
                    continue
                er, ei = (pr, pi) if er is None else (er * pr - ei * pi, er * pi + ei * pr)
            else:
                bit = ((j >> b) & 1) == 1
                if er is None:
                    er, ei = jnp.where(bit, pr, 1.0), jnp.where(bit, pi, 0.0)
                else:
                    er, ei = jnp.where(bit, er * pr - ei * pi, er), jnp.where(bit, er * pi + ei * pr, ei)
        return er, ei

    for gi in range(gb):
        dt = jnp.exp(ld_ref[gi])
        lr_r, li_r = lr_r_ref[gi], li_r_ref[gi]
        pows_r = squarings(lr_r, li_r, dt)
        pows_c = squarings(lr_c_ref[gi], li_c_ref[gi], dt)

        ar, ai = pows_r[0]
        den = lr_r * lr_r + li_r * li_r
        cr = ((ar - 1.0) * lr_r + ai * li_r) / den
        ci = (ai * lr_r - (ar - 1.0) * li_r) / den

        brt, bit = brt_ref[gi], bit_ref[gi]
        bbt_re = cr * brt - ci * bit
        bbt_im = cr * bit + ci * brt
        bbt_re_t = jnp.concatenate([bbt_re] * chunk, axis=0)
        bbt_im_t = jnp.concatenate([bbt_im] * chunk, axis=0)

        er, ei = cpow(pows_r, (chunk - 1) - row_j)
        s_ref[gi, :, 0:N] = er * bbt_re_t - ei * bbt_im_t
        s_ref[gi, :, N:2 * N] = er * bbt_im_t + ei * bbt_re_t

        crt_t, cit_t = crt_t_ref[gi], cit_t_ref[gi]
        er, ei = cpow(pows_c, lane_j)
        r_re = crt_t * er - cit_t * ei
        r_im = crt_t * ei + cit_t * er
        er, ei = cpow(pows_c, lane_j + 1)
        o_ref[gi, 0:N, :] = (crt_t * er - cit_t * ei).astype(o_ref.dtype)
        o_ref[gi, N:2 * N, :] = (-(crt_t * ei + cit_t * er)).astype(o_ref.dtype)

        krow = (jnp.dot(bbt_re, r_re, preferred_element_type=F32, precision=lax.Precision.HIGHEST)
                - jnp.dot(bbt_im, r_im, preferred_element_type=F32, precision=lax.Precision.HIGHEST))
        krow = krow + jnp.where(lane_i == row_i, d_ref[gi], 0.0)
        for ti in range(chunk):
            blk = krow if ti == 0 else pltpu.roll(krow, ti * P, axis=1)
            blk = jnp.where(lane_i >= ti * P, blk, 0.0)
            m_ref[gi, ti * P:(ti + 1) * P, :] = blk.astype(m_ref.dtype)

        apow_ref[gi] = jnp.concatenate(cpow(pows_r, chunk), axis=0)

        rows, cols = slice(gi * P, (gi + 1) * P), slice(gi * N, (gi + 1) * N)
        cols_im = slice((gb + gi) * N, (gb + gi + 1) * N)
        bband_ref[rows, cols] = bbt_re
        bband_ref[rows, cols_im] = bbt_im
        cband_ref[cols, rows] = crt_t[:, rows]
        cband_ref[cols_im, rows] = -cit_t[:, rows]
        arow_ref[0:1, cols] = ar
        arow_ref[1:2, cols] = ai


def _ssm_prep(A_re, A_im, log_dt, B_re, B_im, C_re, C_im, D, *, chunk, gb=8):
    G, N = A_re.shape
    P = SSM_GROUP_CH
    LP = chunk * P
    brt = B_re.transpose(0, 2, 1)
    bit = B_im.transpose(0, 2, 1)
    crt_t = jnp.tile(C_re.transpose(0, 2, 1), (1, 1, chunk))
    cit_t = jnp.tile(C_im.transpose(0, 2, 1), (1, 1, chunk))
    ins = [log_dt.reshape(G, 1, 1), A_re.reshape(G, 1, N), A_im.reshape(G, 1, N),
           A_re.reshape(G, N, 1), A_im.reshape(G, N, 1), brt, bit, crt_t, cit_t, D.reshape(G, P, 1)]
    gspec = lambda *s: pl.BlockSpec((gb,) + s, lambda i: (i, 0, 0))
    in_specs = [gspec(*a.shape[1:]) for a in ins]
    out_shape = [jax.ShapeDtypeStruct((G, LP, LP), BF16), jax.ShapeDtypeStruct((G, LP, 2 * N), F32),
                 jax.ShapeDtypeStruct((G, 2 * N, LP), BF16), jax.ShapeDtypeStruct((G, 2, N), F32)]
    out_specs = [gspec(*o.shape[1:]) for o in out_shape]
    band_shape = [jax.ShapeDtypeStruct((G // gb, gb * P, 2 * gb * N), F32),
                  jax.ShapeDtypeStruct((G // gb, 2 * gb * N, gb * P), F32),
                  jax.ShapeDtypeStruct((G // gb, 2, gb * N), F32)]
    out_shape += band_shape
    out_specs += [pl.BlockSpec((None,) + o.shape[1:], lambda i: (i, 0, 0)) for o in band_shape]
    return pl.pallas_call(
        functools.partial(_ssm_prep_kernel, gb=gb, chunk=chunk),
        grid=(G // gb,), in_specs=in_specs, out_specs=out_specs, out_shape=out_shape,
        compiler_params=_params("arbitrary"), name="ssm_prep")(*ins)


def _conv3_steps_apart(v, w_ref, s_ref, prev, *, ts):
    tm = v.shape[0]
    s_ref[0:2 * ts, :] = prev
    s_ref[2 * ts:2 * ts + tm, :] = v
    out = w_ref[0:1, :] * s_ref[0:tm, :] + w_ref[1:2, :] * s_ref[ts:ts + tm, :] + w_ref[2:3, :] * v
    return out, s_ref[tm:tm + 2 * ts, :]


def _conv3_rows(v, w_ref, c_ref):
    tm = v.shape[0]
    head = c_ref[...]
    row = lax.broadcasted_iota(jnp.int32, head.shape, 0)

    def back(k):
        r = pltpu.roll(v, k, axis=0)
        first = jnp.where(row < k, pltpu.roll(head, k, axis=0), r[0:8])
        return jnp.concatenate([first, r[8:]], axis=0)

    out = w_ref[0:1, :] * back(2) + w_ref[1:2, :] * back(1) + w_ref[2:3, :] * v
    c_ref[...] = v[tm - 8:tm]
    return out


def _piece_transpose8(v):
    width = LANES // 8
    piece = lax.broadcasted_iota(jnp.int32, v[0].shape, 1) // width
    for bit in range(3):
        s = 1 << bit
        hi = ((piece >> bit) & 1) == 1
        nv = list(v)
        for i in range(8):
            if i & s:
                continue
            a, b = v[i], v[i + s]
            nv[i] = jnp.where(hi, pltpu.roll(b, s * width, axis=1), a)
            nv[i + s] = jnp.where(hi, b, pltpu.roll(a, LANES - s * width, axis=1))
        v = nv
    return v


def _tokens_to_groups(zu, u_ref, *, nseg, seg):
    nc = seg // SSM_CHUNK
    ms = zu.shape[1]
    z_tb = jnp.transpose(zu.reshape(nseg, seg, ms), (1, 0, 2))
    for h in range(SSM_CHUNK // 8):
        for q in range(ms // LANES):
            xs = [jnp.concatenate([z_tb[SSM_CHUNK * c + 8 * h + i, :, q * LANES:(q + 1) * LANES] for c in range(nc)],
                                  axis=0) for i in range(8)]
            w = _piece_transpose8(xs)
            for k in range(8):
                u_ref[8 * q + k, :, h * LANES:(h + 1) * LANES] = w[k].astype(u_ref.dtype)


def _groups_to_tokens(y_ref, *, nseg, seg):
    nc = seg // SSM_CHUNK
    nq = y_ref.shape[0] // 8
    tiles = [[None] * nq for _ in range(seg)]
    for h in range(SSM_CHUNK // 8):
        for q in range(nq):
            v = _piece_transpose8([y_ref[8 * q + k, :, h * LANES:(h + 1) * LANES].astype(F32) for k in range(8)])
            for i in range(8):
                for c in range(nc):
                    tiles[SSM_CHUNK * c + 8 * h + i][q] = v[i][c * nseg:(c + 1) * nseg]
    y_tb = jnp.stack([jnp.concatenate(row, axis=1) for row in tiles], axis=0)
    return jnp.transpose(y_tb, (1, 0, 2)).reshape(nseg * seg, nq * LANES)


def _mix_in_core(x, g_ref, w_ref, gc_ref, conv, *, ms, mc):
    hb = _rms(x, g_ref[...]).astype(BF16)
    zu = _dot(hb, w_ref[:, 0:ms])
    xin = _dot(hb, w_ref[:, ms:ms + mc])
    cg = _dot(hb, w_ref[:, ms + 2 * mc:ms + 3 * mc])
    cv = conv(cg * xin)
    bg = _dot(hb, w_ref[:, ms + mc:ms + 2 * mc])
    return zu, _rms(bg * cv, gc_ref[...]).astype(BF16)


def _mix_in_kernel(x_ref, g_ref, w_ref, cw_ref, gc_ref, u_ref, yc_ref, buf_ref, s_ref,
                   *, nseg, seg, nsub, ms, mc):
    sub = seg // nsub
    rg = sub // SSM_CHUNK * nseg

    @pl.when(pl.program_id(0) == 0)
    def _():
        s_ref[...] = jnp.zeros(s_ref.shape, F32)

    def conv(v):
        return jnp.concatenate([_conv3_rows(v[b * sub:(b + 1) * sub], cw_ref, s_ref.at[b]) for b in range(nseg)],
                               axis=0)

    for h in range(nsub):
        steps = slice(h * sub, (h + 1) * sub)
        zu, ycn = _mix_in_core(x_ref[:, steps, :].reshape(nseg * sub, x_ref.shape[-1]), g_ref, w_ref, gc_ref, conv,
                               ms=ms, mc=mc)
        _tokens_to_groups(zu, u_ref.at[:, pl.ds(h * rg, rg), :], nseg=nseg, seg=sub)
        yc_ref[:, steps, :] = ycn.reshape(nseg, sub, mc)
    buf_ref[...] = s_ref[:, 6:8, :]


def _mix_in(x3d, g, w_bf, conv_w, g_conv, *, seg, ms, mc):
    nseg, tlen, d = x3d.shape
    G = ms // SSM_GROUP_CH
    lp = SSM_CHUNK * SSM_GROUP_CH
    row_spec = lambda c: pl.BlockSpec((nseg, seg, c), lambda t: (0, t, 0))
    in_specs = [row_spec(d), _const_spec((1, d)), _const_spec(w_bf.shape), _const_spec(conv_w.shape),
                _const_spec((1, mc))]
    return pl.pallas_call(
        functools.partial(_mix_in_kernel, nseg=nseg, seg=seg, nsub=MIX_SUBBLOCKS, ms=ms, mc=mc),
        grid=(tlen // seg,), in_specs=in_specs,
        out_specs=[pl.BlockSpec((G, seg // SSM_CHUNK * nseg, lp), lambda t: (0, t, 0)), row_spec(mc),
                   pl.BlockSpec((nseg, 2, mc), lambda t: (0, 0, 0))],
        out_shape=[jax.ShapeDtypeStruct((G, tlen // SSM_CHUNK * nseg, lp), BF16),
                   jax.ShapeDtypeStruct((nseg, tlen, mc), BF16), jax.ShapeDtypeStruct((nseg, 2, mc), F32)],
        scratch_shapes=[pltpu.VMEM((nseg, 8, mc), F32)],
        compiler_params=_params("arbitrary"), name="mix_in")(x3d, g.reshape(1, d), w_bf, conv_w,
                                                              g_conv.reshape(1, mc))


def _ssm_kernel(u_ref, m_ref, s_ref, o_ref, ap_ref, y_ref, fr_ref, fi_ref, lre, lim, ire, iim, *, gb, nb, nchunk):
    n = SSM_STATE
    for gi in range(gb):
        loc = _dot(u_ref[gi], s_ref[gi].astype(BF16))
        lre[gi] = loc[:, 0:n]
        lim[gi] = loc[:, n:2 * n]
    ar = [jnp.broadcast_to(ap_ref[gi, 0:1, :], (nb, n)) for gi in range(gb)]
    ai = [jnp.broadcast_to(ap_ref[gi, 1:2, :], (nb, n)) for gi in range(gb)]
    st0 = tuple((jnp.zeros((nb, n), F32), jnp.zeros((nb, n), F32)) for _ in range(gb))

    def step(c, st):
        r = pl.multiple_of(c * nb, nb)
        new = []
        for gi in range(gb):
            sr, si = st[gi]
            ire[gi, pl.ds(r, nb), :] = sr
            iim[gi, pl.ds(r, nb), :] = si
            nr = ar[gi] * sr - ai[gi] * si + lre[gi, pl.ds(r, nb), :]
            ni = ar[gi] * si + ai[gi] * sr + lim[gi, pl.ds(r, nb), :]
            new.append((nr, ni))
        return tuple(new)

    st = lax.fori_loop(0, nchunk, step, st0, unroll=4)
    for gi in range(gb):
        fr_ref[gi] = st[gi][0]
        fi_ref[gi] = st[gi][1]
        ini = jnp.concatenate([ire[gi], iim[gi]], axis=1).astype(BF16)
        y = _dot(u_ref[gi], m_ref[gi]) + _dot(ini, o_ref[gi])
        y_ref[gi] = y.astype(y_ref.dtype)


def _ssm(u_g, mats, apow, *, nb, nchunk, gb=4):
    G, R, LP = u_g.shape
    n = SSM_STATE
    ins = [u_g, *mats, apow]
    gspec = lambda *s: pl.BlockSpec((gb,) + s, lambda i: (i, 0, 0))
    out_shape = [jax.ShapeDtypeStruct((G, R, LP), F32), jax.ShapeDtypeStruct((G, nb, n), F32),
                 jax.ShapeDtypeStruct((G, nb, n), F32)]
    return pl.pallas_call(
        functools.partial(_ssm_kernel, gb=gb, nb=nb, nchunk=nchunk),
        grid=(G // gb,), in_specs=[gspec(*a.shape[1:]) for a in ins],
        out_specs=[gspec(*o.shape[1:]) for o in out_shape],
        out_shape=out_shape, scratch_shapes=[pltpu.VMEM((gb, R, n), F32)] * 4,
        compiler_params=_params("arbitrary"), name="ssm_chunk")(*ins)


def _mix_out_core(x, ys, yc, wg_ref, bg_ref, gs_ref, wo_ref, gx_ref, wq_ref, *, ms, q_scale):
    y = jax.nn.gelu(ys)
    y = y * jax.nn.sigmoid(_dot(y.astype(BF16), wg_ref[...]) + bg_ref[...])
    ysn = _rms(y, gs_ref[...]).astype(BF16)
    x1 = x + _dot(jnp.concatenate([ysn, yc], axis=1), wo_ref[...])
    h = _rms(x1, gx_ref[...]).astype(BF16)
    return x1, _dot(h, wq_ref[...]) * q_scale


def _mix_out_kernel(x_ref, ys_ref, yc_ref, wg_ref, bg_ref, gs_ref, wo_ref, gx_ref, wq_ref, x1_ref, q_ref,
                    *, nseg, seg, nsub, ms, q_scale):
    sub = seg // nsub
    rg = sub // SSM_CHUNK * nseg
    d = x_ref.shape[-1]
    for h in range(nsub):
        steps = slice(h * sub, (h + 1) * sub)
        ys = _groups_to_tokens(ys_ref.at[:, pl.ds(h * rg, rg), :], nseg=nseg, seg=sub)
        x1, q = _mix_out_core(x_ref[:, steps, :].reshape(nseg * sub, d),
                              ys, yc_ref[:, steps, :].reshape(nseg * sub, yc_ref.shape[-1]),
                              wg_ref, bg_ref, gs_ref, wo_ref, gx_ref, wq_ref, ms=ms, q_scale=q_scale)
        x1_ref[:, steps, :] = x1.reshape(nseg, sub, d)
        q_ref[:, steps, :] = q.astype(q_ref.dtype).reshape(nseg, sub, d)


def _mix_out(x3d, ys, yc, w_glu, b_glu, g_ssm, w_out, g_x, w_q, *, seg):
    nseg, tlen, d = x3d.shape
    ms = w_glu.shape[0]
    hd = d // XATTN_HEADS
    row_spec = lambda c: pl.BlockSpec((nseg, seg, c), lambda t: (0, t, 0))
    ys_spec = pl.BlockSpec((ys.shape[0], seg // SSM_CHUNK * nseg, ys.shape[2]), lambda t: (0, t, 0))
    in_specs = [row_spec(d), ys_spec, row_spec(yc.shape[-1]), _const_spec(w_glu.shape), _const_spec((1, ms)),
                _const_spec((1, ms)), _const_spec(w_out.shape), _const_spec((1, d)), _const_spec(w_q.shape)]
    return pl.pallas_call(
        functools.partial(_mix_out_kernel, nseg=nseg, seg=seg, nsub=MIX_SUBBLOCKS, ms=ms, q_scale=hd ** -0.5),
        grid=(tlen // seg,), in_specs=in_specs, out_specs=[row_spec(d), row_spec(d)],
        out_shape=[jax.ShapeDtypeStruct((nseg, tlen, d), F32), jax.ShapeDtypeStruct((nseg, tlen, d), BF16)],
        compiler_params=_params("parallel"), name="mix_out")(
            x3d, ys, yc, w_glu, b_glu.reshape(1, ms), g_ssm.reshape(1, ms), w_out, g_x.reshape(1, d), w_q)


def _step_mix_kernel(x_ref, g_ref, w_ref, cw_ref, gc_ref, prev_ref, s0r_ref, s0i_ref, bband_ref, cband_ref, arow_ref,
                     d_ref, wg_ref, bg_ref, gs_ref, wo_ref, gx_ref, wq_ref,
                     x1_ref, q_ref, buf_ref, fr_ref, fi_ref, s_ref, ys_ref, *, nseq, tlen, ms, mc, q_scale):
    x = x_ref[...]

    def conv(v):
        out, buf_ref[...] = _conv3_steps_apart(v, cw_ref, s_ref, prev_ref[...], ts=nseq)
        return out

    zu, ycn = _mix_in_core(x, g_ref, w_ref, gc_ref, conv, ms=ms, mc=mc)

    wu, ws = bband_ref.shape[1], arow_ref.shape[2]
    for i in range(bband_ref.shape[0]):
        ucols, scols = slice(i * wu, (i + 1) * wu), slice(i * ws, (i + 1) * ws)
        sr, si = s0r_ref[:, scols], s0i_ref[:, scols]
        ar, ai = arow_ref[i, 0:1, :], arow_ref[i, 1:2, :]
        bband, cband = bband_ref[i].astype(BF16), cband_ref[i].astype(BF16)
        for t in range(tlen):
            ut = zu[t * nseq:(t + 1) * nseq, ucols]
            bu = _dot(ut.astype(BF16), bband)
            sr, si = ar * sr - ai * si + bu[:, 0:ws], ar * si + ai * sr + bu[:, ws:]
            cs = _dot(jnp.concatenate([sr, si], axis=1).astype(BF16), cband)
            ys_ref[t * nseq:(t + 1) * nseq, ucols] = cs + d_ref[:, ucols] * ut
        fr_ref[:, scols] = sr
        fi_ref[:, scols] = si

    x1, q = _mix_out_core(x, ys_ref[...], ycn, wg_ref, bg_ref, gs_ref, wo_ref, gx_ref, wq_ref, ms=ms, q_scale=q_scale)
    x1_ref[...] = x1
    q_ref[...] = q.astype(q_ref.dtype)


def _step_mix(x2d, p, conv_prev, s0_re, s0_im, bands, d_row, *, nseq, tlen):
    rows, d = x2d.shape
    ms = p["w_glu"].shape[0]
    mc = p["conv_w"].shape[1]
    hd = d // XATTN_HEADS
    bband, cband, arow = bands
    ins = [x2d, p["norm_mix"].reshape(1, d), p["w_in"], p["conv_w"], p["norm_conv_out"].reshape(1, mc), conv_prev,
           s0_re, s0_im, bband, cband, arow, d_row, p["w_glu"], p["b_glu"].reshape(1, ms),
           p["norm_ssm_out"].reshape(1, ms), p["w_out"], p["norm_xattn"].reshape(1, d), p["w_q"]]
    full = lambda shape: pl.BlockSpec(shape, lambda i: (0,) * len(shape))
    out_shape = [jax.ShapeDtypeStruct((rows, d), F32), jax.ShapeDtypeStruct((rows, d), F32),
                 jax.ShapeDtypeStruct(conv_prev.shape, F32), jax.ShapeDtypeStruct(s0_re.shape, F32),
                 jax.ShapeDtypeStruct(s0_im.shape, F32)]
    return pl.pallas_call(
        functools.partial(_step_mix_kernel, nseq=nseq, tlen=tlen, ms=ms, mc=mc, q_scale=hd ** -0.5),
        grid=(1,), in_specs=[_const_spec(a.shape) for a in ins], out_specs=[full(o.shape) for o in out_shape],
        out_shape=out_shape,
        scratch_shapes=[pltpu.VMEM((2 * nseq + rows, mc), F32), pltpu.VMEM((rows, ms), F32)],
        compiler_params=_params("arbitrary"), name="step_mix")(*ins)


def _kv_kernel(m_ref, g_ref, wk_ref, wv_ref, k_ref, v_ref):
    h = _rms(m_ref[...], g_ref[...]).astype(BF16)
    k_ref[...] = _dot(h, wk_ref[...])
    v_ref[...] = _dot(h, wv_ref[...])


def _kv_proj(mem2d, g, w_k, w_v, *, tm):
    rows, d = mem2d.shape
    row_spec = pl.BlockSpec((tm, d), lambda i: (i, 0))
    return pl.pallas_call(
        _kv_kernel, grid=(rows // tm,),
        in_specs=[row_spec, _const_spec((1, d)), _const_spec(w_k.shape), _const_spec(w_v.shape)],
        out_specs=[row_spec, row_spec],
        out_shape=[jax.ShapeDtypeStruct((rows, d), F32)] * 2,
        compiler_params=_params("parallel"), name="kv_proj")(mem2d, g.reshape(1, d), w_k, w_v)


def _softmax_attention(qs, ks, vs):
    tq = qs[0].shape[0]
    s = jnp.concatenate([lax.dot_general(q, k, (((1,), (1,)), ((), ())), preferred_element_type=F32)
                         for q, k in zip(qs, ks)], axis=0)
    p = jnp.exp(s - jnp.max(s, axis=-1, keepdims=True))
    p = p / jnp.sum(p, axis=-1, keepdims=True)
    return [_dot(p[i * tq:(i + 1) * tq].astype(BF16), v) for i, v in enumerate(vs)]


def _head_slices(ref, rows, hd):
    return [ref[rows, h * hd:(h + 1) * hd].astype(BF16) for h in range(XATTN_HEADS)]


def _attend_sequences(q_ref, k_ref, v_ref, o_ref, *, sb, tq):
    hd = q_ref.shape[1] // XATTN_HEADS
    qs, ks, vs = [], [], []
    for j in range(sb):
        qs += _head_slices(q_ref, slice(j * tq, (j + 1) * tq), hd)
        ks += _head_slices(k_ref, slice(j * N_MEM, (j + 1) * N_MEM), hd)
        vs += _head_slices(v_ref, slice(j * N_MEM, (j + 1) * N_MEM), hd)
    outs = _softmax_attention(qs, ks, vs)
    for j in range(sb):
        heads = outs[j * XATTN_HEADS:(j + 1) * XATTN_HEADS]
        o_ref[j * tq:(j + 1) * tq, :] = jnp.concatenate(heads, axis=1).astype(o_ref.dtype)


def _ffn_kernel(*refs, ts, has_prev, has_kv, side):
    x_ref, o_ref = refs[:2]
    refs = refs[2:]
    if has_kv:
        k_ref, v_ref = refs[:2]
        refs = refs[2:]
    if side:
        sq_ref, sk_ref, sv_ref = refs[:3]
        refs = refs[3:]
    wxo_ref, gf_ref, wup_ref, wgate_ref, cw_ref, wdn_ref, gl_ref = refs[:7]
    refs = refs[7:]
    if has_prev:
        prev_ref, refs = refs[0], refs[1:]
    if side:
        y_ref, buf_ref, so_ref, s_ref = refs
        _attend_sequences(sq_ref, sk_ref, sv_ref, so_ref, sb=side[0], tq=side[1])
    else:
        y_ref, buf_ref, s_ref = refs
    if not has_prev:
        @pl.when(pl.program_id(1) == 0)
        def _():
            s_ref[...] = jnp.zeros(s_ref.shape, F32)

    if has_kv:
        hd = o_ref.shape[1] // XATTN_HEADS
        rows = slice(None)
        heads = _softmax_attention(_head_slices(o_ref, rows, hd), _head_slices(k_ref, rows, hd),
                                   _head_slices(v_ref, rows, hd))
        o = jnp.concatenate(heads, axis=1).astype(BF16)
    else:
        o = o_ref[...].astype(BF16)
    x2 = x_ref[...] + _dot(o, wxo_ref[...])
    hb = _rms(x2, gf_ref[...]).astype(BF16)
    up = _dot(hb, wup_ref[...])
    if has_prev:
        a, buf = _conv3_steps_apart(up, cw_ref, s_ref, prev_ref[...], ts=ts)
    else:
        a = _conv3_rows(up, cw_ref, s_ref)
        buf = s_ref[6:8, :]
    buf_ref[...] = buf.reshape(buf_ref.shape)
    act = (jax.nn.gelu(a) * _dot(hb, wgate_ref[...])).astype(BF16)
    x3 = x2 + _dot(act, wdn_ref[...])
    y_ref[...] = _rms(x3, gl_ref[...])


def _ffn(x2d, o2d, kv, w_xo, g_ffn, w_up, w_gate, conv_w, w_down, g_last, prev, *, nb, nt, tm, ts, side=None):
    rows, d = x2d.shape
    dff = w_up.shape[1]
    has_prev = prev is not None
    row_spec = pl.BlockSpec((tm, d), lambda b, t: (b * nt + t, 0))
    in_specs = [row_spec, row_spec]
    ins = [x2d, o2d]
    if kv is not None:
        in_specs += [pl.BlockSpec((N_MEM, d), lambda b, t: (b, 0))] * 2
        ins += list(kv)
    side_static = None
    if side is not None:
        sq, sk, sv, tq = side
        sb = sq.shape[0] // tq // (nb * nt)
        assert sb * nb * nt * tq == sq.shape[0]
        side_static = (sb, tq)
        side_q_spec = pl.BlockSpec((sb * tq, d), lambda b, t: (b * nt + t, 0))
        in_specs += [side_q_spec] + [pl.BlockSpec((sb * N_MEM, d), lambda b, t: (b * nt + t, 0))] * 2
        ins += [sq, sk, sv]
    in_specs += [_const_spec(w_xo.shape), _const_spec((1, d)), _const_spec(w_up.shape), _const_spec(w_gate.shape),
                 _const_spec(conv_w.shape), _const_spec(w_down.shape), _const_spec((1, d))]
    ins += [w_xo, g_ffn.reshape(1, d), w_up, w_gate, conv_w, w_down, g_last.reshape(1, d)]
    if has_prev:
        in_specs.append(_const_spec(prev.shape))
        ins.append(prev)
        buf_shape = jax.ShapeDtypeStruct((2 * ts, dff), F32)
        buf_spec = pl.BlockSpec((2 * ts, dff), lambda b, t: (0, 0))
    else:
        buf_shape = jax.ShapeDtypeStruct((nb, 2 * ts, dff), F32)
        buf_spec = pl.BlockSpec((None, 2 * ts, dff), lambda b, t: (b, 0, 0))
    out_specs = [row_spec, buf_spec]
    out_shape = [jax.ShapeDtypeStruct((rows, d), F32), buf_shape]
    if side is not None:
        out_specs.append(side_q_spec)
        out_shape.append(jax.ShapeDtypeStruct(sq.shape, F32))
    return pl.pallas_call(
        functools.partial(_ffn_kernel, ts=ts, has_prev=has_prev, has_kv=kv is not None, side=side_static),
        grid=(nb, nt), in_specs=in_specs, out_specs=out_specs, out_shape=out_shape,
        scratch_shapes=[pltpu.VMEM((2 * ts + tm if has_prev else 8, dff), F32)],
        compiler_params=_params("parallel", "arbitrary", vmem=V7X_VMEM_LIMIT if side is None else V7X_VMEM_LIMIT_MAX),
        name="conv_ffn")(*ins)


def _prompt_layer(x3d, k2d, v2d, p, mats, apow, g_last, side):
    nseq, tlen, d = x3d.shape
    rows = nseq * tlen
    ms = p["w_glu"].shape[0]
    mc = p["conv_w"].shape[1]
    u, ycn, conv_buf = _mix_in(x3d, p["norm_mix"], p["w_in"], p["conv_w"], p["norm_conv_out"],
                               seg=MIX_STEPS, ms=ms, mc=mc)
    y_g, f_re, f_im = _ssm(u, mats, apow, nb=nseq, nchunk=tlen // SSM_CHUNK)
    x1, q = _mix_out(x3d, y_g, ycn, p["w_glu"], p["b_glu"], p["norm_ssm_out"], p["w_out"], p["norm_xattn"],
                     p["w_q"], seg=MIX_STEPS)
    y, ffn_buf, side_o = _ffn(x1.reshape(rows, d), q.reshape(rows, d), (k2d, v2d), p["w_xo"], p["norm_ffn"],
                              p["w_up"], p["w_gate"], p["ffn_conv_w"], p["w_down"], g_last, None,
                              nb=nseq, nt=tlen // FFN_ROWS, tm=FFN_ROWS, ts=1, side=side)
    return y, (f_re.transpose(1, 0, 2), f_im.transpose(1, 0, 2), conv_buf, ffn_buf), side_o


def _step_rows(a):
    return a.transpose(1, 0, 2).reshape(a.shape[0] * a.shape[1], a.shape[2])


def _seq_rows(a, nseq):
    return a.reshape(a.shape[0] // nseq, nseq, a.shape[1]).transpose(1, 0, 2)


def _step_layer_mix(x_bm, prev, p, bands, d_row):
    nseq, tlen, d = x_bm.shape
    G, N = prev[0].shape[1:]
    x1, q, conv_buf, f_re, f_im = _step_mix(_step_rows(x_bm), p, _step_rows(prev[2]), prev[0].reshape(nseq, G * N),
                                            prev[1].reshape(nseq, G * N), bands, d_row, nseq=nseq, tlen=tlen)
    state = (f_re.reshape(nseq, G, N), f_im.reshape(nseq, G, N), _seq_rows(conv_buf, nseq))
    return x1, _seq_rows(q, nseq).reshape(nseq * tlen, d), state


def _step_layer_ffn(x1, o_bm, prev_ffn, p, g_last, *, nseq):
    rows, d = x1.shape
    o = _step_rows(o_bm.reshape(nseq, rows // nseq, d))
    y, ffn_buf = _ffn(x1, o, None, p["w_xo"], p["norm_ffn"], p["w_up"], p["w_gate"], p["ffn_conv_w"], p["w_down"],
                      g_last, _step_rows(prev_ffn), nb=1, nt=1, tm=rows, ts=nseq)
    return _seq_rows(y, nseq), _seq_rows(ffn_buf, nseq)


def kernel(x_prompt, x_sample, mem_prompt, cache_mem_k, cache_mem_v, state_ssm_re, state_ssm_im, state_conv, state_ffn_conv, norm_mix, w_in, ssm_A_re, ssm_A_im, ssm_log_dt, ssm_B_re, ssm_B_im, ssm_C_re, ssm_C_im, ssm_D, w_glu, b_glu, conv_w, norm_ssm_out, norm_conv_out, w_out, norm_xattn, norm_mem, w_q, w_k, w_v, w_xo, norm_ffn, w_up, w_gate, ffn_conv_w, w_down, norm_final):
    depth = w_in.shape[0]
    assert depth == 1, "the final norm is fused into the last (only) layer's ConvFFN kernel"
    nbp, tp, d = x_prompt.shape
    nbs = x_sample.shape[0]
    assert tp % ATTN_ROWS == 0 and tp % FFN_ROWS == 0 and tp % MIX_STEPS == 0 and nbp == 8
    hd = d // XATTN_HEADS

    xp, xs = x_prompt, x_sample
    outs_p, outs_s, mk_p, mv_p = [], [], [], []
    for l in range(depth):
        p = dict(norm_mix=norm_mix[l], w_in=w_in[l].astype(BF16), w_glu=w_glu[l].astype(BF16), b_glu=b_glu[l],
                 conv_w=conv_w[l], norm_ssm_out=norm_ssm_out[l], norm_conv_out=norm_conv_out[l],
                 w_out=w_out[l].astype(BF16), norm_xattn=norm_xattn[l], w_q=w_q[l].astype(BF16),
                 w_xo=w_xo[l].astype(BF16), norm_ffn=norm_ffn[l], w_up=w_up[l].astype(BF16),
                 w_gate=w_gate[l].astype(BF16), ffn_conv_w=ffn_conv_w[l], w_down=w_down[l].astype(BF16))
        *mats, apow, bband, cband, arow = _ssm_prep(ssm_A_re[l], ssm_A_im[l], ssm_log_dt[l], ssm_B_re[l],
                                                    ssm_B_im[l], ssm_C_re[l], ssm_C_im[l], ssm_D[l], chunk=SSM_CHUNK)

        k2d, v2d = _kv_proj(mem_prompt.reshape(nbp * N_MEM, d), norm_mem[l], w_k[l].astype(BF16),
                            w_v[l].astype(BF16), tm=ATTN_ROWS)
        mk_p.append(k2d.reshape(nbp, N_MEM, XATTN_HEADS, hd))
        mv_p.append(v2d.reshape(nbp, N_MEM, XATTN_HEADS, hd))

        x1s, qs, st_s = _step_layer_mix(xs, (state_ssm_re[l], state_ssm_im[l], state_conv[l]), p,
                                        (bband, cband, arow), ssm_D[l].reshape(1, -1))
        side = (qs, cache_mem_k[l].reshape(nbs * N_MEM, d), cache_mem_v[l].reshape(nbs * N_MEM, d), xs.shape[1])
        xp, st_p, os_bm = _prompt_layer(xp, k2d, v2d, p, mats, apow, norm_final, side)
        outs_p.append(st_p)
        xs, ffn_buf_s = _step_layer_ffn(x1s, os_bm, state_ffn_conv[l], p, norm_final, nseq=nbs)
        outs_s.append(st_s + (ffn_buf_s,))

    stack = lambda outs, i: jnp.stack([o[i] for o in outs])
    return (xp.reshape(nbp, tp, d), xs, jnp.stack(mk_p), jnp.stack(mv_p),
            stack(outs_p, 0), stack(outs_p, 1), stack(outs_p, 2), stack(outs_p, 3),
            stack(outs_s, 0), stack(outs_s, 1), stack(outs_s, 2), stack(outs_s, 3))
```

```python
import functools

import jax
import jax.numpy as jnp
from jax import lax
from jax.experimental import pallas as pl
from jax.experimental.pallas import tpu as pltpu

F32 = jnp.float32
BF16 = jnp.bfloat16

EPS = 1e-6
SSM_GROUP_CH = 16
SSM_STATE = 64
CONV_K = 3
N_MEM = 256
XATTN_HEADS = 4
SSM_CHUNK = 16
LANES = 128
MIX_STEPS = 128
MIX_SUBBLOCKS = 2
ATTN_ROWS = 512
FFN_ROWS = 512
V7X_VMEM_LIMIT = 56 * 1024 * 1024
V7X_VMEM_LIMIT_MAX = 62 * 1024 * 1024


def _rms(x, g):
    ms = jnp.mean(x * x, axis=-1, keepdims=True)
    return x * lax.rsqrt(ms + EPS) * g


def _dot(a, b):
    return jnp.dot(a, b, preferred_element_type=F32)


def _const_spec(shape):
    nd = len(shape)
    return pl.BlockSpec(shape, lambda *_: (0,) * nd, pipeline_mode=pl.Buffered(1))


def _params(*sem, vmem=V7X_VMEM_LIMIT):
    return pltpu.CompilerParams(dimension_semantics=sem, vmem_limit_bytes=vmem)


def _ssm_prep_kernel(ld_ref, lr_r_ref, li_r_ref, lr_c_ref, li_c_ref, brt_ref, bit_ref, crt_t_ref, cit_t_ref, d_ref,
                     m_ref, s_ref, o_ref, apow_ref, bband_ref, cband_ref, arow_ref, *, gb, chunk):
    P, N = SSM_GROUP_CH, SSM_STATE
    LP = chunk * P
    bband_ref[...] = jnp.zeros(bband_ref.shape, bband_ref.dtype)
    cband_ref[...] = jnp.zeros(cband_ref.shape, cband_ref.dtype)
    lane_j = lax.broadcasted_iota(jnp.int32, (1, LP), 1) // P
    row_j = lax.broadcasted_iota(jnp.int32, (LP, 1), 0) // P
    lane_i = lax.broadcasted_iota(jnp.int32, (P, LP), 1)
    row_i = lax.broadcasted_iota(jnp.int32, (P, LP), 0)
    nbits = chunk.bit_length()

    def squarings(lr, li, dt):
        mag = jnp.exp(dt * lr)
        pr, pi = mag * jnp.cos(dt * li), mag * jnp.sin(dt * li)
        out = [(pr, pi)]
        for _ in range(nbits - 1):
            pr, pi = pr * pr - pi * pi, 2.0 * pr * pi
            out.append((pr, pi))
        return out

    def cpow(pows, j):
        er = ei = None
        for b, (pr, pi) in enumerate(pows):
            if isinstance(j, int):
                if not (j >> b) & 1:
                    continue
                er, ei = (pr, pi) if er is None else (er * pr - ei * pi, er * pi + ei * pr)
            else:
                bit = ((j >> b) & 1) == 1
                if er is None:
                    er, ei = jnp.where(bit, pr, 1.0), jnp.where(bit, pi, 0.0)
                else:
                    er, ei = jnp.where(bit, er * pr - ei * pi, er), jnp.where(bit, er * pi + ei * pr, ei)
        return er, ei

    for gi in range(gb):
        dt = jnp.exp(ld_ref[gi])
        lr_r, li_r = lr_r_ref[gi], li_r_ref[gi]
        pows_r = squarings(lr_r, li_r, dt)
        pows_c = squarings(lr_c_ref[gi], li_c_ref[gi], dt)

        ar, ai = pows_r[0]
        den = lr_r * lr_r + li_r * li_r
        cr = ((ar - 1.0) * lr_r + ai * li_r) / den
        ci = (ai * lr_r - (ar - 1.0) * li_r) / den

        brt, bit = brt_ref[gi], bit_ref[gi]
        bbt_re = cr * brt - ci * bit
        bbt_im = cr * bit + ci * brt
        bbt_re_t = jnp.concatenate([bbt_re] * chunk, axis=0)
        bbt_im_t = jnp.concatenate([bbt_im] * chunk, axis=0)

        er, ei = cpow(pows_r, (chunk - 1) - row_j)
        s_ref[gi, :, 0:N] = er * bbt_re_t - ei * bbt_im_t
        s_ref[gi, :, N:2 * N] = er * bbt_im_t + ei * bbt_re_t

        crt_t, cit_t = crt_t_ref[gi], cit_t_ref[gi]
        er, ei = cpow(pows_c, lane_j)
        r_re = crt_t * er - cit_t * ei
        r_im = crt_t * ei + cit_t * er
        er, ei = cpow(pows_c, lane_j + 1)
        o_ref[gi, 0:N, :] = (crt_t * er - cit_t * ei).astype(o_ref.dtype)
        o_ref[gi, N:2 * N, :] = (-(crt_t * ei + cit_t * er)).astype(o_ref.dtype)

        krow = (jnp.dot(bbt_re, r_re, preferred_element_type=F32, precision=lax.Precision.HIGHEST)
                - jnp.dot(bbt_im, r_im, preferred_element_type=F32, precision=lax.Precision.HIGHEST))
        krow = krow + jnp.where(lane_i == row_i, d_ref[gi], 0.0)
        for ti in range(chunk):
            blk = krow if ti == 0 else pltpu.roll(krow, ti * P, axis=1)
            blk = jnp.where(lane_i >= ti * P, blk, 0.0)
            m_ref[gi, ti * P:(ti + 1) * P, :] = blk.astype(m_ref.dtype)

        apow_ref[gi] = jnp.concatenate(cpow(pows_r, chunk), axis=0)

        rows, cols = slice(gi * P, (gi + 1) * P), slice(gi * N, (gi + 1) * N)
        cols_im = slice((gb + gi) * N, (gb + gi + 1) * N)
        bband_ref[rows, cols] = bbt_re
        bband_ref[rows, cols_im] = bbt_im
        cband_ref[cols, rows] = crt_t[:, rows]
        cband_ref[cols_im, rows] = -cit_t[:, rows]
        arow_ref[0:1, cols] = ar
        arow_ref[1:2, cols] = ai


def _ssm_prep(A_re, A_im, log_dt, B_re, B_im, C_re, C_im, D, *, chunk, gb=8):
    G, N = A_re.shape
    P = SSM_GROUP_CH
    LP = chunk * P
    brt = B_re.transpose(0, 2, 1)
    bit = B_im.transpose(0, 2, 1)
    crt_t = jnp.tile(C_re.transpose(0, 2, 1), (1, 1, chunk))
    cit_t = jnp.tile(C_im.transpose(0, 2, 1), (1, 1, chunk))
    ins = [log_dt.reshape(G, 1, 1), A_re.reshape(G, 1, N), A_im.reshape(G, 1, N),
           A_re.reshape(G, N, 1), A_im.reshape(G, N, 1), brt, bit, crt_t, cit_t, D.reshape(G, P, 1)]
    gspec = lambda *s: pl.BlockSpec((gb,) + s, lambda i: (i, 0, 0))
    in_specs = [gspec(*a.shape[1:]) for a in ins]
    out_shape = [jax.ShapeDtypeStruct((G, LP, LP), BF16), jax.ShapeDtypeStruct((G, LP, 2 * N), F32),
                 jax.ShapeDtypeStruct((G, 2 * N, LP), BF16), jax.ShapeDtypeStruct((G, 2, N), F32)]
    out_specs = [gspec(*o.shape[1:]) for o in out_shape]
    band_shape = [jax.ShapeDtypeStruct((G // gb, gb * P, 2 * gb * N), F32),
                  jax.ShapeDtypeStruct((G // gb, 2 * gb * N, gb * P), F32),
                  jax.ShapeDtypeStruct((G // gb, 2, gb * N), F32)]
    out_shape += band_shape
    out_specs += [pl.BlockSpec((None,) + o.shape[1:], lambda i: (i, 0, 0)) for o in band_shape]
    return pl.pallas_call(
        functools.partial(_ssm_prep_kernel, gb=gb, chunk=chunk),
        grid=(G // gb,), in_specs=in_specs, out_specs=out_specs, out_shape=out_shape,
        compiler_params=_params("arbitrary"), name="ssm_prep")(*ins)


def _conv3_steps_apart(v, w_ref, s_ref, prev, *, ts):
    tm = v.shape[0]
    s_ref[0:2 * ts, :] = prev
    s_ref[2 * ts:2 * ts + tm, :] = v
    out = w_ref[0:1, :] * s_ref[0:tm, :] + w_ref[1:2, :] * s_ref[ts:ts + tm, :] + w_ref[2:3, :] * v
    return out, s_ref[tm:tm + 2 * ts, :]


def _conv3_rows(v, w_ref, c_ref):
    tm = v.shape[0]
    head = c_ref[...]
    row = lax.broadcasted_iota(jnp.int32, head.shape, 0)

    def back(k):
        r = pltpu.roll(v, k, axis=0)
        first = jnp.where(row < k, pltpu.roll(head, k, axis=0), r[0:8])
        return jnp.concatenate([first, r[8:]], axis=0)

    out = w_ref[0:1, :] * back(2) + w_ref[1:2, :] * back(1) + w_ref[2:3, :] * v
    c_ref[...] = v[tm - 8:tm]
    return out


def _piece_transpose8(v):
    width = LANES // 8
    piece = lax.broadcasted_iota(jnp.int32, v[0].shape, 1) // width
    for bit in range(3):
        s = 1 << bit
        hi = ((piece >> bit) & 1) == 1
        nv = list(v)
        for i in range(8):
            if i & s:
                continue
            a, b = v[i], v[i + s]
            nv[i] = jnp.where(hi, pltpu.roll(b, s * width, axis=1), a)
            nv[i + s] = jnp.where(hi, b, pltpu.roll(a, LANES - s * width, axis=1))
        v = nv
    return v


def _tokens_to_groups(zu, u_ref, *, nseg, seg):
    nc = seg // SSM_CHUNK
    ms = zu.shape[1]
    z_tb = jnp.transpose(zu.reshape(nseg, seg, ms), (1, 0, 2))
    for h in range(SSM_CHUNK // 8):
        for q in range(ms // LANES):
            xs = [jnp.concatenate([z_tb[SSM_CHUNK * c + 8 * h + i, :, q * LANES:(q + 1) * LANES] for c in range(nc)],
                                  axis=0) for i in range(8)]
            w = _piece_transpose8(xs)
            for k in range(8):
                u_ref[8 * q + k, :, h * LANES:(h + 1) * LANES] = w[k].astype(u_ref.dtype)


def _groups_to_tokens(y_ref, *, nseg, seg):
    nc = seg // SSM_CHUNK
    nq = y_ref.shape[0] // 8
    tiles = [[None] * nq for _ in range(seg)]
    for h in range(SSM_CHUNK // 8):
        for q in range(nq):
            v = _piece_transpose8([y_ref[8 * q + k, :, h * LANES:(h + 1) * LANES].astype(F32) for k in range(8)])
            for i in range(8):
                for c in range(nc):
                    tiles[SSM_CHUNK * c + 8 * h + i][q] = v[i][c * nseg:(c + 1) * nseg]
    y_tb = jnp.stack([jnp.concatenate(row, axis=1) for row in tiles], axis=0)
    return jnp.transpose(y_tb, (1, 0, 2)).reshape(nseg * seg, nq * LANES)


def _mix_in_core(x, g_ref, w_ref, gc_ref, conv, *, ms, mc):
    hb = _rms(x, g_ref[...]).astype(BF16)
    zu = _dot(hb, w_ref[:, 0:ms])
    xin = _dot(hb, w_ref[:, ms:ms + mc])
    cg = _dot(hb, w_ref[:, ms + 2 * mc:ms + 3 * mc])
    cv = conv(cg * xin)
    bg = _dot(hb, w_ref[:, ms + mc:ms + 2 * mc])
    return zu, _rms(bg * cv, gc_ref[...]).astype(BF16)


def _mix_in_kernel(x_ref, g_ref, w_ref, cw_ref, gc_ref, u_ref, yc_ref, buf_ref, s_ref,
                   *, nseg, seg, nsub, ms, mc):
    sub = seg // nsub
    rg = sub // SSM_CHUNK * nseg

    @pl.when(pl.program_id(0) == 0)
    def _():
        s_ref[...] = jnp.zeros(s_ref.shape, F32)

    def conv(v):
        return jnp.concatenate([_conv3_rows(v[b * sub:(b + 1) * sub], cw_ref, s_ref.at[b]) for b in range(nseg)],
                               axis=0)

    for h in range(nsub):
        steps = slice(h * sub, (h + 1) * sub)
        zu, ycn = _mix_in_core(x_ref[:, steps, :].reshape(nseg * sub, x_ref.shape[-1]), g_ref, w_ref, gc_ref, conv,
                               ms=ms, mc=mc)
        _tokens_to_groups(zu, u_ref.at[:, pl.ds(h * rg, rg), :], nseg=nseg, seg=sub)
        yc_ref[:, steps, :] = ycn.reshape(nseg, sub, mc)
    buf_ref[...] = s_ref[:, 6:8, :]


def _mix_in(x3d, g, w_bf, conv_w, g_conv, *, seg, ms, mc):
    nseg, tlen, d = x3d.shape
    G = ms // SSM_GROUP_CH
    lp = SSM_CHUNK * SSM_GROUP_CH
    row_spec = lambda c: pl.BlockSpec((nseg, seg, c), lambda t: (0, t, 0))
    in_specs = [row_spec(d), _const_spec((1, d)), _const_spec(w_bf.shape), _const_spec(conv_w.shape),
                _const_spec((1, mc))]
    return pl.pallas_call(
        functools.partial(_mix_in_kernel, nseg=nseg, seg=seg, nsub=MIX_SUBBLOCKS, ms=ms, mc=mc),
        grid=(tlen // seg,), in_specs=in_specs,
        out_specs=[pl.BlockSpec((G, seg // SSM_CHUNK * nseg, lp), lambda t: (0, t, 0)), row_spec(mc),
                   pl.BlockSpec((nseg, 2, mc), lambda t: (0, 0, 0))],
        out_shape=[jax.ShapeDtypeStruct((G, tlen // SSM_CHUNK * nseg, lp), BF16),
                   jax.ShapeDtypeStruct((nseg, tlen, mc), BF16), jax.ShapeDtypeStruct((nseg, 2, mc), F32)],
        scratch_shapes=[pltpu.VMEM((nseg, 8, mc), F32)],
        compiler_params=_params("arbitrary"), name="mix_in")(x3d, g.reshape(1, d), w_bf, conv_w,
                                                              g_conv.reshape(1, mc))


def _ssm_kernel(u_ref, m_ref, s_ref, o_ref, ap_ref, y_ref, fr_ref, fi_ref, lre, lim, ire, iim, *, gb, nb, nchunk):
    n = SSM_STATE
    for gi in range(gb):
        loc = _dot(u_ref[gi], s_ref[gi].astype(BF16))
        lre[gi] = loc[:, 0:n]
        lim[gi] = loc[:, n:2 * n]
    ar = [jnp.broadcast_to(ap_ref[gi, 0:1, :], (nb, n)) for gi in range(gb)]
    ai = [jnp.broadcast_to(ap_ref[gi, 1:2, :], (nb, n)) for gi in range(gb)]
    st0 = tuple((jnp.zeros((nb, n), F32), jnp.zeros((nb, n), F32)) for _ in range(gb))

    def step(c, st):
        r = pl.multiple_of(c * nb, nb)
        new = []
        for gi in range(gb):
            sr, si = st[gi]
            ire[gi, pl.ds(r, nb), :] = sr
            iim[gi, pl.ds(r, nb), :] = si
            nr = ar[gi] * sr - ai[gi] * si + lre[gi, pl.ds(r, nb), :]
            ni = ar[gi] * si + ai[gi] * sr + lim[gi, pl.ds(r, nb), :]
            new.append((nr, ni))
        return tuple(new)

    st = lax.fori_loop(0, nchunk, step, st0, unroll=4)
    for gi in range(gb):
        fr_ref[gi] = st[gi][0]
        fi_ref[gi] = st[gi][1]
        ini = jnp.concatenate([ire[gi], iim[gi]], axis=1).astype(BF16)
        y = _dot(u_ref[gi], m_ref[gi]) + _dot(ini, o_ref[gi])
        y_ref[gi] = y.astype(y_ref.dtype)


def _ssm(u_g, mats, apow, *, nb, nchunk, gb=4):
    G, R, LP = u_g.shape
    n = SSM_STATE
    ins = [u_g, *mats, apow]
    gspec = lambda *s: pl.BlockSpec((gb,) + s, lambda i: (i, 0, 0))
    out_shape = [jax.ShapeDtypeStruct((G, R, LP), F32), jax.ShapeDtypeStruct((G, nb, n), F32),
                 jax.ShapeDtypeStruct((G, nb, n), F32)]
    return pl.pallas_call(
        functools.partial(_ssm_kernel, gb=gb, nb=nb, nchunk=nchunk),
        grid=(G // gb,), in_specs=[gspec(*a.shape[1:]) for a in ins],
        out_specs=[gspec(*o.shape[1:]) for o in out_shape],
        out_shape=out_shape, scratch_shapes=[pltpu.VMEM((gb, R, n), F32)] * 4,
        compiler_params=_params("arbitrary"), name="ssm_chunk")(*ins)


def _mix_out_core(x, ys, yc, wg_ref, bg_ref, gs_ref, wo_ref, gx_ref, wq_ref, *, ms, q_scale):
    y = jax.nn.gelu(ys)
    y = y * jax.nn.sigmoid(_dot(y.astype(BF16), wg_ref[...]) + bg_ref[...])
    ysn = _rms(y, gs_ref[...]).astype(BF16)
    x1 = x + _dot(jnp.concatenate([ysn, yc], axis=1), wo_ref[...])
    h = _rms(x1, gx_ref[...]).astype(BF16)
    return x1, _dot(h, wq_ref[...]) * q_scale


def _mix_out_kernel(x_ref, ys_ref, yc_ref, wg_ref, bg_ref, gs_ref, wo_ref, gx_ref, wq_ref, x1_ref, q_ref,
                    *, nseg, seg, nsub, ms, q_scale):
    sub = seg // nsub
    rg = sub // SSM_CHUNK * nseg
    d = x_ref.shape[-1]
    for h in range(nsub):
        steps = slice(h * sub, (h + 1) * sub)
        ys = _groups_to_tokens(ys_ref.at[:, pl.ds(h * rg, rg), :], nseg=nseg, seg=sub)
        x1, q = _mix_out_core(x_ref[:, steps, :].reshape(nseg * sub, d),
                              ys, yc_ref[:, steps, :].reshape(nseg * sub, yc_ref.shape[-1]),
                              wg_ref, bg_ref, gs_ref, wo_ref, gx_ref, wq_ref, ms=ms, q_scale=q_scale)
        x1_ref[:, steps, :] = x1.reshape(nseg, sub, d)
        q_ref[:, steps, :] = q.astype(q_ref.dtype).reshape(nseg, sub, d)


def _mix_out(x3d, ys, yc, w_glu, b_glu, g_ssm, w_out, g_x, w_q, *, seg):
    nseg, tlen, d = x3d.shape
    ms = w_glu.shape[0]
    hd = d // XATTN_HEADS
    row_spec = lambda c: pl.BlockSpec((nseg, seg, c), lambda t: (0, t, 0))
    ys_spec = pl.BlockSpec((ys.shape[0], seg // SSM_CHUNK * nseg, ys.shape[2]), lambda t: (0, t, 0))
    in_specs = [row_spec(d), ys_spec, row_spec(yc.shape[-1]), _const_spec(w_glu.shape), _const_spec((1, ms)),
                _const_spec((1, ms)), _const_spec(w_out.shape), _const_spec((1, d)), _const_spec(w_q.shape)]
    return pl.pallas_call(
        functools.partial(_mix_out_kernel, nseg=nseg, seg=seg, nsub=MIX_SUBBLOCKS, ms=ms, q_scale=hd ** -0.5),
        grid=(tlen // seg,), in_specs=in_specs, out_specs=[row_spec(d), row_spec(d)],
        out_shape=[jax.ShapeDtypeStruct((nseg, tlen, d), F32), jax.ShapeDtypeStruct((nseg, tlen, d), BF16)],
        compiler_params=_params("parallel"), name="mix_out")(
            x3d, ys, yc, w_glu, b_glu.reshape(1, ms), g_ssm.reshape(1, ms), w_out, g_x.reshape(1, d), w_q)


def _step_mix_kernel(x_ref, g_ref, w_ref, cw_ref, gc_ref, prev_ref, s0r_ref, s0i_ref, bband_ref, cband_ref, arow_ref,
                     d_ref, wg_ref, bg_ref, gs_ref, wo_ref, gx_ref, wq_ref,
                     x1_ref, q_ref, buf_ref, fr_ref, fi_ref, s_ref, ys_ref, *, nseq, tlen, ms, mc, q_scale):
    x = x_ref[...]

    def conv(v):
        out, buf_ref[...] = _conv3_steps_apart(v, cw_ref, s_ref, prev_ref[...], ts=nseq)
        return out

    zu, ycn = _mix_in_core(x, g_ref, w_ref, gc_ref, conv, ms=ms, mc=mc)

    wu, ws = bband_ref.shape[1], arow_ref.shape[2]
    for i in range(bband_ref.shape[0]):
        ucols, scols = slice(i * wu, (i + 1) * wu), slice(i * ws, (i + 1) * ws)
        sr, si = s0r_ref[:, scols], s0i_ref[:, scols]
        ar, ai = arow_ref[i, 0:1, :], arow_ref[i, 1:2, :]
        bband, cband = bband_ref[i].astype(BF16), cband_ref[i].astype(BF16)
        for t in range(tlen):
            ut = zu[t * nseq:(t + 1) * nseq, ucols]
            bu = _dot(ut.astype(BF16), bband)
            sr, si = ar * sr - ai * si + bu[:, 0:ws], ar * si + ai * sr + bu[:, ws:]
            cs = _dot(jnp.concatenate([sr, si], axis=1).astype(BF16), cband)
            ys_ref[t * nseq:(t + 1) * nseq, ucols] = cs + d_ref[:, ucols] * ut
        fr_ref[:, scols] = sr
        fi_ref[:, scols] = si

    x1, q = _mix_out_core(x, ys_ref[...], ycn, wg_ref, bg_ref, gs_ref, wo_ref, gx_ref, wq_ref, ms=ms, q_scale=q_scale)
    x1_ref[...] = x1
    q_ref[...] = q.astype(q_ref.dtype)


def _step_mix(x2d, p, conv_prev, s0_re, s0_im, bands, d_row, *, nseq, tlen):
    rows, d = x2d.shape
    ms = p["w_glu"].shape[0]
    mc = p["conv_w"].shape[1]
    hd = d // XATTN_HEADS
    bband, cband, arow = bands
    ins = [x2d, p["norm_mix"].reshape(1, d), p["w_in"], p["conv_w"], p["norm_conv_out"].reshape(1, mc), conv_prev,
           s0_re, s0_im, bband, cband, arow, d_row, p["w_glu"], p["b_glu"].reshape(1, ms),
           p["norm_ssm_out"].reshape(1, ms), p["w_out"], p["norm_xattn"].reshape(1, d), p["w_q"]]
    full = lambda shape: pl.BlockSpec(shape, lambda i: (0,) * len(shape))
    out_shape = [jax.ShapeDtypeStruct((rows, d), F32), jax.ShapeDtypeStruct((rows, d), F32),
                 jax.ShapeDtypeStruct(conv_prev.shape, F32), jax.ShapeDtypeStruct(s0_re.shape, F32),
                 jax.ShapeDtypeStruct(s0_im.shape, F32)]
    return pl.pallas_call(
        functools.partial(_step_mix_kernel, nseq=nseq, tlen=tlen, ms=ms, mc=mc, q_scale=hd ** -0.5),
        grid=(1,), in_specs=[_const_spec(a.shape) for a in ins], out_specs=[full(o.shape) for o in out_shape],
        out_shape=out_shape,
        scratch_shapes=[pltpu.VMEM((2 * nseq + rows, mc), F32), pltpu.VMEM((rows, ms), F32)],
        compiler_params=_params("arbitrary"), name="step_mix")(*ins)


def _kv_kernel(m_ref, g_ref, wk_ref, wv_ref, k_ref, v_ref):
    h = _rms(m_ref[...], g_ref[...]).astype(BF16)
    k_ref[...] = _dot(h, wk_ref[...])
    v_ref[...] = _dot(h, wv_ref[...])


def _kv_proj(mem2d, g, w_k, w_v, *, tm):
    rows, d = mem2d.shape
    row_spec = pl.BlockSpec((tm, d), lambda i: (i, 0))
    return pl.pallas_call(
        _kv_kernel, grid=(rows // tm,),
        in_specs=[row_spec, _const_spec((1, d)), _const_spec(w_k.shape), _const_spec(w_v.shape)],
        out_specs=[row_spec, row_spec],
        out_shape=[jax.ShapeDtypeStruct((rows, d), F32)] * 2,
        compiler_params=_params("parallel"), name="kv_proj")(mem2d, g.reshape(1, d), w_k, w_v)


def _softmax_attention(qs, ks, vs):
    tq = qs[0].shape[0]
    s = jnp.concatenate([lax.dot_general(q, k, (((1,), (1,)), ((), ())), preferred_element_type=F32)
                         for q, k in zip(qs, ks)], axis=0)
    p = jnp.exp(s - jnp.max(s, axis=-1, keepdims=True))
    p = p / jnp.sum(p, axis=-1, keepdims=True)
    return [_dot(p[i * tq:(i + 1) * tq].astype(BF16), v) for i, v in enumerate(vs)]


def _head_slices(ref, rows, hd):
    return [ref[rows, h * hd:(h + 1) * hd].astype(BF16) for h in range(XATTN_HEADS)]


def _attend_sequences(q_ref, k_ref, v_ref, o_ref, *, sb, tq):
    hd = q_ref.shape[1] // XATTN_HEADS

    def heads_of(ref, j):
        hm = jnp.transpose(ref[j], (1, 0, 2))
        return [hm[h].astype(BF16) for h in range(XATTN_HEADS)]

    qs, ks, vs = [], [], []
    for j in range(sb):
        qs += _head_slices(q_ref, slice(j * tq, (j + 1) * tq), hd)
        ks += heads_of(k_ref, j)
        vs += heads_of(v_ref, j)
    outs = _softmax_attention(qs, ks, vs)
    for j in range(sb):
        heads = outs[j * XATTN_HEADS:(j + 1) * XATTN_HEADS]
        o_ref[j * tq:(j + 1) * tq, :] = jnp.concatenate(heads, axis=1).astype(o_ref.dtype)


def _ffn_kernel(*refs, ts, has_prev, has_kv, side):
    x_ref, o_ref = refs[:2]
    refs = refs[2:]
    if has_kv:
        k_ref, v_ref = refs[:2]
        refs = refs[2:]
    if side:
        sq_ref, sk_ref, sv_ref = refs[:3]
        refs = refs[3:]
    wxo_ref, gf_ref, wup_ref, wgate_ref, cw_ref, wdn_ref, gl_ref = refs[:7]
    refs = refs[7:]
    if has_prev:
        prev_ref, refs = refs[0], refs[1:]
    if side:
        y_ref, buf_ref, so_ref, s_ref = refs
        _attend_sequences(sq_ref, sk_ref, sv_ref, so_ref, sb=side[0], tq=side[1])
    else:
        y_ref, buf_ref, s_ref = refs
    if not has_prev:
        @pl.when(pl.program_id(1) == 0)
        def _():
            s_ref[...] = jnp.zeros(s_ref.shape, F32)

    if has_kv:
        hd = o_ref.shape[1] // XATTN_HEADS
        rows = slice(None)
        heads = _softmax_attention(_head_slices(o_ref, rows, hd), _head_slices(k_ref, rows, hd),
                                   _head_slices(v_ref, rows, hd))
        o = jnp.concatenate(heads, axis=1).astype(BF16)
    else:
        o = o_ref[...].astype(BF16)
    x2 = x_ref[...] + _dot(o, wxo_ref[...])
    hb = _rms(x2, gf_ref[...]).astype(BF16)
    up = _dot(hb, wup_ref[...])
    if has_prev:
        a, buf = _conv3_steps_apart(up, cw_ref, s_ref, prev_ref[...], ts=ts)
    else:
        a = _conv3_rows(up, cw_ref, s_ref)
        buf = s_ref[6:8, :]
    buf_ref[...] = buf.reshape(buf_ref.shape)
    act = (jax.nn.gelu(a) * _dot(hb, wgate_ref[...])).astype(BF16)
    x3 = x2 + _dot(act, wdn_ref[...])
    y_ref[...] = _rms(x3, gl_ref[...])


def _ffn(x2d, o2d, kv, w_xo, g_ffn, w_up, w_gate, conv_w, w_down, g_last, prev, *, nb, nt, tm, ts, side=None):
    rows, d = x2d.shape
    dff = w_up.shape[1]
    has_prev = prev is not None
    row_spec = pl.BlockSpec((tm, d), lambda b, t: (b * nt + t, 0))
    in_specs = [row_spec, row_spec]
    ins = [x2d, o2d]
    if kv is not None:
        in_specs += [pl.BlockSpec((N_MEM, d), lambda b, t: (b, 0))] * 2
        ins += list(kv)
    side_static = None
    if side is not None:
        sq, sk, sv, tq = side
        sb = sq.shape[0] // tq // (nb * nt)
        assert sb * nb * nt * tq == sq.shape[0]
        side_static = (sb, tq)
        side_q_spec = pl.BlockSpec((sb * tq, d), lambda b, t: (b * nt + t, 0))
        in_specs += [side_q_spec] + [pl.BlockSpec((sb,) + sk.shape[1:], lambda b, t: (b * nt + t, 0, 0, 0))] * 2
        ins += [sq, sk, sv]
    in_specs += [_const_spec(w_xo.shape), _const_spec((1, d)), _const_spec(w_up.shape), _const_spec(w_gate.shape),
                 _const_spec(conv_w.shape), _const_spec(w_down.shape), _const_spec((1, d))]
    ins += [w_xo, g_ffn.reshape(1, d), w_up, w_gate, conv_w, w_down, g_last.reshape(1, d)]
    if has_prev:
        in_specs.append(_const_spec(prev.shape))
        ins.append(prev)
        buf_shape = jax.ShapeDtypeStruct((2 * ts, dff), F32)
        buf_spec = pl.BlockSpec((2 * ts, dff), lambda b, t: (0, 0))
    else:
        buf_shape = jax.ShapeDtypeStruct((nb, 2 * ts, dff), F32)
        buf_spec = pl.BlockSpec((None, 2 * ts, dff), lambda b, t: (b, 0, 0))
    out_specs = [row_spec, buf_spec]
    out_shape = [jax.ShapeDtypeStruct((rows, d), F32), buf_shape]
    if side is not None:
        out_specs.append(side_q_spec)
        out_shape.append(jax.ShapeDtypeStruct(sq.shape, F32))
    return pl.pallas_call(
        functools.partial(_ffn_kernel, ts=ts, has_prev=has_prev, has_kv=kv is not None, side=side_static),
        grid=(nb, nt), in_specs=in_specs, out_specs=out_specs, out_shape=out_shape,
        scratch_shapes=[pltpu.VMEM((2 * ts + tm if has_prev else 8, dff), F32)],
        compiler_params=_params("parallel", "arbitrary", vmem=V7X_VMEM_LIMIT if side is None else V7X_VMEM_LIMIT_MAX),
        name="conv_ffn")(*ins)


def _prompt_layer(x3d, k2d, v2d, p, mats, apow, g_last, side):
    nseq, tlen, d = x3d.shape
    rows = nseq * tlen
    ms = p["w_glu"].shape[0]
    mc = p["conv_w"].shape[1]
    u, ycn, conv_buf = _mix_in(x3d, p["norm_mix"], p["w_in"], p["conv_w"], p["norm_conv_out"],
                               seg=MIX_STEPS, ms=ms, mc=mc)
    y_g, f_re, f_im = _ssm(u, mats, apow, nb=nseq, nchunk=tlen // SSM_CHUNK)
    x1, q = _mix_out(x3d, y_g, ycn, p["w_glu"], p["b_glu"], p["norm_ssm_out"], p["w_out"], p["norm_xattn"],
                     p["w_q"], seg=MIX_STEPS)
    y, ffn_buf, side_o = _ffn(x1.reshape(rows, d), q.reshape(rows, d), (k2d, v2d), p["w_xo"], p["norm_ffn"],
                              p["w_up"], p["w_gate"], p["ffn_conv_w"], p["w_down"], g_last, None,
                              nb=nseq, nt=tlen // FFN_ROWS, tm=FFN_ROWS, ts=1, side=side)
    return y, (f_re.transpose(1, 0, 2), f_im.transpose(1, 0, 2), conv_buf, ffn_buf), side_o


def _step_rows(a):
    return a.transpose(1, 0, 2).reshape(a.shape[0] * a.shape[1], a.shape[2])


def _seq_rows(a, nseq):
    return a.reshape(a.shape[0] // nseq, nseq, a.shape[1]).transpose(1, 0, 2)


def _step_layer_mix(x_bm, prev, p, bands, d_row):
    nseq, tlen, d = x_bm.shape
    G, N = prev[0].shape[1:]
    x1, q, conv_buf, f_re, f_im = _step_mix(_step_rows(x_bm), p, _step_rows(prev[2]), prev[0].reshape(nseq, G * N),
                                            prev[1].reshape(nseq, G * N), bands, d_row, nseq=nseq, tlen=tlen)
    state = (f_re.reshape(nseq, G, N), f_im.reshape(nseq, G, N), _seq_rows(conv_buf, nseq))
    return x1, _seq_rows(q, nseq).reshape(nseq * tlen, d), state


def _step_layer_ffn(x1, o_bm, prev_ffn, p, g_last, *, nseq):
    rows, d = x1.shape
    o = _step_rows(o_bm.reshape(nseq, rows // nseq, d))
    y, ffn_buf = _ffn(x1, o, None, p["w_xo"], p["norm_ffn"], p["w_up"], p["w_gate"], p["ffn_conv_w"], p["w_down"],
                      g_last, _step_rows(prev_ffn), nb=1, nt=1, tm=rows, ts=nseq)
    return _seq_rows(y, nseq), _seq_rows(ffn_buf, nseq)


def kernel(x_prompt, x_sample, mem_prompt, cache_mem_k, cache_mem_v, state_ssm_re, state_ssm_im, state_conv, state_ffn_conv, norm_mix, w_in, ssm_A_re, ssm_A_im, ssm_log_dt, ssm_B_re, ssm_B_im, ssm_C_re, ssm_C_im, ssm_D, w_glu, b_glu, conv_w, norm_ssm_out, norm_conv_out, w_out, norm_xattn, norm_mem, w_q, w_k, w_v, w_xo, norm_ffn, w_up, w_gate, ffn_conv_w, w_down, norm_final):
    depth = w_in.shape[0]
    assert depth == 1, "the final norm is fused into the last (only) layer's ConvFFN kernel"
    nbp, tp, d = x_prompt.shape
    nbs = x_sample.shape[0]
    assert tp % ATTN_ROWS == 0 and tp % FFN_ROWS == 0 and tp % MIX_STEPS == 0 and nbp == 8
    hd = d // XATTN_HEADS

    xp, xs = x_prompt, x_sample
    outs_p, outs_s, mk_p, mv_p = [], [], [], []
    for l in range(depth):
        p = dict(norm_mix=norm_mix[l], w_in=w_in[l].astype(BF16), w_glu=w_glu[l].astype(BF16), b_glu=b_glu[l],
                 conv_w=conv_w[l], norm_ssm_out=norm_ssm_out[l], norm_conv_out=norm_conv_out[l],
                 w_out=w_out[l].astype(BF16), norm_xattn=norm_xattn[l], w_q=w_q[l].astype(BF16),
                 w_xo=w_xo[l].astype(BF16), norm_ffn=norm_ffn[l], w_up=w_up[l].astype(BF16),
                 w_gate=w_gate[l].astype(BF16), ffn_conv_w=ffn_conv_w[l], w_down=w_down[l].astype(BF16))
        *mats, apow, bband, cband, arow = _ssm_prep(ssm_A_re[l], ssm_A_im[l], ssm_log_dt[l], ssm_B_re[l],
                                                    ssm_B_im[l], ssm_C_re[l], ssm_C_im[l], ssm_D[l], chunk=SSM_CHUNK)

        k2d, v2d = _kv_proj(mem_prompt.reshape(nbp * N_MEM, d), norm_mem[l], w_k[l].astype(BF16),
                            w_v[l].astype(BF16), tm=ATTN_ROWS)
        mk_p.append(k2d.reshape(nbp, N_MEM, XATTN_HEADS, hd))
        mv_p.append(v2d.reshape(nbp, N_MEM, XATTN_HEADS, hd))

        x1s, qs, st_s = _step_layer_mix(xs, (state_ssm_re[l], state_ssm_im[l], state_conv[l]), p,
                                        (bband, cband, arow), ssm_D[l].reshape(1, -1))
        side = (qs, cache_mem_k[l], cache_mem_v[l], xs.shape[1])
        xp, st_p, os_bm = _prompt_layer(xp, k2d, v2d, p, mats, apow, norm_final, side)
        outs_p.append(st_p)
        xs, ffn_buf_s = _step_layer_ffn(x1s, os_bm, state_ffn_conv[l], p, norm_final, nseq=nbs)
        outs_s.append(st_s + (ffn_buf_s,))

    stack = lambda outs, i: jnp.stack([o[i] for o in outs])
    return (xp.reshape(nbp, tp, d), xs, jnp.stack(mk_p), jnp.stack(mv_p),
            stack(outs_p, 0), stack(outs_p, 1), stack(outs_p, 2), stack(outs_p, 3),
            stack(outs_s, 0), stack(outs_s, 1), stack(outs_s, 2), stack(outs_s, 3))
```

```python
import functools

import jax
import jax.numpy as jnp
from jax import lax
from jax.experimental import pallas as pl
from jax.experimental.pallas import tpu as pltpu

F32 = jnp.float32
BF16 = jnp.bfloat16

EPS = 1e-6
SSM_GROUP_CH = 16
SSM_STATE = 64
CONV_K = 3
N_MEM = 256
XATTN_HEADS = 4
SSM_CHUNK = 16
LANES = 128
MIX_STEPS = 128
MIX_SUBBLOCKS = 2
ATTN_ROWS = 512
FFN_ROWS = 512
V7X_VMEM_LIMIT = 56 * 1024 * 1024
V7X_VMEM_LIMIT_MAX = 62 * 1024 * 1024


def _rms(x, g):
    ms = jnp.mean(x * x, axis=-1, keepdims=True)
    return x * lax.rsqrt(ms + EPS) * g


def _dot(a, b):
    return jnp.dot(a, b, preferred_element_type=F32)


def _const_spec(shape):
    nd = len(shape)
    return pl.BlockSpec(shape, lambda *_: (0,) * nd, pipeline_mode=pl.Buffered(1))


def _params(*sem, vmem=V7X_VMEM_LIMIT):
    return pltpu.CompilerParams(dimension_semantics=sem, vmem_limit_bytes=vmem)


def _ssm_prep_kernel(ld_ref, lr_r_ref, li_r_ref, brt_ref, bit_ref, crt_t_ref, cit_t_ref, d_ref,
                     m_ref, s_ref, o_ref, apow_ref, bband_ref, cband_ref, arow_ref, *, gb, chunk):
    P, N = SSM_GROUP_CH, SSM_STATE
    LP = chunk * P
    bband_ref[...] = jnp.zeros(bband_ref.shape, bband_ref.dtype)
    cband_ref[...] = jnp.zeros(cband_ref.shape, cband_ref.dtype)
    lane_j = lax.broadcasted_iota(jnp.int32, (1, LP), 1) // P
    row_j = lax.broadcasted_iota(jnp.int32, (LP, 1), 0) // P
    lane_i = lax.broadcasted_iota(jnp.int32, (P, LP), 1)
    row_i = lax.broadcasted_iota(jnp.int32, (P, LP), 0)
    nbits = chunk.bit_length()

    eye = lax.broadcasted_iota(jnp.int32, (N, N), 0) == lax.broadcasted_iota(jnp.int32, (N, N), 1)

    def as_column(row):
        return jnp.sum(jnp.where(eye, row, 0.0), axis=1, keepdims=True)

    def squarings(pr, pi):
        out = [(pr, pi)]
        for _ in range(nbits - 1):
            pr, pi = pr * pr - pi * pi, 2.0 * pr * pi
            out.append((pr, pi))
        return out

    def cpow(pows, j):
        er = ei = None
        for b, (pr, pi) in enumerate(pows):
            if isinstance(j, int):
                if not (j >> b) & 1:
                    continue
                er, ei = (pr, pi) if er is None else (er * pr - ei * pi, er * pi + ei * pr)
            else:
                bit = ((j >> b) & 1) == 1
                if er is None:
                    er, ei = jnp.where(bit, pr, 1.0), jnp.where(bit, pi, 0.0)
                else:
                    er, ei = jnp.where(bit, er * pr - ei * pi, er), jnp.where(bit, er * pi + ei * pr, ei)
        return er, ei

    for gi in range(gb):
        dt = jnp.exp(ld_ref[gi])
        lr_r, li_r = lr_r_ref[gi], li_r_ref[gi]
        mag = jnp.exp(dt * lr_r)
        ar, ai = mag * jnp.cos(dt * li_r), mag * jnp.sin(dt * li_r)
        pows_r = squarings(ar, ai)
        pows_c = squarings(as_column(ar), as_column(ai))

        den = lr_r * lr_r + li_r * li_r
        cr = ((ar - 1.0) * lr_r + ai * li_r) / den
        ci = (ai * lr_r - (ar - 1.0) * li_r) / den

        brt, bit = brt_ref[gi], bit_ref[gi]
        bbt_re = cr * brt - ci * bit
        bbt_im = cr * bit + ci * brt
        bbt_re_t = jnp.concatenate([bbt_re] * chunk, axis=0)
        bbt_im_t = jnp.concatenate([bbt_im] * chunk, axis=0)

        er, ei = cpow(pows_r, (chunk - 1) - row_j)
        s_ref[gi, :, 0:N] = er * bbt_re_t - ei * bbt_im_t
        s_ref[gi, :, N:2 * N] = er * bbt_im_t + ei * bbt_re_t

        crt_t, cit_t = crt_t_ref[gi], cit_t_ref[gi]
        er, ei = cpow(pows_c, lane_j)
        r_re = crt_t * er - cit_t * ei
        r_im = crt_t * ei + cit_t * er
        er, ei = er * pows_c[0][0] - ei * pows_c[0][1], er * pows_c[0][1] + ei * pows_c[0][0]
        o_ref[gi, 0:N, :] = (crt_t * er - cit_t * ei).astype(o_ref.dtype)
        o_ref[gi, N:2 * N, :] = (-(crt_t * ei + cit_t * er)).astype(o_ref.dtype)

        krow = (jnp.dot(bbt_re, r_re, preferred_element_type=F32, precision=lax.Precision.HIGHEST)
                - jnp.dot(bbt_im, r_im, preferred_element_type=F32, precision=lax.Precision.HIGHEST))
        krow = krow + jnp.where(lane_i == row_i, d_ref[gi], 0.0)
        for ti in range(chunk):
            blk = krow if ti == 0 else pltpu.roll(krow, ti * P, axis=1)
            blk = jnp.where(lane_i >= ti * P, blk, 0.0)
            m_ref[gi, ti * P:(ti + 1) * P, :] = blk.astype(m_ref.dtype)

        apow_ref[gi] = jnp.concatenate(cpow(pows_r, chunk), axis=0)

        rows, cols = slice(gi * P, (gi + 1) * P), slice(gi * N, (gi + 1) * N)
        cols_im = slice((gb + gi) * N, (gb + gi + 1) * N)
        bband_ref[rows, cols] = bbt_re
        bband_ref[rows, cols_im] = bbt_im
        cband_ref[cols, rows] = crt_t[:, rows]
        cband_ref[cols_im, rows] = -cit_t[:, rows]
        arow_ref[0:1, cols] = ar
        arow_ref[1:2, cols] = ai


def _ssm_prep(A_re, A_im, log_dt, B_re, B_im, C_re, C_im, D, *, chunk, gb=8):
    G, N = A_re.shape
    P = SSM_GROUP_CH
    LP = chunk * P
    brt = B_re.transpose(0, 2, 1)
    bit = B_im.transpose(0, 2, 1)
    crt_t = jnp.tile(C_re.transpose(0, 2, 1), (1, 1, chunk))
    cit_t = jnp.tile(C_im.transpose(0, 2, 1), (1, 1, chunk))
    ins = [log_dt.reshape(G, 1, 1), A_re.reshape(G, 1, N), A_im.reshape(G, 1, N), brt, bit, crt_t, cit_t,
           D.reshape(G, P, 1)]
    gspec = lambda *s: pl.BlockSpec((gb,) + s, lambda i: (i, 0, 0))
    in_specs = [gspec(*a.shape[1:]) for a in ins]
    out_shape = [jax.ShapeDtypeStruct((G, LP, LP), BF16), jax.ShapeDtypeStruct((G, LP, 2 * N), F32),
                 jax.ShapeDtypeStruct((G, 2 * N, LP), BF16), jax.ShapeDtypeStruct((G, 2, N), F32)]
    out_specs = [gspec(*o.shape[1:]) for o in out_shape]
    band_shape = [jax.ShapeDtypeStruct((G // gb, gb * P, 2 * gb * N), F32),
                  jax.ShapeDtypeStruct((G // gb, 2 * gb * N, gb * P), F32),
                  jax.ShapeDtypeStruct((G // gb, 2, gb * N), F32)]
    out_shape += band_shape
    out_specs += [pl.BlockSpec((None,) + o.shape[1:], lambda i: (i, 0, 0)) for o in band_shape]
    return pl.pallas_call(
        functools.partial(_ssm_prep_kernel, gb=gb, chunk=chunk),
        grid=(G // gb,), in_specs=in_specs, out_specs=out_specs, out_shape=out_shape,
        compiler_params=_params("arbitrary"), name="ssm_prep")(*ins)


def _conv3_steps_apart(v, w_ref, s_ref, prev, *, ts):
    tm = v.shape[0]
    s_ref[0:2 * ts, :] = prev
    s_ref[2 * ts:2 * ts + tm, :] = v
    out = w_ref[0:1, :] * s_ref[0:tm, :] + w_ref[1:2, :] * s_ref[ts:ts + tm, :] + w_ref[2:3, :] * v
    return out, s_ref[tm:tm + 2 * ts, :]


def _conv3_rows(v, w_ref, c_ref):
    tm = v.shape[0]
    head = c_ref[...]
    row = lax.broadcasted_iota(jnp.int32, head.shape, 0)

    def back(k):
        r = pltpu.roll(v, k, axis=0)
        first = jnp.where(row < k, pltpu.roll(head, k, axis=0), r[0:8])
        return jnp.concatenate([first, r[8:]], axis=0)

    out = w_ref[0:1, :] * back(2) + w_ref[1:2, :] * back(1) + w_ref[2:3, :] * v
    c_ref[...] = v[tm - 8:tm]
    return out


def _piece_transpose8(v):
    width = LANES // 8
    piece = lax.broadcasted_iota(jnp.int32, v[0].shape, 1) // width
    for bit in range(3):
        s = 1 << bit
        hi = ((piece >> bit) & 1) == 1
        nv = list(v)
        for i in range(8):
            if i & s:
                continue
            a, b = v[i], v[i + s]
            nv[i] = jnp.where(hi, pltpu.roll(b, s * width, axis=1), a)
            nv[i + s] = jnp.where(hi, b, pltpu.roll(a, LANES - s * width, axis=1))
        v = nv
    return v


def _tokens_to_groups(zu, u_ref, *, nseg, seg):
    nc = seg // SSM_CHUNK
    ms = zu.shape[1]
    z_tb = jnp.transpose(zu.reshape(nseg, seg, ms), (1, 0, 2))
    for h in range(SSM_CHUNK // 8):
        for q in range(ms // LANES):
            xs = [jnp.concatenate([z_tb[SSM_CHUNK * c + 8 * h + i, :, q * LANES:(q + 1) * LANES] for c in range(nc)],
                                  axis=0) for i in range(8)]
            w = _piece_transpose8(xs)
            for k in range(8):
                u_ref[8 * q + k, :, h * LANES:(h + 1) * LANES] = w[k].astype(u_ref.dtype)


def _groups_to_tokens(y_ref, *, nseg, seg):
    nc = seg // SSM_CHUNK
    nq = y_ref.shape[0] // 8
    tiles = [[None] * nq for _ in range(seg)]
    for h in range(SSM_CHUNK // 8):
        for q in range(nq):
            v = _piece_transpose8([y_ref[8 * q + k, :, h * LANES:(h + 1) * LANES].astype(F32) for k in range(8)])
            for i in range(8):
                for c in range(nc):
                    tiles[SSM_CHUNK * c + 8 * h + i][q] = v[i][c * nseg:(c + 1) * nseg]
    y_tb = jnp.stack([jnp.concatenate(row, axis=1) for row in tiles], axis=0)
    return jnp.transpose(y_tb, (1, 0, 2)).reshape(nseg * seg, nq * LANES)


def _mix_in_core(x, g_ref, w_ref, gc_ref, conv, *, ms, mc):
    hb = _rms(x, g_ref[...]).astype(BF16)
    zu = _dot(hb, w_ref[:, 0:ms])
    xin = _dot(hb, w_ref[:, ms:ms + mc])
    cg = _dot(hb, w_ref[:, ms + 2 * mc:ms + 3 * mc])
    cv = conv(cg * xin)
    bg = _dot(hb, w_ref[:, ms + mc:ms + 2 * mc])
    return zu, _rms(bg * cv, gc_ref[...]).astype(BF16)


def _mix_in_kernel(*refs, nseg, seg, nsub, ncast, ms, mc):
    x_ref, g_ref, w_ref, cw_ref, gc_ref = refs[:5]
    cast_in, (u_ref, yc_ref, buf_ref) = refs[5:5 + ncast], refs[5 + ncast:8 + ncast]
    cast_out, s_ref = refs[8 + ncast:8 + 2 * ncast], refs[8 + 2 * ncast]
    for src, dst in zip(cast_in, cast_out):
        dst[...] = src[...].astype(dst.dtype)

    sub = seg // nsub
    rg = sub // SSM_CHUNK * nseg

    @pl.when(pl.program_id(0) == 0)
    def _():
        s_ref[...] = jnp.zeros(s_ref.shape, F32)

    def conv(v):
        return jnp.concatenate([_conv3_rows(v[b * sub:(b + 1) * sub], cw_ref, s_ref.at[b]) for b in range(nseg)],
                               axis=0)

    for h in range(nsub):
        steps = slice(h * sub, (h + 1) * sub)
        zu, ycn = _mix_in_core(x_ref[:, steps, :].reshape(nseg * sub, x_ref.shape[-1]), g_ref, w_ref, gc_ref, conv,
                               ms=ms, mc=mc)
        _tokens_to_groups(zu, u_ref.at[:, pl.ds(h * rg, rg), :], nseg=nseg, seg=sub)
        yc_ref[:, steps, :] = ycn.reshape(nseg, sub, mc)
    buf_ref[...] = s_ref[:, 6:8, :]


def _mix_in(x3d, g, w_bf, conv_w, g_conv, to_bf16, *, seg, ms, mc):
    nseg, tlen, d = x3d.shape
    G = ms // SSM_GROUP_CH
    lp = SSM_CHUNK * SSM_GROUP_CH
    nsteps = tlen // seg
    row_spec = lambda c: pl.BlockSpec((nseg, seg, c), lambda t: (0, t, 0))
    slab_specs = [pl.BlockSpec((w.shape[0] // nsteps, w.shape[1]), lambda t: (t, 0)) for w in to_bf16]
    in_specs = [row_spec(d), _const_spec((1, d)), _const_spec(w_bf.shape), _const_spec(conv_w.shape),
                _const_spec((1, mc))] + slab_specs
    return pl.pallas_call(
        functools.partial(_mix_in_kernel, nseg=nseg, seg=seg, nsub=MIX_SUBBLOCKS, ncast=len(to_bf16), ms=ms, mc=mc),
        grid=(nsteps,), in_specs=in_specs,
        out_specs=[pl.BlockSpec((G, seg // SSM_CHUNK * nseg, lp), lambda t: (0, t, 0)), row_spec(mc),
                   pl.BlockSpec((nseg, 2, mc), lambda t: (0, 0, 0))] + slab_specs,
        out_shape=[jax.ShapeDtypeStruct((G, tlen // SSM_CHUNK * nseg, lp), BF16),
                   jax.ShapeDtypeStruct((nseg, tlen, mc), BF16), jax.ShapeDtypeStruct((nseg, 2, mc), F32)]
        + [jax.ShapeDtypeStruct(w.shape, BF16) for w in to_bf16],
        scratch_shapes=[pltpu.VMEM((nseg, 8, mc), F32)],
        compiler_params=_params("arbitrary"), name="mix_in")(x3d, g.reshape(1, d), w_bf, conv_w,
                                                              g_conv.reshape(1, mc), *to_bf16)


def _ssm_kernel(u_ref, m_ref, s_ref, o_ref, ap_ref, y_ref, fr_ref, fi_ref, lre, lim, ire, iim, *, gb, nb, nchunk):
    n = SSM_STATE
    for gi in range(gb):
        loc = _dot(u_ref[gi], s_ref[gi].astype(BF16))
        lre[gi] = loc[:, 0:n]
        lim[gi] = loc[:, n:2 * n]
    ar = [jnp.broadcast_to(ap_ref[gi, 0:1, :], (nb, n)) for gi in range(gb)]
    ai = [jnp.broadcast_to(ap_ref[gi, 1:2, :], (nb, n)) for gi in range(gb)]
    st0 = tuple((jnp.zeros((nb, n), F32), jnp.zeros((nb, n), F32)) for _ in range(gb))

    def step(c, st):
        r = pl.multiple_of(c * nb, nb)
        new = []
        for gi in range(gb):
            sr, si = st[gi]
            ire[gi, pl.ds(r, nb), :] = sr
            iim[gi, pl.ds(r, nb), :] = si
            nr = ar[gi] * sr - ai[gi] * si + lre[gi, pl.ds(r, nb), :]
            ni = ar[gi] * si + ai[gi] * sr + lim[gi, pl.ds(r, nb), :]
            new.append((nr, ni))
        return tuple(new)

    st = lax.fori_loop(0, nchunk, step, st0, unroll=4)
    for gi in range(gb):
        fr_ref[gi] = st[gi][0]
        fi_ref[gi] = st[gi][1]
        ini = jnp.concatenate([ire[gi], iim[gi]], axis=1).astype(BF16)
        y = _dot(u_ref[gi], m_ref[gi]) + _dot(ini, o_ref[gi])
        y_ref[gi] = y.astype(y_ref.dtype)


def _ssm(u_g, mats, apow, *, nb, nchunk, gb=4):
    G, R, LP = u_g.shape
    n = SSM_STATE
    ins = [u_g, *mats, apow]
    gspec = lambda *s: pl.BlockSpec((gb,) + s, lambda i: (i, 0, 0))
    out_shape = [jax.ShapeDtypeStruct((G, R, LP), F32), jax.ShapeDtypeStruct((G, nb, n), F32),
                 jax.ShapeDtypeStruct((G, nb, n), F32)]
    return pl.pallas_call(
        functools.partial(_ssm_kernel, gb=gb, nb=nb, nchunk=nchunk),
        grid=(G // gb,), in_specs=[gspec(*a.shape[1:]) for a in ins],
        out_specs=[gspec(*o.shape[1:]) for o in out_shape],
        out_shape=out_shape, scratch_shapes=[pltpu.VMEM((gb, R, n), F32)] * 4,
        compiler_params=_params("arbitrary"), name="ssm_chunk")(*ins)


def _mix_out_core(x, ys, yc, wg_ref, bg_ref, gs_ref, wo_ref, gx_ref, wq_ref, *, ms, q_scale):
    y = jax.nn.gelu(ys)
    y = y * jax.nn.sigmoid(_dot(y.astype(BF16), wg_ref[...]) + bg_ref[...])
    ysn = _rms(y, gs_ref[...]).astype(BF16)
    x1 = x + _dot(jnp.concatenate([ysn, yc], axis=1), wo_ref[...])
    h = _rms(x1, gx_ref[...]).astype(BF16)
    return x1, _dot(h, wq_ref[...]) * q_scale


def _mix_out_kernel(x_ref, ys_ref, yc_ref, wg_ref, bg_ref, gs_ref, wo_ref, gx_ref, wq_ref, x1_ref, q_ref,
                    *, nseg, seg, nsub, ms, q_scale):
    sub = seg // nsub
    rg = sub // SSM_CHUNK * nseg
    d = x_ref.shape[-1]
    for h in range(nsub):
        steps = slice(h * sub, (h + 1) * sub)
        ys = _groups_to_tokens(ys_ref.at[:, pl.ds(h * rg, rg), :], nseg=nseg, seg=sub)
        x1, q = _mix_out_core(x_ref[:, steps, :].reshape(nseg * sub, d),
                              ys, yc_ref[:, steps, :].reshape(nseg * sub, yc_ref.shape[-1]),
                              wg_ref, bg_ref, gs_ref, wo_ref, gx_ref, wq_ref, ms=ms, q_scale=q_scale)
        x1_ref[:, steps, :] = x1.reshape(nseg, sub, d)
        q_ref[:, steps, :] = q.astype(q_ref.dtype).reshape(nseg, sub, d)


def _mix_out(x3d, ys, yc, w_glu, b_glu, g_ssm, w_out, g_x, w_q, *, seg):
    nseg, tlen, d = x3d.shape
    ms = w_glu.shape[0]
    hd = d // XATTN_HEADS
    row_spec = lambda c: pl.BlockSpec((nseg, seg, c), lambda t: (0, t, 0))
    ys_spec = pl.BlockSpec((ys.shape[0], seg // SSM_CHUNK * nseg, ys.shape[2]), lambda t: (0, t, 0))
    in_specs = [row_spec(d), ys_spec, row_spec(yc.shape[-1]), _const_spec(w_glu.shape), _const_spec((1, ms)),
                _const_spec((1, ms)), _const_spec(w_out.shape), _const_spec((1, d)), _const_spec(w_q.shape)]
    return pl.pallas_call(
        functools.partial(_mix_out_kernel, nseg=nseg, seg=seg, nsub=MIX_SUBBLOCKS, ms=ms, q_scale=hd ** -0.5),
        grid=(tlen // seg,), in_specs=in_specs, out_specs=[row_spec(d), row_spec(d)],
        out_shape=[jax.ShapeDtypeStruct((nseg, tlen, d), F32), jax.ShapeDtypeStruct((nseg, tlen, d), BF16)],
        compiler_params=_params("parallel"), name="mix_out")(
            x3d, ys, yc, w_glu, b_glu.reshape(1, ms), g_ssm.reshape(1, ms), w_out, g_x.reshape(1, d), w_q)


def _step_mix_kernel(x_ref, g_ref, w_ref, cw_ref, gc_ref, prev_ref, s0r_ref, s0i_ref, bband_ref, cband_ref, arow_ref,
                     d_ref, wg_ref, bg_ref, gs_ref, wo_ref, gx_ref, wq_ref,
                     x1_ref, q_ref, buf_ref, fr_ref, fi_ref, s_ref, ys_ref, *, nseq, tlen, ms, mc, q_scale):
    x = x_ref[...]

    def conv(v):
        out, buf_ref[...] = _conv3_steps_apart(v, cw_ref, s_ref, prev_ref[...], ts=nseq)
        return out

    zu, ycn = _mix_in_core(x, g_ref, w_ref, gc_ref, conv, ms=ms, mc=mc)

    wu, ws = bband_ref.shape[1], arow_ref.shape[2]
    for i in range(bband_ref.shape[0]):
        ucols, scols = slice(i * wu, (i + 1) * wu), slice(i * ws, (i + 1) * ws)
        sr, si = s0r_ref[:, scols], s0i_ref[:, scols]
        ar, ai = arow_ref[i, 0:1, :], arow_ref[i, 1:2, :]
        bband, cband = bband_ref[i].astype(BF16), cband_ref[i].astype(BF16)
        for t in range(tlen):
            ut = zu[t * nseq:(t + 1) * nseq, ucols]
            bu = _dot(ut.astype(BF16), bband)
            sr, si = ar * sr - ai * si + bu[:, 0:ws], ar * si + ai * sr + bu[:, ws:]
            cs = _dot(jnp.concatenate([sr, si], axis=1).astype(BF16), cband)
            ys_ref[t * nseq:(t + 1) * nseq, ucols] = cs + d_ref[:, ucols] * ut
        fr_ref[:, scols] = sr
        fi_ref[:, scols] = si

    x1, q = _mix_out_core(x, ys_ref[...], ycn, wg_ref, bg_ref, gs_ref, wo_ref, gx_ref, wq_ref, ms=ms, q_scale=q_scale)
    x1_ref[...] = x1
    q_ref[...] = q.astype(q_ref.dtype)


def _step_mix(x2d, p, conv_prev, s0_re, s0_im, bands, d_row, *, nseq, tlen):
    rows, d = x2d.shape
    ms = p["w_glu"].shape[0]
    mc = p["conv_w"].shape[1]
    hd = d // XATTN_HEADS
    bband, cband, arow = bands
    ins = [x2d, p["norm_mix"].reshape(1, d), p["w_in"], p["conv_w"], p["norm_conv_out"].reshape(1, mc), conv_prev,
           s0_re, s0_im, bband, cband, arow, d_row, p["w_glu"], p["b_glu"].reshape(1, ms),
           p["norm_ssm_out"].reshape(1, ms), p["w_out"], p["norm_xattn"].reshape(1, d), p["w_q"]]
    full = lambda shape: pl.BlockSpec(shape, lambda i: (0,) * len(shape))
    out_shape = [jax.ShapeDtypeStruct((rows, d), F32), jax.ShapeDtypeStruct((rows, d), F32),
                 jax.ShapeDtypeStruct(conv_prev.shape, F32), jax.ShapeDtypeStruct(s0_re.shape, F32),
                 jax.ShapeDtypeStruct(s0_im.shape, F32)]
    return pl.pallas_call(
        functools.partial(_step_mix_kernel, nseq=nseq, tlen=tlen, ms=ms, mc=mc, q_scale=hd ** -0.5),
        grid=(1,), in_specs=[_const_spec(a.shape) for a in ins], out_specs=[full(o.shape) for o in out_shape],
        out_shape=out_shape,
        scratch_shapes=[pltpu.VMEM((2 * nseq + rows, mc), F32), pltpu.VMEM((rows, ms), F32)],
        compiler_params=_params("arbitrary"), name="step_mix")(*ins)


def _kv_kernel(m_ref, g_ref, wk_ref, wv_ref, k_ref, v_ref):
    h = _rms(m_ref[...], g_ref[...]).astype(BF16)
    k_ref[...] = _dot(h, wk_ref[...])
    v_ref[...] = _dot(h, wv_ref[...])


def _kv_proj(mem2d, g, w_k, w_v, *, tm):
    rows, d = mem2d.shape
    row_spec = pl.BlockSpec((tm, d), lambda i: (i, 0))
    return pl.pallas_call(
        _kv_kernel, grid=(rows // tm,),
        in_specs=[row_spec, _const_spec((1, d)), _const_spec(w_k.shape), _const_spec(w_v.shape)],
        out_specs=[row_spec, row_spec],
        out_shape=[jax.ShapeDtypeStruct((rows, d), F32)] * 2,
        compiler_params=_params("parallel"), name="kv_proj")(mem2d, g.reshape(1, d), w_k, w_v)


def _softmax_attention(qs, ks, vs):
    tq = qs[0].shape[0]
    s = jnp.concatenate([lax.dot_general(q, k, (((1,), (1,)), ((), ())), preferred_element_type=F32)
                         for q, k in zip(qs, ks)], axis=0)
    p = jnp.exp(s - jnp.max(s, axis=-1, keepdims=True))
    p = p / jnp.sum(p, axis=-1, keepdims=True)
    return [_dot(p[i * tq:(i + 1) * tq].astype(BF16), v) for i, v in enumerate(vs)]


def _head_slices(ref, rows, hd):
    return [ref[rows, h * hd:(h + 1) * hd].astype(BF16) for h in range(XATTN_HEADS)]


def _attend_sequences(q_ref, k_ref, v_ref, o_ref, *, sb, tq):
    hd = q_ref.shape[1] // XATTN_HEADS

    def heads_of(ref, j):
        hm = jnp.transpose(ref[j], (1, 0, 2))
        return [hm[h].astype(BF16) for h in range(XATTN_HEADS)]

    qs, ks, vs = [], [], []
    for j in range(sb):
        qs += _head_slices(q_ref, slice(j * tq, (j + 1) * tq), hd)
        ks += heads_of(k_ref, j)
        vs += heads_of(v_ref, j)
    outs = _softmax_attention(qs, ks, vs)
    for j in range(sb):
        heads = outs[j * XATTN_HEADS:(j + 1) * XATTN_HEADS]
        o_ref[j * tq:(j + 1) * tq, :] = jnp.concatenate(heads, axis=1).astype(o_ref.dtype)


def _ffn_kernel(*refs, ts, has_prev, has_kv, side):
    x_ref, o_ref = refs[:2]
    refs = refs[2:]
    if has_kv:
        k_ref, v_ref = refs[:2]
        refs = refs[2:]
    if side:
        sq_ref, sk_ref, sv_ref = refs[:3]
        refs = refs[3:]
    wxo_ref, gf_ref, wup_ref, wgate_ref, cw_ref, wdn_ref, gl_ref = refs[:7]
    refs = refs[7:]
    if has_prev:
        prev_ref, refs = refs[0], refs[1:]
    if side:
        y_ref, buf_ref, so_ref, s_ref = refs
    else:
        y_ref, buf_ref, s_ref = refs
    if not has_prev:
        @pl.when(pl.program_id(1) == 0)
        def _():
            s_ref[...] = jnp.zeros(s_ref.shape, F32)

    if has_kv:
        hd = o_ref.shape[1] // XATTN_HEADS
        rows = slice(None)
        heads = _softmax_attention(_head_slices(o_ref, rows, hd), _head_slices(k_ref, rows, hd),
                                   _head_slices(v_ref, rows, hd))
        o = jnp.concatenate(heads, axis=1).astype(BF16)
    else:
        o = o_ref[...].astype(BF16)
    x2 = x_ref[...] + _dot(o, wxo_ref[...])
    hb = _rms(x2, gf_ref[...]).astype(BF16)
    up = _dot(hb, wup_ref[...])
    if has_prev:
        a, buf = _conv3_steps_apart(up, cw_ref, s_ref, prev_ref[...], ts=ts)
    else:
        a = _conv3_rows(up, cw_ref, s_ref)
        buf = s_ref[6:8, :]
    buf_ref[...] = buf.reshape(buf_ref.shape)
    act = (jax.nn.gelu(a) * _dot(hb, wgate_ref[...])).astype(BF16)
    x3 = x2 + _dot(act, wdn_ref[...])
    y_ref[...] = _rms(x3, gl_ref[...])
    if side:
        _attend_sequences(sq_ref, sk_ref, sv_ref, so_ref, sb=side[0], tq=side[1])


def _ffn(x2d, o2d, kv, w_xo, g_ffn, w_up, w_gate, conv_w, w_down, g_last, prev, *, nb, nt, tm, ts, side=None):
    rows, d = x2d.shape
    dff = w_up.shape[1]
    has_prev = prev is not None
    row_spec = pl.BlockSpec((tm, d), lambda b, t: (b * nt + t, 0))
    in_specs = [row_spec, row_spec]
    ins = [x2d, o2d]
    if kv is not None:
        in_specs += [pl.BlockSpec((N_MEM, d), lambda b, t: (b, 0))] * 2
        ins += list(kv)
    side_static = None
    if side is not None:
        sq, sk, sv, tq = side
        sb = sq.shape[0] // tq // (nb * nt)
        assert sb * nb * nt * tq == sq.shape[0]
        side_static = (sb, tq)
        side_q_spec = pl.BlockSpec((sb * tq, d), lambda b, t: (b * nt + t, 0))
        in_specs += [side_q_spec] + [pl.BlockSpec((sb,) + sk.shape[1:], lambda b, t: (b * nt + t, 0, 0, 0))] * 2
        ins += [sq, sk, sv]
    in_specs += [_const_spec(w_xo.shape), _const_spec((1, d)), _const_spec(w_up.shape), _const_spec(w_gate.shape),
                 _const_spec(conv_w.shape), _const_spec(w_down.shape), _const_spec((1, d))]
    ins += [w_xo, g_ffn.reshape(1, d), w_up, w_gate, conv_w, w_down, g_last.reshape(1, d)]
    if has_prev:
        in_specs.append(_const_spec(prev.shape))
        ins.append(prev)
        buf_shape = jax.ShapeDtypeStruct((2 * ts, dff), F32)
        buf_spec = pl.BlockSpec((2 * ts, dff), lambda b, t: (0, 0))
    else:
        buf_shape = jax.ShapeDtypeStruct((nb, 2 * ts, dff), F32)
        buf_spec = pl.BlockSpec((None, 2 * ts, dff), lambda b, t: (b, 0, 0))
    out_specs = [row_spec, buf_spec]
    out_shape = [jax.ShapeDtypeStruct((rows, d), F32), buf_shape]
    if side is not None:
        out_specs.append(side_q_spec)
        out_shape.append(jax.ShapeDtypeStruct(sq.shape, F32))
    return pl.pallas_call(
        functools.partial(_ffn_kernel, ts=ts, has_prev=has_prev, has_kv=kv is not None, side=side_static),
        grid=(nb, nt), in_specs=in_specs, out_specs=out_specs, out_shape=out_shape,
        scratch_shapes=[pltpu.VMEM((2 * ts + tm if has_prev else 8, dff), F32)],
        compiler_params=_params("parallel", "arbitrary", vmem=V7X_VMEM_LIMIT if side is None else V7X_VMEM_LIMIT_MAX),
        name="conv_ffn")(*ins)


def _prompt_mixer(x3d, p, mats, apow, later_weights):
    nseq, tlen, d = x3d.shape
    ms = p["w_glu"].shape[0]
    mc = p["conv_w"].shape[1]
    u, ycn, conv_buf, *later_bf16 = _mix_in(x3d, p["norm_mix"], p["w_in"], p["conv_w"], p["norm_conv_out"],
                                            later_weights, seg=MIX_STEPS, ms=ms, mc=mc)
    y_g, f_re, f_im = _ssm(u, mats, apow, nb=nseq, nchunk=tlen // SSM_CHUNK)
    x1, q = _mix_out(x3d, y_g, ycn, p["w_glu"], p["b_glu"], p["norm_ssm_out"], p["w_out"], p["norm_xattn"],
                     p["w_q"], seg=MIX_STEPS)
    return x1, q, (f_re.transpose(1, 0, 2), f_im.transpose(1, 0, 2), conv_buf), later_bf16


def _prompt_ffn(x1, q, k2d, v2d, p, g_last, side):
    nseq, tlen, d = x1.shape
    rows = nseq * tlen
    return _ffn(x1.reshape(rows, d), q.reshape(rows, d), (k2d, v2d), p["w_xo"], p["norm_ffn"], p["w_up"],
                p["w_gate"], p["ffn_conv_w"], p["w_down"], g_last, None,
                nb=nseq, nt=tlen // FFN_ROWS, tm=FFN_ROWS, ts=1, side=side)


def _step_rows(a):
    return a.transpose(1, 0, 2).reshape(a.shape[0] * a.shape[1], a.shape[2])


def _seq_rows(a, nseq):
    return a.reshape(a.shape[0] // nseq, nseq, a.shape[1]).transpose(1, 0, 2)


def _step_layer_mix(x_bm, prev, p, bands, d_row):
    nseq, tlen, d = x_bm.shape
    G, N = prev[0].shape[1:]
    x1, q, conv_buf, f_re, f_im = _step_mix(_step_rows(x_bm), p, _step_rows(prev[2]), prev[0].reshape(nseq, G * N),
                                            prev[1].reshape(nseq, G * N), bands, d_row, nseq=nseq, tlen=tlen)
    state = (f_re.reshape(nseq, G, N), f_im.reshape(nseq, G, N), _seq_rows(conv_buf, nseq))
    return x1, _seq_rows(q, nseq).reshape(nseq * tlen, d), state


def _step_layer_ffn(x1, o_bm, prev_ffn, p, g_last, *, nseq):
    rows, d = x1.shape
    o = _step_rows(o_bm.reshape(nseq, rows // nseq, d))
    y, ffn_buf = _ffn(x1, o, None, p["w_xo"], p["norm_ffn"], p["w_up"], p["w_gate"], p["ffn_conv_w"], p["w_down"],
                      g_last, _step_rows(prev_ffn), nb=1, nt=1, tm=rows, ts=nseq)
    return _seq_rows(y, nseq), _seq_rows(ffn_buf, nseq)


def kernel(x_prompt, x_sample, mem_prompt, cache_mem_k, cache_mem_v, state_ssm_re, state_ssm_im, state_conv, state_ffn_conv, norm_mix, w_in, ssm_A_re, ssm_A_im, ssm_log_dt, ssm_B_re, ssm_B_im, ssm_C_re, ssm_C_im, ssm_D, w_glu, b_glu, conv_w, norm_ssm_out, norm_conv_out, w_out, norm_xattn, norm_mem, w_q, w_k, w_v, w_xo, norm_ffn, w_up, w_gate, ffn_conv_w, w_down, norm_final):
    depth = w_in.shape[0]
    assert depth == 1, "the final norm is fused into the last (only) layer's ConvFFN kernel"
    nbp, tp, d = x_prompt.shape
    nbs = x_sample.shape[0]
    assert tp % ATTN_ROWS == 0 and tp % FFN_ROWS == 0 and tp % MIX_STEPS == 0 and nbp == 8
    hd = d // XATTN_HEADS

    xp, xs = x_prompt, x_sample
    outs_p, outs_s, mk_p, mv_p = [], [], [], []
    for l in range(depth):
        p = dict(norm_mix=norm_mix[l], w_in=w_in[l].astype(BF16), w_glu=w_glu[l].astype(BF16), b_glu=b_glu[l],
                 conv_w=conv_w[l], norm_ssm_out=norm_ssm_out[l], norm_conv_out=norm_conv_out[l],
                 w_out=w_out[l].astype(BF16), norm_xattn=norm_xattn[l], w_q=w_q[l].astype(BF16),
                 norm_ffn=norm_ffn[l], ffn_conv_w=ffn_conv_w[l])
        *mats, apow, bband, cband, arow = _ssm_prep(ssm_A_re[l], ssm_A_im[l], ssm_log_dt[l], ssm_B_re[l],
                                                    ssm_B_im[l], ssm_C_re[l], ssm_C_im[l], ssm_D[l], chunk=SSM_CHUNK)

        x1s, qs, st_s = _step_layer_mix(xs, (state_ssm_re[l], state_ssm_im[l], state_conv[l]), p,
                                        (bband, cband, arow), ssm_D[l].reshape(1, -1))
        x1p, qp, st_p, later = _prompt_mixer(xp, p, mats, apow,
                                             [w_xo[l], w_up[l], w_gate[l], w_down[l], w_k[l], w_v[l]])
        p["w_xo"], p["w_up"], p["w_gate"], p["w_down"], w_k_bf, w_v_bf = later

        k2d, v2d = _kv_proj(mem_prompt.reshape(nbp * N_MEM, d), norm_mem[l], w_k_bf, w_v_bf, tm=ATTN_ROWS)
        mk_p.append(k2d.reshape(nbp, N_MEM, XATTN_HEADS, hd))
        mv_p.append(v2d.reshape(nbp, N_MEM, XATTN_HEADS, hd))

        side = (qs, cache_mem_k[l], cache_mem_v[l], xs.shape[1])
        xp, ffn_buf_p, os_bm = _prompt_ffn(x1p, qp, k2d, v2d, p, norm_final, side)
        outs_p.append(st_p + (ffn_buf_p,))
        xs, ffn_buf_s = _step_layer_ffn(x1s, os_bm, state_ffn_conv[l], p, norm_final, nseq=nbs)
        outs_s.append(st_s + (ffn_buf_s,))

    stack = lambda outs, i: jnp.stack([o[i] for o in outs])
    return (xp.reshape(nbp, tp, d), xs, jnp.stack(mk_p), jnp.stack(mv_p),
            stack(outs_p, 0), stack(outs_p, 1), stack(outs_p, 2), stack(outs_p, 3),
            stack(outs_s, 0), stack(outs_s, 1), stack(outs_s, 2), stack(outs_s, 3))
```

```python
import functools

import jax
import jax.numpy as jnp
from jax import lax
from jax.experimental import pallas as pl
from jax.experimental.pallas import tpu as pltpu

F32 = jnp.float32
BF16 = jnp.bfloat16

EPS = 1e-6
SSM_GROUP_CH = 16
SSM_STATE = 64
CONV_K = 3
N_MEM = 256
XATTN_HEADS = 4
SSM_CHUNK = 16
LANES = 128
MIX_STEPS = 128
MIX_SUBBLOCKS = 2
FFN_ROWS = 512
V7X_VMEM_LIMIT = 56 * 1024 * 1024
V7X_VMEM_LIMIT_MAX = 62 * 1024 * 1024


def _rms(x, g):
    ms = jnp.mean(x * x, axis=-1, keepdims=True)
    return x * lax.rsqrt(ms + EPS) * g


def _dot(a, b):
    return jnp.dot(a, b, preferred_element_type=F32)


def _const_spec(shape):
    nd = len(shape)
    return pl.BlockSpec(shape, lambda *_: (0,) * nd, pipeline_mode=pl.Buffered(1))


def _params(*sem, vmem=V7X_VMEM_LIMIT):
    return pltpu.CompilerParams(dimension_semantics=sem, vmem_limit_bytes=vmem)


def _ssm_prep_kernel(ld_ref, lr_r_ref, li_r_ref, brt_ref, bit_ref, crt_t_ref, cit_t_ref, d_ref,
                     m_ref, s_ref, o_ref, apow_ref, bband_ref, cband_ref, arow_ref, *, gb, chunk):
    P, N = SSM_GROUP_CH, SSM_STATE
    LP = chunk * P
    bband_ref[...] = jnp.zeros(bband_ref.shape, bband_ref.dtype)
    cband_ref[...] = jnp.zeros(cband_ref.shape, cband_ref.dtype)
    lane_j = lax.broadcasted_iota(jnp.int32, (1, LP), 1) // P
    row_j = lax.broadcasted_iota(jnp.int32, (LP, 1), 0) // P
    lane_i = lax.broadcasted_iota(jnp.int32, (P, LP), 1)
    row_i = lax.broadcasted_iota(jnp.int32, (P, LP), 0)
    nbits = chunk.bit_length()

    eye = lax.broadcasted_iota(jnp.int32, (N, N), 0) == lax.broadcasted_iota(jnp.int32, (N, N), 1)

    def as_column(row):
        return jnp.sum(jnp.where(eye, row, 0.0), axis=1, keepdims=True)

    def squarings(pr, pi):
        out = [(pr, pi)]
        for _ in range(nbits - 1):
            pr, pi = pr * pr - pi * pi, 2.0 * pr * pi
            out.append((pr, pi))
        return out

    def cpow(pows, j):
        er = ei = None
        for b, (pr, pi) in enumerate(pows):
            if isinstance(j, int):
                if not (j >> b) & 1:
                    continue
                er, ei = (pr, pi) if er is None else (er * pr - ei * pi, er * pi + ei * pr)
            else:
                bit = ((j >> b) & 1) == 1
                if er is None:
                    er, ei = jnp.where(bit, pr, 1.0), jnp.where(bit, pi, 0.0)
                else:
                    er, ei = jnp.where(bit, er * pr - ei * pi, er), jnp.where(bit, er * pi + ei * pr, ei)
        return er, ei

    for gi in range(gb):
        dt = jnp.exp(ld_ref[gi])
        lr_r, li_r = lr_r_ref[gi], li_r_ref[gi]
        mag = jnp.exp(dt * lr_r)
        ar, ai = mag * jnp.cos(dt * li_r), mag * jnp.sin(dt * li_r)
        pows_r = squarings(ar, ai)
        pows_c = squarings(as_column(ar), as_column(ai))

        den = lr_r * lr_r + li_r * li_r
        cr = ((ar - 1.0) * lr_r + ai * li_r) / den
        ci = (ai * lr_r - (ar - 1.0) * li_r) / den

        brt, bit = brt_ref[gi], bit_ref[gi]
        bbt_re = cr * brt - ci * bit
        bbt_im = cr * bit + ci * brt
        bbt_re_t = jnp.concatenate([bbt_re] * chunk, axis=0)
        bbt_im_t = jnp.concatenate([bbt_im] * chunk, axis=0)

        er, ei = cpow(pows_r, (chunk - 1) - row_j)
        s_ref[gi, :, 0:N] = er * bbt_re_t - ei * bbt_im_t
        s_ref[gi, :, N:2 * N] = er * bbt_im_t + ei * bbt_re_t

        crt_t, cit_t = crt_t_ref[gi], cit_t_ref[gi]
        er, ei = cpow(pows_c, lane_j)
        r_re = crt_t * er - cit_t * ei
        r_im = crt_t * ei + cit_t * er
        er, ei = er * pows_c[0][0] - ei * pows_c[0][1], er * pows_c[0][1] + ei * pows_c[0][0]
        o_ref[gi, 0:N, :] = (crt_t * er - cit_t * ei).astype(o_ref.dtype)
        o_ref[gi, N:2 * N, :] = (-(crt_t * ei + cit_t * er)).astype(o_ref.dtype)

        krow = (jnp.dot(bbt_re, r_re, preferred_element_type=F32, precision=lax.Precision.HIGHEST)
                - jnp.dot(bbt_im, r_im, preferred_element_type=F32, precision=lax.Precision.HIGHEST))
        krow = krow + jnp.where(lane_i == row_i, d_ref[gi], 0.0)
        for ti in range(chunk):
            blk = krow if ti == 0 else pltpu.roll(krow, ti * P, axis=1)
            blk = jnp.where(lane_i >= ti * P, blk, 0.0)
            m_ref[gi, ti * P:(ti + 1) * P, :] = blk.astype(m_ref.dtype)

        apow_ref[gi] = jnp.concatenate(cpow(pows_r, chunk), axis=0)

        rows, cols = slice(gi * P, (gi + 1) * P), slice(gi * N, (gi + 1) * N)
        cols_im = slice((gb + gi) * N, (gb + gi + 1) * N)
        bband_ref[rows, cols] = bbt_re
        bband_ref[rows, cols_im] = bbt_im
        cband_ref[cols, rows] = crt_t[:, rows]
        cband_ref[cols_im, rows] = -cit_t[:, rows]
        arow_ref[0:1, cols] = ar
        arow_ref[1:2, cols] = ai


def _ssm_prep(A_re, A_im, log_dt, B_re, B_im, C_re, C_im, D, *, chunk, gb=8):
    G, N = A_re.shape
    P = SSM_GROUP_CH
    LP = chunk * P
    brt = B_re.transpose(0, 2, 1)
    bit = B_im.transpose(0, 2, 1)
    crt_t = jnp.tile(C_re.transpose(0, 2, 1), (1, 1, chunk))
    cit_t = jnp.tile(C_im.transpose(0, 2, 1), (1, 1, chunk))
    ins = [log_dt.reshape(G, 1, 1), A_re.reshape(G, 1, N), A_im.reshape(G, 1, N), brt, bit, crt_t, cit_t,
           D.reshape(G, P, 1)]
    gspec = lambda *s: pl.BlockSpec((gb,) + s, lambda i: (i, 0, 0))
    in_specs = [gspec(*a.shape[1:]) for a in ins]
    out_shape = [jax.ShapeDtypeStruct((G, LP, LP), BF16), jax.ShapeDtypeStruct((G, LP, 2 * N), F32),
                 jax.ShapeDtypeStruct((G, 2 * N, LP), BF16), jax.ShapeDtypeStruct((G, 2, N), F32)]
    out_specs = [gspec(*o.shape[1:]) for o in out_shape]
    band_shape = [jax.ShapeDtypeStruct((G // gb, gb * P, 2 * gb * N), F32),
                  jax.ShapeDtypeStruct((G // gb, 2 * gb * N, gb * P), F32),
                  jax.ShapeDtypeStruct((G // gb, 2, gb * N), F32)]
    out_shape += band_shape
    out_specs += [pl.BlockSpec((None,) + o.shape[1:], lambda i: (i, 0, 0)) for o in band_shape]
    return pl.pallas_call(
        functools.partial(_ssm_prep_kernel, gb=gb, chunk=chunk),
        grid=(G // gb,), in_specs=in_specs, out_specs=out_specs, out_shape=out_shape,
        compiler_params=_params("arbitrary"), name="ssm_prep")(*ins)


def _conv3_steps_apart(v, w_ref, s_ref, prev, *, ts):
    tm = v.shape[0]
    s_ref[0:2 * ts, :] = prev
    s_ref[2 * ts:2 * ts + tm, :] = v
    out = w_ref[0:1, :] * s_ref[0:tm, :] + w_ref[1:2, :] * s_ref[ts:ts + tm, :] + w_ref[2:3, :] * v
    return out, s_ref[tm:tm + 2 * ts, :]


def _conv3_rows(v, w_ref, c_ref):
    tm = v.shape[0]
    head = c_ref[...]
    row = lax.broadcasted_iota(jnp.int32, head.shape, 0)

    def back(k):
        r = pltpu.roll(v, k, axis=0)
        first = jnp.where(row < k, pltpu.roll(head, k, axis=0), r[0:8])
        return jnp.concatenate([first, r[8:]], axis=0)

    out = w_ref[0:1, :] * back(2) + w_ref[1:2, :] * back(1) + w_ref[2:3, :] * v
    c_ref[...] = v[tm - 8:tm]
    return out


def _piece_transpose8(v):
    width = LANES // 8
    piece = lax.broadcasted_iota(jnp.int32, v[0].shape, 1) // width
    for bit in range(3):
        s = 1 << bit
        hi = ((piece >> bit) & 1) == 1
        nv = list(v)
        for i in range(8):
            if i & s:
                continue
            a, b = v[i], v[i + s]
            nv[i] = jnp.where(hi, pltpu.roll(b, s * width, axis=1), a)
            nv[i + s] = jnp.where(hi, b, pltpu.roll(a, LANES - s * width, axis=1))
        v = nv
    return v


def _tokens_to_groups(zu, u_ref, *, nseg, seg):
    nc = seg // SSM_CHUNK
    ms = zu.shape[1]
    z_tb = jnp.transpose(zu.reshape(nseg, seg, ms), (1, 0, 2))
    for h in range(SSM_CHUNK // 8):
        for q in range(ms // LANES):
            xs = [jnp.concatenate([z_tb[SSM_CHUNK * c + 8 * h + i, :, q * LANES:(q + 1) * LANES] for c in range(nc)],
                                  axis=0) for i in range(8)]
            w = _piece_transpose8(xs)
            for k in range(8):
                u_ref[8 * q + k, :, h * LANES:(h + 1) * LANES] = w[k].astype(u_ref.dtype)


def _groups_to_tokens(y_ref, *, nseg, seg):
    nc = seg // SSM_CHUNK
    nq = y_ref.shape[0] // 8
    tiles = [[None] * nq for _ in range(seg)]
    for h in range(SSM_CHUNK // 8):
        for q in range(nq):
            v = _piece_transpose8([y_ref[8 * q + k, :, h * LANES:(h + 1) * LANES].astype(F32) for k in range(8)])
            for i in range(8):
                for c in range(nc):
                    tiles[SSM_CHUNK * c + 8 * h + i][q] = v[i][c * nseg:(c + 1) * nseg]
    y_tb = jnp.stack([jnp.concatenate(row, axis=1) for row in tiles], axis=0)
    return jnp.transpose(y_tb, (1, 0, 2)).reshape(nseg * seg, nq * LANES)


def _mix_in_core(x, g_ref, w_ref, gc_ref, conv, *, ms, mc):
    hb = _rms(x, g_ref[...]).astype(BF16)
    zu = _dot(hb, w_ref[:, 0:ms])
    xin = _dot(hb, w_ref[:, ms:ms + mc])
    cg = _dot(hb, w_ref[:, ms + 2 * mc:ms + 3 * mc])
    cv = conv(cg * xin)
    bg = _dot(hb, w_ref[:, ms + mc:ms + 2 * mc])
    return zu, _rms(bg * cv, gc_ref[...]).astype(BF16)


def _mix_in_kernel(*refs, nseg, seg, nsub, ncast, ms, mc):
    x_ref, g_ref, w_ref, cw_ref, gc_ref = refs[:5]
    cast_in, (u_ref, yc_ref, buf_ref) = refs[5:5 + ncast], refs[5 + ncast:8 + ncast]
    cast_out, s_ref = refs[8 + ncast:8 + 2 * ncast], refs[8 + 2 * ncast]
    for src, dst in zip(cast_in, cast_out):
        dst[...] = src[...].astype(dst.dtype)

    sub = seg // nsub
    rg = sub // SSM_CHUNK * nseg

    @pl.when(pl.program_id(0) == 0)
    def _():
        s_ref[...] = jnp.zeros(s_ref.shape, F32)

    def conv(v):
        return jnp.concatenate([_conv3_rows(v[b * sub:(b + 1) * sub], cw_ref, s_ref.at[b]) for b in range(nseg)],
                               axis=0)

    for h in range(nsub):
        steps = slice(h * sub, (h + 1) * sub)
        zu, ycn = _mix_in_core(x_ref[:, steps, :].reshape(nseg * sub, x_ref.shape[-1]), g_ref, w_ref, gc_ref, conv,
                               ms=ms, mc=mc)
        _tokens_to_groups(zu, u_ref.at[:, pl.ds(h * rg, rg), :], nseg=nseg, seg=sub)
        yc_ref[:, steps, :] = ycn.reshape(nseg, sub, mc)
    buf_ref[...] = s_ref[:, 6:8, :]


def _mix_in(x3d, g, w_bf, conv_w, g_conv, to_bf16, *, seg, ms, mc):
    nseg, tlen, d = x3d.shape
    G = ms // SSM_GROUP_CH
    lp = SSM_CHUNK * SSM_GROUP_CH
    nsteps = tlen // seg
    row_spec = lambda c: pl.BlockSpec((nseg, seg, c), lambda t: (0, t, 0))
    slab_specs = [pl.BlockSpec((w.shape[0] // nsteps, w.shape[1]), lambda t: (t, 0)) for w in to_bf16]
    in_specs = [row_spec(d), _const_spec((1, d)), _const_spec(w_bf.shape), _const_spec(conv_w.shape),
                _const_spec((1, mc))] + slab_specs
    return pl.pallas_call(
        functools.partial(_mix_in_kernel, nseg=nseg, seg=seg, nsub=MIX_SUBBLOCKS, ncast=len(to_bf16), ms=ms, mc=mc),
        grid=(nsteps,), in_specs=in_specs,
        out_specs=[pl.BlockSpec((G, seg // SSM_CHUNK * nseg, lp), lambda t: (0, t, 0)), row_spec(mc),
                   pl.BlockSpec((nseg, 2, mc), lambda t: (0, 0, 0))] + slab_specs,
        out_shape=[jax.ShapeDtypeStruct((G, tlen // SSM_CHUNK * nseg, lp), BF16),
                   jax.ShapeDtypeStruct((nseg, tlen, mc), BF16), jax.ShapeDtypeStruct((nseg, 2, mc), F32)]
        + [jax.ShapeDtypeStruct(w.shape, BF16) for w in to_bf16],
        scratch_shapes=[pltpu.VMEM((nseg, 8, mc), F32)],
        compiler_params=_params("arbitrary"), name="mix_in")(x3d, g.reshape(1, d), w_bf, conv_w,
                                                              g_conv.reshape(1, mc), *to_bf16)


def _ssm_kernel(u_ref, m_ref, s_ref, o_ref, ap_ref, y_ref, fr_ref, fi_ref, lre, lim, ire, iim, *, gb, nb, nchunk):
    n = SSM_STATE
    for gi in range(gb):
        loc = _dot(u_ref[gi], s_ref[gi].astype(BF16))
        lre[gi] = loc[:, 0:n]
        lim[gi] = loc[:, n:2 * n]
    ar = [jnp.broadcast_to(ap_ref[gi, 0:1, :], (nb, n)) for gi in range(gb)]
    ai = [jnp.broadcast_to(ap_ref[gi, 1:2, :], (nb, n)) for gi in range(gb)]
    st0 = tuple((jnp.zeros((nb, n), F32), jnp.zeros((nb, n), F32)) for _ in range(gb))

    def step(c, st):
        r = pl.multiple_of(c * nb, nb)
        new = []
        for gi in range(gb):
            sr, si = st[gi]
            ire[gi, pl.ds(r, nb), :] = sr
            iim[gi, pl.ds(r, nb), :] = si
            nr = ar[gi] * sr - ai[gi] * si + lre[gi, pl.ds(r, nb), :]
            ni = ar[gi] * si + ai[gi] * sr + lim[gi, pl.ds(r, nb), :]
            new.append((nr, ni))
        return tuple(new)

    st = lax.fori_loop(0, nchunk, step, st0, unroll=4)
    for gi in range(gb):
        fr_ref[gi] = st[gi][0]
        fi_ref[gi] = st[gi][1]
        ini = jnp.concatenate([ire[gi], iim[gi]], axis=1).astype(BF16)
        y = _dot(u_ref[gi], m_ref[gi]) + _dot(ini, o_ref[gi])
        y_ref[gi] = y.astype(y_ref.dtype)


def _ssm(u_g, mats, apow, *, nb, nchunk, gb=4):
    G, R, LP = u_g.shape
    n = SSM_STATE
    ins = [u_g, *mats, apow]
    gspec = lambda *s: pl.BlockSpec((gb,) + s, lambda i: (i, 0, 0))
    out_shape = [jax.ShapeDtypeStruct((G, R, LP), F32), jax.ShapeDtypeStruct((G, nb, n), F32),
                 jax.ShapeDtypeStruct((G, nb, n), F32)]
    return pl.pallas_call(
        functools.partial(_ssm_kernel, gb=gb, nb=nb, nchunk=nchunk),
        grid=(G // gb,), in_specs=[gspec(*a.shape[1:]) for a in ins],
        out_specs=[gspec(*o.shape[1:]) for o in out_shape],
        out_shape=out_shape, scratch_shapes=[pltpu.VMEM((gb, R, n), F32)] * 4,
        compiler_params=_params("arbitrary"), name="ssm_chunk")(*ins)


def _mix_out_core(x, ys, yc, wg_ref, bg_ref, gs_ref, wo_ref, gx_ref, wq_ref, *, ms, q_scale):
    y = jax.nn.gelu(ys)
    y = y * jax.nn.sigmoid(_dot(y.astype(BF16), wg_ref[...]) + bg_ref[...])
    ysn = _rms(y, gs_ref[...]).astype(BF16)
    x1 = x + _dot(jnp.concatenate([ysn, yc], axis=1), wo_ref[...])
    h = _rms(x1, gx_ref[...]).astype(BF16)
    return x1, _dot(h, wq_ref[...]) * q_scale


def _mix_out_kernel(x_ref, ys_ref, yc_ref, wg_ref, bg_ref, gs_ref, wo_ref, gx_ref, wq_ref, x1_ref, q_ref,
                    *, nseg, seg, nsub, ms, q_scale):
    sub = seg // nsub
    rg = sub // SSM_CHUNK * nseg
    d = x_ref.shape[-1]
    for h in range(nsub):
        steps = slice(h * sub, (h + 1) * sub)
        ys = _groups_to_tokens(ys_ref.at[:, pl.ds(h * rg, rg), :], nseg=nseg, seg=sub)
        x1, q = _mix_out_core(x_ref[:, steps, :].reshape(nseg * sub, d),
                              ys, yc_ref[:, steps, :].reshape(nseg * sub, yc_ref.shape[-1]),
                              wg_ref, bg_ref, gs_ref, wo_ref, gx_ref, wq_ref, ms=ms, q_scale=q_scale)
        x1_ref[:, steps, :] = x1.reshape(nseg, sub, d)
        q_ref[:, steps, :] = q.astype(q_ref.dtype).reshape(nseg, sub, d)


def _mix_out(x3d, ys, yc, w_glu, b_glu, g_ssm, w_out, g_x, w_q, *, seg):
    nseg, tlen, d = x3d.shape
    ms = w_glu.shape[0]
    hd = d // XATTN_HEADS
    row_spec = lambda c: pl.BlockSpec((nseg, seg, c), lambda t: (0, t, 0))
    ys_spec = pl.BlockSpec((ys.shape[0], seg // SSM_CHUNK * nseg, ys.shape[2]), lambda t: (0, t, 0))
    in_specs = [row_spec(d), ys_spec, row_spec(yc.shape[-1]), _const_spec(w_glu.shape), _const_spec((1, ms)),
                _const_spec((1, ms)), _const_spec(w_out.shape), _const_spec((1, d)), _const_spec(w_q.shape)]
    return pl.pallas_call(
        functools.partial(_mix_out_kernel, nseg=nseg, seg=seg, nsub=MIX_SUBBLOCKS, ms=ms, q_scale=hd ** -0.5),
        grid=(tlen // seg,), in_specs=in_specs, out_specs=[row_spec(d), row_spec(d)],
        out_shape=[jax.ShapeDtypeStruct((nseg, tlen, d), F32), jax.ShapeDtypeStruct((nseg, tlen, d), BF16)],
        compiler_params=_params("parallel"), name="mix_out")(
            x3d, ys, yc, w_glu, b_glu.reshape(1, ms), g_ssm.reshape(1, ms), w_out, g_x.reshape(1, d), w_q)


def _step_mix_kernel(x_ref, g_ref, w_ref, cw_ref, gc_ref, prev_ref, s0r_ref, s0i_ref, bband_ref, cband_ref, arow_ref,
                     d_ref, wg_ref, bg_ref, gs_ref, wo_ref, gx_ref, wq_ref,
                     x1_ref, q_ref, buf_ref, fr_ref, fi_ref, s_ref, ys_ref, *, nseq, tlen, ms, mc, q_scale):
    x = x_ref[...]

    def conv(v):
        out, buf_ref[...] = _conv3_steps_apart(v, cw_ref, s_ref, prev_ref[...], ts=nseq)
        return out

    zu, ycn = _mix_in_core(x, g_ref, w_ref, gc_ref, conv, ms=ms, mc=mc)

    wu, ws = bband_ref.shape[1], arow_ref.shape[2]
    for i in range(bband_ref.shape[0]):
        ucols, scols = slice(i * wu, (i + 1) * wu), slice(i * ws, (i + 1) * ws)
        sr, si = s0r_ref[:, scols], s0i_ref[:, scols]
        ar, ai = arow_ref[i, 0:1, :], arow_ref[i, 1:2, :]
        bband, cband = bband_ref[i].astype(BF16), cband_ref[i].astype(BF16)
        for t in range(tlen):
            ut = zu[t * nseq:(t + 1) * nseq, ucols]
            bu = _dot(ut.astype(BF16), bband)
            sr, si = ar * sr - ai * si + bu[:, 0:ws], ar * si + ai * sr + bu[:, ws:]
            cs = _dot(jnp.concatenate([sr, si], axis=1).astype(BF16), cband)
            ys_ref[t * nseq:(t + 1) * nseq, ucols] = cs + d_ref[:, ucols] * ut
        fr_ref[:, scols] = sr
        fi_ref[:, scols] = si

    x1, q = _mix_out_core(x, ys_ref[...], ycn, wg_ref, bg_ref, gs_ref, wo_ref, gx_ref, wq_ref, ms=ms, q_scale=q_scale)
    x1_ref[...] = x1
    q_ref[...] = q.astype(q_ref.dtype)


def _step_mix(x2d, p, conv_prev, s0_re, s0_im, bands, d_row, *, nseq, tlen):
    rows, d = x2d.shape
    ms = p["w_glu"].shape[0]
    mc = p["conv_w"].shape[1]
    hd = d // XATTN_HEADS
    bband, cband, arow = bands
    ins = [x2d, p["norm_mix"].reshape(1, d), p["w_in"], p["conv_w"], p["norm_conv_out"].reshape(1, mc), conv_prev,
           s0_re, s0_im, bband, cband, arow, d_row, p["w_glu"], p["b_glu"].reshape(1, ms),
           p["norm_ssm_out"].reshape(1, ms), p["w_out"], p["norm_xattn"].reshape(1, d), p["w_q"]]
    full = lambda shape: pl.BlockSpec(shape, lambda i: (0,) * len(shape))
    out_shape = [jax.ShapeDtypeStruct((rows, d), F32), jax.ShapeDtypeStruct((rows, d), F32),
                 jax.ShapeDtypeStruct(conv_prev.shape, F32), jax.ShapeDtypeStruct(s0_re.shape, F32),
                 jax.ShapeDtypeStruct(s0_im.shape, F32)]
    return pl.pallas_call(
        functools.partial(_step_mix_kernel, nseq=nseq, tlen=tlen, ms=ms, mc=mc, q_scale=hd ** -0.5),
        grid=(1,), in_specs=[_const_spec(a.shape) for a in ins], out_specs=[full(o.shape) for o in out_shape],
        out_shape=out_shape,
        scratch_shapes=[pltpu.VMEM((2 * nseq + rows, mc), F32), pltpu.VMEM((rows, ms), F32)],
        compiler_params=_params("arbitrary"), name="step_mix")(*ins)


def _kv_kernel(m_ref, g_ref, wk_ref, wv_ref, k_ref, v_ref):
    h = _rms(m_ref[0], g_ref[...]).astype(BF16)
    hd = k_ref.shape[-1]
    for w_ref, out_ref in ((wk_ref, k_ref), (wv_ref, v_ref)):
        kv = _dot(h, w_ref[...])
        heads = jnp.stack([kv[:, i * hd:(i + 1) * hd] for i in range(XATTN_HEADS)], axis=0)
        out_ref[0] = jnp.transpose(heads, (1, 0, 2))


def _kv_proj(mem, g, w_k, w_v):
    nseq, nm, d = mem.shape
    hd = d // XATTN_HEADS
    out_spec = pl.BlockSpec((1, nm, XATTN_HEADS, hd), lambda i: (i, 0, 0, 0))
    return pl.pallas_call(
        _kv_kernel, grid=(nseq,),
        in_specs=[pl.BlockSpec((1, nm, d), lambda i: (i, 0, 0)), _const_spec((1, d)), _const_spec(w_k.shape),
                  _const_spec(w_v.shape)],
        out_specs=[out_spec, out_spec],
        out_shape=[jax.ShapeDtypeStruct((nseq, nm, XATTN_HEADS, hd), F32)] * 2,
        compiler_params=_params("parallel"), name="kv_proj")(mem, g.reshape(1, d), w_k, w_v)


def _softmax_attention(qs, ks, vs):
    tq = qs[0].shape[0]
    s = jnp.concatenate([lax.dot_general(q, k, (((1,), (1,)), ((), ())), preferred_element_type=F32)
                         for q, k in zip(qs, ks)], axis=0)
    p = jnp.exp(s - jnp.max(s, axis=-1, keepdims=True))
    p = p / jnp.sum(p, axis=-1, keepdims=True)
    return [_dot(p[i * tq:(i + 1) * tq].astype(BF16), v) for i, v in enumerate(vs)]


def _head_slices(ref, rows, hd):
    return [ref[rows, h * hd:(h + 1) * hd].astype(BF16) for h in range(XATTN_HEADS)]


def _kv_heads(ref, j):
    hm = jnp.transpose(ref[j], (1, 0, 2))
    return [hm[h].astype(BF16) for h in range(XATTN_HEADS)]


def _attend_sequences(q_ref, k_ref, v_ref, o_ref, *, sb, tq):
    hd = q_ref.shape[1] // XATTN_HEADS
    qs, ks, vs = [], [], []
    for j in range(sb):
        qs += _head_slices(q_ref, slice(j * tq, (j + 1) * tq), hd)
        ks += _kv_heads(k_ref, j)
        vs += _kv_heads(v_ref, j)
    outs = _softmax_attention(qs, ks, vs)
    for j in range(sb):
        heads = outs[j * XATTN_HEADS:(j + 1) * XATTN_HEADS]
        o_ref[j * tq:(j + 1) * tq, :] = jnp.concatenate(heads, axis=1).astype(o_ref.dtype)


def _ffn_kernel(*refs, ts, has_prev, has_kv, side):
    x_ref, o_ref = refs[:2]
    refs = refs[2:]
    if has_kv:
        k_ref, v_ref = refs[:2]
        refs = refs[2:]
    if side:
        sq_ref, sk_ref, sv_ref = refs[:3]
        refs = refs[3:]
    wxo_ref, gf_ref, wup_ref, wgate_ref, cw_ref, wdn_ref, gl_ref = refs[:7]
    refs = refs[7:]
    if has_prev:
        prev_ref, refs = refs[0], refs[1:]
    if side:
        y_ref, buf_ref, so_ref, s_ref = refs
    else:
        y_ref, buf_ref, s_ref = refs
    if not has_prev:
        @pl.when(pl.program_id(1) == 0)
        def _():
            s_ref[...] = jnp.zeros(s_ref.shape, F32)

    if has_kv:
        hd = o_ref.shape[1] // XATTN_HEADS
        rows = slice(None)
        heads = _softmax_attention(_head_slices(o_ref, rows, hd), _kv_heads(k_ref, 0), _kv_heads(v_ref, 0))
        o = jnp.concatenate(heads, axis=1).astype(BF16)
    else:
        o = o_ref[...].astype(BF16)
    x2 = x_ref[...] + _dot(o, wxo_ref[...])
    hb = _rms(x2, gf_ref[...]).astype(BF16)
    up = _dot(hb, wup_ref[...])
    if has_prev:
        a, buf = _conv3_steps_apart(up, cw_ref, s_ref, prev_ref[...], ts=ts)
    else:
        a = _conv3_rows(up, cw_ref, s_ref)
        buf = s_ref[6:8, :]
    buf_ref[...] = buf.reshape(buf_ref.shape)
    act = (jax.nn.gelu(a) * _dot(hb, wgate_ref[...])).astype(BF16)
    x3 = x2 + _dot(act, wdn_ref[...])
    y_ref[...] = _rms(x3, gl_ref[...])
    if side:
        _attend_sequences(sq_ref, sk_ref, sv_ref, so_ref, sb=side[0], tq=side[1])


def _ffn(x2d, o2d, kv, w_xo, g_ffn, w_up, w_gate, conv_w, w_down, g_last, prev, *, nb, nt, tm, ts, side=None):
    rows, d = x2d.shape
    dff = w_up.shape[1]
    has_prev = prev is not None
    row_spec = pl.BlockSpec((tm, d), lambda b, t: (b * nt + t, 0))
    in_specs = [row_spec, row_spec]
    ins = [x2d, o2d]
    if kv is not None:
        in_specs += [pl.BlockSpec((1,) + kv[0].shape[1:], lambda b, t: (b, 0, 0, 0))] * 2
        ins += list(kv)
    side_static = None
    if side is not None:
        sq, sk, sv, tq = side
        sb = sq.shape[0] // tq // (nb * nt)
        assert sb * nb * nt * tq == sq.shape[0]
        side_static = (sb, tq)
        side_q_spec = pl.BlockSpec((sb * tq, d), lambda b, t: (b * nt + t, 0))
        in_specs += [side_q_spec] + [pl.BlockSpec((sb,) + sk.shape[1:], lambda b, t: (b * nt + t, 0, 0, 0))] * 2
        ins += [sq, sk, sv]
    in_specs += [_const_spec(w_xo.shape), _const_spec((1, d)), _const_spec(w_up.shape), _const_spec(w_gate.shape),
                 _const_spec(conv_w.shape), _const_spec(w_down.shape), _const_spec((1, d))]
    ins += [w_xo, g_ffn.reshape(1, d), w_up, w_gate, conv_w, w_down, g_last.reshape(1, d)]
    if has_prev:
        in_specs.append(_const_spec(prev.shape))
        ins.append(prev)
        buf_shape = jax.ShapeDtypeStruct((2 * ts, dff), F32)
        buf_spec = pl.BlockSpec((2 * ts, dff), lambda b, t: (0, 0))
    else:
        buf_shape = jax.ShapeDtypeStruct((nb, 2 * ts, dff), F32)
        buf_spec = pl.BlockSpec((None, 2 * ts, dff), lambda b, t: (b, 0, 0))
    out_specs = [row_spec, buf_spec]
    out_shape = [jax.ShapeDtypeStruct((rows, d), F32), buf_shape]
    if side is not None:
        out_specs.append(side_q_spec)
        out_shape.append(jax.ShapeDtypeStruct(sq.shape, F32))
    return pl.pallas_call(
        functools.partial(_ffn_kernel, ts=ts, has_prev=has_prev, has_kv=kv is not None, side=side_static),
        grid=(nb, nt), in_specs=in_specs, out_specs=out_specs, out_shape=out_shape,
        scratch_shapes=[pltpu.VMEM((2 * ts + tm if has_prev else 8, dff), F32)],
        compiler_params=_params("parallel", "arbitrary", vmem=V7X_VMEM_LIMIT if side is None else V7X_VMEM_LIMIT_MAX),
        name="conv_ffn")(*ins)


def _prompt_mixer(x3d, p, mats, apow, later_weights):
    nseq, tlen, d = x3d.shape
    ms = p["w_glu"].shape[0]
    mc = p["conv_w"].shape[1]
    u, ycn, conv_buf, *later_bf16 = _mix_in(x3d, p["norm_mix"], p["w_in"], p["conv_w"], p["norm_conv_out"],
                                            later_weights, seg=MIX_STEPS, ms=ms, mc=mc)
    y_g, f_re, f_im = _ssm(u, mats, apow, nb=nseq, nchunk=tlen // SSM_CHUNK)
    x1, q = _mix_out(x3d, y_g, ycn, p["w_glu"], p["b_glu"], p["norm_ssm_out"], p["w_out"], p["norm_xattn"],
                     p["w_q"], seg=MIX_STEPS)
    return x1, q, (f_re.transpose(1, 0, 2), f_im.transpose(1, 0, 2), conv_buf), later_bf16


def _prompt_ffn(x1, q, k, v, p, g_last, side):
    nseq, tlen, d = x1.shape
    rows = nseq * tlen
    return _ffn(x1.reshape(rows, d), q.reshape(rows, d), (k, v), p["w_xo"], p["norm_ffn"], p["w_up"],
                p["w_gate"], p["ffn_conv_w"], p["w_down"], g_last, None,
                nb=nseq, nt=tlen // FFN_ROWS, tm=FFN_ROWS, ts=1, side=side)


def _step_rows(a):
    return a.transpose(1, 0, 2).reshape(a.shape[0] * a.shape[1], a.shape[2])


def _seq_rows(a, nseq):
    return a.reshape(a.shape[0] // nseq, nseq, a.shape[1]).transpose(1, 0, 2)


def _step_layer_mix(x_bm, prev, p, bands, d_row):
    nseq, tlen, d = x_bm.shape
    G, N = prev[0].shape[1:]
    x1, q, conv_buf, f_re, f_im = _step_mix(_step_rows(x_bm), p, _step_rows(prev[2]), prev[0].reshape(nseq, G * N),
                                            prev[1].reshape(nseq, G * N), bands, d_row, nseq=nseq, tlen=tlen)
    state = (f_re.reshape(nseq, G, N), f_im.reshape(nseq, G, N), _seq_rows(conv_buf, nseq))
    return x1, _seq_rows(q, nseq).reshape(nseq * tlen, d), state


def _step_layer_ffn(x1, o_bm, prev_ffn, p, g_last, *, nseq):
    rows, d = x1.shape
    o = _step_rows(o_bm.reshape(nseq, rows // nseq, d))
    y, ffn_buf = _ffn(x1, o, None, p["w_xo"], p["norm_ffn"], p["w_up"], p["w_gate"], p["ffn_conv_w"], p["w_down"],
                      g_last, _step_rows(prev_ffn), nb=1, nt=1, tm=rows, ts=nseq)
    return _seq_rows(y, nseq), _seq_rows(ffn_buf, nseq)


def kernel(x_prompt, x_sample, mem_prompt, cache_mem_k, cache_mem_v, state_ssm_re, state_ssm_im, state_conv, state_ffn_conv, norm_mix, w_in, ssm_A_re, ssm_A_im, ssm_log_dt, ssm_B_re, ssm_B_im, ssm_C_re, ssm_C_im, ssm_D, w_glu, b_glu, conv_w, norm_ssm_out, norm_conv_out, w_out, norm_xattn, norm_mem, w_q, w_k, w_v, w_xo, norm_ffn, w_up, w_gate, ffn_conv_w, w_down, norm_final):
    depth = w_in.shape[0]
    assert depth == 1, "the final norm is fused into the last (only) layer's ConvFFN kernel"
    nbp, tp, d = x_prompt.shape
    nbs = x_sample.shape[0]
    assert tp % FFN_ROWS == 0 and tp % MIX_STEPS == 0 and nbp == 8

    xp, xs = x_prompt, x_sample
    outs_p, outs_s, mk_p, mv_p = [], [], [], []
    for l in range(depth):
        p = dict(norm_mix=norm_mix[l], w_in=w_in[l].astype(BF16), w_glu=w_glu[l].astype(BF16), b_glu=b_glu[l],
                 conv_w=conv_w[l], norm_ssm_out=norm_ssm_out[l], norm_conv_out=norm_conv_out[l],
                 w_out=w_out[l].astype(BF16), norm_xattn=norm_xattn[l], w_q=w_q[l].astype(BF16),
                 norm_ffn=norm_ffn[l], ffn_conv_w=ffn_conv_w[l])
        *mats, apow, bband, cband, arow = _ssm_prep(ssm_A_re[l], ssm_A_im[l], ssm_log_dt[l], ssm_B_re[l],
                                                    ssm_B_im[l], ssm_C_re[l], ssm_C_im[l], ssm_D[l], chunk=SSM_CHUNK)

        x1s, qs, st_s = _step_layer_mix(xs, (state_ssm_re[l], state_ssm_im[l], state_conv[l]), p,
                                        (bband, cband, arow), ssm_D[l].reshape(1, -1))
        x1p, qp, st_p, later = _prompt_mixer(xp, p, mats, apow,
                                             [w_xo[l], w_up[l], w_gate[l], w_down[l], w_k[l], w_v[l]])
        p["w_xo"], p["w_up"], p["w_gate"], p["w_down"], w_k_bf, w_v_bf = later

        k_p, v_p = _kv_proj(mem_prompt, norm_mem[l], w_k_bf, w_v_bf)
        mk_p.append(k_p)
        mv_p.append(v_p)

        side = (qs, cache_mem_k[l], cache_mem_v[l], xs.shape[1])
        xp, ffn_buf_p, os_bm = _prompt_ffn(x1p, qp, k_p, v_p, p, norm_final, side)
        outs_p.append(st_p + (ffn_buf_p,))
        xs, ffn_buf_s = _step_layer_ffn(x1s, os_bm, state_ffn_conv[l], p, norm_final, nseq=nbs)
        outs_s.append(st_s + (ffn_buf_s,))

    stack = lambda outs, i: jnp.stack([o[i] for o in outs])
    return (xp.reshape(nbp, tp, d), xs, jnp.stack(mk_p), jnp.stack(mv_p),
            stack(outs_p, 0), stack(outs_p, 1), stack(outs_p, 2), stack(outs_p, 3),
            stack(outs_s, 0), stack(outs_s, 1), stack(outs_s, 2), stack(outs_s, 3))
```

```python
import functools

import jax
import jax.numpy as jnp
from jax import lax
from jax.experimental import pallas as pl
from jax.experimental.pallas import tpu as pltpu

F32 = jnp.float32
BF16 = jnp.bfloat16

EPS = 1e-6
SSM_GROUP_CH = 16
SSM_STATE = 64
CONV_K = 3
N_MEM = 256
XATTN_HEADS = 4
SSM_CHUNK = 16
LANES = 128
MIX_STEPS = 128
MIX_SUBBLOCKS = 2
FFN_ROWS = 512
V7X_VMEM_LIMIT = 56 * 1024 * 1024
V7X_VMEM_LIMIT_MAX = 62 * 1024 * 1024


def _rms(x, g):
    ms = jnp.mean(x * x, axis=-1, keepdims=True)
    return x * lax.rsqrt(ms + EPS) * g


def _dot(a, b):
    return jnp.dot(a, b, preferred_element_type=F32)


def _const_spec(shape):
    nd = len(shape)
    return pl.BlockSpec(shape, lambda *_: (0,) * nd, pipeline_mode=pl.Buffered(1))


def _params(*sem, vmem=V7X_VMEM_LIMIT):
    return pltpu.CompilerParams(dimension_semantics=sem, vmem_limit_bytes=vmem)


def _cast_slabs(srcs, dsts):
    for src, dst in zip(srcs, dsts):
        dst[...] = src[...].astype(dst.dtype)


def _ssm_prep_kernel(*refs, gb, chunk, ncast):
    ld_ref, lr_r_ref, li_r_ref, brt_ref, bit_ref, crt_t_ref, cit_t_ref, d_ref = refs[:8]
    m_ref, s_ref, o_ref, apow_ref, bband_ref, cband_ref, arow_ref = refs[8 + ncast:15 + ncast]
    _cast_slabs(refs[8:8 + ncast], refs[15 + ncast:])
    _ssm_prep_groups(ld_ref, lr_r_ref, li_r_ref, brt_ref, bit_ref, crt_t_ref, cit_t_ref, d_ref,
                     m_ref, s_ref, o_ref, apow_ref, bband_ref, cband_ref, arow_ref, gb=gb, chunk=chunk)


def _ssm_prep_groups(ld_ref, lr_r_ref, li_r_ref, brt_ref, bit_ref, crt_t_ref, cit_t_ref, d_ref,
                     m_ref, s_ref, o_ref, apow_ref, bband_ref, cband_ref, arow_ref, *, gb, chunk):
    P, N = SSM_GROUP_CH, SSM_STATE
    LP = chunk * P
    bband_ref[...] = jnp.zeros(bband_ref.shape, bband_ref.dtype)
    cband_ref[...] = jnp.zeros(cband_ref.shape, cband_ref.dtype)
    lane_j = lax.broadcasted_iota(jnp.int32, (1, LP), 1) // P
    row_j = lax.broadcasted_iota(jnp.int32, (LP, 1), 0) // P
    lane_i = lax.broadcasted_iota(jnp.int32, (P, LP), 1)
    row_i = lax.broadcasted_iota(jnp.int32, (P, LP), 0)
    nbits = chunk.bit_length()

    eye = lax.broadcasted_iota(jnp.int32, (N, N), 0) == lax.broadcasted_iota(jnp.int32, (N, N), 1)

    def as_column(row):
        return jnp.sum(jnp.where(eye, row, 0.0), axis=1, keepdims=True)

    def squarings(pr, pi):
        out = [(pr, pi)]
        for _ in range(nbits - 1):
            pr, pi = pr * pr - pi * pi, 2.0 * pr * pi
            out.append((pr, pi))
        return out

    def cpow(pows, j):
        er = ei = None
        for b, (pr, pi) in enumerate(pows):
            if isinstance(j, int):
                if not (j >> b) & 1:
                    continue
                er, ei = (pr, pi) if er is None else (er * pr - ei * pi, er * pi + ei * pr)
            else:
                bit = ((j >> b) & 1) == 1
                if er is None:
                    er, ei = jnp.where(bit, pr, 1.0), jnp.where(bit, pi, 0.0)
                else:
                    er, ei = jnp.where(bit, er * pr - ei * pi, er), jnp.where(bit, er * pi + ei * pr, ei)
        return er, ei

    for gi in range(gb):
        dt = jnp.exp(ld_ref[gi])
        lr_r, li_r = lr_r_ref[gi], li_r_ref[gi]
        mag = jnp.exp(dt * lr_r)
        ar, ai = mag * jnp.cos(dt * li_r), mag * jnp.sin(dt * li_r)
        pows_r = squarings(ar, ai)
        pows_c = squarings(as_column(ar), as_column(ai))

        den = lr_r * lr_r + li_r * li_r
        cr = ((ar - 1.0) * lr_r + ai * li_r) / den
        ci = (ai * lr_r - (ar - 1.0) * li_r) / den

        brt, bit = brt_ref[gi], bit_ref[gi]
        bbt_re = cr * brt - ci * bit
        bbt_im = cr * bit + ci * brt
        bbt_re_t = jnp.concatenate([bbt_re] * chunk, axis=0)
        bbt_im_t = jnp.concatenate([bbt_im] * chunk, axis=0)

        er, ei = cpow(pows_r, (chunk - 1) - row_j)
        s_ref[gi, :, 0:N] = er * bbt_re_t - ei * bbt_im_t
        s_ref[gi, :, N:2 * N] = er * bbt_im_t + ei * bbt_re_t

        crt_t, cit_t = crt_t_ref[gi], cit_t_ref[gi]
        er, ei = cpow(pows_c, lane_j)
        r_re = crt_t * er - cit_t * ei
        r_im = crt_t * ei + cit_t * er
        er, ei = er * pows_c[0][0] - ei * pows_c[0][1], er * pows_c[0][1] + ei * pows_c[0][0]
        o_ref[gi, 0:N, :] = (crt_t * er - cit_t * ei).astype(o_ref.dtype)
        o_ref[gi, N:2 * N, :] = (-(crt_t * ei + cit_t * er)).astype(o_ref.dtype)

        krow = (jnp.dot(bbt_re, r_re, preferred_element_type=F32, precision=lax.Precision.HIGHEST)
                - jnp.dot(bbt_im, r_im, preferred_element_type=F32, precision=lax.Precision.HIGHEST))
        krow = krow + jnp.where(lane_i == row_i, d_ref[gi], 0.0)
        for ti in range(chunk):
            blk = krow if ti == 0 else pltpu.roll(krow, ti * P, axis=1)
            blk = jnp.where(lane_i >= ti * P, blk, 0.0)
            m_ref[gi, ti * P:(ti + 1) * P, :] = blk.astype(m_ref.dtype)

        apow_ref[gi] = jnp.concatenate(cpow(pows_r, chunk), axis=0)

        rows, cols = slice(gi * P, (gi + 1) * P), slice(gi * N, (gi + 1) * N)
        cols_im = slice((gb + gi) * N, (gb + gi + 1) * N)
        bband_ref[rows, cols] = bbt_re
        bband_ref[rows, cols_im] = bbt_im
        cband_ref[cols, rows] = crt_t[:, rows]
        cband_ref[cols_im, rows] = -cit_t[:, rows]
        arow_ref[0:1, cols] = ar
        arow_ref[1:2, cols] = ai


def _ssm_prep(A_re, A_im, log_dt, B_re, B_im, C_re, C_im, D, to_bf16, *, chunk, gb=8):
    G, N = A_re.shape
    P = SSM_GROUP_CH
    LP = chunk * P
    brt = B_re.transpose(0, 2, 1)
    bit = B_im.transpose(0, 2, 1)
    crt_t = jnp.tile(C_re.transpose(0, 2, 1), (1, 1, chunk))
    cit_t = jnp.tile(C_im.transpose(0, 2, 1), (1, 1, chunk))
    ins = [log_dt.reshape(G, 1, 1), A_re.reshape(G, 1, N), A_im.reshape(G, 1, N), brt, bit, crt_t, cit_t,
           D.reshape(G, P, 1)]
    gspec = lambda *s: pl.BlockSpec((gb,) + s, lambda i: (i, 0, 0))
    in_specs = [gspec(*a.shape[1:]) for a in ins]
    out_shape = [jax.ShapeDtypeStruct((G, LP, LP), BF16), jax.ShapeDtypeStruct((G, LP, 2 * N), F32),
                 jax.ShapeDtypeStruct((G, 2 * N, LP), BF16), jax.ShapeDtypeStruct((G, 2, N), F32)]
    out_specs = [gspec(*o.shape[1:]) for o in out_shape]
    band_shape = [jax.ShapeDtypeStruct((G // gb, gb * P, 2 * gb * N), F32),
                  jax.ShapeDtypeStruct((G // gb, 2 * gb * N, gb * P), F32),
                  jax.ShapeDtypeStruct((G // gb, 2, gb * N), F32)]
    out_shape += band_shape
    out_specs += [pl.BlockSpec((None,) + o.shape[1:], lambda i: (i, 0, 0)) for o in band_shape]
    slab_specs = [pl.BlockSpec((w.shape[0] // (G // gb), w.shape[1]), lambda i: (i, 0)) for w in to_bf16]
    return pl.pallas_call(
        functools.partial(_ssm_prep_kernel, gb=gb, chunk=chunk, ncast=len(to_bf16)),
        grid=(G // gb,), in_specs=in_specs + slab_specs, out_specs=out_specs + slab_specs,
        out_shape=out_shape + [jax.ShapeDtypeStruct(w.shape, BF16) for w in to_bf16],
        compiler_params=_params("arbitrary"), name="ssm_prep")(*ins, *to_bf16)


def _step_rows(a):
    return jnp.transpose(a, (1, 0, 2)).reshape(a.shape[0] * a.shape[1], a.shape[2])


def _seq_major(a, nseq):
    return jnp.transpose(a.reshape(a.shape[0] // nseq, nseq, a.shape[1]), (1, 0, 2))


def _conv3_steps_apart(v, w_ref, s_ref, prev, *, ts):
    tm = v.shape[0]
    s_ref[0:2 * ts, :] = prev
    s_ref[2 * ts:2 * ts + tm, :] = v
    out = w_ref[0:1, :] * s_ref[0:tm, :] + w_ref[1:2, :] * s_ref[ts:ts + tm, :] + w_ref[2:3, :] * v
    return out, s_ref[tm:tm + 2 * ts, :]


def _conv3_rows(v, w_ref, c_ref):
    tm = v.shape[0]
    head = c_ref[...]
    row = lax.broadcasted_iota(jnp.int32, head.shape, 0)

    def back(k):
        r = pltpu.roll(v, k, axis=0)
        first = jnp.where(row < k, pltpu.roll(head, k, axis=0), r[0:8])
        return jnp.concatenate([first, r[8:]], axis=0)

    out = w_ref[0:1, :] * back(2) + w_ref[1:2, :] * back(1) + w_ref[2:3, :] * v
    c_ref[...] = v[tm - 8:tm]
    return out


def _piece_transpose8(v):
    width = LANES // 8
    piece = lax.broadcasted_iota(jnp.int32, v[0].shape, 1) // width
    for bit in range(3):
        s = 1 << bit
        hi = ((piece >> bit) & 1) == 1
        nv = list(v)
        for i in range(8):
            if i & s:
                continue
            a, b = v[i], v[i + s]
            nv[i] = jnp.where(hi, pltpu.roll(b, s * width, axis=1), a)
            nv[i + s] = jnp.where(hi, b, pltpu.roll(a, LANES - s * width, axis=1))
        v = nv
    return v


def _tokens_to_groups(zu, u_ref, *, nseg, seg):
    nc = seg // SSM_CHUNK
    ms = zu.shape[1]
    z_tb = jnp.transpose(zu.reshape(nseg, seg, ms), (1, 0, 2))
    for h in range(SSM_CHUNK // 8):
        for q in range(ms // LANES):
            xs = [jnp.concatenate([z_tb[SSM_CHUNK * c + 8 * h + i, :, q * LANES:(q + 1) * LANES] for c in range(nc)],
                                  axis=0) for i in range(8)]
            w = _piece_transpose8(xs)
            for k in range(8):
                u_ref[8 * q + k, :, h * LANES:(h + 1) * LANES] = w[k].astype(u_ref.dtype)


def _groups_to_tokens(y_ref, *, nseg, seg):
    nc = seg // SSM_CHUNK
    nq = y_ref.shape[0] // 8
    tiles = [[None] * nq for _ in range(seg)]
    for h in range(SSM_CHUNK // 8):
        for q in range(nq):
            v = _piece_transpose8([y_ref[8 * q + k, :, h * LANES:(h + 1) * LANES].astype(F32) for k in range(8)])
            for i in range(8):
                for c in range(nc):
                    tiles[SSM_CHUNK * c + 8 * h + i][q] = v[i][c * nseg:(c + 1) * nseg]
    y_tb = jnp.stack([jnp.concatenate(row, axis=1) for row in tiles], axis=0)
    return jnp.transpose(y_tb, (1, 0, 2)).reshape(nseg * seg, nq * LANES)


def _mix_in_core(x, g_ref, w_ref, gc_ref, conv, *, ms, mc):
    hb = _rms(x, g_ref[...]).astype(BF16)
    zu = _dot(hb, w_ref[:, 0:ms])
    xin = _dot(hb, w_ref[:, ms:ms + mc])
    cg = _dot(hb, w_ref[:, ms + 2 * mc:ms + 3 * mc])
    cv = conv(cg * xin)
    bg = _dot(hb, w_ref[:, ms + mc:ms + 2 * mc])
    return zu, _rms(bg * cv, gc_ref[...]).astype(BF16)


def _mix_in_kernel(*refs, nseg, seg, nsub, ncast, ms, mc):
    x_ref, g_ref, w_ref, cw_ref, gc_ref = refs[:5]
    cast_in, (u_ref, yc_ref, buf_ref) = refs[5:5 + ncast], refs[5 + ncast:8 + ncast]
    cast_out, s_ref = refs[8 + ncast:8 + 2 * ncast], refs[8 + 2 * ncast]
    _cast_slabs(cast_in, cast_out)

    sub = seg // nsub
    rg = sub // SSM_CHUNK * nseg

    @pl.when(pl.program_id(0) == 0)
    def _():
        s_ref[...] = jnp.zeros(s_ref.shape, F32)

    def conv(v):
        return jnp.concatenate([_conv3_rows(v[b * sub:(b + 1) * sub], cw_ref, s_ref.at[b]) for b in range(nseg)],
                               axis=0)

    for h in range(nsub):
        steps = slice(h * sub, (h + 1) * sub)
        zu, ycn = _mix_in_core(x_ref[:, steps, :].reshape(nseg * sub, x_ref.shape[-1]), g_ref, w_ref, gc_ref, conv,
                               ms=ms, mc=mc)
        _tokens_to_groups(zu, u_ref.at[:, pl.ds(h * rg, rg), :], nseg=nseg, seg=sub)
        yc_ref[:, steps, :] = ycn.reshape(nseg, sub, mc)
    buf_ref[...] = s_ref[:, 6:8, :]


def _mix_in(x3d, g, w_bf, conv_w, g_conv, to_bf16, *, seg, ms, mc):
    nseg, tlen, d = x3d.shape
    G = ms // SSM_GROUP_CH
    lp = SSM_CHUNK * SSM_GROUP_CH
    nsteps = tlen // seg
    row_spec = lambda c: pl.BlockSpec((nseg, seg, c), lambda t: (0, t, 0))
    slab_specs = [pl.BlockSpec((w.shape[0] // nsteps, w.shape[1]), lambda t: (t, 0)) for w in to_bf16]
    in_specs = [row_spec(d), _const_spec((1, d)), _const_spec(w_bf.shape), _const_spec(conv_w.shape),
                _const_spec((1, mc))] + slab_specs
    return pl.pallas_call(
        functools.partial(_mix_in_kernel, nseg=nseg, seg=seg, nsub=MIX_SUBBLOCKS, ncast=len(to_bf16), ms=ms, mc=mc),
        grid=(nsteps,), in_specs=in_specs,
        out_specs=[pl.BlockSpec((G, seg // SSM_CHUNK * nseg, lp), lambda t: (0, t, 0)), row_spec(mc),
                   pl.BlockSpec((nseg, 2, mc), lambda t: (0, 0, 0))] + slab_specs,
        out_shape=[jax.ShapeDtypeStruct((G, tlen // SSM_CHUNK * nseg, lp), BF16),
                   jax.ShapeDtypeStruct((nseg, tlen, mc), BF16), jax.ShapeDtypeStruct((nseg, 2, mc), F32)]
        + [jax.ShapeDtypeStruct(w.shape, BF16) for w in to_bf16],
        scratch_shapes=[pltpu.VMEM((nseg, 8, mc), F32)],
        compiler_params=_params("arbitrary"), name="mix_in")(x3d, g.reshape(1, d), w_bf, conv_w,
                                                              g_conv.reshape(1, mc), *to_bf16)


def _ssm_kernel(u_ref, m_ref, s_ref, o_ref, ap_ref, y_ref, fr_ref, fi_ref, lre, lim, ire, iim, *, gb, nb, nchunk):
    n = SSM_STATE
    for gi in range(gb):
        loc = _dot(u_ref[gi], s_ref[gi].astype(BF16))
        lre[gi] = loc[:, 0:n]
        lim[gi] = loc[:, n:2 * n]
    ar = [jnp.broadcast_to(ap_ref[gi, 0:1, :], (nb, n)) for gi in range(gb)]
    ai = [jnp.broadcast_to(ap_ref[gi, 1:2, :], (nb, n)) for gi in range(gb)]
    st0 = tuple((jnp.zeros((nb, n), F32), jnp.zeros((nb, n), F32)) for _ in range(gb))

    def step(c, st):
        r = pl.multiple_of(c * nb, nb)
        new = []
        for gi in range(gb):
            sr, si = st[gi]
            ire[gi, pl.ds(r, nb), :] = sr
            iim[gi, pl.ds(r, nb), :] = si
            nr = ar[gi] * sr - ai[gi] * si + lre[gi, pl.ds(r, nb), :]
            ni = ar[gi] * si + ai[gi] * sr + lim[gi, pl.ds(r, nb), :]
            new.append((nr, ni))
        return tuple(new)

    st = lax.fori_loop(0, nchunk, step, st0, unroll=4)
    for gi in range(gb):
        fr_ref[gi] = st[gi][0]
        fi_ref[gi] = st[gi][1]
        ini = jnp.concatenate([ire[gi], iim[gi]], axis=1).astype(BF16)
        y = _dot(u_ref[gi], m_ref[gi]) + _dot(ini, o_ref[gi])
        y_ref[gi] = y.astype(y_ref.dtype)


def _ssm(u_g, mats, apow, *, nb, nchunk, gb=4):
    G, R, LP = u_g.shape
    n = SSM_STATE
    ins = [u_g, *mats, apow]
    gspec = lambda *s: pl.BlockSpec((gb,) + s, lambda i: (i, 0, 0))
    out_shape = [jax.ShapeDtypeStruct((G, R, LP), F32), jax.ShapeDtypeStruct((G, nb, n), F32),
                 jax.ShapeDtypeStruct((G, nb, n), F32)]
    return pl.pallas_call(
        functools.partial(_ssm_kernel, gb=gb, nb=nb, nchunk=nchunk),
        grid=(G // gb,), in_specs=[gspec(*a.shape[1:]) for a in ins],
        out_specs=[gspec(*o.shape[1:]) for o in out_shape],
        out_shape=out_shape, scratch_shapes=[pltpu.VMEM((gb, R, n), F32)] * 4,
        compiler_params=_params("arbitrary"), name="ssm_chunk")(*ins)


def _mix_out_core(x, ys, yc, wg_ref, bg_ref, gs_ref, wo_ref, gx_ref, wq_ref, *, ms, q_scale):
    y = jax.nn.gelu(ys)
    y = y * jax.nn.sigmoid(_dot(y.astype(BF16), wg_ref[...]) + bg_ref[...])
    ysn = _rms(y, gs_ref[...]).astype(BF16)
    x1 = x + _dot(jnp.concatenate([ysn, yc], axis=1), wo_ref[...])
    h = _rms(x1, gx_ref[...]).astype(BF16)
    return x1, _dot(h, wq_ref[...]) * q_scale


def _mix_out_kernel(x_ref, ys_ref, yc_ref, wg_ref, bg_ref, gs_ref, wo_ref, gx_ref, wq_ref, x1_ref, q_ref,
                    *, nseg, seg, nsub, ms, q_scale):
    sub = seg // nsub
    rg = sub // SSM_CHUNK * nseg
    d = x_ref.shape[-1]
    for h in range(nsub):
        steps = slice(h * sub, (h + 1) * sub)
        ys = _groups_to_tokens(ys_ref.at[:, pl.ds(h * rg, rg), :], nseg=nseg, seg=sub)
        x1, q = _mix_out_core(x_ref[:, steps, :].reshape(nseg * sub, d),
                              ys, yc_ref[:, steps, :].reshape(nseg * sub, yc_ref.shape[-1]),
                              wg_ref, bg_ref, gs_ref, wo_ref, gx_ref, wq_ref, ms=ms, q_scale=q_scale)
        x1_ref[:, steps, :] = x1.reshape(nseg, sub, d)
        q_ref[:, steps, :] = q.astype(q_ref.dtype).reshape(nseg, sub, d)


def _mix_out(x3d, ys, yc, w_glu, b_glu, g_ssm, w_out, g_x, w_q, *, seg):
    nseg, tlen, d = x3d.shape
    ms = w_glu.shape[0]
    hd = d // XATTN_HEADS
    row_spec = lambda c: pl.BlockSpec((nseg, seg, c), lambda t: (0, t, 0))
    ys_spec = pl.BlockSpec((ys.shape[0], seg // SSM_CHUNK * nseg, ys.shape[2]), lambda t: (0, t, 0))
    in_specs = [row_spec(d), ys_spec, row_spec(yc.shape[-1]), _const_spec(w_glu.shape), _const_spec((1, ms)),
                _const_spec((1, ms)), _const_spec(w_out.shape), _const_spec((1, d)), _const_spec(w_q.shape)]
    return pl.pallas_call(
        functools.partial(_mix_out_kernel, nseg=nseg, seg=seg, nsub=MIX_SUBBLOCKS, ms=ms, q_scale=hd ** -0.5),
        grid=(tlen // seg,), in_specs=in_specs, out_specs=[row_spec(d), row_spec(d)],
        out_shape=[jax.ShapeDtypeStruct((nseg, tlen, d), F32), jax.ShapeDtypeStruct((nseg, tlen, d), BF16)],
        compiler_params=_params("parallel"), name="mix_out")(
            x3d, ys, yc, w_glu, b_glu.reshape(1, ms), g_ssm.reshape(1, ms), w_out, g_x.reshape(1, d), w_q)


def _step_mix_kernel(x_ref, g_ref, w_ref, cw_ref, gc_ref, prev_ref, s0r_ref, s0i_ref, bband_ref, cband_ref, arow_ref,
                     d_ref, wg_ref, bg_ref, gs_ref, wo_ref, gx_ref, wq_ref,
                     x1_ref, q_ref, buf_ref, fr_ref, fi_ref, s_ref, ys_ref, *, nseq, tlen, ms, mc, q_scale):
    x = _step_rows(x_ref[...])

    def conv(v):
        out, buf = _conv3_steps_apart(v, cw_ref, s_ref, _step_rows(prev_ref[...]), ts=nseq)
        buf_ref[...] = _seq_major(buf, nseq)
        return out

    zu, ycn = _mix_in_core(x, g_ref, w_ref, gc_ref, conv, ms=ms, mc=mc)

    wu, ws = bband_ref.shape[1], arow_ref.shape[2]
    for i in range(bband_ref.shape[0]):
        ucols, scols = slice(i * wu, (i + 1) * wu), slice(i * ws, (i + 1) * ws)
        sr, si = s0r_ref[:, scols], s0i_ref[:, scols]
        ar, ai = arow_ref[i, 0:1, :], arow_ref[i, 1:2, :]
        bband, cband = bband_ref[i].astype(BF16), cband_ref[i].astype(BF16)
        for t in range(tlen):
            ut = zu[t * nseq:(t + 1) * nseq, ucols]
            bu = _dot(ut.astype(BF16), bband)
            sr, si = ar * sr - ai * si + bu[:, 0:ws], ar * si + ai * sr + bu[:, ws:]
            cs = _dot(jnp.concatenate([sr, si], axis=1).astype(BF16), cband)
            ys_ref[t * nseq:(t + 1) * nseq, ucols] = cs + d_ref[:, ucols] * ut
        fr_ref[:, scols] = sr
        fi_ref[:, scols] = si

    x1, q = _mix_out_core(x, ys_ref[...], ycn, wg_ref, bg_ref, gs_ref, wo_ref, gx_ref, wq_ref, ms=ms, q_scale=q_scale)
    x1_ref[...] = x1
    q_ref[...] = _seq_major(q, nseq)


def _step_mix(x3d, p, conv_prev, s0_re, s0_im, bands, d_row):
    nseq, tlen, d = x3d.shape
    rows = nseq * tlen
    ms = p["w_glu"].shape[0]
    mc = p["conv_w"].shape[1]
    hd = d // XATTN_HEADS
    bband, cband, arow = bands
    ins = [x3d, p["norm_mix"].reshape(1, d), p["w_in"], p["conv_w"], p["norm_conv_out"].reshape(1, mc), conv_prev,
           s0_re, s0_im, bband, cband, arow, d_row, p["w_glu"], p["b_glu"].reshape(1, ms),
           p["norm_ssm_out"].reshape(1, ms), p["w_out"], p["norm_xattn"].reshape(1, d), p["w_q"]]
    full = lambda shape: pl.BlockSpec(shape, lambda i: (0,) * len(shape))
    out_shape = [jax.ShapeDtypeStruct((rows, d), F32), jax.ShapeDtypeStruct((nseq, tlen, d), F32),
                 jax.ShapeDtypeStruct(conv_prev.shape, F32), jax.ShapeDtypeStruct(s0_re.shape, F32),
                 jax.ShapeDtypeStruct(s0_im.shape, F32)]
    return pl.pallas_call(
        functools.partial(_step_mix_kernel, nseq=nseq, tlen=tlen, ms=ms, mc=mc, q_scale=hd ** -0.5),
        grid=(1,), in_specs=[_const_spec(a.shape) for a in ins], out_specs=[full(o.shape) for o in out_shape],
        out_shape=out_shape,
        scratch_shapes=[pltpu.VMEM((2 * nseq + rows, mc), F32), pltpu.VMEM((rows, ms), F32)],
        compiler_params=_params("arbitrary"), name="step_mix")(*ins)


def _kv_kernel(m_ref, g_ref, wk_ref, wv_ref, k_ref, v_ref):
    h = _rms(m_ref[0], g_ref[...]).astype(BF16)
    hd = k_ref.shape[-1]
    for w_ref, out_ref in ((wk_ref, k_ref), (wv_ref, v_ref)):
        kv = _dot(h, w_ref[...])
        heads = jnp.stack([kv[:, i * hd:(i + 1) * hd] for i in range(XATTN_HEADS)], axis=0)
        out_ref[0] = jnp.transpose(heads, (1, 0, 2))


def _kv_proj(mem, g, w_k, w_v):
    nseq, nm, d = mem.shape
    hd = d // XATTN_HEADS
    out_spec = pl.BlockSpec((1, nm, XATTN_HEADS, hd), lambda i: (i, 0, 0, 0))
    return pl.pallas_call(
        _kv_kernel, grid=(nseq,),
        in_specs=[pl.BlockSpec((1, nm, d), lambda i: (i, 0, 0)), _const_spec((1, d)), _const_spec(w_k.shape),
                  _const_spec(w_v.shape)],
        out_specs=[out_spec, out_spec],
        out_shape=[jax.ShapeDtypeStruct((nseq, nm, XATTN_HEADS, hd), F32)] * 2,
        compiler_params=_params("parallel"), name="kv_proj")(mem, g.reshape(1, d), w_k, w_v)


def _softmax_attention(qs, ks, vs):
    tq = qs[0].shape[0]
    s = jnp.concatenate([lax.dot_general(q, k, (((1,), (1,)), ((), ())), preferred_element_type=F32)
                         for q, k in zip(qs, ks)], axis=0)
    p = jnp.exp(s - jnp.max(s, axis=-1, keepdims=True))
    p = p / jnp.sum(p, axis=-1, keepdims=True)
    return [_dot(p[i * tq:(i + 1) * tq].astype(BF16), v) for i, v in enumerate(vs)]


def _head_slices(ref, rows, hd):
    return [ref[rows, h * hd:(h + 1) * hd].astype(BF16) for h in range(XATTN_HEADS)]


def _kv_heads(ref, j):
    hm = jnp.transpose(ref[j], (1, 0, 2))
    return [hm[h].astype(BF16) for h in range(XATTN_HEADS)]


def _attend_sequences(q_ref, k_ref, v_ref, o_ref):
    sb, _, d = q_ref.shape
    hd = d // XATTN_HEADS
    qs, ks, vs = [], [], []
    for j in range(sb):
        qs += _head_slices(q_ref.at[j], slice(None), hd)
        ks += _kv_heads(k_ref, j)
        vs += _kv_heads(v_ref, j)
    outs = _softmax_attention(qs, ks, vs)
    for j in range(sb):
        o_ref[j] = jnp.concatenate(outs[j * XATTN_HEADS:(j + 1) * XATTN_HEADS], axis=1).astype(o_ref.dtype)


def _ffn_kernel(*refs, ts, has_prev, has_kv, side):
    x_ref, o_ref = refs[:2]
    refs = refs[2:]
    if has_kv:
        k_ref, v_ref = refs[:2]
        refs = refs[2:]
    if side:
        sq_ref, sk_ref, sv_ref = refs[:3]
        refs = refs[3:]
    wxo_ref, gf_ref, wup_ref, wgate_ref, cw_ref, wdn_ref, gl_ref = refs[:7]
    refs = refs[7:]
    if has_prev:
        prev_ref, refs = refs[0], refs[1:]
    if side:
        y_ref, buf_ref, so_ref, s_ref = refs
    else:
        y_ref, buf_ref, s_ref = refs
    if not has_prev:
        @pl.when(pl.program_id(1) == 0)
        def _():
            s_ref[...] = jnp.zeros(s_ref.shape, F32)

    if has_kv:
        hd = o_ref.shape[1] // XATTN_HEADS
        rows = slice(None)
        heads = _softmax_attention(_head_slices(o_ref, rows, hd), _kv_heads(k_ref, 0), _kv_heads(v_ref, 0))
        o = jnp.concatenate(heads, axis=1).astype(BF16)
    elif has_prev:
        o = _step_rows(o_ref[...]).astype(BF16)
    else:
        o = o_ref[...].astype(BF16)
    x2 = x_ref[...] + _dot(o, wxo_ref[...])
    hb = _rms(x2, gf_ref[...]).astype(BF16)
    up = _dot(hb, wup_ref[...])
    if has_prev:
        a, buf = _conv3_steps_apart(up, cw_ref, s_ref, _step_rows(prev_ref[...]), ts=ts)
        buf_ref[...] = _seq_major(buf, ts)
    else:
        a = _conv3_rows(up, cw_ref, s_ref)
        buf_ref[...] = s_ref[6:8, :]
    act = (jax.nn.gelu(a) * _dot(hb, wgate_ref[...])).astype(BF16)
    y = _rms(x2 + _dot(act, wdn_ref[...]), gl_ref[...])
    y_ref[...] = _seq_major(y, ts) if has_prev else y
    if side:
        _attend_sequences(sq_ref, sk_ref, sv_ref, so_ref)


def _ffn(x2d, o2d, kv, w_xo, g_ffn, w_up, w_gate, conv_w, w_down, g_last, prev, *, nb, nt, tm, ts, side=None):
    rows, d = x2d.shape
    dff = w_up.shape[1]
    has_prev = prev is not None
    row_spec = pl.BlockSpec((tm, d), lambda b, t: (b * nt + t, 0))
    in_specs = [row_spec, _const_spec(o2d.shape) if has_prev else row_spec]
    ins = [x2d, o2d]
    if kv is not None:
        in_specs += [pl.BlockSpec((1,) + kv[0].shape[1:], lambda b, t: (b, 0, 0, 0))] * 2
        ins += list(kv)
    if side is not None:
        sq, sk, sv = side
        sb = sq.shape[0] // (nb * nt)
        assert sb * nb * nt == sq.shape[0]
        side_q_spec = pl.BlockSpec((sb,) + sq.shape[1:], lambda b, t: (b * nt + t, 0, 0))
        in_specs += [side_q_spec] + [pl.BlockSpec((sb,) + sk.shape[1:], lambda b, t: (b * nt + t, 0, 0, 0))] * 2
        ins += [sq, sk, sv]
    in_specs += [_const_spec(w_xo.shape), _const_spec((1, d)), _const_spec(w_up.shape), _const_spec(w_gate.shape),
                 _const_spec(conv_w.shape), _const_spec(w_down.shape), _const_spec((1, d))]
    ins += [w_xo, g_ffn.reshape(1, d), w_up, w_gate, conv_w, w_down, g_last.reshape(1, d)]
    if has_prev:
        in_specs.append(_const_spec(prev.shape))
        ins.append(prev)
        out_shape = [jax.ShapeDtypeStruct(o2d.shape, F32), jax.ShapeDtypeStruct(prev.shape, F32)]
        out_specs = [pl.BlockSpec(o.shape, lambda b, t: (0, 0, 0)) for o in out_shape]
    else:
        out_shape = [jax.ShapeDtypeStruct((rows, d), F32), jax.ShapeDtypeStruct((nb, 2, dff), F32)]
        out_specs = [row_spec, pl.BlockSpec((None, 2, dff), lambda b, t: (b, 0, 0))]
    if side is not None:
        out_specs.append(side_q_spec)
        out_shape.append(jax.ShapeDtypeStruct(sq.shape, F32))
    return pl.pallas_call(
        functools.partial(_ffn_kernel, ts=ts, has_prev=has_prev, has_kv=kv is not None, side=side is not None),
        grid=(nb, nt), in_specs=in_specs, out_specs=out_specs, out_shape=out_shape,
        scratch_shapes=[pltpu.VMEM((2 * ts + tm if has_prev else 8, dff), F32)],
        compiler_params=_params("parallel", "arbitrary", vmem=V7X_VMEM_LIMIT if side is None else V7X_VMEM_LIMIT_MAX),
        name="conv_ffn")(*ins)


def _prompt_mixer(x3d, p, mats, apow, later_weights):
    nseq, tlen, d = x3d.shape
    ms = p["w_glu"].shape[0]
    mc = p["conv_w"].shape[1]
    u, ycn, conv_buf, *later_bf16 = _mix_in(x3d, p["norm_mix"], p["w_in"], p["conv_w"], p["norm_conv_out"],
                                            later_weights, seg=MIX_STEPS, ms=ms, mc=mc)
    y_g, f_re, f_im = _ssm(u, mats, apow, nb=nseq, nchunk=tlen // SSM_CHUNK)
    x1, q = _mix_out(x3d, y_g, ycn, p["w_glu"], p["b_glu"], p["norm_ssm_out"], p["w_out"], p["norm_xattn"],
                     p["w_q"], seg=MIX_STEPS)
    return x1, q, (f_re.transpose(1, 0, 2), f_im.transpose(1, 0, 2), conv_buf), later_bf16


def _prompt_ffn(x1, q, k, v, p, g_last, side):
    nseq, tlen, d = x1.shape
    rows = nseq * tlen
    return _ffn(x1.reshape(rows, d), q.reshape(rows, d), (k, v), p["w_xo"], p["norm_ffn"], p["w_up"],
                p["w_gate"], p["ffn_conv_w"], p["w_down"], g_last, None,
                nb=nseq, nt=tlen // FFN_ROWS, tm=FFN_ROWS, ts=1, side=side)


def _step_layer_mix(x_bm, prev, p, bands, d_row):
    nseq = x_bm.shape[0]
    G, N = prev[0].shape[1:]
    x1, q, conv_buf, f_re, f_im = _step_mix(x_bm, p, prev[2], prev[0].reshape(nseq, G * N),
                                            prev[1].reshape(nseq, G * N), bands, d_row)
    return x1, q, (f_re.reshape(nseq, G, N), f_im.reshape(nseq, G, N), conv_buf)


def _step_layer_ffn(x1, o_bm, prev_ffn, p, g_last):
    return _ffn(x1, o_bm, None, p["w_xo"], p["norm_ffn"], p["w_up"], p["w_gate"], p["ffn_conv_w"], p["w_down"],
                g_last, prev_ffn, nb=1, nt=1, tm=x1.shape[0], ts=o_bm.shape[0])


def kernel(x_prompt, x_sample, mem_prompt, cache_mem_k, cache_mem_v, state_ssm_re, state_ssm_im, state_conv, state_ffn_conv, norm_mix, w_in, ssm_A_re, ssm_A_im, ssm_log_dt, ssm_B_re, ssm_B_im, ssm_C_re, ssm_C_im, ssm_D, w_glu, b_glu, conv_w, norm_ssm_out, norm_conv_out, w_out, norm_xattn, norm_mem, w_q, w_k, w_v, w_xo, norm_ffn, w_up, w_gate, ffn_conv_w, w_down, norm_final):
    depth = w_in.shape[0]
    assert depth == 1, "the final norm is fused into the last (only) layer's ConvFFN kernel"
    nbp, tp, d = x_prompt.shape
    assert tp % FFN_ROWS == 0 and tp % MIX_STEPS == 0 and nbp == 8

    xp, xs = x_prompt, x_sample
    outs_p, outs_s, mk_p, mv_p = [], [], [], []
    for l in range(depth):
        m, s_cat, o_cat, apow, bband, cband, arow, *mixer_w = _ssm_prep(
            ssm_A_re[l], ssm_A_im[l], ssm_log_dt[l], ssm_B_re[l], ssm_B_im[l], ssm_C_re[l], ssm_C_im[l], ssm_D[l],
            [w_in[l], w_glu[l], w_out[l], w_q[l]], chunk=SSM_CHUNK)
        mats = (m, s_cat, o_cat)
        p = dict(norm_mix=norm_mix[l], b_glu=b_glu[l], conv_w=conv_w[l], norm_ssm_out=norm_ssm_out[l],
                 norm_conv_out=norm_conv_out[l], norm_xattn=norm_xattn[l], norm_ffn=norm_ffn[l],
                 ffn_conv_w=ffn_conv_w[l])
        p["w_in"], p["w_glu"], p["w_out"], p["w_q"] = mixer_w

        x1s, qs, st_s = _step_layer_mix(xs, (state_ssm_re[l], state_ssm_im[l], state_conv[l]), p,
                                        (bband, cband, arow), ssm_D[l].reshape(1, -1))
        x1p, qp, st_p, later = _prompt_mixer(xp, p, mats, apow,
                                             [w_xo[l], w_up[l], w_gate[l], w_down[l], w_k[l], w_v[l]])
        p["w_xo"], p["w_up"], p["w_gate"], p["w_down"], w_k_bf, w_v_bf = later

        k_p, v_p = _kv_proj(mem_prompt, norm_mem[l], w_k_bf, w_v_bf)
        mk_p.append(k_p)
        mv_p.append(v_p)

        side = (qs, cache_mem_k[l], cache_mem_v[l])
        xp, ffn_buf_p, os_bm = _prompt_ffn(x1p, qp, k_p, v_p, p, norm_final, side)
        outs_p.append(st_p + (ffn_buf_p,))
        xs, ffn_buf_s = _step_layer_ffn(x1s, os_bm, state_ffn_conv[l], p, norm_final)
        outs_s.append(st_s + (ffn_buf_s,))

    stack = lambda outs, i: jnp.stack([o[i] for o in outs])
    return (xp.reshape(nbp, tp, d), xs, jnp.stack(mk_p), jnp.stack(mv_p),
            stack(outs_p, 0), stack(outs_p, 1), stack(outs_p, 2), stack(outs_p, 3),
            stack(outs_s, 0), stack(outs_s, 1), stack(outs_s, 2), stack(outs_s, 3))
```

```python
import functools

import jax
import jax.numpy as jnp
from jax import lax
from jax.experimental import pallas as pl
from jax.experimental.pallas import tpu as pltpu

F32 = jnp.float32
BF16 = jnp.bfloat16

EPS = 1e-6
SSM_GROUP_CH = 16
SSM_STATE = 64
N_MEM = 256
XATTN_HEADS = 4
SSM_CHUNK = 16
LANES = 128
SUBLANES = 8
MIX_STEPS = 128
MIX_SUBBLOCKS = 2
FFN_ROWS = 512
V7X_VMEM_LIMIT = 56 * 1024 * 1024
V7X_VMEM_LIMIT_MAX = 62 * 1024 * 1024


def _rms(x, g):
    ms = jnp.mean(x * x, axis=-1, keepdims=True)
    return x * lax.rsqrt(ms + EPS) * g


def _dot(a, b):
    return jnp.dot(a, b, preferred_element_type=F32)


def _const_spec(shape):
    nd = len(shape)
    return pl.BlockSpec(shape, lambda *_: (0,) * nd, pipeline_mode=pl.Buffered(1))


def _params(*sem, vmem=V7X_VMEM_LIMIT):
    return pltpu.CompilerParams(dimension_semantics=sem, vmem_limit_bytes=vmem)


def _cast_slabs(srcs, dsts):
    for src, dst in zip(srcs, dsts):
        dst[...] = src[...].astype(dst.dtype)


def _ssm_prep_kernel(*refs, gb, chunk, ncast):
    ld_ref, lr_r_ref, li_r_ref, brt_ref, bit_ref, crt_t_ref, cit_t_ref, d_ref = refs[:8]
    m_ref, s_ref, o_ref, apow_ref, bband_ref, cband_ref, arow_ref = refs[8 + ncast:15 + ncast]
    _cast_slabs(refs[8:8 + ncast], refs[15 + ncast:])
    _ssm_prep_groups(ld_ref, lr_r_ref, li_r_ref, brt_ref, bit_ref, crt_t_ref, cit_t_ref, d_ref,
                     m_ref, s_ref, o_ref, apow_ref, bband_ref, cband_ref, arow_ref, gb=gb, chunk=chunk)


def _ssm_prep_groups(ld_ref, lr_r_ref, li_r_ref, brt_ref, bit_ref, crt_t_ref, cit_t_ref, d_ref,
                     m_ref, s_ref, o_ref, apow_ref, bband_ref, cband_ref, arow_ref, *, gb, chunk):
    P, N = SSM_GROUP_CH, SSM_STATE
    LP = chunk * P
    bband_ref[...] = jnp.zeros(bband_ref.shape, bband_ref.dtype)
    cband_ref[...] = jnp.zeros(cband_ref.shape, cband_ref.dtype)
    lane_j = lax.broadcasted_iota(jnp.int32, (1, LP), 1) // P
    row_j = lax.broadcasted_iota(jnp.int32, (LP, 1), 0) // P
    lane_i = lax.broadcasted_iota(jnp.int32, (P, LP), 1)
    row_i = lax.broadcasted_iota(jnp.int32, (P, LP), 0)
    nbits = chunk.bit_length()

    eye = lax.broadcasted_iota(jnp.int32, (N, N), 0) == lax.broadcasted_iota(jnp.int32, (N, N), 1)

    def as_column(row):
        return jnp.sum(jnp.where(eye, row, 0.0), axis=1, keepdims=True)

    def squarings(pr, pi):
        out = [(pr, pi)]
        for _ in range(nbits - 1):
            pr, pi = pr * pr - pi * pi, 2.0 * pr * pi
            out.append((pr, pi))
        return out

    def cpow(pows, j):
        er = ei = None
        for b, (pr, pi) in enumerate(pows):
            if isinstance(j, int):
                if not (j >> b) & 1:
                    continue
                er, ei = (pr, pi) if er is None else (er * pr - ei * pi, er * pi + ei * pr)
            else:
                bit = ((j >> b) & 1) == 1
                if er is None:
                    er, ei = jnp.where(bit, pr, 1.0), jnp.where(bit, pi, 0.0)
                else:
                    er, ei = jnp.where(bit, er * pr - ei * pi, er), jnp.where(bit, er * pi + ei * pr, ei)
        return er, ei

    for gi in range(gb):
        dt = jnp.exp(ld_ref[gi])
        lr_r, li_r = lr_r_ref[gi], li_r_ref[gi]
        mag = jnp.exp(dt * lr_r)
        ar, ai = mag * jnp.cos(dt * li_r), mag * jnp.sin(dt * li_r)
        pows_r = squarings(ar, ai)
        pows_c = squarings(as_column(ar), as_column(ai))

        den = lr_r * lr_r + li_r * li_r
        cr = ((ar - 1.0) * lr_r + ai * li_r) / den
        ci = (ai * lr_r - (ar - 1.0) * li_r) / den

        brt, bit = brt_ref[gi], bit_ref[gi]
        bbt_re = cr * brt - ci * bit
        bbt_im = cr * bit + ci * brt
        bbt_re_t = jnp.concatenate([bbt_re] * chunk, axis=0)
        bbt_im_t = jnp.concatenate([bbt_im] * chunk, axis=0)

        er, ei = cpow(pows_r, (chunk - 1) - row_j)
        s_ref[gi, :, 0:N] = er * bbt_re_t - ei * bbt_im_t
        s_ref[gi, :, N:2 * N] = er * bbt_im_t + ei * bbt_re_t

        crt_t, cit_t = crt_t_ref[gi], cit_t_ref[gi]
        er, ei = cpow(pows_c, lane_j)
        r_re = crt_t * er - cit_t * ei
        r_im = crt_t * ei + cit_t * er
        er, ei = er * pows_c[0][0] - ei * pows_c[0][1], er * pows_c[0][1] + ei * pows_c[0][0]
        o_ref[gi, 0:N, :] = (crt_t * er - cit_t * ei).astype(o_ref.dtype)
        o_ref[gi, N:2 * N, :] = (-(crt_t * ei + cit_t * er)).astype(o_ref.dtype)

        krow = (jnp.dot(bbt_re, r_re, preferred_element_type=F32, precision=lax.Precision.HIGHEST)
                - jnp.dot(bbt_im, r_im, preferred_element_type=F32, precision=lax.Precision.HIGHEST))
        krow = krow + jnp.where(lane_i == row_i, d_ref[gi], 0.0)
        for ti in range(chunk):
            blk = krow if ti == 0 else pltpu.roll(krow, ti * P, axis=1)
            blk = jnp.where(lane_i >= ti * P, blk, 0.0)
            m_ref[gi, ti * P:(ti + 1) * P, :] = blk.astype(m_ref.dtype)

        apow_ref[gi] = jnp.concatenate(cpow(pows_r, chunk), axis=0)

        rows, cols = slice(gi * P, (gi + 1) * P), slice(gi * N, (gi + 1) * N)
        cols_im = slice((gb + gi) * N, (gb + gi + 1) * N)
        bband_ref[rows, cols] = bbt_re
        bband_ref[rows, cols_im] = bbt_im
        cband_ref[cols, rows] = crt_t[:, rows]
        cband_ref[cols_im, rows] = -cit_t[:, rows]
        arow_ref[0:1, cols] = ar
        arow_ref[1:2, cols] = ai


def _ssm_prep(A_re, A_im, log_dt, B_re, B_im, C_re, C_im, D, to_bf16, *, chunk, gb=8):
    G, N = A_re.shape
    P = SSM_GROUP_CH
    LP = chunk * P
    brt = B_re.transpose(0, 2, 1)
    bit = B_im.transpose(0, 2, 1)
    crt_t = jnp.tile(C_re.transpose(0, 2, 1), (1, 1, chunk))
    cit_t = jnp.tile(C_im.transpose(0, 2, 1), (1, 1, chunk))
    ins = [log_dt.reshape(G, 1, 1), A_re.reshape(G, 1, N), A_im.reshape(G, 1, N), brt, bit, crt_t, cit_t,
           D.reshape(G, P, 1)]
    gspec = lambda *s: pl.BlockSpec((gb,) + s, lambda i: (i, 0, 0))
    in_specs = [gspec(*a.shape[1:]) for a in ins]
    out_shape = [jax.ShapeDtypeStruct((G, LP, LP), BF16), jax.ShapeDtypeStruct((G, LP, 2 * N), F32),
                 jax.ShapeDtypeStruct((G, 2 * N, LP), BF16), jax.ShapeDtypeStruct((G, 2, N), F32)]
    out_specs = [gspec(*o.shape[1:]) for o in out_shape]
    band_shape = [jax.ShapeDtypeStruct((G // gb, gb * P, 2 * gb * N), F32),
                  jax.ShapeDtypeStruct((G // gb, 2 * gb * N, gb * P), F32),
                  jax.ShapeDtypeStruct((G // gb, 2, gb * N), F32)]
    out_shape += band_shape
    out_specs += [pl.BlockSpec((None,) + o.shape[1:], lambda i: (i, 0, 0)) for o in band_shape]
    slab_specs = [pl.BlockSpec((w.shape[0] // (G // gb), w.shape[1]), lambda i: (i, 0)) for w in to_bf16]
    return pl.pallas_call(
        functools.partial(_ssm_prep_kernel, gb=gb, chunk=chunk, ncast=len(to_bf16)),
        grid=(G // gb,), in_specs=in_specs + slab_specs, out_specs=out_specs + slab_specs,
        out_shape=out_shape + [jax.ShapeDtypeStruct(w.shape, BF16) for w in to_bf16],
        compiler_params=_params("arbitrary"), name="ssm_prep")(*ins, *to_bf16)


def _step_rows(a):
    return jnp.transpose(a, (1, 0, 2)).reshape(a.shape[0] * a.shape[1], a.shape[2])


def _seq_major(a, nseq):
    return jnp.transpose(a.reshape(a.shape[0] // nseq, nseq, a.shape[1]), (1, 0, 2))


def _conv3_steps_apart(v, w_ref, s_ref, prev, *, ts):
    tm = v.shape[0]
    s_ref[0:2 * ts, :] = prev
    s_ref[2 * ts:2 * ts + tm, :] = v
    out = w_ref[0:1, :] * s_ref[0:tm, :] + w_ref[1:2, :] * s_ref[ts:ts + tm, :] + w_ref[2:3, :] * v
    return out, s_ref[tm:tm + 2 * ts, :]


def _conv3_rows(v, w_ref, c_ref):
    tm = v.shape[0]
    head = c_ref[...]
    row = lax.broadcasted_iota(jnp.int32, head.shape, 0)

    def back(k):
        r = pltpu.roll(v, k, axis=0)
        first = jnp.where(row < k, pltpu.roll(head, k, axis=0), r[0:SUBLANES])
        return jnp.concatenate([first, r[SUBLANES:]], axis=0)

    out = w_ref[0:1, :] * back(2) + w_ref[1:2, :] * back(1) + w_ref[2:3, :] * v
    c_ref[...] = v[tm - SUBLANES:tm]
    return out


def _conv3_state(c_ref):
    return c_ref[..., SUBLANES - 2:SUBLANES, :]


def _piece_transpose8(v):
    width = LANES // 8
    piece = lax.broadcasted_iota(jnp.int32, v[0].shape, 1) // width
    for bit in range(3):
        s = 1 << bit
        hi = ((piece >> bit) & 1) == 1
        nv = list(v)
        for i in range(8):
            if i & s:
                continue
            a, b = v[i], v[i + s]
            nv[i] = jnp.where(hi, pltpu.roll(b, s * width, axis=1), a)
            nv[i + s] = jnp.where(hi, b, pltpu.roll(a, LANES - s * width, axis=1))
        v = nv
    return v


def _tokens_to_groups(zu, u_ref, *, nseg, seg):
    nc = seg // SSM_CHUNK
    ms = zu.shape[1]
    z_tb = jnp.transpose(zu.reshape(nseg, seg, ms), (1, 0, 2))
    for h in range(SSM_CHUNK // 8):
        for q in range(ms // LANES):
            xs = [jnp.concatenate([z_tb[SSM_CHUNK * c + 8 * h + i, :, q * LANES:(q + 1) * LANES] for c in range(nc)],
                                  axis=0) for i in range(8)]
            w = _piece_transpose8(xs)
            for k in range(8):
                u_ref[8 * q + k, :, h * LANES:(h + 1) * LANES] = w[k].astype(u_ref.dtype)


def _groups_to_tokens(y_ref, *, nseg, seg):
    nc = seg // SSM_CHUNK
    nq = y_ref.shape[0] // 8
    tiles = [[None] * nq for _ in range(seg)]
    for h in range(SSM_CHUNK // 8):
        for q in range(nq):
            v = _piece_transpose8([y_ref[8 * q + k, :, h * LANES:(h + 1) * LANES].astype(F32) for k in range(8)])
            for i in range(8):
                for c in range(nc):
                    tiles[SSM_CHUNK * c + 8 * h + i][q] = v[i][c * nseg:(c + 1) * nseg]
    y_tb = jnp.stack([jnp.concatenate(row, axis=1) for row in tiles], axis=0)
    return jnp.transpose(y_tb, (1, 0, 2)).reshape(nseg * seg, nq * LANES)


def _mix_in_core(x, g_ref, w_ref, gc_ref, conv, *, ms, mc):
    hb = _rms(x, g_ref[...]).astype(BF16)
    zu = _dot(hb, w_ref[:, 0:ms])
    xin = _dot(hb, w_ref[:, ms:ms + mc])
    cg = _dot(hb, w_ref[:, ms + 2 * mc:ms + 3 * mc])
    cv = conv(cg * xin)
    bg = _dot(hb, w_ref[:, ms + mc:ms + 2 * mc])
    return zu, _rms(bg * cv, gc_ref[...]).astype(BF16)


def _mix_in_kernel(*refs, nseg, seg, nsub, ncast, ms, mc):
    x_ref, g_ref, w_ref, cw_ref, gc_ref = refs[:5]
    cast_in, (u_ref, yc_ref, buf_ref) = refs[5:5 + ncast], refs[5 + ncast:8 + ncast]
    cast_out, s_ref = refs[8 + ncast:8 + 2 * ncast], refs[8 + 2 * ncast]
    _cast_slabs(cast_in, cast_out)

    sub = seg // nsub
    rg = sub // SSM_CHUNK * nseg

    @pl.when(pl.program_id(0) == 0)
    def _():
        s_ref[...] = jnp.zeros(s_ref.shape, F32)

    def conv(v):
        return jnp.concatenate([_conv3_rows(v[b * sub:(b + 1) * sub], cw_ref, s_ref.at[b]) for b in range(nseg)],
                               axis=0)

    for h in range(nsub):
        steps = slice(h * sub, (h + 1) * sub)
        zu, ycn = _mix_in_core(x_ref[:, steps, :].reshape(nseg * sub, x_ref.shape[-1]), g_ref, w_ref, gc_ref, conv,
                               ms=ms, mc=mc)
        _tokens_to_groups(zu, u_ref.at[:, pl.ds(h * rg, rg), :], nseg=nseg, seg=sub)
        yc_ref[:, steps, :] = ycn.reshape(nseg, sub, mc)
    buf_ref[...] = _conv3_state(s_ref)


def _mix_in(x3d, g, w_bf, conv_w, g_conv, to_bf16, *, seg, ms, mc):
    nseg, tlen, d = x3d.shape
    G = ms // SSM_GROUP_CH
    lp = SSM_CHUNK * SSM_GROUP_CH
    nsteps = tlen // seg
    row_spec = lambda c: pl.BlockSpec((nseg, seg, c), lambda t: (0, t, 0))
    slab_specs = [pl.BlockSpec((w.shape[0] // nsteps, w.shape[1]), lambda t: (t, 0)) for w in to_bf16]
    in_specs = [row_spec(d), _const_spec((1, d)), _const_spec(w_bf.shape), _const_spec(conv_w.shape),
                _const_spec((1, mc))] + slab_specs
    return pl.pallas_call(
        functools.partial(_mix_in_kernel, nseg=nseg, seg=seg, nsub=MIX_SUBBLOCKS, ncast=len(to_bf16), ms=ms, mc=mc),
        grid=(nsteps,), in_specs=in_specs,
        out_specs=[pl.BlockSpec((G, seg // SSM_CHUNK * nseg, lp), lambda t: (0, t, 0)), row_spec(mc),
                   pl.BlockSpec((nseg, 2, mc), lambda t: (0, 0, 0))] + slab_specs,
        out_shape=[jax.ShapeDtypeStruct((G, tlen // SSM_CHUNK * nseg, lp), BF16),
                   jax.ShapeDtypeStruct((nseg, tlen, mc), BF16), jax.ShapeDtypeStruct((nseg, 2, mc), F32)]
        + [jax.ShapeDtypeStruct(w.shape, BF16) for w in to_bf16],
        scratch_shapes=[pltpu.VMEM((nseg, SUBLANES, mc), F32)],
        compiler_params=_params("arbitrary"), name="mix_in")(x3d, g.reshape(1, d), w_bf, conv_w,
                                                              g_conv.reshape(1, mc), *to_bf16)


def _ssm_kernel(u_ref, m_ref, s_ref, o_ref, ap_ref, y_ref, fr_ref, fi_ref, lre, lim, ire, iim, *, gb, nb, nchunk):
    n = SSM_STATE
    for gi in range(gb):
        loc = _dot(u_ref[gi], s_ref[gi].astype(BF16))
        lre[gi] = loc[:, 0:n]
        lim[gi] = loc[:, n:2 * n]
    ar = [jnp.broadcast_to(ap_ref[gi, 0:1, :], (nb, n)) for gi in range(gb)]
    ai = [jnp.broadcast_to(ap_ref[gi, 1:2, :], (nb, n)) for gi in range(gb)]
    st0 = tuple((jnp.zeros((nb, n), F32), jnp.zeros((nb, n), F32)) for _ in range(gb))

    def step(c, st):
        r = pl.multiple_of(c * nb, nb)
        new = []
        for gi in range(gb):
            sr, si = st[gi]
            ire[gi, pl.ds(r, nb), :] = sr
            iim[gi, pl.ds(r, nb), :] = si
            nr = ar[gi] * sr - ai[gi] * si + lre[gi, pl.ds(r, nb), :]
            ni = ar[gi] * si + ai[gi] * sr + lim[gi, pl.ds(r, nb), :]
            new.append((nr, ni))
        return tuple(new)

    st = lax.fori_loop(0, nchunk, step, st0, unroll=4)
    for gi in range(gb):
        fr_ref[gi] = st[gi][0]
        fi_ref[gi] = st[gi][1]
        ini = jnp.concatenate([ire[gi], iim[gi]], axis=1).astype(BF16)
        y = _dot(u_ref[gi], m_ref[gi]) + _dot(ini, o_ref[gi])
        y_ref[gi] = y.astype(y_ref.dtype)


def _ssm(u_g, mats, apow, *, nb, nchunk, gb=4):
    G, R, LP = u_g.shape
    n = SSM_STATE
    ins = [u_g, *mats, apow]
    gspec = lambda *s: pl.BlockSpec((gb,) + s, lambda i: (i, 0, 0))
    out_shape = [jax.ShapeDtypeStruct((G, R, LP), F32), jax.ShapeDtypeStruct((G, nb, n), F32),
                 jax.ShapeDtypeStruct((G, nb, n), F32)]
    return pl.pallas_call(
        functools.partial(_ssm_kernel, gb=gb, nb=nb, nchunk=nchunk),
        grid=(G // gb,), in_specs=[gspec(*a.shape[1:]) for a in ins],
        out_specs=[gspec(*o.shape[1:]) for o in out_shape],
        out_shape=out_shape, scratch_shapes=[pltpu.VMEM((gb, R, n), F32)] * 4,
        compiler_params=_params("arbitrary"), name="ssm_chunk")(*ins)


def _mix_out_core(x, ys, yc, wg_ref, bg_ref, gs_ref, wo_ref, gx_ref, wq_ref, *, ms, q_scale):
    y = jax.nn.gelu(ys)
    y = y * jax.nn.sigmoid(_dot(y.astype(BF16), wg_ref[...]) + bg_ref[...])
    ysn = _rms(y, gs_ref[...]).astype(BF16)
    x1 = x + _dot(jnp.concatenate([ysn, yc], axis=1), wo_ref[...])
    h = _rms(x1, gx_ref[...]).astype(BF16)
    return x1, _dot(h, wq_ref[...]) * q_scale


def _mix_out_kernel(x_ref, ys_ref, yc_ref, wg_ref, bg_ref, gs_ref, wo_ref, gx_ref, wq_ref, x1_ref, q_ref,
                    *, nseg, seg, nsub, ms, q_scale):
    sub = seg // nsub
    rg = sub // SSM_CHUNK * nseg
    d = x_ref.shape[-1]
    for h in range(nsub):
        steps = slice(h * sub, (h + 1) * sub)
        ys = _groups_to_tokens(ys_ref.at[:, pl.ds(h * rg, rg), :], nseg=nseg, seg=sub)
        x1, q = _mix_out_core(x_ref[:, steps, :].reshape(nseg * sub, d),
                              ys, yc_ref[:, steps, :].reshape(nseg * sub, yc_ref.shape[-1]),
                              wg_ref, bg_ref, gs_ref, wo_ref, gx_ref, wq_ref, ms=ms, q_scale=q_scale)
        x1_ref[:, steps, :] = x1.reshape(nseg, sub, d)
        q_ref[:, steps, :] = q.astype(q_ref.dtype).reshape(nseg, sub, d)


def _mix_out(x3d, ys, yc, w_glu, b_glu, g_ssm, w_out, g_x, w_q, *, seg):
    nseg, tlen, d = x3d.shape
    ms = w_glu.shape[0]
    hd = d // XATTN_HEADS
    row_spec = lambda c: pl.BlockSpec((nseg, seg, c), lambda t: (0, t, 0))
    ys_spec = pl.BlockSpec((ys.shape[0], seg // SSM_CHUNK * nseg, ys.shape[2]), lambda t: (0, t, 0))
    in_specs = [row_spec(d), ys_spec, row_spec(yc.shape[-1]), _const_spec(w_glu.shape), _const_spec((1, ms)),
                _const_spec((1, ms)), _const_spec(w_out.shape), _const_spec((1, d)), _const_spec(w_q.shape)]
    return pl.pallas_call(
        functools.partial(_mix_out_kernel, nseg=nseg, seg=seg, nsub=MIX_SUBBLOCKS, ms=ms, q_scale=hd ** -0.5),
        grid=(tlen // seg,), in_specs=in_specs, out_specs=[row_spec(d), row_spec(d)],
        out_shape=[jax.ShapeDtypeStruct((nseg, tlen, d), F32), jax.ShapeDtypeStruct((nseg, tlen, d), BF16)],
        compiler_params=_params("parallel"), name="mix_out")(
            x3d, ys, yc, w_glu, b_glu.reshape(1, ms), g_ssm.reshape(1, ms), w_out, g_x.reshape(1, d), w_q)


def _step_mix_kernel(x_ref, g_ref, w_ref, cw_ref, gc_ref, prev_ref, s0r_ref, s0i_ref, bband_ref, cband_ref, arow_ref,
                     d_ref, wg_ref, bg_ref, gs_ref, wo_ref, gx_ref, wq_ref,
                     x1_ref, q_ref, buf_ref, fr_ref, fi_ref, s_ref, ys_ref, *, nseq, tlen, ms, mc, q_scale):
    x = _step_rows(x_ref[...])

    def conv(v):
        out, buf = _conv3_steps_apart(v, cw_ref, s_ref, _step_rows(prev_ref[...]), ts=nseq)
        buf_ref[...] = _seq_major(buf, nseq)
        return out

    zu, ycn = _mix_in_core(x, g_ref, w_ref, gc_ref, conv, ms=ms, mc=mc)

    wu, ws = bband_ref.shape[1], arow_ref.shape[2]
    for i in range(bband_ref.shape[0]):
        ucols, scols = slice(i * wu, (i + 1) * wu), slice(i * ws, (i + 1) * ws)
        sr, si = s0r_ref[:, scols], s0i_ref[:, scols]
        ar, ai = arow_ref[i, 0:1, :], arow_ref[i, 1:2, :]
        bband, cband = bband_ref[i].astype(BF16), cband_ref[i].astype(BF16)
        for t in range(tlen):
            ut = zu[t * nseq:(t + 1) * nseq, ucols]
            bu = _dot(ut.astype(BF16), bband)
            sr, si = ar * sr - ai * si + bu[:, 0:ws], ar * si + ai * sr + bu[:, ws:]
            cs = _dot(jnp.concatenate([sr, si], axis=1).astype(BF16), cband)
            ys_ref[t * nseq:(t + 1) * nseq, ucols] = cs + d_ref[:, ucols] * ut
        fr_ref[:, scols] = sr
        fi_ref[:, scols] = si

    x1, q = _mix_out_core(x, ys_ref[...], ycn, wg_ref, bg_ref, gs_ref, wo_ref, gx_ref, wq_ref, ms=ms, q_scale=q_scale)
    x1_ref[...] = x1
    q_ref[...] = _seq_major(q, nseq)


def _step_mix(x3d, p, conv_prev, s0_re, s0_im, bands, d_row):
    nseq, tlen, d = x3d.shape
    rows = nseq * tlen
    ms = p["w_glu"].shape[0]
    mc = p["conv_w"].shape[1]
    hd = d // XATTN_HEADS
    bband, cband, arow = bands
    ins = [x3d, p["norm_mix"].reshape(1, d), p["w_in"], p["conv_w"], p["norm_conv_out"].reshape(1, mc), conv_prev,
           s0_re, s0_im, bband, cband, arow, d_row, p["w_glu"], p["b_glu"].reshape(1, ms),
           p["norm_ssm_out"].reshape(1, ms), p["w_out"], p["norm_xattn"].reshape(1, d), p["w_q"]]
    full = lambda shape: pl.BlockSpec(shape, lambda i: (0,) * len(shape))
    out_shape = [jax.ShapeDtypeStruct((rows, d), F32), jax.ShapeDtypeStruct((nseq, tlen, d), F32),
                 jax.ShapeDtypeStruct(conv_prev.shape, F32), jax.ShapeDtypeStruct(s0_re.shape, F32),
                 jax.ShapeDtypeStruct(s0_im.shape, F32)]
    return pl.pallas_call(
        functools.partial(_step_mix_kernel, nseq=nseq, tlen=tlen, ms=ms, mc=mc, q_scale=hd ** -0.5),
        grid=(1,), in_specs=[_const_spec(a.shape) for a in ins], out_specs=[full(o.shape) for o in out_shape],
        out_shape=out_shape,
        scratch_shapes=[pltpu.VMEM((2 * nseq + rows, mc), F32), pltpu.VMEM((rows, ms), F32)],
        compiler_params=_params("arbitrary"), name="step_mix")(*ins)


def _kv_kernel(m_ref, g_ref, wk_ref, wv_ref, k_ref, v_ref, kb_ref, vb_ref):
    h = _rms(m_ref[0], g_ref[...]).astype(BF16)
    hd = k_ref.shape[-1]
    for w_ref, out_ref, bf_ref in ((wk_ref, k_ref, kb_ref), (wv_ref, v_ref, vb_ref)):
        kv = _dot(h, w_ref[...])
        heads = jnp.stack([kv[:, i * hd:(i + 1) * hd] for i in range(XATTN_HEADS)], axis=0)
        out_ref[0] = jnp.transpose(heads, (1, 0, 2))
        bf_ref[0] = heads.astype(BF16)


def _kv_proj(mem, g, w_k, w_v):
    nseq, nm, d = mem.shape
    hd = d // XATTN_HEADS
    out_spec = pl.BlockSpec((1, nm, XATTN_HEADS, hd), lambda i: (i, 0, 0, 0))
    bf_spec = pl.BlockSpec((1, XATTN_HEADS, nm, hd), lambda i: (i, 0, 0, 0))
    return pl.pallas_call(
        _kv_kernel, grid=(nseq,),
        in_specs=[pl.BlockSpec((1, nm, d), lambda i: (i, 0, 0)), _const_spec((1, d)), _const_spec(w_k.shape),
                  _const_spec(w_v.shape)],
        out_specs=[out_spec, out_spec, bf_spec, bf_spec],
        out_shape=[jax.ShapeDtypeStruct((nseq, nm, XATTN_HEADS, hd), F32)] * 2
        + [jax.ShapeDtypeStruct((nseq, XATTN_HEADS, nm, hd), BF16)] * 2,
        compiler_params=_params("parallel"), name="kv_proj")(mem, g.reshape(1, d), w_k, w_v)


def _softmax_attention(qs, ks, vs):
    tq = qs[0].shape[0]
    s = jnp.concatenate([lax.dot_general(q, k, (((1,), (1,)), ((), ())), preferred_element_type=F32)
                         for q, k in zip(qs, ks)], axis=0)
    p = jnp.exp(s - jnp.max(s, axis=-1, keepdims=True))
    p = p / jnp.sum(p, axis=-1, keepdims=True)
    return [_dot(p[i * tq:(i + 1) * tq].astype(BF16), v) for i, v in enumerate(vs)]


def _head_slices(ref, rows, hd):
    return [ref[rows, h * hd:(h + 1) * hd].astype(BF16) for h in range(XATTN_HEADS)]


def _kv_heads(ref, j):
    hm = jnp.transpose(ref[j], (1, 0, 2))
    return [hm[h].astype(BF16) for h in range(XATTN_HEADS)]


def _attend_sequences(q_ref, k_ref, v_ref, o_ref):
    sb, _, d = q_ref.shape
    hd = d // XATTN_HEADS
    qs, ks, vs = [], [], []
    for j in range(sb):
        qs += _head_slices(q_ref.at[j], slice(None), hd)
        ks += _kv_heads(k_ref, j)
        vs += _kv_heads(v_ref, j)
    outs = _softmax_attention(qs, ks, vs)
    for j in range(sb):
        o_ref[j] = jnp.concatenate(outs[j * XATTN_HEADS:(j + 1) * XATTN_HEADS], axis=1).astype(o_ref.dtype)


def _ffn_kernel(*refs, ts, has_prev, has_kv, side):
    x_ref, o_ref = refs[:2]
    refs = refs[2:]
    if has_kv:
        k_ref, v_ref = refs[:2]
        refs = refs[2:]
    if side:
        sq_ref, sk_ref, sv_ref = refs[:3]
        refs = refs[3:]
    wxo_ref, gf_ref, wup_ref, wgate_ref, cw_ref, wdn_ref, gl_ref = refs[:7]
    refs = refs[7:]
    if has_prev:
        prev_ref, refs = refs[0], refs[1:]
    if side:
        y_ref, buf_ref, so_ref, s_ref = refs
    else:
        y_ref, buf_ref, s_ref = refs
    if not has_prev:
        @pl.when(pl.program_id(1) == 0)
        def _():
            s_ref[...] = jnp.zeros(s_ref.shape, F32)

    if has_kv:
        hd = o_ref.shape[1] // XATTN_HEADS
        rows = slice(None)
        heads = _softmax_attention(_head_slices(o_ref, rows, hd), [k_ref[0, h] for h in range(XATTN_HEADS)],
                                   [v_ref[0, h] for h in range(XATTN_HEADS)])
        o = jnp.concatenate(heads, axis=1).astype(BF16)
    elif has_prev:
        o = _step_rows(o_ref[...]).astype(BF16)
    else:
        o = o_ref[...].astype(BF16)
    x2 = x_ref[...] + _dot(o, wxo_ref[...])
    hb = _rms(x2, gf_ref[...]).astype(BF16)
    up = _dot(hb, wup_ref[...])
    if has_prev:
        a, buf = _conv3_steps_apart(up, cw_ref, s_ref, _step_rows(prev_ref[...]), ts=ts)
        buf_ref[...] = _seq_major(buf, ts)
    else:
        a = _conv3_rows(up, cw_ref, s_ref)
        buf_ref[...] = _conv3_state(s_ref)
    act = (jax.nn.gelu(a) * _dot(hb, wgate_ref[...])).astype(BF16)
    y = _rms(x2 + _dot(act, wdn_ref[...]), gl_ref[...])
    y_ref[...] = _seq_major(y, ts) if has_prev else y
    if side:
        _attend_sequences(sq_ref, sk_ref, sv_ref, so_ref)


def _ffn(x2d, o2d, kv, w_xo, g_ffn, w_up, w_gate, conv_w, w_down, g_last, prev, *, nb, nt, tm, ts, side=None):
    rows, d = x2d.shape
    dff = w_up.shape[1]
    has_prev = prev is not None
    row_spec = pl.BlockSpec((tm, d), lambda b, t: (b * nt + t, 0))
    in_specs = [row_spec, _const_spec(o2d.shape) if has_prev else row_spec]
    ins = [x2d, o2d]
    if kv is not None:
        in_specs += [pl.BlockSpec((1,) + kv[0].shape[1:], lambda b, t: (b, 0, 0, 0))] * 2
        ins += list(kv)
    if side is not None:
        sq, sk, sv = side
        sb = sq.shape[0] // (nb * nt)
        assert sb * nb * nt == sq.shape[0]
        side_q_spec = pl.BlockSpec((sb,) + sq.shape[1:], lambda b, t: (b * nt + t, 0, 0))
        in_specs += [side_q_spec] + [pl.BlockSpec((sb,) + sk.shape[1:], lambda b, t: (b * nt + t, 0, 0, 0))] * 2
        ins += [sq, sk, sv]
    in_specs += [_const_spec(w_xo.shape), _const_spec((1, d)), _const_spec(w_up.shape), _const_spec(w_gate.shape),
                 _const_spec(conv_w.shape), _const_spec(w_down.shape), _const_spec((1, d))]
    ins += [w_xo, g_ffn.reshape(1, d), w_up, w_gate, conv_w, w_down, g_last.reshape(1, d)]
    if has_prev:
        in_specs.append(_const_spec(prev.shape))
        ins.append(prev)
        out_shape = [jax.ShapeDtypeStruct(o2d.shape, F32), jax.ShapeDtypeStruct(prev.shape, F32)]
        out_specs = [pl.BlockSpec(o.shape, lambda b, t: (0, 0, 0)) for o in out_shape]
    else:
        out_shape = [jax.ShapeDtypeStruct((rows, d), F32), jax.ShapeDtypeStruct((nb, 2, dff), F32)]
        out_specs = [row_spec, pl.BlockSpec((None, 2, dff), lambda b, t: (b, 0, 0))]
    if side is not None:
        out_specs.append(side_q_spec)
        out_shape.append(jax.ShapeDtypeStruct(sq.shape, F32))
    return pl.pallas_call(
        functools.partial(_ffn_kernel, ts=ts, has_prev=has_prev, has_kv=kv is not None, side=side is not None),
        grid=(nb, nt), in_specs=in_specs, out_specs=out_specs, out_shape=out_shape,
        scratch_shapes=[pltpu.VMEM((2 * ts + tm if has_prev else SUBLANES, dff), F32)],
        compiler_params=_params("parallel", "arbitrary", vmem=V7X_VMEM_LIMIT if side is None else V7X_VMEM_LIMIT_MAX),
        name="conv_ffn")(*ins)


def _prompt_mixer(x3d, p, mats, apow, later_weights):
    nseq, tlen, d = x3d.shape
    ms = p["w_glu"].shape[0]
    mc = p["conv_w"].shape[1]
    u, ycn, conv_buf, *later_bf16 = _mix_in(x3d, p["norm_mix"], p["w_in"], p["conv_w"], p["norm_conv_out"],
                                            later_weights, seg=MIX_STEPS, ms=ms, mc=mc)
    y_g, f_re, f_im = _ssm(u, mats, apow, nb=nseq, nchunk=tlen // SSM_CHUNK)
    x1, q = _mix_out(x3d, y_g, ycn, p["w_glu"], p["b_glu"], p["norm_ssm_out"], p["w_out"], p["norm_xattn"],
                     p["w_q"], seg=MIX_STEPS)
    return x1, q, (f_re.transpose(1, 0, 2), f_im.transpose(1, 0, 2), conv_buf), later_bf16


def _prompt_ffn(x1, q, k, v, p, g_last, side):
    nseq, tlen, d = x1.shape
    rows = nseq * tlen
    return _ffn(x1.reshape(rows, d), q.reshape(rows, d), (k, v), p["w_xo"], p["norm_ffn"], p["w_up"],
                p["w_gate"], p["ffn_conv_w"], p["w_down"], g_last, None,
                nb=nseq, nt=tlen // FFN_ROWS, tm=FFN_ROWS, ts=1, side=side)


def _step_layer_mix(x_bm, prev, p, bands, d_row):
    nseq = x_bm.shape[0]
    G, N = prev[0].shape[1:]
    x1, q, conv_buf, f_re, f_im = _step_mix(x_bm, p, prev[2], prev[0].reshape(nseq, G * N),
                                            prev[1].reshape(nseq, G * N), bands, d_row)
    return x1, q, (f_re.reshape(nseq, G, N), f_im.reshape(nseq, G, N), conv_buf)


def _step_layer_ffn(x1, o_bm, prev_ffn, p, g_last):
    return _ffn(x1, o_bm, None, p["w_xo"], p["norm_ffn"], p["w_up"], p["w_gate"], p["ffn_conv_w"], p["w_down"],
                g_last, prev_ffn, nb=1, nt=1, tm=x1.shape[0], ts=o_bm.shape[0])


def kernel(x_prompt, x_sample, mem_prompt, cache_mem_k, cache_mem_v, state_ssm_re, state_ssm_im, state_conv, state_ffn_conv, norm_mix, w_in, ssm_A_re, ssm_A_im, ssm_log_dt, ssm_B_re, ssm_B_im, ssm_C_re, ssm_C_im, ssm_D, w_glu, b_glu, conv_w, norm_ssm_out, norm_conv_out, w_out, norm_xattn, norm_mem, w_q, w_k, w_v, w_xo, norm_ffn, w_up, w_gate, ffn_conv_w, w_down, norm_final):
    depth = w_in.shape[0]
    assert depth == 1, "the final norm is fused into the last (only) layer's ConvFFN kernel"
    nbp, tp, d = x_prompt.shape
    assert tp % FFN_ROWS == 0 and tp % MIX_STEPS == 0 and nbp == 8

    xp, xs = x_prompt, x_sample
    outs_p, outs_s, mk_p, mv_p = [], [], [], []
    for l in range(depth):
        m, s_cat, o_cat, apow, bband, cband, arow, *mixer_w = _ssm_prep(
            ssm_A_re[l], ssm_A_im[l], ssm_log_dt[l], ssm_B_re[l], ssm_B_im[l], ssm_C_re[l], ssm_C_im[l], ssm_D[l],
            [w_in[l], w_glu[l], w_out[l], w_q[l]], chunk=SSM_CHUNK)
        mats = (m, s_cat, o_cat)
        p = dict(norm_mix=norm_mix[l], b_glu=b_glu[l], conv_w=conv_w[l], norm_ssm_out=norm_ssm_out[l],
                 norm_conv_out=norm_conv_out[l], norm_xattn=norm_xattn[l], norm_ffn=norm_ffn[l],
                 ffn_conv_w=ffn_conv_w[l])
        p["w_in"], p["w_glu"], p["w_out"], p["w_q"] = mixer_w

        x1s, qs, st_s = _step_layer_mix(xs, (state_ssm_re[l], state_ssm_im[l], state_conv[l]), p,
                                        (bband, cband, arow), ssm_D[l].reshape(1, -1))
        x1p, qp, st_p, later = _prompt_mixer(xp, p, mats, apow,
                                             [w_xo[l], w_up[l], w_gate[l], w_down[l], w_k[l], w_v[l]])
        p["w_xo"], p["w_up"], p["w_gate"], p["w_down"], w_k_bf, w_v_bf = later

        k_p, v_p, kb_p, vb_p = _kv_proj(mem_prompt, norm_mem[l], w_k_bf, w_v_bf)
        mk_p.append(k_p)
        mv_p.append(v_p)

        side = (qs, cache_mem_k[l], cache_mem_v[l])
        xp, ffn_buf_p, os_bm = _prompt_ffn(x1p, qp, kb_p, vb_p, p, norm_final, side)
        outs_p.append(st_p + (ffn_buf_p,))
        xs, ffn_buf_s = _step_layer_ffn(x1s, os_bm, state_ffn_conv[l], p, norm_final)
        outs_s.append(st_s + (ffn_buf_s,))

    stack = lambda outs, i: jnp.stack([o[i] for o in outs])
    return (xp.reshape(nbp, tp, d), xs, jnp.stack(mk_p), jnp.stack(mv_p),
            stack(outs_p, 0), stack(outs_p, 1), stack(outs_p, 2), stack(outs_p, 3),
            stack(outs_s, 0), stack(outs_s, 1), stack(outs_s, 2), stack(outs_s, 3))
```

```python
import functools

import jax
import jax.numpy as jnp
from jax import lax
from jax.experimental import pallas as pl
from jax.experimental.pallas import tpu as pltpu

F32 = jnp.float32
BF16 = jnp.bfloat16

EPS = 1e-6
SSM_GROUP_CH = 16
SSM_STATE = 64
N_MEM = 256
XATTN_HEADS = 4
SSM_CHUNK = 16
LANES = 128
SUBLANES = 8
MIX_STEPS = 128
MIX_SUBBLOCKS = 2
FFN_ROWS = 512
V7X_VMEM_LIMIT = 56 * 1024 * 1024
V7X_VMEM_LIMIT_MAX = 62 * 1024 * 1024


def _rms(x, g):
    ms = jnp.mean(x * x, axis=-1, keepdims=True)
    return x * lax.rsqrt(ms + EPS) * g


def _dot(a, b):
    return jnp.dot(a, b, preferred_element_type=F32)


def _const_spec(shape):
    nd = len(shape)
    return pl.BlockSpec(shape, lambda *_: (0,) * nd, pipeline_mode=pl.Buffered(1))


def _params(*sem, vmem=V7X_VMEM_LIMIT):
    return pltpu.CompilerParams(dimension_semantics=sem, vmem_limit_bytes=vmem)


def _cast_slabs(srcs, dsts):
    for src, dst in zip(srcs, dsts):
        dst[...] = src[...].astype(dst.dtype)


def _ssm_prep_kernel(*refs, gb, chunk, ncast):
    ld_ref, lr_r_ref, li_r_ref, brt_ref, bit_ref, crt_t_ref, cit_t_ref, d_ref = refs[:8]
    m_ref, s_ref, o_ref, apow_ref, bband_ref, cband_ref, arow_ref = refs[8 + ncast:15 + ncast]
    _cast_slabs(refs[8:8 + ncast], refs[15 + ncast:])
    _ssm_prep_groups(ld_ref, lr_r_ref, li_r_ref, brt_ref, bit_ref, crt_t_ref, cit_t_ref, d_ref,
                     m_ref, s_ref, o_ref, apow_ref, bband_ref, cband_ref, arow_ref, gb=gb, chunk=chunk)


def _ssm_prep_groups(ld_ref, lr_r_ref, li_r_ref, brt_ref, bit_ref, crt_t_ref, cit_t_ref, d_ref,
                     m_ref, s_ref, o_ref, apow_ref, bband_ref, cband_ref, arow_ref, *, gb, chunk):
    P, N = SSM_GROUP_CH, SSM_STATE
    LP = chunk * P
    bband_ref[...] = jnp.zeros(bband_ref.shape, bband_ref.dtype)
    cband_ref[...] = jnp.zeros(cband_ref.shape, cband_ref.dtype)
    lane_j = lax.broadcasted_iota(jnp.int32, (1, LP), 1) // P
    row_j = lax.broadcasted_iota(jnp.int32, (LP, 1), 0) // P
    lane_i = lax.broadcasted_iota(jnp.int32, (P, LP), 1)
    row_i = lax.broadcasted_iota(jnp.int32, (P, LP), 0)
    nbits = chunk.bit_length()

    eye = lax.broadcasted_iota(jnp.int32, (N, N), 0) == lax.broadcasted_iota(jnp.int32, (N, N), 1)

    def as_column(row):
        return jnp.sum(jnp.where(eye, row, 0.0), axis=1, keepdims=True)

    def squarings(pr, pi):
        out = [(pr, pi)]
        for _ in range(nbits - 1):
            pr, pi = pr * pr - pi * pi, 2.0 * pr * pi
            out.append((pr, pi))
        return out

    def cpow(pows, j):
        er = ei = None
        for b, (pr, pi) in enumerate(pows):
            if isinstance(j, int):
                if not (j >> b) & 1:
                    continue
                er, ei = (pr, pi) if er is None else (er * pr - ei * pi, er * pi + ei * pr)
            else:
                bit = ((j >> b) & 1) == 1
                if er is None:
                    er, ei = jnp.where(bit, pr, 1.0), jnp.where(bit, pi, 0.0)
                else:
                    er, ei = jnp.where(bit, er * pr - ei * pi, er), jnp.where(bit, er * pi + ei * pr, ei)
        return er, ei

    for gi in range(gb):
        dt = jnp.exp(ld_ref[gi])
        lr_r, li_r = lr_r_ref[gi], li_r_ref[gi]
        mag = jnp.exp(dt * lr_r)
        ar, ai = mag * jnp.cos(dt * li_r), mag * jnp.sin(dt * li_r)
        pows_r = squarings(ar, ai)
        pows_c = squarings(as_column(ar), as_column(ai))

        den = lr_r * lr_r + li_r * li_r
        cr = ((ar - 1.0) * lr_r + ai * li_r) / den
        ci = (ai * lr_r - (ar - 1.0) * li_r) / den

        brt, bit = brt_ref[gi], bit_ref[gi]
        bbt_re = cr * brt - ci * bit
        bbt_im = cr * bit + ci * brt
        bbt_re_t = jnp.concatenate([bbt_re] * chunk, axis=0)
        bbt_im_t = jnp.concatenate([bbt_im] * chunk, axis=0)

        er, ei = cpow(pows_r, (chunk - 1) - row_j)
        s_ref[gi, :, 0:N] = er * bbt_re_t - ei * bbt_im_t
        s_ref[gi, :, N:2 * N] = er * bbt_im_t + ei * bbt_re_t

        crt_t, cit_t = crt_t_ref[gi], cit_t_ref[gi]
        er, ei = cpow(pows_c, lane_j)
        r_re = crt_t * er - cit_t * ei
        r_im = crt_t * ei + cit_t * er
        er, ei = er * pows_c[0][0] - ei * pows_c[0][1], er * pows_c[0][1] + ei * pows_c[0][0]
        o_ref[gi, 0:N, :] = (crt_t * er - cit_t * ei).astype(o_ref.dtype)
        o_ref[gi, N:2 * N, :] = (-(crt_t * ei + cit_t * er)).astype(o_ref.dtype)

        krow = (jnp.dot(bbt_re, r_re, preferred_element_type=F32, precision=lax.Precision.HIGHEST)
                - jnp.dot(bbt_im, r_im, preferred_element_type=F32, precision=lax.Precision.HIGHEST))
        krow = krow + jnp.where(lane_i == row_i, d_ref[gi], 0.0)
        for ti in range(chunk):
            blk = krow if ti == 0 else pltpu.roll(krow, ti * P, axis=1)
            blk = jnp.where(lane_i >= ti * P, blk, 0.0)
            m_ref[gi, ti * P:(ti + 1) * P, :] = blk.astype(m_ref.dtype)

        apow_ref[gi] = jnp.concatenate(cpow(pows_r, chunk), axis=0)

        rows, cols = slice(gi * P, (gi + 1) * P), slice(gi * N, (gi + 1) * N)
        cols_im = slice((gb + gi) * N, (gb + gi + 1) * N)
        bband_ref[rows, cols] = bbt_re
        bband_ref[rows, cols_im] = bbt_im
        cband_ref[cols, rows] = crt_t[:, rows]
        cband_ref[cols_im, rows] = -cit_t[:, rows]
        arow_ref[0:1, cols] = ar
        arow_ref[1:2, cols] = ai


def _ssm_prep(A_re, A_im, log_dt, B_re, B_im, C_re, C_im, D, to_bf16, *, chunk, gb=8):
    G, N = A_re.shape
    P = SSM_GROUP_CH
    LP = chunk * P
    brt = B_re.transpose(0, 2, 1)
    bit = B_im.transpose(0, 2, 1)
    crt_t = jnp.tile(C_re.transpose(0, 2, 1), (1, 1, chunk))
    cit_t = jnp.tile(C_im.transpose(0, 2, 1), (1, 1, chunk))
    ins = [log_dt.reshape(G, 1, 1), A_re.reshape(G, 1, N), A_im.reshape(G, 1, N), brt, bit, crt_t, cit_t,
           D.reshape(G, P, 1)]
    gspec = lambda *s: pl.BlockSpec((gb,) + s, lambda i: (i, 0, 0))
    in_specs = [gspec(*a.shape[1:]) for a in ins]
    out_shape = [jax.ShapeDtypeStruct((G, LP, LP), BF16), jax.ShapeDtypeStruct((G, LP, 2 * N), F32),
                 jax.ShapeDtypeStruct((G, 2 * N, LP), BF16), jax.ShapeDtypeStruct((G, 2, N), F32)]
    out_specs = [gspec(*o.shape[1:]) for o in out_shape]
    band_shape = [jax.ShapeDtypeStruct((G // gb, gb * P, 2 * gb * N), F32),
                  jax.ShapeDtypeStruct((G // gb, 2 * gb * N, gb * P), F32),
                  jax.ShapeDtypeStruct((G // gb, 2, gb * N), F32)]
    out_shape += band_shape
    out_specs += [pl.BlockSpec((None,) + o.shape[1:], lambda i: (i, 0, 0)) for o in band_shape]
    slab_specs = [pl.BlockSpec((w.shape[0] // (G // gb), w.shape[1]), lambda i: (i, 0)) for w in to_bf16]
    return pl.pallas_call(
        functools.partial(_ssm_prep_kernel, gb=gb, chunk=chunk, ncast=len(to_bf16)),
        grid=(G // gb,), in_specs=in_specs + slab_specs, out_specs=out_specs + slab_specs,
        out_shape=out_shape + [jax.ShapeDtypeStruct(w.shape, BF16) for w in to_bf16],
        compiler_params=_params("arbitrary"), name="ssm_prep")(*ins, *to_bf16)


def _step_rows(a):
    return jnp.transpose(a, (1, 0, 2)).reshape(a.shape[0] * a.shape[1], a.shape[2])


def _seq_major(a, nseq):
    return jnp.transpose(a.reshape(a.shape[0] // nseq, nseq, a.shape[1]), (1, 0, 2))


def _conv3_steps_apart(v, w_ref, s_ref, prev, *, ts):
    tm = v.shape[0]
    s_ref[0:2 * ts, :] = prev
    s_ref[2 * ts:2 * ts + tm, :] = v
    out = w_ref[0:1, :] * s_ref[0:tm, :] + w_ref[1:2, :] * s_ref[ts:ts + tm, :] + w_ref[2:3, :] * v
    return out, s_ref[tm:tm + 2 * ts, :]


def _conv3_rows(v, w_ref, c_ref):
    tm = v.shape[0]
    head = c_ref[...]
    row = lax.broadcasted_iota(jnp.int32, head.shape, 0)

    def back(k):
        r = pltpu.roll(v, k, axis=0)
        first = jnp.where(row < k, pltpu.roll(head, k, axis=0), r[0:SUBLANES])
        return jnp.concatenate([first, r[SUBLANES:]], axis=0)

    out = w_ref[0:1, :] * back(2) + w_ref[1:2, :] * back(1) + w_ref[2:3, :] * v
    c_ref[...] = v[tm - SUBLANES:tm]
    return out


def _conv3_state(c_ref):
    return c_ref[..., SUBLANES - 2:SUBLANES, :]


def _piece_transpose8(v):
    width = LANES // 8
    piece = lax.broadcasted_iota(jnp.int32, v[0].shape, 1) // width
    for bit in range(3):
        s = 1 << bit
        hi = ((piece >> bit) & 1) == 1
        nv = list(v)
        for i in range(8):
            if i & s:
                continue
            a, b = v[i], v[i + s]
            nv[i] = jnp.where(hi, pltpu.roll(b, s * width, axis=1), a)
            nv[i + s] = jnp.where(hi, b, pltpu.roll(a, LANES - s * width, axis=1))
        v = nv
    return v


def _tokens_to_groups(zu, u_ref, *, nseg, seg):
    nc = seg // SSM_CHUNK
    ms = zu.shape[1]
    z_tb = jnp.transpose(zu.reshape(nseg, seg, ms), (1, 0, 2))
    for h in range(SSM_CHUNK // 8):
        for q in range(ms // LANES):
            xs = [jnp.concatenate([z_tb[SSM_CHUNK * c + 8 * h + i, :, q * LANES:(q + 1) * LANES] for c in range(nc)],
                                  axis=0) for i in range(8)]
            w = _piece_transpose8(xs)
            for k in range(8):
                u_ref[8 * q + k, :, h * LANES:(h + 1) * LANES] = w[k].astype(u_ref.dtype)


def _groups_to_tokens(y_ref, *, nseg, seg):
    nc = seg // SSM_CHUNK
    nq = y_ref.shape[0] // 8
    tiles = [[None] * nq for _ in range(seg)]
    for h in range(SSM_CHUNK // 8):
        for q in range(nq):
            v = _piece_transpose8([y_ref[8 * q + k, :, h * LANES:(h + 1) * LANES].astype(F32) for k in range(8)])
            for i in range(8):
                for c in range(nc):
                    tiles[SSM_CHUNK * c + 8 * h + i][q] = v[i][c * nseg:(c + 1) * nseg]
    y_tb = jnp.stack([jnp.concatenate(row, axis=1) for row in tiles], axis=0)
    return jnp.transpose(y_tb, (1, 0, 2)).reshape(nseg * seg, nq * LANES)


def _mix_in_core(x, g_ref, w_ref, gc_ref, conv, *, ms, mc):
    hb = _rms(x, g_ref[...]).astype(BF16)
    zu = _dot(hb, w_ref[:, 0:ms])
    xin = _dot(hb, w_ref[:, ms:ms + mc])
    cg = _dot(hb, w_ref[:, ms + 2 * mc:ms + 3 * mc])
    cv = conv(cg * xin)
    bg = _dot(hb, w_ref[:, ms + mc:ms + 2 * mc])
    return zu, _rms(bg * cv, gc_ref[...]).astype(BF16)


def _mix_in_kernel(*refs, nseg, seg, nsub, ncast, ms, mc):
    x_ref, g_ref, w_ref, cw_ref, gc_ref = refs[:5]
    cast_in, (u_ref, yc_ref, buf_ref) = refs[5:5 + ncast], refs[5 + ncast:8 + ncast]
    cast_out, s_ref = refs[8 + ncast:8 + 2 * ncast], refs[8 + 2 * ncast]
    _cast_slabs(cast_in, cast_out)

    sub = seg // nsub
    rg = sub // SSM_CHUNK * nseg

    @pl.when(pl.program_id(0) == 0)
    def _():
        s_ref[...] = jnp.zeros(s_ref.shape, F32)

    def conv(v):
        return jnp.concatenate([_conv3_rows(v[b * sub:(b + 1) * sub], cw_ref, s_ref.at[b]) for b in range(nseg)],
                               axis=0)

    for h in range(nsub):
        steps = slice(h * sub, (h + 1) * sub)
        zu, ycn = _mix_in_core(x_ref[:, steps, :].reshape(nseg * sub, x_ref.shape[-1]), g_ref, w_ref, gc_ref, conv,
                               ms=ms, mc=mc)
        _tokens_to_groups(zu, u_ref.at[:, pl.ds(h * rg, rg), :], nseg=nseg, seg=sub)
        yc_ref[:, steps, :] = ycn.reshape(nseg, sub, mc)
    buf_ref[...] = _conv3_state(s_ref)


def _mix_in(x3d, g, w_bf, conv_w, g_conv, to_bf16, *, seg, ms, mc):
    nseg, tlen, d = x3d.shape
    G = ms // SSM_GROUP_CH
    lp = SSM_CHUNK * SSM_GROUP_CH
    nsteps = tlen // seg
    row_spec = lambda c: pl.BlockSpec((nseg, seg, c), lambda t: (0, t, 0))
    slab_specs = [pl.BlockSpec((w.shape[0] // nsteps, w.shape[1]), lambda t: (t, 0)) for w in to_bf16]
    in_specs = [row_spec(d), _const_spec((1, d)), _const_spec(w_bf.shape), _const_spec(conv_w.shape),
                _const_spec((1, mc))] + slab_specs
    return pl.pallas_call(
        functools.partial(_mix_in_kernel, nseg=nseg, seg=seg, nsub=MIX_SUBBLOCKS, ncast=len(to_bf16), ms=ms, mc=mc),
        grid=(nsteps,), in_specs=in_specs,
        out_specs=[pl.BlockSpec((G, seg // SSM_CHUNK * nseg, lp), lambda t: (0, t, 0)), row_spec(mc),
                   pl.BlockSpec((nseg, 2, mc), lambda t: (0, 0, 0))] + slab_specs,
        out_shape=[jax.ShapeDtypeStruct((G, tlen // SSM_CHUNK * nseg, lp), BF16),
                   jax.ShapeDtypeStruct((nseg, tlen, mc), BF16), jax.ShapeDtypeStruct((nseg, 2, mc), F32)]
        + [jax.ShapeDtypeStruct(w.shape, BF16) for w in to_bf16],
        scratch_shapes=[pltpu.VMEM((nseg, SUBLANES, mc), F32)],
        compiler_params=_params("arbitrary"), name="mix_in")(x3d, g.reshape(1, d), w_bf, conv_w,
                                                              g_conv.reshape(1, mc), *to_bf16)


def _ssm_kernel(u_ref, m_ref, s_ref, o_ref, ap_ref, y_ref, fr_ref, fi_ref, lre, lim, ire, iim, *, gb, nb, nchunk):
    n = SSM_STATE
    for gi in range(gb):
        loc = _dot(u_ref[gi], s_ref[gi].astype(BF16))
        lre[gi] = loc[:, 0:n]
        lim[gi] = loc[:, n:2 * n]
    ar = [jnp.broadcast_to(ap_ref[gi, 0:1, :], (nb, n)) for gi in range(gb)]
    ai = [jnp.broadcast_to(ap_ref[gi, 1:2, :], (nb, n)) for gi in range(gb)]
    st0 = tuple((jnp.zeros((nb, n), F32), jnp.zeros((nb, n), F32)) for _ in range(gb))

    def step(c, st):
        r = pl.multiple_of(c * nb, nb)
        new = []
        for gi in range(gb):
            sr, si = st[gi]
            ire[gi, pl.ds(r, nb), :] = sr
            iim[gi, pl.ds(r, nb), :] = si
            nr = ar[gi] * sr - ai[gi] * si + lre[gi, pl.ds(r, nb), :]
            ni = ar[gi] * si + ai[gi] * sr + lim[gi, pl.ds(r, nb), :]
            new.append((nr, ni))
        return tuple(new)

    st = lax.fori_loop(0, nchunk, step, st0, unroll=4)
    for gi in range(gb):
        fr_ref[gi] = st[gi][0]
        fi_ref[gi] = st[gi][1]
        ini = jnp.concatenate([ire[gi], iim[gi]], axis=1).astype(BF16)
        y = _dot(u_ref[gi], m_ref[gi]) + _dot(ini, o_ref[gi])
        y_ref[gi] = y.astype(y_ref.dtype)


def _ssm(u_g, mats, apow, *, nb, nchunk, gb=4):
    G, R, LP = u_g.shape
    n = SSM_STATE
    ins = [u_g, *mats, apow]
    gspec = lambda *s: pl.BlockSpec((gb,) + s, lambda i: (i, 0, 0))
    out_shape = [jax.ShapeDtypeStruct((G, R, LP), F32), jax.ShapeDtypeStruct((G, nb, n), F32),
                 jax.ShapeDtypeStruct((G, nb, n), F32)]
    return pl.pallas_call(
        functools.partial(_ssm_kernel, gb=gb, nb=nb, nchunk=nchunk),
        grid=(G // gb,), in_specs=[gspec(*a.shape[1:]) for a in ins],
        out_specs=[gspec(*o.shape[1:]) for o in out_shape],
        out_shape=out_shape, scratch_shapes=[pltpu.VMEM((gb, R, n), F32)] * 4,
        compiler_params=_params("arbitrary"), name="ssm_chunk")(*ins)


def _mix_out_core(x, ys, yc, wg_ref, bg_ref, gs_ref, wo_ref, gx_ref, wq_ref, *, ms, q_scale):
    y = jax.nn.gelu(ys)
    y = y * jax.nn.sigmoid(_dot(y.astype(BF16), wg_ref[...]) + bg_ref[...])
    ysn = _rms(y, gs_ref[...]).astype(BF16)
    x1 = x + _dot(jnp.concatenate([ysn, yc], axis=1), wo_ref[...])
    h = _rms(x1, gx_ref[...]).astype(BF16)
    return x1, _dot(h, wq_ref[...]) * q_scale


def _memory_kv(m_ref, g_ref, wk_ref, wv_ref, k_ref, v_ref, kb_ref, vb_ref):
    h = _rms(m_ref[0], g_ref[...]).astype(BF16)
    hd = k_ref.shape[-1]
    for w_ref, out_ref, bf_ref in ((wk_ref, k_ref, kb_ref), (wv_ref, v_ref, vb_ref)):
        kv = _dot(h, w_ref[...])
        heads = jnp.stack([kv[:, i * hd:(i + 1) * hd] for i in range(XATTN_HEADS)], axis=0)
        out_ref[0] = jnp.transpose(heads, (1, 0, 2))
        bf_ref[0] = heads.astype(BF16)


def _mix_out_kernel(x_ref, ys_ref, yc_ref, wg_ref, bg_ref, gs_ref, wo_ref, gx_ref, wq_ref, mem_ref, gm_ref, wk_ref,
                    wv_ref, x1_ref, q_ref, k_ref, v_ref, kb_ref, vb_ref, *, nseg, seg, nsub, ms, q_scale):
    _memory_kv(mem_ref, gm_ref, wk_ref, wv_ref, k_ref, v_ref, kb_ref, vb_ref)
    sub = seg // nsub
    rg = sub // SSM_CHUNK * nseg
    d = x_ref.shape[-1]
    for h in range(nsub):
        steps = slice(h * sub, (h + 1) * sub)
        ys = _groups_to_tokens(ys_ref.at[:, pl.ds(h * rg, rg), :], nseg=nseg, seg=sub)
        x1, q = _mix_out_core(x_ref[:, steps, :].reshape(nseg * sub, d),
                              ys, yc_ref[:, steps, :].reshape(nseg * sub, yc_ref.shape[-1]),
                              wg_ref, bg_ref, gs_ref, wo_ref, gx_ref, wq_ref, ms=ms, q_scale=q_scale)
        x1_ref[:, steps, :] = x1.reshape(nseg, sub, d)
        q_ref[:, steps, :] = q.astype(q_ref.dtype).reshape(nseg, sub, d)


def _mix_out(x3d, ys, yc, w_glu, b_glu, g_ssm, w_out, g_x, w_q, mem, g_mem, w_k, w_v, *, seg):
    nseg, tlen, d = x3d.shape
    ms = w_glu.shape[0]
    hd = d // XATTN_HEADS
    nsteps = tlen // seg
    nmem, nm, _ = mem.shape
    per_seq = nsteps // nmem
    mt = nm // per_seq
    assert per_seq * nmem == nsteps and mt * per_seq == nm
    row_spec = lambda c: pl.BlockSpec((nseg, seg, c), lambda t: (0, t, 0))
    ys_spec = pl.BlockSpec((ys.shape[0], seg // SSM_CHUNK * nseg, ys.shape[2]), lambda t: (0, t, 0))
    in_specs = [row_spec(d), ys_spec, row_spec(yc.shape[-1]), _const_spec(w_glu.shape), _const_spec((1, ms)),
                _const_spec((1, ms)), _const_spec(w_out.shape), _const_spec((1, d)), _const_spec(w_q.shape),
                pl.BlockSpec((1, mt, d), lambda t: (t // per_seq, t % per_seq, 0)), _const_spec((1, d)),
                _const_spec(w_k.shape), _const_spec(w_v.shape)]
    kv_spec = pl.BlockSpec((1, mt, XATTN_HEADS, hd), lambda t: (t // per_seq, t % per_seq, 0, 0))
    kvb_spec = pl.BlockSpec((1, XATTN_HEADS, mt, hd), lambda t: (t // per_seq, 0, t % per_seq, 0))
    return pl.pallas_call(
        functools.partial(_mix_out_kernel, nseg=nseg, seg=seg, nsub=MIX_SUBBLOCKS, ms=ms, q_scale=hd ** -0.5),
        grid=(nsteps,), in_specs=in_specs, out_specs=[row_spec(d), row_spec(d), kv_spec, kv_spec, kvb_spec, kvb_spec],
        out_shape=[jax.ShapeDtypeStruct((nseg, tlen, d), F32), jax.ShapeDtypeStruct((nseg, tlen, d), BF16)]
        + [jax.ShapeDtypeStruct((nmem, nm, XATTN_HEADS, hd), F32)] * 2
        + [jax.ShapeDtypeStruct((nmem, XATTN_HEADS, nm, hd), BF16)] * 2,
        compiler_params=_params("parallel"), name="mix_out")(
            x3d, ys, yc, w_glu, b_glu.reshape(1, ms), g_ssm.reshape(1, ms), w_out, g_x.reshape(1, d), w_q,
            mem, g_mem.reshape(1, d), w_k, w_v)


def _step_mix_kernel(x_ref, g_ref, w_ref, cw_ref, gc_ref, prev_ref, s0r_ref, s0i_ref, bband_ref, cband_ref, arow_ref,
                     d_ref, wg_ref, bg_ref, gs_ref, wo_ref, gx_ref, wq_ref,
                     x1_ref, q_ref, buf_ref, fr_ref, fi_ref, s_ref, ys_ref, *, nseq, tlen, ms, mc, q_scale):
    x = _step_rows(x_ref[...])

    def conv(v):
        out, buf = _conv3_steps_apart(v, cw_ref, s_ref, _step_rows(prev_ref[...]), ts=nseq)
        buf_ref[...] = _seq_major(buf, nseq)
        return out

    zu, ycn = _mix_in_core(x, g_ref, w_ref, gc_ref, conv, ms=ms, mc=mc)

    wu, ws = bband_ref.shape[1], arow_ref.shape[2]
    for i in range(bband_ref.shape[0]):
        ucols, scols = slice(i * wu, (i + 1) * wu), slice(i * ws, (i + 1) * ws)
        sr, si = s0r_ref[:, scols], s0i_ref[:, scols]
        ar, ai = arow_ref[i, 0:1, :], arow_ref[i, 1:2, :]
        bband, cband = bband_ref[i].astype(BF16), cband_ref[i].astype(BF16)
        for t in range(tlen):
            ut = zu[t * nseq:(t + 1) * nseq, ucols]
            bu = _dot(ut.astype(BF16), bband)
            sr, si = ar * sr - ai * si + bu[:, 0:ws], ar * si + ai * sr + bu[:, ws:]
            cs = _dot(jnp.concatenate([sr, si], axis=1).astype(BF16), cband)
            ys_ref[t * nseq:(t + 1) * nseq, ucols] = cs + d_ref[:, ucols] * ut
        fr_ref[:, scols] = sr
        fi_ref[:, scols] = si

    x1, q = _mix_out_core(x, ys_ref[...], ycn, wg_ref, bg_ref, gs_ref, wo_ref, gx_ref, wq_ref, ms=ms, q_scale=q_scale)
    x1_ref[...] = x1
    q_ref[...] = _seq_major(q, nseq)


def _step_mix(x3d, p, conv_prev, s0_re, s0_im, bands, d_row):
    nseq, tlen, d = x3d.shape
    rows = nseq * tlen
    ms = p["w_glu"].shape[0]
    mc = p["conv_w"].shape[1]
    hd = d // XATTN_HEADS
    bband, cband, arow = bands
    ins = [x3d, p["norm_mix"].reshape(1, d), p["w_in"], p["conv_w"], p["norm_conv_out"].reshape(1, mc), conv_prev,
           s0_re, s0_im, bband, cband, arow, d_row, p["w_glu"], p["b_glu"].reshape(1, ms),
           p["norm_ssm_out"].reshape(1, ms), p["w_out"], p["norm_xattn"].reshape(1, d), p["w_q"]]
    full = lambda shape: pl.BlockSpec(shape, lambda i: (0,) * len(shape))
    out_shape = [jax.ShapeDtypeStruct((rows, d), F32), jax.ShapeDtypeStruct((nseq, tlen, d), F32),
                 jax.ShapeDtypeStruct(conv_prev.shape, F32), jax.ShapeDtypeStruct(s0_re.shape, F32),
                 jax.ShapeDtypeStruct(s0_im.shape, F32)]
    return pl.pallas_call(
        functools.partial(_step_mix_kernel, nseq=nseq, tlen=tlen, ms=ms, mc=mc, q_scale=hd ** -0.5),
        grid=(1,), in_specs=[_const_spec(a.shape) for a in ins], out_specs=[full(o.shape) for o in out_shape],
        out_shape=out_shape,
        scratch_shapes=[pltpu.VMEM((2 * nseq + rows, mc), F32), pltpu.VMEM((rows, ms), F32)],
        compiler_params=_params("arbitrary"), name="step_mix")(*ins)


def _softmax_attention(qs, ks, vs):
    tq = qs[0].shape[0]
    s = jnp.concatenate([lax.dot_general(q, k, (((1,), (1,)), ((), ())), preferred_element_type=F32)
                         for q, k in zip(qs, ks)], axis=0)
    p = jnp.exp(s - jnp.max(s, axis=-1, keepdims=True))
    p = p / jnp.sum(p, axis=-1, keepdims=True)
    return [_dot(p[i * tq:(i + 1) * tq].astype(BF16), v) for i, v in enumerate(vs)]


def _head_slices(ref, rows, hd):
    return [ref[rows, h * hd:(h + 1) * hd].astype(BF16) for h in range(XATTN_HEADS)]


def _kv_heads(ref, j):
    hm = jnp.transpose(ref[j], (1, 0, 2))
    return [hm[h].astype(BF16) for h in range(XATTN_HEADS)]


def _attend_sequences(q_ref, k_ref, v_ref, o_ref):
    sb, _, d = q_ref.shape
    hd = d // XATTN_HEADS
    qs, ks, vs = [], [], []
    for j in range(sb):
        qs += _head_slices(q_ref.at[j], slice(None), hd)
        ks += _kv_heads(k_ref, j)
        vs += _kv_heads(v_ref, j)
    outs = _softmax_attention(qs, ks, vs)
    for j in range(sb):
        o_ref[j] = jnp.concatenate(outs[j * XATTN_HEADS:(j + 1) * XATTN_HEADS], axis=1).astype(o_ref.dtype)


def _ffn_kernel(*refs, ts, has_prev, has_kv, side):
    x_ref, o_ref = refs[:2]
    refs = refs[2:]
    if has_kv:
        k_ref, v_ref = refs[:2]
        refs = refs[2:]
    if side:
        sq_ref, sk_ref, sv_ref = refs[:3]
        refs = refs[3:]
    wxo_ref, gf_ref, wup_ref, wgate_ref, cw_ref, wdn_ref, gl_ref = refs[:7]
    refs = refs[7:]
    if has_prev:
        prev_ref, refs = refs[0], refs[1:]
    if side:
        y_ref, buf_ref, so_ref, s_ref = refs
    else:
        y_ref, buf_ref, s_ref = refs
    if not has_prev:
        @pl.when(pl.program_id(1) == 0)
        def _():
            s_ref[...] = jnp.zeros(s_ref.shape, F32)

    if has_kv:
        hd = o_ref.shape[1] // XATTN_HEADS
        rows = slice(None)
        heads = _softmax_attention(_head_slices(o_ref, rows, hd), [k_ref[0, h] for h in range(XATTN_HEADS)],
                                   [v_ref[0, h] for h in range(XATTN_HEADS)])
        o = jnp.concatenate(heads, axis=1).astype(BF16)
    elif has_prev:
        o = _step_rows(o_ref[...]).astype(BF16)
    else:
        o = o_ref[...].astype(BF16)
    x2 = x_ref[...] + _dot(o, wxo_ref[...])
    hb = _rms(x2, gf_ref[...]).astype(BF16)
    up = _dot(hb, wup_ref[...])
    if has_prev:
        a, buf = _conv3_steps_apart(up, cw_ref, s_ref, _step_rows(prev_ref[...]), ts=ts)
        buf_ref[...] = _seq_major(buf, ts)
    else:
        a = _conv3_rows(up, cw_ref, s_ref)
        buf_ref[...] = _conv3_state(s_ref)
    act = (jax.nn.gelu(a) * _dot(hb, wgate_ref[...])).astype(BF16)
    y = _rms(x2 + _dot(act, wdn_ref[...]), gl_ref[...])
    y_ref[...] = _seq_major(y, ts) if has_prev else y
    if side:
        _attend_sequences(sq_ref, sk_ref, sv_ref, so_ref)


def _ffn(x2d, o2d, kv, w_xo, g_ffn, w_up, w_gate, conv_w, w_down, g_last, prev, *, nb, nt, tm, ts, side=None):
    rows, d = x2d.shape
    dff = w_up.shape[1]
    has_prev = prev is not None
    row_spec = pl.BlockSpec((tm, d), lambda b, t: (b * nt + t, 0))
    in_specs = [row_spec, _const_spec(o2d.shape) if has_prev else row_spec]
    ins = [x2d, o2d]
    if kv is not None:
        in_specs += [pl.BlockSpec((1,) + kv[0].shape[1:], lambda b, t: (b, 0, 0, 0))] * 2
        ins += list(kv)
    if side is not None:
        sq, sk, sv = side
        sb = sq.shape[0] // (nb * nt)
        assert sb * nb * nt == sq.shape[0]
        side_q_spec = pl.BlockSpec((sb,) + sq.shape[1:], lambda b, t: (b * nt + t, 0, 0))
        in_specs += [side_q_spec] + [pl.BlockSpec((sb,) + sk.shape[1:], lambda b, t: (b * nt + t, 0, 0, 0))] * 2
        ins += [sq, sk, sv]
    in_specs += [_const_spec(w_xo.shape), _const_spec((1, d)), _const_spec(w_up.shape), _const_spec(w_gate.shape),
                 _const_spec(conv_w.shape), _const_spec(w_down.shape), _const_spec((1, d))]
    ins += [w_xo, g_ffn.reshape(1, d), w_up, w_gate, conv_w, w_down, g_last.reshape(1, d)]
    if has_prev:
        in_specs.append(_const_spec(prev.shape))
        ins.append(prev)
        out_shape = [jax.ShapeDtypeStruct(o2d.shape, F32), jax.ShapeDtypeStruct(prev.shape, F32)]
        out_specs = [pl.BlockSpec(o.shape, lambda b, t: (0, 0, 0)) for o in out_shape]
    else:
        out_shape = [jax.ShapeDtypeStruct((rows, d), F32), jax.ShapeDtypeStruct((nb, 2, dff), F32)]
        out_specs = [row_spec, pl.BlockSpec((None, 2, dff), lambda b, t: (b, 0, 0))]
    if side is not None:
        out_specs.append(side_q_spec)
        out_shape.append(jax.ShapeDtypeStruct(sq.shape, F32))
    return pl.pallas_call(
        functools.partial(_ffn_kernel, ts=ts, has_prev=has_prev, has_kv=kv is not None, side=side is not None),
        grid=(nb, nt), in_specs=in_specs, out_specs=out_specs, out_shape=out_shape,
        scratch_shapes=[pltpu.VMEM((2 * ts + tm if has_prev else SUBLANES, dff), F32)],
        compiler_params=_params("parallel", "arbitrary", vmem=V7X_VMEM_LIMIT if side is None else V7X_VMEM_LIMIT_MAX),
        name="conv_ffn")(*ins)


def _prompt_mixer(x3d, mem, p, mats, apow, later_weights):
    nseq, tlen, d = x3d.shape
    ms = p["w_glu"].shape[0]
    mc = p["conv_w"].shape[1]
    u, ycn, conv_buf, *later_bf16 = _mix_in(x3d, p["norm_mix"], p["w_in"], p["conv_w"], p["norm_conv_out"],
                                            later_weights, seg=MIX_STEPS, ms=ms, mc=mc)
    y_g, f_re, f_im = _ssm(u, mats, apow, nb=nseq, nchunk=tlen // SSM_CHUNK)
    *later_bf16, w_k, w_v = later_bf16
    x1, q, *kv = _mix_out(x3d, y_g, ycn, p["w_glu"], p["b_glu"], p["norm_ssm_out"], p["w_out"], p["norm_xattn"],
                          p["w_q"], mem, p["norm_mem"], w_k, w_v, seg=MIX_STEPS)
    return x1, q, (f_re.transpose(1, 0, 2), f_im.transpose(1, 0, 2), conv_buf), kv, later_bf16


def _prompt_ffn(x1, q, k, v, p, g_last, side):
    nseq, tlen, d = x1.shape
    rows = nseq * tlen
    return _ffn(x1.reshape(rows, d), q.reshape(rows, d), (k, v), p["w_xo"], p["norm_ffn"], p["w_up"],
                p["w_gate"], p["ffn_conv_w"], p["w_down"], g_last, None,
                nb=nseq, nt=tlen // FFN_ROWS, tm=FFN_ROWS, ts=1, side=side)


def _step_layer_mix(x_bm, prev, p, bands, d_row):
    nseq = x_bm.shape[0]
    G, N = prev[0].shape[1:]
    x1, q, conv_buf, f_re, f_im = _step_mix(x_bm, p, prev[2], prev[0].reshape(nseq, G * N),
                                            prev[1].reshape(nseq, G * N), bands, d_row)
    return x1, q, (f_re.reshape(nseq, G, N), f_im.reshape(nseq, G, N), conv_buf)


def _step_layer_ffn(x1, o_bm, prev_ffn, p, g_last):
    return _ffn(x1, o_bm, None, p["w_xo"], p["norm_ffn"], p["w_up"], p["w_gate"], p["ffn_conv_w"], p["w_down"],
                g_last, prev_ffn, nb=1, nt=1, tm=x1.shape[0], ts=o_bm.shape[0])


def kernel(x_prompt, x_sample, mem_prompt, cache_mem_k, cache_mem_v, state_ssm_re, state_ssm_im, state_conv, state_ffn_conv, norm_mix, w_in, ssm_A_re, ssm_A_im, ssm_log_dt, ssm_B_re, ssm_B_im, ssm_C_re, ssm_C_im, ssm_D, w_glu, b_glu, conv_w, norm_ssm_out, norm_conv_out, w_out, norm_xattn, norm_mem, w_q, w_k, w_v, w_xo, norm_ffn, w_up, w_gate, ffn_conv_w, w_down, norm_final):
    depth = w_in.shape[0]
    assert depth == 1, "the final norm is fused into the last (only) layer's ConvFFN kernel"
    nbp, tp, d = x_prompt.shape
    assert tp % FFN_ROWS == 0 and tp % MIX_STEPS == 0 and nbp == 8

    xp, xs = x_prompt, x_sample
    outs_p, outs_s, mk_p, mv_p = [], [], [], []
    for l in range(depth):
        m, s_cat, o_cat, apow, bband, cband, arow, *mixer_w = _ssm_prep(
            ssm_A_re[l], ssm_A_im[l], ssm_log_dt[l], ssm_B_re[l], ssm_B_im[l], ssm_C_re[l], ssm_C_im[l], ssm_D[l],
            [w_in[l], w_glu[l], w_out[l], w_q[l]], chunk=SSM_CHUNK)
        mats = (m, s_cat, o_cat)
        p = dict(norm_mix=norm_mix[l], b_glu=b_glu[l], conv_w=conv_w[l], norm_ssm_out=norm_ssm_out[l],
                 norm_conv_out=norm_conv_out[l], norm_xattn=norm_xattn[l], norm_ffn=norm_ffn[l],
                 ffn_conv_w=ffn_conv_w[l], norm_mem=norm_mem[l])
        p["w_in"], p["w_glu"], p["w_out"], p["w_q"] = mixer_w

        x1s, qs, st_s = _step_layer_mix(xs, (state_ssm_re[l], state_ssm_im[l], state_conv[l]), p,
                                        (bband, cband, arow), ssm_D[l].reshape(1, -1))
        x1p, qp, st_p, (k_p, v_p, kb_p, vb_p), later = _prompt_mixer(
            xp, mem_prompt, p, mats, apow, [w_xo[l], w_up[l], w_gate[l], w_down[l], w_k[l], w_v[l]])
        p["w_xo"], p["w_up"], p["w_gate"], p["w_down"] = later
        mk_p.append(k_p)
        mv_p.append(v_p)

        side = (qs, cache_mem_k[l], cache_mem_v[l])
        xp, ffn_buf_p, os_bm = _prompt_ffn(x1p, qp, kb_p, vb_p, p, norm_final, side)
        outs_p.append(st_p + (ffn_buf_p,))
        xs, ffn_buf_s = _step_layer_ffn(x1s, os_bm, state_ffn_conv[l], p, norm_final)
        outs_s.append(st_s + (ffn_buf_s,))

    stack = lambda outs, i: jnp.stack([o[i] for o in outs])
    return (xp.reshape(nbp, tp, d), xs, jnp.stack(mk_p), jnp.stack(mv_p),
            stack(outs_p, 0), stack(outs_p, 1), stack(outs_p, 2), stack(outs_p, 3),
            stack(outs_s, 0), stack(outs_s, 1), stack(outs_s, 2), stack(outs_s, 3))
```

```python
import functools

import jax
import jax.numpy as jnp
from jax import lax
from jax.experimental import pallas as pl
from jax.experimental.pallas import tpu as pltpu

F32 = jnp.float32
BF16 = jnp.bfloat16

EPS = 1e-6
SSM_GROUP_CH = 16
SSM_STATE = 64
N_MEM = 256
XATTN_HEADS = 4
SSM_CHUNK = 16
LANES = 128
SUBLANES = 8
MIX_STEPS = 128
MIX_SUBBLOCKS = 2
FFN_ROWS = 512
V7X_VMEM_LIMIT = 56 * 1024 * 1024
V7X_VMEM_LIMIT_MAX = 62 * 1024 * 1024


def _rms(x, g):
    ms = jnp.mean(x * x, axis=-1, keepdims=True)
    return x * lax.rsqrt(ms + EPS) * g


def _dot(a, b):
    return jnp.dot(a, b, preferred_element_type=F32)


def _const_spec(shape):
    nd = len(shape)
    return pl.BlockSpec(shape, lambda *_: (0,) * nd, pipeline_mode=pl.Buffered(1))


def _params(*sem, vmem=V7X_VMEM_LIMIT):
    return pltpu.CompilerParams(dimension_semantics=sem, vmem_limit_bytes=vmem)


def _cast_slabs(srcs, dsts):
    for src, dst in zip(srcs, dsts):
        dst[...] = src[...].astype(dst.dtype)


def _ssm_prep_kernel(*refs, gb, chunk, ncast):
    ld_ref, lr_r_ref, li_r_ref, brt_ref, bit_ref, crt_ref, cit_ref, d_ref = refs[:8]
    m_ref, s_ref, o_ref, apow_ref, bband_ref, cband_ref, arow_ref = refs[8 + ncast:15 + ncast]
    _cast_slabs(refs[8:8 + ncast], refs[15 + ncast:])
    _ssm_prep_groups(ld_ref, lr_r_ref, li_r_ref, brt_ref, bit_ref, crt_ref, cit_ref, d_ref,
                     m_ref, s_ref, o_ref, apow_ref, bband_ref, cband_ref, arow_ref, gb=gb, chunk=chunk)


def _ssm_prep_groups(ld_ref, lr_r_ref, li_r_ref, brt_ref, bit_ref, crt_ref, cit_ref, d_ref,
                     m_ref, s_ref, o_ref, apow_ref, bband_ref, cband_ref, arow_ref, *, gb, chunk):
    P, N = SSM_GROUP_CH, SSM_STATE
    LP = chunk * P
    bband_ref[...] = jnp.zeros(bband_ref.shape, bband_ref.dtype)
    cband_ref[...] = jnp.zeros(cband_ref.shape, cband_ref.dtype)
    lane_j = lax.broadcasted_iota(jnp.int32, (1, LP), 1) // P
    row_j = lax.broadcasted_iota(jnp.int32, (LP, 1), 0) // P
    lane_i = lax.broadcasted_iota(jnp.int32, (P, LP), 1)
    row_i = lax.broadcasted_iota(jnp.int32, (P, LP), 0)
    nbits = chunk.bit_length()

    eye = lax.broadcasted_iota(jnp.int32, (N, N), 0) == lax.broadcasted_iota(jnp.int32, (N, N), 1)
    repeat = jnp.where(lane_i % P == row_i, 1.0, 0.0)

    def tile_lanes(x):
        return jnp.dot(x, repeat, preferred_element_type=F32, precision=lax.Precision.HIGHEST)

    def as_column(row):
        return jnp.sum(jnp.where(eye, row, 0.0), axis=1, keepdims=True)

    def squarings(pr, pi):
        out = [(pr, pi)]
        for _ in range(nbits - 1):
            pr, pi = pr * pr - pi * pi, 2.0 * pr * pi
            out.append((pr, pi))
        return out

    def cpow(pows, j):
        er = ei = None
        for b, (pr, pi) in enumerate(pows):
            if isinstance(j, int):
                if not (j >> b) & 1:
                    continue
                er, ei = (pr, pi) if er is None else (er * pr - ei * pi, er * pi + ei * pr)
            else:
                bit = ((j >> b) & 1) == 1
                if er is None:
                    er, ei = jnp.where(bit, pr, 1.0), jnp.where(bit, pi, 0.0)
                else:
                    er, ei = jnp.where(bit, er * pr - ei * pi, er), jnp.where(bit, er * pi + ei * pr, ei)
        return er, ei

    for gi in range(gb):
        dt = jnp.exp(ld_ref[gi])
        lr_r, li_r = lr_r_ref[gi], li_r_ref[gi]
        mag = jnp.exp(dt * lr_r)
        ar, ai = mag * jnp.cos(dt * li_r), mag * jnp.sin(dt * li_r)
        pows_r = squarings(ar, ai)
        pows_c = squarings(as_column(ar), as_column(ai))

        den = lr_r * lr_r + li_r * li_r
        cr = ((ar - 1.0) * lr_r + ai * li_r) / den
        ci = (ai * lr_r - (ar - 1.0) * li_r) / den

        brt, bit = brt_ref[gi], bit_ref[gi]
        bbt_re = cr * brt - ci * bit
        bbt_im = cr * bit + ci * brt
        bbt_re_t = jnp.concatenate([bbt_re] * chunk, axis=0)
        bbt_im_t = jnp.concatenate([bbt_im] * chunk, axis=0)

        er, ei = cpow(pows_r, (chunk - 1) - row_j)
        s_ref[gi, :, 0:N] = er * bbt_re_t - ei * bbt_im_t
        s_ref[gi, :, N:2 * N] = er * bbt_im_t + ei * bbt_re_t

        crt_t, cit_t = tile_lanes(crt_ref[gi]), tile_lanes(cit_ref[gi])
        er, ei = cpow(pows_c, lane_j)
        r_re = crt_t * er - cit_t * ei
        r_im = crt_t * ei + cit_t * er
        er, ei = er * pows_c[0][0] - ei * pows_c[0][1], er * pows_c[0][1] + ei * pows_c[0][0]
        o_ref[gi, 0:N, :] = (crt_t * er - cit_t * ei).astype(o_ref.dtype)
        o_ref[gi, N:2 * N, :] = (-(crt_t * ei + cit_t * er)).astype(o_ref.dtype)

        krow = (jnp.dot(bbt_re, r_re, preferred_element_type=F32, precision=lax.Precision.HIGHEST)
                - jnp.dot(bbt_im, r_im, preferred_element_type=F32, precision=lax.Precision.HIGHEST))
        krow = krow + jnp.where(lane_i == row_i, d_ref[gi], 0.0)
        for ti in range(chunk):
            blk = krow if ti == 0 else pltpu.roll(krow, ti * P, axis=1)
            blk = jnp.where(lane_i >= ti * P, blk, 0.0)
            m_ref[gi, ti * P:(ti + 1) * P, :] = blk.astype(m_ref.dtype)

        apow_ref[gi] = jnp.concatenate(cpow(pows_r, chunk), axis=0)

        rows, cols = slice(gi * P, (gi + 1) * P), slice(gi * N, (gi + 1) * N)
        cols_im = slice((gb + gi) * N, (gb + gi + 1) * N)
        bband_ref[rows, cols] = bbt_re
        bband_ref[rows, cols_im] = bbt_im
        cband_ref[cols, rows] = crt_t[:, rows]
        cband_ref[cols_im, rows] = -cit_t[:, rows]
        arow_ref[0:1, cols] = ar
        arow_ref[1:2, cols] = ai


def _ssm_prep(A_re, A_im, log_dt, B_re, B_im, C_re, C_im, D, to_bf16, *, chunk, gb=8):
    G, N = A_re.shape
    P = SSM_GROUP_CH
    LP = chunk * P
    brt = B_re.transpose(0, 2, 1)
    bit = B_im.transpose(0, 2, 1)
    ins = [log_dt.reshape(G, 1, 1), A_re.reshape(G, 1, N), A_im.reshape(G, 1, N), brt, bit,
           C_re.transpose(0, 2, 1), C_im.transpose(0, 2, 1), D.reshape(G, P, 1)]
    gspec = lambda *s: pl.BlockSpec((gb,) + s, lambda i: (i, 0, 0))
    in_specs = [gspec(*a.shape[1:]) for a in ins]
    out_shape = [jax.ShapeDtypeStruct((G, LP, LP), BF16), jax.ShapeDtypeStruct((G, LP, 2 * N), F32),
                 jax.ShapeDtypeStruct((G, 2 * N, LP), BF16), jax.ShapeDtypeStruct((G, 2, N), F32)]
    out_specs = [gspec(*o.shape[1:]) for o in out_shape]
    band_shape = [jax.ShapeDtypeStruct((G // gb, gb * P, 2 * gb * N), F32),
                  jax.ShapeDtypeStruct((G // gb, 2 * gb * N, gb * P), F32),
                  jax.ShapeDtypeStruct((G // gb, 2, gb * N), F32)]
    out_shape += band_shape
    out_specs += [pl.BlockSpec((None,) + o.shape[1:], lambda i: (i, 0, 0)) for o in band_shape]
    slab_specs = [pl.BlockSpec((w.shape[0] // (G // gb), w.shape[1]), lambda i: (i, 0)) for w in to_bf16]
    return pl.pallas_call(
        functools.partial(_ssm_prep_kernel, gb=gb, chunk=chunk, ncast=len(to_bf16)),
        grid=(G // gb,), in_specs=in_specs + slab_specs, out_specs=out_specs + slab_specs,
        out_shape=out_shape + [jax.ShapeDtypeStruct(w.shape, BF16) for w in to_bf16],
        compiler_params=_params("arbitrary"), name="ssm_prep")(*ins, *to_bf16)


def _step_rows(a):
    return jnp.transpose(a, (1, 0, 2)).reshape(a.shape[0] * a.shape[1], a.shape[2])


def _seq_major(a, nseq):
    return jnp.transpose(a.reshape(a.shape[0] // nseq, nseq, a.shape[1]), (1, 0, 2))


def _conv3_steps_apart(v, w_ref, s_ref, prev, *, ts):
    tm = v.shape[0]
    s_ref[0:2 * ts, :] = prev
    s_ref[2 * ts:2 * ts + tm, :] = v
    out = w_ref[0:1, :] * s_ref[0:tm, :] + w_ref[1:2, :] * s_ref[ts:ts + tm, :] + w_ref[2:3, :] * v
    return out, s_ref[tm:tm + 2 * ts, :]


def _conv3_rows(v, w_ref, c_ref):
    tm = v.shape[0]
    head = c_ref[...]
    row = lax.broadcasted_iota(jnp.int32, head.shape, 0)

    def back(k):
        r = pltpu.roll(v, k, axis=0)
        first = jnp.where(row < k, pltpu.roll(head, k, axis=0), r[0:SUBLANES])
        return jnp.concatenate([first, r[SUBLANES:]], axis=0)

    out = w_ref[0:1, :] * back(2) + w_ref[1:2, :] * back(1) + w_ref[2:3, :] * v
    c_ref[...] = v[tm - SUBLANES:tm]
    return out


def _conv3_state(c_ref):
    return c_ref[..., SUBLANES - 2:SUBLANES, :]


def _piece_transpose8(v):
    width = LANES // 8
    piece = lax.broadcasted_iota(jnp.int32, v[0].shape, 1) // width
    for bit in range(3):
        s = 1 << bit
        hi = ((piece >> bit) & 1) == 1
        nv = list(v)
        for i in range(8):
            if i & s:
                continue
            a, b = v[i], v[i + s]
            nv[i] = jnp.where(hi, pltpu.roll(b, s * width, axis=1), a)
            nv[i + s] = jnp.where(hi, b, pltpu.roll(a, LANES - s * width, axis=1))
        v = nv
    return v


def _tokens_to_groups(zu, u_ref, *, nseg, seg):
    nc = seg // SSM_CHUNK
    ms = zu.shape[1]
    z_tb = jnp.transpose(zu.reshape(nseg, seg, ms), (1, 0, 2))
    for h in range(SSM_CHUNK // 8):
        for q in range(ms // LANES):
            xs = [jnp.concatenate([z_tb[SSM_CHUNK * c + 8 * h + i, :, q * LANES:(q + 1) * LANES] for c in range(nc)],
                                  axis=0) for i in range(8)]
            w = _piece_transpose8(xs)
            for k in range(8):
                u_ref[8 * q + k, :, h * LANES:(h + 1) * LANES] = w[k].astype(u_ref.dtype)


def _groups_to_tokens(y_ref, *, nseg, seg):
    nc = seg // SSM_CHUNK
    nq = y_ref.shape[0] // 8
    tiles = [[None] * nq for _ in range(seg)]
    for h in range(SSM_CHUNK // 8):
        for q in range(nq):
            v = _piece_transpose8([y_ref[8 * q + k, :, h * LANES:(h + 1) * LANES].astype(F32) for k in range(8)])
            for i in range(8):
                for c in range(nc):
                    tiles[SSM_CHUNK * c + 8 * h + i][q] = v[i][c * nseg:(c + 1) * nseg]
    y_tb = jnp.stack([jnp.concatenate(row, axis=1) for row in tiles], axis=0)
    return jnp.transpose(y_tb, (1, 0, 2)).reshape(nseg * seg, nq * LANES)


def _mix_in_core(x, g_ref, w_ref, gc_ref, conv, *, ms, mc):
    hb = _rms(x, g_ref[...]).astype(BF16)
    zu = _dot(hb, w_ref[:, 0:ms])
    xin = _dot(hb, w_ref[:, ms:ms + mc])
    cg = _dot(hb, w_ref[:, ms + 2 * mc:ms + 3 * mc])
    cv = conv(cg * xin)
    bg = _dot(hb, w_ref[:, ms + mc:ms + 2 * mc])
    return zu, _rms(bg * cv, gc_ref[...]).astype(BF16)


def _mix_in_kernel(*refs, nseg, seg, nsub, ncast, ms, mc):
    x_ref, g_ref, w_ref, cw_ref, gc_ref = refs[:5]
    cast_in, (u_ref, yc_ref, buf_ref) = refs[5:5 + ncast], refs[5 + ncast:8 + ncast]
    cast_out, s_ref = refs[8 + ncast:8 + 2 * ncast], refs[8 + 2 * ncast]
    _cast_slabs(cast_in, cast_out)

    sub = seg // nsub
    rg = sub // SSM_CHUNK * nseg

    @pl.when(pl.program_id(0) == 0)
    def _():
        s_ref[...] = jnp.zeros(s_ref.shape, F32)

    def conv(v):
        return jnp.concatenate([_conv3_rows(v[b * sub:(b + 1) * sub], cw_ref, s_ref.at[b]) for b in range(nseg)],
                               axis=0)

    for h in range(nsub):
        steps = slice(h * sub, (h + 1) * sub)
        zu, ycn = _mix_in_core(x_ref[:, steps, :].reshape(nseg * sub, x_ref.shape[-1]), g_ref, w_ref, gc_ref, conv,
                               ms=ms, mc=mc)
        _tokens_to_groups(zu, u_ref.at[:, pl.ds(h * rg, rg), :], nseg=nseg, seg=sub)
        yc_ref[:, steps, :] = ycn.reshape(nseg, sub, mc)
    buf_ref[...] = _conv3_state(s_ref)


def _mix_in(x3d, g, w_bf, conv_w, g_conv, to_bf16, *, seg, ms, mc):
    nseg, tlen, d = x3d.shape
    G = ms // SSM_GROUP_CH
    lp = SSM_CHUNK * SSM_GROUP_CH
    nsteps = tlen // seg
    row_spec = lambda c: pl.BlockSpec((nseg, seg, c), lambda t: (0, t, 0))
    slab_specs = [pl.BlockSpec((w.shape[0] // nsteps, w.shape[1]), lambda t: (t, 0)) for w in to_bf16]
    in_specs = [row_spec(d), _const_spec((1, d)), _const_spec(w_bf.shape), _const_spec(conv_w.shape),
                _const_spec((1, mc))] + slab_specs
    return pl.pallas_call(
        functools.partial(_mix_in_kernel, nseg=nseg, seg=seg, nsub=MIX_SUBBLOCKS, ncast=len(to_bf16), ms=ms, mc=mc),
        grid=(nsteps,), in_specs=in_specs,
        out_specs=[pl.BlockSpec((G, seg // SSM_CHUNK * nseg, lp), lambda t: (0, t, 0)), row_spec(mc),
                   pl.BlockSpec((nseg, 2, mc), lambda t: (0, 0, 0))] + slab_specs,
        out_shape=[jax.ShapeDtypeStruct((G, tlen // SSM_CHUNK * nseg, lp), BF16),
                   jax.ShapeDtypeStruct((nseg, tlen, mc), BF16), jax.ShapeDtypeStruct((nseg, 2, mc), F32)]
        + [jax.ShapeDtypeStruct(w.shape, BF16) for w in to_bf16],
        scratch_shapes=[pltpu.VMEM((nseg, SUBLANES, mc), F32)],
        compiler_params=_params("arbitrary"), name="mix_in")(x3d, g.reshape(1, d), w_bf, conv_w,
                                                              g_conv.reshape(1, mc), *to_bf16)


def _ssm_kernel(u_ref, m_ref, s_ref, o_ref, ap_ref, y_ref, fr_ref, fi_ref, lre, lim, ire, iim, *, gb, nb, nchunk):
    n = SSM_STATE
    for gi in range(gb):
        loc = _dot(u_ref[gi], s_ref[gi].astype(BF16))
        lre[gi] = loc[:, 0:n]
        lim[gi] = loc[:, n:2 * n]
    ar = [jnp.broadcast_to(ap_ref[gi, 0:1, :], (nb, n)) for gi in range(gb)]
    ai = [jnp.broadcast_to(ap_ref[gi, 1:2, :], (nb, n)) for gi in range(gb)]
    st0 = tuple((jnp.zeros((nb, n), F32), jnp.zeros((nb, n), F32)) for _ in range(gb))

    def step(c, st):
        r = pl.multiple_of(c * nb, nb)
        new = []
        for gi in range(gb):
            sr, si = st[gi]
            ire[gi, pl.ds(r, nb), :] = sr
            iim[gi, pl.ds(r, nb), :] = si
            nr = ar[gi] * sr - ai[gi] * si + lre[gi, pl.ds(r, nb), :]
            ni = ar[gi] * si + ai[gi] * sr + lim[gi, pl.ds(r, nb), :]
            new.append((nr, ni))
        return tuple(new)

    st = lax.fori_loop(0, nchunk, step, st0, unroll=4)
    for gi in range(gb):
        fr_ref[gi] = st[gi][0]
        fi_ref[gi] = st[gi][1]
        ini = jnp.concatenate([ire[gi], iim[gi]], axis=1).astype(BF16)
        y = _dot(u_ref[gi], m_ref[gi]) + _dot(ini, o_ref[gi])
        y_ref[gi] = y.astype(y_ref.dtype)


def _ssm(u_g, mats, apow, *, nb, nchunk, gb=4):
    G, R, LP = u_g.shape
    n = SSM_STATE
    ins = [u_g, *mats, apow]
    gspec = lambda *s: pl.BlockSpec((gb,) + s, lambda i: (i, 0, 0))
    out_shape = [jax.ShapeDtypeStruct((G, R, LP), F32), jax.ShapeDtypeStruct((G, nb, n), F32),
                 jax.ShapeDtypeStruct((G, nb, n), F32)]
    return pl.pallas_call(
        functools.partial(_ssm_kernel, gb=gb, nb=nb, nchunk=nchunk),
        grid=(G // gb,), in_specs=[gspec(*a.shape[1:]) for a in ins],
        out_specs=[gspec(*o.shape[1:]) for o in out_shape],
        out_shape=out_shape, scratch_shapes=[pltpu.VMEM((gb, R, n), F32)] * 4,
        compiler_params=_params("arbitrary"), name="ssm_chunk")(*ins)


def _mix_out_core(x, ys, yc, wg_ref, bg_ref, gs_ref, wo_ref, gx_ref, wq_ref, *, ms, q_scale):
    y = jax.nn.gelu(ys)
    y = y * jax.nn.sigmoid(_dot(y.astype(BF16), wg_ref[...]) + bg_ref[...])
    ysn = _rms(y, gs_ref[...]).astype(BF16)
    x1 = x + _dot(jnp.concatenate([ysn, yc], axis=1), wo_ref[...])
    h = _rms(x1, gx_ref[...]).astype(BF16)
    return x1, _dot(h, wq_ref[...]) * q_scale


def _memory_kv(m_ref, g_ref, wk_ref, wv_ref, k_ref, v_ref, kb_ref, vb_ref):
    h = _rms(m_ref[0], g_ref[...]).astype(BF16)
    hd = k_ref.shape[-1]
    for w_ref, out_ref, bf_ref in ((wk_ref, k_ref, kb_ref), (wv_ref, v_ref, vb_ref)):
        kv = _dot(h, w_ref[...])
        heads = jnp.stack([kv[:, i * hd:(i + 1) * hd] for i in range(XATTN_HEADS)], axis=0)
        out_ref[0] = jnp.transpose(heads, (1, 0, 2))
        bf_ref[0] = heads.astype(BF16)


def _mix_out_kernel(x_ref, ys_ref, yc_ref, wg_ref, bg_ref, gs_ref, wo_ref, gx_ref, wq_ref, mem_ref, gm_ref, wk_ref,
                    wv_ref, x1_ref, q_ref, k_ref, v_ref, kb_ref, vb_ref, *, nseg, seg, nsub, ms, q_scale):
    _memory_kv(mem_ref, gm_ref, wk_ref, wv_ref, k_ref, v_ref, kb_ref, vb_ref)
    sub = seg // nsub
    rg = sub // SSM_CHUNK * nseg
    d = x_ref.shape[-1]
    for h in range(nsub):
        steps = slice(h * sub, (h + 1) * sub)
        ys = _groups_to_tokens(ys_ref.at[:, pl.ds(h * rg, rg), :], nseg=nseg, seg=sub)
        x1, q = _mix_out_core(x_ref[:, steps, :].reshape(nseg * sub, d),
                              ys, yc_ref[:, steps, :].reshape(nseg * sub, yc_ref.shape[-1]),
                              wg_ref, bg_ref, gs_ref, wo_ref, gx_ref, wq_ref, ms=ms, q_scale=q_scale)
        x1_ref[:, steps, :] = x1.reshape(nseg, sub, d)
        q_ref[:, steps, :] = q.astype(q_ref.dtype).reshape(nseg, sub, d)


def _mix_out(x3d, ys, yc, w_glu, b_glu, g_ssm, w_out, g_x, w_q, mem, g_mem, w_k, w_v, *, seg):
    nseg, tlen, d = x3d.shape
    ms = w_glu.shape[0]
    hd = d // XATTN_HEADS
    nsteps = tlen // seg
    nmem, nm, _ = mem.shape
    per_seq = nsteps // nmem
    mt = nm // per_seq
    assert per_seq * nmem == nsteps and mt * per_seq == nm
    row_spec = lambda c: pl.BlockSpec((nseg, seg, c), lambda t: (0, t, 0))
    ys_spec = pl.BlockSpec((ys.shape[0], seg // SSM_CHUNK * nseg, ys.shape[2]), lambda t: (0, t, 0))
    in_specs = [row_spec(d), ys_spec, row_spec(yc.shape[-1]), _const_spec(w_glu.shape), _const_spec((1, ms)),
                _const_spec((1, ms)), _const_spec(w_out.shape), _const_spec((1, d)), _const_spec(w_q.shape),
                pl.BlockSpec((1, mt, d), lambda t: (t // per_seq, t % per_seq, 0)), _const_spec((1, d)),
                _const_spec(w_k.shape), _const_spec(w_v.shape)]
    kv_spec = pl.BlockSpec((1, mt, XATTN_HEADS, hd), lambda t: (t // per_seq, t % per_seq, 0, 0))
    kvb_spec = pl.BlockSpec((1, XATTN_HEADS, mt, hd), lambda t: (t // per_seq, 0, t % per_seq, 0))
    return pl.pallas_call(
        functools.partial(_mix_out_kernel, nseg=nseg, seg=seg, nsub=MIX_SUBBLOCKS, ms=ms, q_scale=hd ** -0.5),
        grid=(nsteps,), in_specs=in_specs, out_specs=[row_spec(d), row_spec(d), kv_spec, kv_spec, kvb_spec, kvb_spec],
        out_shape=[jax.ShapeDtypeStruct((nseg, tlen, d), F32), jax.ShapeDtypeStruct((nseg, tlen, d), BF16)]
        + [jax.ShapeDtypeStruct((nmem, nm, XATTN_HEADS, hd), F32)] * 2
        + [jax.ShapeDtypeStruct((nmem, XATTN_HEADS, nm, hd), BF16)] * 2,
        compiler_params=_params("parallel"), name="mix_out")(
            x3d, ys, yc, w_glu, b_glu.reshape(1, ms), g_ssm.reshape(1, ms), w_out, g_x.reshape(1, d), w_q,
            mem, g_mem.reshape(1, d), w_k, w_v)


def _step_mix_kernel(x_ref, g_ref, w_ref, cw_ref, gc_ref, prev_ref, s0r_ref, s0i_ref, bband_ref, cband_ref, arow_ref,
                     d_ref, wg_ref, bg_ref, gs_ref, wo_ref, gx_ref, wq_ref,
                     x1_ref, q_ref, buf_ref, fr_ref, fi_ref, s_ref, ys_ref, *, nseq, tlen, ms, mc, q_scale):
    x = _step_rows(x_ref[...])

    def conv(v):
        out, buf = _conv3_steps_apart(v, cw_ref, s_ref, _step_rows(prev_ref[...]), ts=nseq)
        buf_ref[...] = _seq_major(buf, nseq)
        return out

    zu, ycn = _mix_in_core(x, g_ref, w_ref, gc_ref, conv, ms=ms, mc=mc)

    wu, ws = bband_ref.shape[1], arow_ref.shape[2]
    gpb = ws // SSM_STATE

    def load_state(ref, i):
        by_group = jnp.transpose(ref[:, i * gpb:(i + 1) * gpb, :], (1, 0, 2))
        return jnp.concatenate([by_group[k] for k in range(gpb)], axis=1)

    def store_state(ref, i, val):
        by_group = jnp.stack([val[:, k * SSM_STATE:(k + 1) * SSM_STATE] for k in range(gpb)], axis=0)
        ref[:, i * gpb:(i + 1) * gpb, :] = jnp.transpose(by_group, (1, 0, 2))

    for i in range(bband_ref.shape[0]):
        ucols = slice(i * wu, (i + 1) * wu)
        sr, si = load_state(s0r_ref, i), load_state(s0i_ref, i)
        ar, ai = arow_ref[i, 0:1, :], arow_ref[i, 1:2, :]
        bband, cband = bband_ref[i].astype(BF16), cband_ref[i].astype(BF16)
        for t in range(tlen):
            ut = zu[t * nseq:(t + 1) * nseq, ucols]
            bu = _dot(ut.astype(BF16), bband)
            sr, si = ar * sr - ai * si + bu[:, 0:ws], ar * si + ai * sr + bu[:, ws:]
            cs = _dot(jnp.concatenate([sr, si], axis=1).astype(BF16), cband)
            ys_ref[t * nseq:(t + 1) * nseq, ucols] = cs + d_ref[:, ucols] * ut
        store_state(fr_ref, i, sr)
        store_state(fi_ref, i, si)

    x1, q = _mix_out_core(x, ys_ref[...], ycn, wg_ref, bg_ref, gs_ref, wo_ref, gx_ref, wq_ref, ms=ms, q_scale=q_scale)
    x1_ref[...] = x1
    q_ref[...] = _seq_major(q, nseq)


def _step_mix(x3d, p, conv_prev, s0_re, s0_im, bands, d_row):
    nseq, tlen, d = x3d.shape
    rows = nseq * tlen
    ms = p["w_glu"].shape[0]
    mc = p["conv_w"].shape[1]
    hd = d // XATTN_HEADS
    bband, cband, arow = bands
    ins = [x3d, p["norm_mix"].reshape(1, d), p["w_in"], p["conv_w"], p["norm_conv_out"].reshape(1, mc), conv_prev,
           s0_re, s0_im, bband, cband, arow, d_row, p["w_glu"], p["b_glu"].reshape(1, ms),
           p["norm_ssm_out"].reshape(1, ms), p["w_out"], p["norm_xattn"].reshape(1, d), p["w_q"]]
    full = lambda shape: pl.BlockSpec(shape, lambda i: (0,) * len(shape))
    out_shape = [jax.ShapeDtypeStruct((rows, d), F32), jax.ShapeDtypeStruct((nseq, tlen, d), F32),
                 jax.ShapeDtypeStruct(conv_prev.shape, F32), jax.ShapeDtypeStruct(s0_re.shape, F32),
                 jax.ShapeDtypeStruct(s0_im.shape, F32)]
    return pl.pallas_call(
        functools.partial(_step_mix_kernel, nseq=nseq, tlen=tlen, ms=ms, mc=mc, q_scale=hd ** -0.5),
        grid=(1,), in_specs=[_const_spec(a.shape) for a in ins], out_specs=[full(o.shape) for o in out_shape],
        out_shape=out_shape,
        scratch_shapes=[pltpu.VMEM((2 * nseq + rows, mc), F32), pltpu.VMEM((rows, ms), F32)],
        compiler_params=_params("arbitrary"), name="step_mix")(*ins)


def _softmax_attention(qs, ks, vs):
    tq = qs[0].shape[0]
    s = jnp.concatenate([lax.dot_general(q, k, (((1,), (1,)), ((), ())), preferred_element_type=F32)
                         for q, k in zip(qs, ks)], axis=0)
    p = jnp.exp(s - jnp.max(s, axis=-1, keepdims=True))
    p = p / jnp.sum(p, axis=-1, keepdims=True)
    return [_dot(p[i * tq:(i + 1) * tq].astype(BF16), v) for i, v in enumerate(vs)]


def _head_slices(ref, rows, hd):
    return [ref[rows, h * hd:(h + 1) * hd].astype(BF16) for h in range(XATTN_HEADS)]


def _kv_heads(ref, j):
    hm = jnp.transpose(ref[j], (1, 0, 2))
    return [hm[h].astype(BF16) for h in range(XATTN_HEADS)]


def _attend_sequences(q_ref, k_ref, v_ref, o_ref):
    sb, _, d = q_ref.shape
    hd = d // XATTN_HEADS
    qs, ks, vs = [], [], []
    for j in range(sb):
        qs += _head_slices(q_ref.at[j], slice(None), hd)
        ks += _kv_heads(k_ref, j)
        vs += _kv_heads(v_ref, j)
    outs = _softmax_attention(qs, ks, vs)
    for j in range(sb):
        o_ref[j] = jnp.concatenate(outs[j * XATTN_HEADS:(j + 1) * XATTN_HEADS], axis=1).astype(o_ref.dtype)


def _ffn_kernel(*refs, ts, has_prev, has_kv, side):
    x_ref, o_ref = refs[:2]
    refs = refs[2:]
    if has_kv:
        k_ref, v_ref = refs[:2]
        refs = refs[2:]
    if side:
        sq_ref, sk_ref, sv_ref = refs[:3]
        refs = refs[3:]
    wxo_ref, gf_ref, wup_ref, wgate_ref, cw_ref, wdn_ref, gl_ref = refs[:7]
    refs = refs[7:]
    if has_prev:
        prev_ref, refs = refs[0], refs[1:]
    if side:
        y_ref, buf_ref, so_ref, s_ref = refs
    else:
        y_ref, buf_ref, s_ref = refs
    if not has_prev:
        @pl.when(pl.program_id(1) == 0)
        def _():
            s_ref[...] = jnp.zeros(s_ref.shape, F32)

    if has_kv:
        hd = o_ref.shape[1] // XATTN_HEADS
        rows = slice(None)
        heads = _softmax_attention(_head_slices(o_ref, rows, hd), [k_ref[0, h] for h in range(XATTN_HEADS)],
                                   [v_ref[0, h] for h in range(XATTN_HEADS)])
        o = jnp.concatenate(heads, axis=1).astype(BF16)
    elif has_prev:
        o = _step_rows(o_ref[...]).astype(BF16)
    else:
        o = o_ref[...].astype(BF16)
    x2 = x_ref[...] + _dot(o, wxo_ref[...])
    hb = _rms(x2, gf_ref[...]).astype(BF16)
    up = _dot(hb, wup_ref[...])
    if has_prev:
        a, buf = _conv3_steps_apart(up, cw_ref, s_ref, _step_rows(prev_ref[...]), ts=ts)
        buf_ref[...] = _seq_major(buf, ts)
    else:
        a = _conv3_rows(up, cw_ref, s_ref)
        buf_ref[...] = _conv3_state(s_ref)
    act = (jax.nn.gelu(a) * _dot(hb, wgate_ref[...])).astype(BF16)
    y = _rms(x2 + _dot(act, wdn_ref[...]), gl_ref[...])
    y_ref[...] = _seq_major(y, ts) if has_prev else y
    if side:
        _attend_sequences(sq_ref, sk_ref, sv_ref, so_ref)


def _ffn(x2d, o2d, kv, w_xo, g_ffn, w_up, w_gate, conv_w, w_down, g_last, prev, *, nb, nt, tm, ts, side=None):
    rows, d = x2d.shape
    dff = w_up.shape[1]
    has_prev = prev is not None
    row_spec = pl.BlockSpec((tm, d), lambda b, t: (b * nt + t, 0))
    in_specs = [row_spec, _const_spec(o2d.shape) if has_prev else row_spec]
    ins = [x2d, o2d]
    if kv is not None:
        in_specs += [pl.BlockSpec((1,) + kv[0].shape[1:], lambda b, t: (b, 0, 0, 0))] * 2
        ins += list(kv)
    if side is not None:
        sq, sk, sv = side
        sb = sq.shape[0] // (nb * nt)
        assert sb * nb * nt == sq.shape[0]
        side_q_spec = pl.BlockSpec((sb,) + sq.shape[1:], lambda b, t: (b * nt + t, 0, 0))
        in_specs += [side_q_spec] + [pl.BlockSpec((sb,) + sk.shape[1:], lambda b, t: (b * nt + t, 0, 0, 0))] * 2
        ins += [sq, sk, sv]
    in_specs += [_const_spec(w_xo.shape), _const_spec((1, d)), _const_spec(w_up.shape), _const_spec(w_gate.shape),
                 _const_spec(conv_w.shape), _const_spec(w_down.shape), _const_spec((1, d))]
    ins += [w_xo, g_ffn.reshape(1, d), w_up, w_gate, conv_w, w_down, g_last.reshape(1, d)]
    if has_prev:
        in_specs.append(_const_spec(prev.shape))
        ins.append(prev)
        out_shape = [jax.ShapeDtypeStruct(o2d.shape, F32), jax.ShapeDtypeStruct(prev.shape, F32)]
        out_specs = [pl.BlockSpec(o.shape, lambda b, t: (0, 0, 0)) for o in out_shape]
    else:
        out_shape = [jax.ShapeDtypeStruct((rows, d), F32), jax.ShapeDtypeStruct((nb, 2, dff), F32)]
        out_specs = [row_spec, pl.BlockSpec((None, 2, dff), lambda b, t: (b, 0, 0))]
    if side is not None:
        out_specs.append(side_q_spec)
        out_shape.append(jax.ShapeDtypeStruct(sq.shape, F32))
    return pl.pallas_call(
        functools.partial(_ffn_kernel, ts=ts, has_prev=has_prev, has_kv=kv is not None, side=side is not None),
        grid=(nb, nt), in_specs=in_specs, out_specs=out_specs, out_shape=out_shape,
        scratch_shapes=[pltpu.VMEM((2 * ts + tm if has_prev else SUBLANES, dff), F32)],
        compiler_params=_params("parallel", "arbitrary", vmem=V7X_VMEM_LIMIT if side is None else V7X_VMEM_LIMIT_MAX),
        name="conv_ffn")(*ins)


def _prompt_mixer(x3d, mem, p, mats, apow, later_weights):
    nseq, tlen, d = x3d.shape
    ms = p["w_glu"].shape[0]
    mc = p["conv_w"].shape[1]
    u, ycn, conv_buf, *later_bf16 = _mix_in(x3d, p["norm_mix"], p["w_in"], p["conv_w"], p["norm_conv_out"],
                                            later_weights, seg=MIX_STEPS, ms=ms, mc=mc)
    y_g, f_re, f_im = _ssm(u, mats, apow, nb=nseq, nchunk=tlen // SSM_CHUNK)
    *later_bf16, w_k, w_v = later_bf16
    x1, q, *kv = _mix_out(x3d, y_g, ycn, p["w_glu"], p["b_glu"], p["norm_ssm_out"], p["w_out"], p["norm_xattn"],
                          p["w_q"], mem, p["norm_mem"], w_k, w_v, seg=MIX_STEPS)
    return x1, q, (f_re.transpose(1, 0, 2), f_im.transpose(1, 0, 2), conv_buf), kv, later_bf16


def _prompt_ffn(x1, q, k, v, p, g_last, side):
    nseq, tlen, d = x1.shape
    rows = nseq * tlen
    return _ffn(x1.reshape(rows, d), q.reshape(rows, d), (k, v), p["w_xo"], p["norm_ffn"], p["w_up"],
                p["w_gate"], p["ffn_conv_w"], p["w_down"], g_last, None,
                nb=nseq, nt=tlen // FFN_ROWS, tm=FFN_ROWS, ts=1, side=side)


def _step_layer_mix(x_bm, prev, p, bands, d_row):
    x1, q, conv_buf, f_re, f_im = _step_mix(x_bm, p, prev[2], prev[0], prev[1], bands, d_row)
    return x1, q, (f_re, f_im, conv_buf)


def _step_layer_ffn(x1, o_bm, prev_ffn, p, g_last):
    return _ffn(x1, o_bm, None, p["w_xo"], p["norm_ffn"], p["w_up"], p["w_gate"], p["ffn_conv_w"], p["w_down"],
                g_last, prev_ffn, nb=1, nt=1, tm=x1.shape[0], ts=o_bm.shape[0])


def kernel(x_prompt, x_sample, mem_prompt, cache_mem_k, cache_mem_v, state_ssm_re, state_ssm_im, state_conv, state_ffn_conv, norm_mix, w_in, ssm_A_re, ssm_A_im, ssm_log_dt, ssm_B_re, ssm_B_im, ssm_C_re, ssm_C_im, ssm_D, w_glu, b_glu, conv_w, norm_ssm_out, norm_conv_out, w_out, norm_xattn, norm_mem, w_q, w_k, w_v, w_xo, norm_ffn, w_up, w_gate, ffn_conv_w, w_down, norm_final):
    depth = w_in.shape[0]
    assert depth == 1, "the final norm is fused into the last (only) layer's ConvFFN kernel"
    nbp, tp, d = x_prompt.shape
    assert tp % FFN_ROWS == 0 and tp % MIX_STEPS == 0 and nbp == 8

    xp, xs = x_prompt, x_sample
    outs_p, outs_s, mk_p, mv_p = [], [], [], []
    for l in range(depth):
        m, s_cat, o_cat, apow, bband, cband, arow, *mixer_w = _ssm_prep(
            ssm_A_re[l], ssm_A_im[l], ssm_log_dt[l], ssm_B_re[l], ssm_B_im[l], ssm_C_re[l], ssm_C_im[l], ssm_D[l],
            [w_in[l], w_glu[l], w_out[l], w_q[l]], chunk=SSM_CHUNK)
        mats = (m, s_cat, o_cat)
        p = dict(norm_mix=norm_mix[l], b_glu=b_glu[l], conv_w=conv_w[l], norm_ssm_out=norm_ssm_out[l],
                 norm_conv_out=norm_conv_out[l], norm_xattn=norm_xattn[l], norm_ffn=norm_ffn[l],
                 ffn_conv_w=ffn_conv_w[l], norm_mem=norm_mem[l])
        p["w_in"], p["w_glu"], p["w_out"], p["w_q"] = mixer_w

        x1s, qs, st_s = _step_layer_mix(xs, (state_ssm_re[l], state_ssm_im[l], state_conv[l]), p,
                                        (bband, cband, arow), ssm_D[l].reshape(1, -1))
        x1p, qp, st_p, (k_p, v_p, kb_p, vb_p), later = _prompt_mixer(
            xp, mem_prompt, p, mats, apow, [w_xo[l], w_up[l], w_gate[l], w_down[l], w_k[l], w_v[l]])
        p["w_xo"], p["w_up"], p["w_gate"], p["w_down"] = later
        mk_p.append(k_p)
        mv_p.append(v_p)

        side = (qs, cache_mem_k[l], cache_mem_v[l])
        xp, ffn_buf_p, os_bm = _prompt_ffn(x1p, qp, kb_p, vb_p, p, norm_final, side)
        outs_p.append(st_p + (ffn_buf_p,))
        xs, ffn_buf_s = _step_layer_ffn(x1s, os_bm, state_ffn_conv[l], p, norm_final)
        outs_s.append(st_s + (ffn_buf_s,))

    stack = lambda outs, i: jnp.stack([o[i] for o in outs])
    return (xp.reshape(nbp, tp, d), xs, jnp.stack(mk_p), jnp.stack(mv_p),
            stack(outs_p, 0), stack(outs_p, 1), stack(outs_p, 2), stack(outs_p, 3),
            stack(outs_s, 0), stack(outs_s, 1), stack(outs_s, 2), stack(outs_s, 3))
```

```python
import functools

import jax
import jax.numpy as jnp
from jax import lax
from jax.experimental import pallas as pl
from jax.experimental.pallas import tpu as pltpu

F32 = jnp.float32
BF16 = jnp.bfloat16

EPS = 1e-6
SSM_GROUP_CH = 16
SSM_STATE = 64
N_MEM = 256
XATTN_HEADS = 4
SSM_CHUNK = 16
LANES = 128
SUBLANES = 8
MIX_STEPS = 128
MIX_SUBBLOCKS = 2
FFN_ROWS = 512
V7X_VMEM_LIMIT = 56 * 1024 * 1024
V7X_VMEM_LIMIT_MAX = 62 * 1024 * 1024


def _rms(x, g):
    ms = jnp.mean(x * x, axis=-1, keepdims=True)
    return x * lax.rsqrt(ms + EPS) * g


def _dot(a, b):
    return jnp.dot(a, b, preferred_element_type=F32)


def _const_spec(shape):
    nd = len(shape)
    return pl.BlockSpec(shape, lambda *_: (0,) * nd, pipeline_mode=pl.Buffered(1))


def _params(*sem, vmem=V7X_VMEM_LIMIT):
    return pltpu.CompilerParams(dimension_semantics=sem, vmem_limit_bytes=vmem)


def _cast_slabs(srcs, dsts):
    for src, dst in zip(srcs, dsts):
        dst[...] = src[...].astype(dst.dtype)


def _ssm_prep_kernel(*refs, gb, chunk, ncast):
    ld_ref, lr_r_ref, li_r_ref, brt_ref, bit_ref, crt_t_ref, cit_t_ref, d_ref = refs[:8]
    m_ref, s_ref, o_ref, apow_ref, bband_ref, cband_ref, arow_ref = refs[8 + ncast:15 + ncast]
    _cast_slabs(refs[8:8 + ncast], refs[15 + ncast:])
    _ssm_prep_groups(ld_ref, lr_r_ref, li_r_ref, brt_ref, bit_ref, crt_t_ref, cit_t_ref, d_ref,
                     m_ref, s_ref, o_ref, apow_ref, bband_ref, cband_ref, arow_ref, gb=gb, chunk=chunk)


def _ssm_prep_groups(ld_ref, lr_r_ref, li_r_ref, brt_ref, bit_ref, crt_t_ref, cit_t_ref, d_ref,
                     m_ref, s_ref, o_ref, apow_ref, bband_ref, cband_ref, arow_ref, *, gb, chunk):
    P, N = SSM_GROUP_CH, SSM_STATE
    LP = chunk * P
    bband_ref[...] = jnp.zeros(bband_ref.shape, bband_ref.dtype)
    cband_ref[...] = jnp.zeros(cband_ref.shape, cband_ref.dtype)
    lane_j = lax.broadcasted_iota(jnp.int32, (1, LP), 1) // P
    row_j = lax.broadcasted_iota(jnp.int32, (LP, 1), 0) // P
    lane_i = lax.broadcasted_iota(jnp.int32, (P, LP), 1)
    row_i = lax.broadcasted_iota(jnp.int32, (P, LP), 0)
    nbits = chunk.bit_length()

    eye = lax.broadcasted_iota(jnp.int32, (N, N), 0) == lax.broadcasted_iota(jnp.int32, (N, N), 1)

    def as_column(row):
        return jnp.sum(jnp.where(eye, row, 0.0), axis=1, keepdims=True)

    def squarings(pr, pi):
        out = [(pr, pi)]
        for _ in range(nbits - 1):
            pr, pi = pr * pr - pi * pi, 2.0 * pr * pi
            out.append((pr, pi))
        return out

    def cpow(pows, j):
        er = ei = None
        for b, (pr, pi) in enumerate(pows):
            if isinstance(j, int):
                if not (j >> b) & 1:
                    continue
                er, ei = (pr, pi) if er is None else (er * pr - ei * pi, er * pi + ei * pr)
            else:
                bit = ((j >> b) & 1) == 1
                if er is None:
                    er, ei = jnp.where(bit, pr, 1.0), jnp.where(bit, pi, 0.0)
                else:
                    er, ei = jnp.where(bit, er * pr - ei * pi, er), jnp.where(bit, er * pi + ei * pr, ei)
        return er, ei

    for gi in range(gb):
        dt = jnp.exp(ld_ref[gi])
        lr_r, li_r = lr_r_ref[gi], li_r_ref[gi]
        mag = jnp.exp(dt * lr_r)
        ar, ai = mag * jnp.cos(dt * li_r), mag * jnp.sin(dt * li_r)
        pows_r = squarings(ar, ai)
        pows_c = squarings(as_column(ar), as_column(ai))

        den = lr_r * lr_r + li_r * li_r
        cr = ((ar - 1.0) * lr_r + ai * li_r) / den
        ci = (ai * lr_r - (ar - 1.0) * li_r) / den

        brt, bit = brt_ref[gi], bit_ref[gi]
        bbt_re = cr * brt - ci * bit
        bbt_im = cr * bit + ci * brt
        bbt_re_t = jnp.concatenate([bbt_re] * chunk, axis=0)
        bbt_im_t = jnp.concatenate([bbt_im] * chunk, axis=0)

        er, ei = cpow(pows_r, (chunk - 1) - row_j)
        s_ref[gi, :, 0:N] = er * bbt_re_t - ei * bbt_im_t
        s_ref[gi, :, N:2 * N] = er * bbt_im_t + ei * bbt_re_t

        crt_t, cit_t = crt_t_ref[gi], cit_t_ref[gi]
        er, ei = cpow(pows_c, lane_j)
        r_re = crt_t * er - cit_t * ei
        r_im = crt_t * ei + cit_t * er
        er, ei = er * pows_c[0][0] - ei * pows_c[0][1], er * pows_c[0][1] + ei * pows_c[0][0]
        o_ref[gi, 0:N, :] = (crt_t * er - cit_t * ei).astype(o_ref.dtype)
        o_ref[gi, N:2 * N, :] = (-(crt_t * ei + cit_t * er)).astype(o_ref.dtype)

        krow = (jnp.dot(bbt_re, r_re, preferred_element_type=F32, precision=lax.Precision.HIGHEST)
                - jnp.dot(bbt_im, r_im, preferred_element_type=F32, precision=lax.Precision.HIGHEST))
        krow = krow + jnp.where(lane_i == row_i, d_ref[gi], 0.0)
        for ti in range(chunk):
            blk = krow if ti == 0 else pltpu.roll(krow, ti * P, axis=1)
            blk = jnp.where(lane_i >= ti * P, blk, 0.0)
            m_ref[gi, ti * P:(ti + 1) * P, :] = blk.astype(m_ref.dtype)

        apow_ref[gi] = jnp.concatenate(cpow(pows_r, chunk), axis=0)

        rows, cols = slice(gi * P, (gi + 1) * P), slice(gi * N, (gi + 1) * N)
        cols_im = slice((gb + gi) * N, (gb + gi + 1) * N)
        bband_ref[rows, cols] = bbt_re
        bband_ref[rows, cols_im] = bbt_im
        cband_ref[cols, rows] = crt_t[:, rows]
        cband_ref[cols_im, rows] = -cit_t[:, rows]
        arow_ref[0:1, cols] = ar
        arow_ref[1:2, cols] = ai


def _ssm_prep(A_re, A_im, log_dt, B_re, B_im, C_re, C_im, D, to_bf16, *, chunk, gb=8):
    G, N = A_re.shape
    P = SSM_GROUP_CH
    LP = chunk * P
    brt = B_re.transpose(0, 2, 1)
    bit = B_im.transpose(0, 2, 1)
    crt_t = jnp.tile(C_re.transpose(0, 2, 1), (1, 1, chunk))
    cit_t = jnp.tile(C_im.transpose(0, 2, 1), (1, 1, chunk))
    ins = [log_dt.reshape(G, 1, 1), A_re.reshape(G, 1, N), A_im.reshape(G, 1, N), brt, bit, crt_t, cit_t,
           D.reshape(G, P, 1)]
    gspec = lambda *s: pl.BlockSpec((gb,) + s, lambda i: (i, 0, 0))
    in_specs = [gspec(*a.shape[1:]) for a in ins]
    out_shape = [jax.ShapeDtypeStruct((G, LP, LP), BF16), jax.ShapeDtypeStruct((G, LP, 2 * N), F32),
                 jax.ShapeDtypeStruct((G, 2 * N, LP), BF16), jax.ShapeDtypeStruct((G, 2, N), F32)]
    out_specs = [gspec(*o.shape[1:]) for o in out_shape]
    band_shape = [jax.ShapeDtypeStruct((G // gb, gb * P, 2 * gb * N), F32),
                  jax.ShapeDtypeStruct((G // gb, 2 * gb * N, gb * P), F32),
                  jax.ShapeDtypeStruct((G // gb, 2, gb * N), F32)]
    out_shape += band_shape
    out_specs += [pl.BlockSpec((None,) + o.shape[1:], lambda i: (i, 0, 0)) for o in band_shape]
    slab_specs = [pl.BlockSpec((w.shape[0] // (G // gb), w.shape[1]), lambda i: (i, 0)) for w in to_bf16]
    return pl.pallas_call(
        functools.partial(_ssm_prep_kernel, gb=gb, chunk=chunk, ncast=len(to_bf16)),
        grid=(G // gb,), in_specs=in_specs + slab_specs, out_specs=out_specs + slab_specs,
        out_shape=out_shape + [jax.ShapeDtypeStruct(w.shape, BF16) for w in to_bf16],
        compiler_params=_params("arbitrary"), name="ssm_prep")(*ins, *to_bf16)


def _step_rows(a):
    return jnp.transpose(a, (1, 0, 2)).reshape(a.shape[0] * a.shape[1], a.shape[2])


def _seq_major(a, nseq):
    return jnp.transpose(a.reshape(a.shape[0] // nseq, nseq, a.shape[1]), (1, 0, 2))


def _conv3_steps_apart(v, w_ref, s_ref, prev, *, ts):
    tm = v.shape[0]
    s_ref[0:2 * ts, :] = prev
    s_ref[2 * ts:2 * ts + tm, :] = v
    out = w_ref[0:1, :] * s_ref[0:tm, :] + w_ref[1:2, :] * s_ref[ts:ts + tm, :] + w_ref[2:3, :] * v
    return out, s_ref[tm:tm + 2 * ts, :]


def _conv3_rows(v, w_ref, c_ref):
    tm = v.shape[0]
    head = c_ref[...]
    row = lax.broadcasted_iota(jnp.int32, head.shape, 0)

    def back(k):
        r = pltpu.roll(v, k, axis=0)
        first = jnp.where(row < k, pltpu.roll(head, k, axis=0), r[0:SUBLANES])
        return jnp.concatenate([first, r[SUBLANES:]], axis=0)

    out = w_ref[0:1, :] * back(2) + w_ref[1:2, :] * back(1) + w_ref[2:3, :] * v
    c_ref[...] = v[tm - SUBLANES:tm]
    return out


def _conv3_state(c_ref):
    return c_ref[..., SUBLANES - 2:SUBLANES, :]


def _piece_transpose8(v):
    width = LANES // 8
    piece = lax.broadcasted_iota(jnp.int32, v[0].shape, 1) // width
    for bit in range(3):
        s = 1 << bit
        hi = ((piece >> bit) & 1) == 1
        nv = list(v)
        for i in range(8):
            if i & s:
                continue
            a, b = v[i], v[i + s]
            nv[i] = jnp.where(hi, pltpu.roll(b, s * width, axis=1), a)
            nv[i + s] = jnp.where(hi, b, pltpu.roll(a, LANES - s * width, axis=1))
        v = nv
    return v


def _tokens_to_groups(zu, u_ref, *, nseg, seg):
    nc = seg // SSM_CHUNK
    ms = zu.shape[1]
    z_tb = jnp.transpose(zu.reshape(nseg, seg, ms), (1, 0, 2))
    for h in range(SSM_CHUNK // 8):
        for q in range(ms // LANES):
            xs = [jnp.concatenate([z_tb[SSM_CHUNK * c + 8 * h + i, :, q * LANES:(q + 1) * LANES] for c in range(nc)],
                                  axis=0) for i in range(8)]
            w = _piece_transpose8(xs)
            for k in range(8):
                u_ref[8 * q + k, :, h * LANES:(h + 1) * LANES] = w[k].astype(u_ref.dtype)


def _groups_to_tokens(y_ref, *, nseg, seg):
    nc = seg // SSM_CHUNK
    nq = y_ref.shape[0] // 8
    tiles = [[None] * nq for _ in range(seg)]
    for h in range(SSM_CHUNK // 8):
        for q in range(nq):
            v = _piece_transpose8([y_ref[8 * q + k, :, h * LANES:(h + 1) * LANES].astype(F32) for k in range(8)])
            for i in range(8):
                for c in range(nc):
                    tiles[SSM_CHUNK * c + 8 * h + i][q] = v[i][c * nseg:(c + 1) * nseg]
    y_tb = jnp.stack([jnp.concatenate(row, axis=1) for row in tiles], axis=0)
    return jnp.transpose(y_tb, (1, 0, 2)).reshape(nseg * seg, nq * LANES)


def _mix_in_core(x, g_ref, w_ref, gc_ref, conv, *, ms, mc):
    hb = _rms(x, g_ref[...]).astype(BF16)
    zu = _dot(hb, w_ref[:, 0:ms])
    xin = _dot(hb, w_ref[:, ms:ms + mc])
    cg = _dot(hb, w_ref[:, ms + 2 * mc:ms + 3 * mc])
    cv = conv(cg * xin)
    bg = _dot(hb, w_ref[:, ms + mc:ms + 2 * mc])
    return zu, _rms(bg * cv, gc_ref[...]).astype(BF16)


def _mix_in_kernel(*refs, nseg, seg, nsub, ncast, ms, mc):
    x_ref, g_ref, w_ref, cw_ref, gc_ref = refs[:5]
    cast_in, (u_ref, yc_ref, buf_ref) = refs[5:5 + ncast], refs[5 + ncast:8 + ncast]
    cast_out, s_ref = refs[8 + ncast:8 + 2 * ncast], refs[8 + 2 * ncast]
    _cast_slabs(cast_in, cast_out)

    sub = seg // nsub
    rg = sub // SSM_CHUNK * nseg

    @pl.when(pl.program_id(0) == 0)
    def _():
        s_ref[...] = jnp.zeros(s_ref.shape, F32)

    def conv(v):
        return jnp.concatenate([_conv3_rows(v[b * sub:(b + 1) * sub], cw_ref, s_ref.at[b]) for b in range(nseg)],
                               axis=0)

    for h in range(nsub):
        steps = slice(h * sub, (h + 1) * sub)
        zu, ycn = _mix_in_core(x_ref[:, steps, :].reshape(nseg * sub, x_ref.shape[-1]), g_ref, w_ref, gc_ref, conv,
                               ms=ms, mc=mc)
        _tokens_to_groups(zu, u_ref.at[:, pl.ds(h * rg, rg), :], nseg=nseg, seg=sub)
        yc_ref[:, steps, :] = ycn.reshape(nseg, sub, mc)
    buf_ref[...] = _conv3_state(s_ref)


def _mix_in(x3d, g, w_bf, conv_w, g_conv, to_bf16, *, seg, ms, mc):
    nseg, tlen, d = x3d.shape
    G = ms // SSM_GROUP_CH
    lp = SSM_CHUNK * SSM_GROUP_CH
    nsteps = tlen // seg
    row_spec = lambda c: pl.BlockSpec((nseg, seg, c), lambda t: (0, t, 0))
    slab_specs = [pl.BlockSpec((w.shape[0] // nsteps, w.shape[1]), lambda t: (t, 0)) for w in to_bf16]
    in_specs = [row_spec(d), _const_spec((1, d)), _const_spec(w_bf.shape), _const_spec(conv_w.shape),
                _const_spec((1, mc))] + slab_specs
    return pl.pallas_call(
        functools.partial(_mix_in_kernel, nseg=nseg, seg=seg, nsub=MIX_SUBBLOCKS, ncast=len(to_bf16), ms=ms, mc=mc),
        grid=(nsteps,), in_specs=in_specs,
        out_specs=[pl.BlockSpec((G, seg // SSM_CHUNK * nseg, lp), lambda t: (0, t, 0)), row_spec(mc),
                   pl.BlockSpec((nseg, 2, mc), lambda t: (0, 0, 0))] + slab_specs,
        out_shape=[jax.ShapeDtypeStruct((G, tlen // SSM_CHUNK * nseg, lp), BF16),
                   jax.ShapeDtypeStruct((nseg, tlen, mc), BF16), jax.ShapeDtypeStruct((nseg, 2, mc), F32)]
        + [jax.ShapeDtypeStruct(w.shape, BF16) for w in to_bf16],
        scratch_shapes=[pltpu.VMEM((nseg, SUBLANES, mc), F32)],
        compiler_params=_params("arbitrary"), name="mix_in")(x3d, g.reshape(1, d), w_bf, conv_w,
                                                              g_conv.reshape(1, mc), *to_bf16)


def _ssm_kernel(u_ref, m_ref, s_ref, o_ref, ap_ref, y_ref, fr_ref, fi_ref, lre, lim, ire, iim, *, gb, nb, nchunk):
    n = SSM_STATE
    for gi in range(gb):
        loc = _dot(u_ref[gi], s_ref[gi].astype(BF16))
        lre[gi] = loc[:, 0:n]
        lim[gi] = loc[:, n:2 * n]
    ar = [jnp.broadcast_to(ap_ref[gi, 0:1, :], (nb, n)) for gi in range(gb)]
    ai = [jnp.broadcast_to(ap_ref[gi, 1:2, :], (nb, n)) for gi in range(gb)]
    st0 = tuple((jnp.zeros((nb, n), F32), jnp.zeros((nb, n), F32)) for _ in range(gb))

    def step(c, st):
        r = pl.multiple_of(c * nb, nb)
        new = []
        for gi in range(gb):
            sr, si = st[gi]
            ire[gi, pl.ds(r, nb), :] = sr
            iim[gi, pl.ds(r, nb), :] = si
            nr = ar[gi] * sr - ai[gi] * si + lre[gi, pl.ds(r, nb), :]
            ni = ar[gi] * si + ai[gi] * sr + lim[gi, pl.ds(r, nb), :]
            new.append((nr, ni))
        return tuple(new)

    st = lax.fori_loop(0, nchunk, step, st0, unroll=4)
    fr_ref[...] = jnp.transpose(jnp.stack([st[gi][0] for gi in range(gb)], axis=0), (1, 0, 2))
    fi_ref[...] = jnp.transpose(jnp.stack([st[gi][1] for gi in range(gb)], axis=0), (1, 0, 2))
    for gi in range(gb):
        ini = jnp.concatenate([ire[gi], iim[gi]], axis=1).astype(BF16)
        y = _dot(u_ref[gi], m_ref[gi]) + _dot(ini, o_ref[gi])
        y_ref[gi] = y.astype(y_ref.dtype)


def _ssm(u_g, mats, apow, *, nb, nchunk, gb=SUBLANES):
    G, R, LP = u_g.shape
    n = SSM_STATE
    ins = [u_g, *mats, apow]
    gspec = lambda *s: pl.BlockSpec((gb,) + s, lambda i: (i, 0, 0))
    out_shape = [jax.ShapeDtypeStruct((G, R, LP), F32), jax.ShapeDtypeStruct((nb, G, n), F32),
                 jax.ShapeDtypeStruct((nb, G, n), F32)]
    state_spec = pl.BlockSpec((nb, gb, n), lambda i: (0, i, 0))
    return pl.pallas_call(
        functools.partial(_ssm_kernel, gb=gb, nb=nb, nchunk=nchunk),
        grid=(G // gb,), in_specs=[gspec(*a.shape[1:]) for a in ins],
        out_specs=[gspec(R, LP), state_spec, state_spec],
        out_shape=out_shape, scratch_shapes=[pltpu.VMEM((gb, R, n), F32)] * 4,
        compiler_params=_params("arbitrary"), name="ssm_chunk")(*ins)


def _mix_out_core(x, ys, yc, wg_ref, bg_ref, gs_ref, wo_ref, gx_ref, wq_ref, *, ms, q_scale):
    y = jax.nn.gelu(ys)
    y = y * jax.nn.sigmoid(_dot(y.astype(BF16), wg_ref[...]) + bg_ref[...])
    ysn = _rms(y, gs_ref[...]).astype(BF16)
    x1 = x + _dot(jnp.concatenate([ysn, yc], axis=1), wo_ref[...])
    h = _rms(x1, gx_ref[...]).astype(BF16)
    return x1, _dot(h, wq_ref[...]) * q_scale


def _memory_kv(m_ref, g_ref, wk_ref, wv_ref, k_ref, v_ref, kb_ref, vb_ref):
    h = _rms(m_ref[0], g_ref[...]).astype(BF16)
    hd = k_ref.shape[-1]
    for w_ref, out_ref, bf_ref in ((wk_ref, k_ref, kb_ref), (wv_ref, v_ref, vb_ref)):
        kv = _dot(h, w_ref[...])
        heads = jnp.stack([kv[:, i * hd:(i + 1) * hd] for i in range(XATTN_HEADS)], axis=0)
        out_ref[0] = jnp.transpose(heads, (1, 0, 2))
        bf_ref[0] = heads.astype(BF16)


def _mix_out_kernel(x_ref, ys_ref, yc_ref, wg_ref, bg_ref, gs_ref, wo_ref, gx_ref, wq_ref, mem_ref, gm_ref, wk_ref,
                    wv_ref, x1_ref, q_ref, k_ref, v_ref, kb_ref, vb_ref, *, nseg, seg, nsub, ms, q_scale):
    _memory_kv(mem_ref, gm_ref, wk_ref, wv_ref, k_ref, v_ref, kb_ref, vb_ref)
    sub = seg // nsub
    rg = sub // SSM_CHUNK * nseg
    d = x_ref.shape[-1]
    for h in range(nsub):
        steps = slice(h * sub, (h + 1) * sub)
        ys = _groups_to_tokens(ys_ref.at[:, pl.ds(h * rg, rg), :], nseg=nseg, seg=sub)
        x1, q = _mix_out_core(x_ref[:, steps, :].reshape(nseg * sub, d),
                              ys, yc_ref[:, steps, :].reshape(nseg * sub, yc_ref.shape[-1]),
                              wg_ref, bg_ref, gs_ref, wo_ref, gx_ref, wq_ref, ms=ms, q_scale=q_scale)
        x1_ref[:, steps, :] = x1.reshape(nseg, sub, d)
        q_ref[:, steps, :] = q.astype(q_ref.dtype).reshape(nseg, sub, d)


def _mix_out(x3d, ys, yc, w_glu, b_glu, g_ssm, w_out, g_x, w_q, mem, g_mem, w_k, w_v, *, seg):
    nseg, tlen, d = x3d.shape
    ms = w_glu.shape[0]
    hd = d // XATTN_HEADS
    nsteps = tlen // seg
    nmem, nm, _ = mem.shape
    per_seq = nsteps // nmem
    mt = nm // per_seq
    assert per_seq * nmem == nsteps and mt * per_seq == nm
    row_spec = lambda c: pl.BlockSpec((nseg, seg, c), lambda t: (0, t, 0))
    ys_spec = pl.BlockSpec((ys.shape[0], seg // SSM_CHUNK * nseg, ys.shape[2]), lambda t: (0, t, 0))
    in_specs = [row_spec(d), ys_spec, row_spec(yc.shape[-1]), _const_spec(w_glu.shape), _const_spec((1, ms)),
                _const_spec((1, ms)), _const_spec(w_out.shape), _const_spec((1, d)), _const_spec(w_q.shape),
                pl.BlockSpec((1, mt, d), lambda t: (t // per_seq, t % per_seq, 0)), _const_spec((1, d)),
                _const_spec(w_k.shape), _const_spec(w_v.shape)]
    kv_spec = pl.BlockSpec((1, mt, XATTN_HEADS, hd), lambda t: (t // per_seq, t % per_seq, 0, 0))
    kvb_spec = pl.BlockSpec((1, XATTN_HEADS, mt, hd), lambda t: (t // per_seq, 0, t % per_seq, 0))
    return pl.pallas_call(
        functools.partial(_mix_out_kernel, nseg=nseg, seg=seg, nsub=MIX_SUBBLOCKS, ms=ms, q_scale=hd ** -0.5),
        grid=(nsteps,), in_specs=in_specs, out_specs=[row_spec(d), row_spec(d), kv_spec, kv_spec, kvb_spec, kvb_spec],
        out_shape=[jax.ShapeDtypeStruct((nseg, tlen, d), F32), jax.ShapeDtypeStruct((nseg, tlen, d), BF16)]
        + [jax.ShapeDtypeStruct((nmem, nm, XATTN_HEADS, hd), F32)] * 2
        + [jax.ShapeDtypeStruct((nmem, XATTN_HEADS, nm, hd), BF16)] * 2,
        compiler_params=_params("parallel"), name="mix_out")(
            x3d, ys, yc, w_glu, b_glu.reshape(1, ms), g_ssm.reshape(1, ms), w_out, g_x.reshape(1, d), w_q,
            mem, g_mem.reshape(1, d), w_k, w_v)


def _step_mix_kernel(x_ref, g_ref, w_ref, cw_ref, gc_ref, prev_ref, s0r_ref, s0i_ref, bband_ref, cband_ref, arow_ref,
                     d_ref, wg_ref, bg_ref, gs_ref, wo_ref, gx_ref, wq_ref,
                     x1_ref, q_ref, buf_ref, fr_ref, fi_ref, s_ref, ys_ref, *, nseq, tlen, ms, mc, q_scale):
    x = _step_rows(x_ref[...])

    def conv(v):
        out, buf = _conv3_steps_apart(v, cw_ref, s_ref, _step_rows(prev_ref[...]), ts=nseq)
        buf_ref[...] = _seq_major(buf, nseq)
        return out

    zu, ycn = _mix_in_core(x, g_ref, w_ref, gc_ref, conv, ms=ms, mc=mc)

    wu, ws = bband_ref.shape[1], arow_ref.shape[2]
    for i in range(bband_ref.shape[0]):
        ucols, scols = slice(i * wu, (i + 1) * wu), slice(i * ws, (i + 1) * ws)
        sr, si = s0r_ref[:, scols], s0i_ref[:, scols]
        ar, ai = arow_ref[i, 0:1, :], arow_ref[i, 1:2, :]
        bband, cband = bband_ref[i].astype(BF16), cband_ref[i].astype(BF16)
        for t in range(tlen):
            ut = zu[t * nseq:(t + 1) * nseq, ucols]
            bu = _dot(ut.astype(BF16), bband)
            sr, si = ar * sr - ai * si + bu[:, 0:ws], ar * si + ai * sr + bu[:, ws:]
            cs = _dot(jnp.concatenate([sr, si], axis=1).astype(BF16), cband)
            ys_ref[t * nseq:(t + 1) * nseq, ucols] = cs + d_ref[:, ucols] * ut
        fr_ref[:, scols] = sr
        fi_ref[:, scols] = si

    x1, q = _mix_out_core(x, ys_ref[...], ycn, wg_ref, bg_ref, gs_ref, wo_ref, gx_ref, wq_ref, ms=ms, q_scale=q_scale)
    x1_ref[...] = x1
    q_ref[...] = _seq_major(q, nseq)


def _step_mix(x3d, p, conv_prev, s0_re, s0_im, bands, d_row):
    nseq, tlen, d = x3d.shape
    rows = nseq * tlen
    ms = p["w_glu"].shape[0]
    mc = p["conv_w"].shape[1]
    hd = d // XATTN_HEADS
    bband, cband, arow = bands
    ins = [x3d, p["norm_mix"].reshape(1, d), p["w_in"], p["conv_w"], p["norm_conv_out"].reshape(1, mc), conv_prev,
           s0_re, s0_im, bband, cband, arow, d_row, p["w_glu"], p["b_glu"].reshape(1, ms),
           p["norm_ssm_out"].reshape(1, ms), p["w_out"], p["norm_xattn"].reshape(1, d), p["w_q"]]
    full = lambda shape: pl.BlockSpec(shape, lambda i: (0,) * len(shape))
    out_shape = [jax.ShapeDtypeStruct((rows, d), F32), jax.ShapeDtypeStruct((nseq, tlen, d), F32),
                 jax.ShapeDtypeStruct(conv_prev.shape, F32), jax.ShapeDtypeStruct(s0_re.shape, F32),
                 jax.ShapeDtypeStruct(s0_im.shape, F32)]
    return pl.pallas_call(
        functools.partial(_step_mix_kernel, nseq=nseq, tlen=tlen, ms=ms, mc=mc, q_scale=hd ** -0.5),
        grid=(1,), in_specs=[_const_spec(a.shape) for a in ins], out_specs=[full(o.shape) for o in out_shape],
        out_shape=out_shape,
        scratch_shapes=[pltpu.VMEM((2 * nseq + rows, mc), F32), pltpu.VMEM((rows, ms), F32)],
        compiler_params=_params("arbitrary"), name="step_mix")(*ins)


def _softmax_attention(qs, ks, vs):
    tq = qs[0].shape[0]
    s = jnp.concatenate([lax.dot_general(q, k, (((1,), (1,)), ((), ())), preferred_element_type=F32)
                         for q, k in zip(qs, ks)], axis=0)
    p = jnp.exp(s - jnp.max(s, axis=-1, keepdims=True))
    p = p / jnp.sum(p, axis=-1, keepdims=True)
    return [_dot(p[i * tq:(i + 1) * tq].astype(BF16), v) for i, v in enumerate(vs)]


def _head_slices(ref, rows, hd):
    return [ref[rows, h * hd:(h + 1) * hd].astype(BF16) for h in range(XATTN_HEADS)]


def _kv_heads(ref, j):
    hm = jnp.transpose(ref[j], (1, 0, 2))
    return [hm[h].astype(BF16) for h in range(XATTN_HEADS)]


def _attend_sequences(q_ref, k_ref, v_ref, o_ref):
    sb, _, d = q_ref.shape
    hd = d // XATTN_HEADS
    qs, ks, vs = [], [], []
    for j in range(sb):
        qs += _head_slices(q_ref.at[j], slice(None), hd)
        ks += _kv_heads(k_ref, j)
        vs += _kv_heads(v_ref, j)
    outs = _softmax_attention(qs, ks, vs)
    for j in range(sb):
        o_ref[j] = jnp.concatenate(outs[j * XATTN_HEADS:(j + 1) * XATTN_HEADS], axis=1).astype(o_ref.dtype)


def _ffn_kernel(*refs, ts, has_prev, has_kv, side):
    x_ref, o_ref = refs[:2]
    refs = refs[2:]
    if has_kv:
        k_ref, v_ref = refs[:2]
        refs = refs[2:]
    if side:
        sq_ref, sk_ref, sv_ref = refs[:3]
        refs = refs[3:]
    wxo_ref, gf_ref, wup_ref, wgate_ref, cw_ref, wdn_ref, gl_ref = refs[:7]
    refs = refs[7:]
    if has_prev:
        prev_ref, refs = refs[0], refs[1:]
    if side:
        y_ref, buf_ref, so_ref, s_ref = refs
    else:
        y_ref, buf_ref, s_ref = refs
    if not has_prev:
        @pl.when(pl.program_id(1) == 0)
        def _():
            s_ref[...] = jnp.zeros(s_ref.shape, F32)

    if has_kv:
        hd = o_ref.shape[1] // XATTN_HEADS
        rows = slice(None)
        heads = _softmax_attention(_head_slices(o_ref, rows, hd), [k_ref[0, h] for h in range(XATTN_HEADS)],
                                   [v_ref[0, h] for h in range(XATTN_HEADS)])
        o = jnp.concatenate(heads, axis=1).astype(BF16)
    elif has_prev:
        o = _step_rows(o_ref[...]).astype(BF16)
    else:
        o = o_ref[...].astype(BF16)
    x2 = x_ref[...] + _dot(o, wxo_ref[...])
    hb = _rms(x2, gf_ref[...]).astype(BF16)
    up = _dot(hb, wup_ref[...])
    if has_prev:
        a, buf = _conv3_steps_apart(up, cw_ref, s_ref, _step_rows(prev_ref[...]), ts=ts)
        buf_ref[...] = _seq_major(buf, ts)
    else:
        a = _conv3_rows(up, cw_ref, s_ref)
        buf_ref[...] = _conv3_state(s_ref)
    act = (jax.nn.gelu(a) * _dot(hb, wgate_ref[...])).astype(BF16)
    y = _rms(x2 + _dot(act, wdn_ref[...]), gl_ref[...])
    y_ref[...] = _seq_major(y, ts) if has_prev else y
    if side:
        _attend_sequences(sq_ref, sk_ref, sv_ref, so_ref)


def _ffn(x2d, o2d, kv, w_xo, g_ffn, w_up, w_gate, conv_w, w_down, g_last, prev, *, nb, nt, tm, ts, side=None):
    rows, d = x2d.shape
    dff = w_up.shape[1]
    has_prev = prev is not None
    row_spec = pl.BlockSpec((tm, d), lambda b, t: (b * nt + t, 0))
    in_specs = [row_spec, _const_spec(o2d.shape) if has_prev else row_spec]
    ins = [x2d, o2d]
    if kv is not None:
        in_specs += [pl.BlockSpec((1,) + kv[0].shape[1:], lambda b, t: (b, 0, 0, 0))] * 2
        ins += list(kv)
    if side is not None:
        sq, sk, sv = side
        sb = sq.shape[0] // (nb * nt)
        assert sb * nb * nt == sq.shape[0]
        side_q_spec = pl.BlockSpec((sb,) + sq.shape[1:], lambda b, t: (b * nt + t, 0, 0))
        in_specs += [side_q_spec] + [pl.BlockSpec((sb,) + sk.shape[1:], lambda b, t: (b * nt + t, 0, 0, 0))] * 2
        ins += [sq, sk, sv]
    in_specs += [_const_spec(w_xo.shape), _const_spec((1, d)), _const_spec(w_up.shape), _const_spec(w_gate.shape),
                 _const_spec(conv_w.shape), _const_spec(w_down.shape), _const_spec((1, d))]
    ins += [w_xo, g_ffn.reshape(1, d), w_up, w_gate, conv_w, w_down, g_last.reshape(1, d)]
    if has_prev:
        in_specs.append(_const_spec(prev.shape))
        ins.append(prev)
        out_shape = [jax.ShapeDtypeStruct(o2d.shape, F32), jax.ShapeDtypeStruct(prev.shape, F32)]
        out_specs = [pl.BlockSpec(o.shape, lambda b, t: (0, 0, 0)) for o in out_shape]
    else:
        out_shape = [jax.ShapeDtypeStruct((rows, d), F32), jax.ShapeDtypeStruct((nb, 2, dff), F32)]
        out_specs = [row_spec, pl.BlockSpec((None, 2, dff), lambda b, t: (b, 0, 0))]
    if side is not None:
        out_specs.append(side_q_spec)
        out_shape.append(jax.ShapeDtypeStruct(sq.shape, F32))
    return pl.pallas_call(
        functools.partial(_ffn_kernel, ts=ts, has_prev=has_prev, has_kv=kv is not None, side=side is not None),
        grid=(nb, nt), in_specs=in_specs, out_specs=out_specs, out_shape=out_shape,
        scratch_shapes=[pltpu.VMEM((2 * ts + tm if has_prev else SUBLANES, dff), F32)],
        compiler_params=_params("parallel", "arbitrary", vmem=V7X_VMEM_LIMIT if side is None else V7X_VMEM_LIMIT_MAX),
        name="conv_ffn")(*ins)


def _prompt_mixer(x3d, mem, p, mats, apow, later_weights):
    nseq, tlen, d = x3d.shape
    ms = p["w_glu"].shape[0]
    mc = p["conv_w"].shape[1]
    u, ycn, conv_buf, *later_bf16 = _mix_in(x3d, p["norm_mix"], p["w_in"], p["conv_w"], p["norm_conv_out"],
                                            later_weights, seg=MIX_STEPS, ms=ms, mc=mc)
    y_g, f_re, f_im = _ssm(u, mats, apow, nb=nseq, nchunk=tlen // SSM_CHUNK)
    *later_bf16, w_k, w_v = later_bf16
    x1, q, *kv = _mix_out(x3d, y_g, ycn, p["w_glu"], p["b_glu"], p["norm_ssm_out"], p["w_out"], p["norm_xattn"],
                          p["w_q"], mem, p["norm_mem"], w_k, w_v, seg=MIX_STEPS)
    return x1, q, (f_re, f_im, conv_buf), kv, later_bf16


def _prompt_ffn(x1, q, k, v, p, g_last, side):
    nseq, tlen, d = x1.shape
    rows = nseq * tlen
    return _ffn(x1.reshape(rows, d), q.reshape(rows, d), (k, v), p["w_xo"], p["norm_ffn"], p["w_up"],
                p["w_gate"], p["ffn_conv_w"], p["w_down"], g_last, None,
                nb=nseq, nt=tlen // FFN_ROWS, tm=FFN_ROWS, ts=1, side=side)


def _step_layer_mix(x_bm, prev, p, bands, d_row):
    nseq = x_bm.shape[0]
    G, N = prev[0].shape[1:]
    x1, q, conv_buf, f_re, f_im = _step_mix(x_bm, p, prev[2], prev[0].reshape(nseq, G * N),
                                            prev[1].reshape(nseq, G * N), bands, d_row)
    return x1, q, (f_re.reshape(nseq, G, N), f_im.reshape(nseq, G, N), conv_buf)


def _step_layer_ffn(x1, o_bm, prev_ffn, p, g_last):
    return _ffn(x1, o_bm, None, p["w_xo"], p["norm_ffn"], p["w_up"], p["w_gate"], p["ffn_conv_w"], p["w_down"],
                g_last, prev_ffn, nb=1, nt=1, tm=x1.shape[0], ts=o_bm.shape[0])


def kernel(x_prompt, x_sample, mem_prompt, cache_mem_k, cache_mem_v, state_ssm_re, state_ssm_im, state_conv, state_ffn_conv, norm_mix, w_in, ssm_A_re, ssm_A_im, ssm_log_dt, ssm_B_re, ssm_B_im, ssm_C_re, ssm_C_im, ssm_D, w_glu, b_glu, conv_w, norm_ssm_out, norm_conv_out, w_out, norm_xattn, norm_mem, w_q, w_k, w_v, w_xo, norm_ffn, w_up, w_gate, ffn_conv_w, w_down, norm_final):
    depth = w_in.shape[0]
    assert depth == 1, "the final norm is fused into the last (only) layer's ConvFFN kernel"
    nbp, tp, d = x_prompt.shape
    assert tp % FFN_ROWS == 0 and tp % MIX_STEPS == 0 and nbp == 8

    xp, xs = x_prompt, x_sample
    outs_p, outs_s, mk_p, mv_p = [], [], [], []
    for l in range(depth):
        m, s_cat, o_cat, apow, bband, cband, arow, *mixer_w = _ssm_prep(
            ssm_A_re[l], ssm_A_im[l], ssm_log_dt[l], ssm_B_re[l], ssm_B_im[l], ssm_C_re[l], ssm_C_im[l], ssm_D[l],
            [w_in[l], w_glu[l], w_out[l], w_q[l]], chunk=SSM_CHUNK)
        mats = (m, s_cat, o_cat)
        p = dict(norm_mix=norm_mix[l], b_glu=b_glu[l], conv_w=conv_w[l], norm_ssm_out=norm_ssm_out[l],
                 norm_conv_out=norm_conv_out[l], norm_xattn=norm_xattn[l], norm_ffn=norm_ffn[l],
                 ffn_conv_w=ffn_conv_w[l], norm_mem=norm_mem[l])
        p["w_in"], p["w_glu"], p["w_out"], p["w_q"] = mixer_w

        x1s, qs, st_s = _step_layer_mix(xs, (state_ssm_re[l], state_ssm_im[l], state_conv[l]), p,
                                        (bband, cband, arow), ssm_D[l].reshape(1, -1))
        x1p, qp, st_p, (k_p, v_p, kb_p, vb_p), later = _prompt_mixer(
            xp, mem_prompt, p, mats, apow, [w_xo[l], w_up[l], w_gate[l], w_down[l], w_k[l], w_v[l]])
        p["w_xo"], p["w_up"], p["w_gate"], p["w_down"] = later
        mk_p.append(k_p)
        mv_p.append(v_p)

        side = (qs, cache_mem_k[l], cache_mem_v[l])
        xp, ffn_buf_p, os_bm = _prompt_ffn(x1p, qp, kb_p, vb_p, p, norm_final, side)
        outs_p.append(st_p + (ffn_buf_p,))
        xs, ffn_buf_s = _step_layer_ffn(x1s, os_bm, state_ffn_conv[l], p, norm_final)
        outs_s.append(st_s + (ffn_buf_s,))

    stack = lambda outs, i: jnp.stack([o[i] for o in outs])
    return (xp.reshape(nbp, tp, d), xs, jnp.stack(mk_p), jnp.stack(mv_p),
            stack(outs_p, 0), stack(outs_p, 1), stack(outs_p, 2), stack(outs_p, 3),
            stack(outs_s, 0), stack(outs_s, 1), stack(outs_s, 2), stack(outs_s, 3))
```
